```python
import math
import jax, jax.numpy as jnp
from jax import lax
import numpy as np

D_MODEL = 2048
BATCH = 32
SEQ = 256
DEPTH = 2
DEC_BATCH = 2
DEC_SEQ = 2048
PAST_LEN = 512

GRID_W = 64
N_EVEN = (DEPTH + 1) // 2
N_ODD = DEPTH // 2
CONV_CH = D_MODEL // 2
CONV_W = 31
SSM_CH = D_MODEL // 2
SSM_P = 16
SSM_G = SSM_CH // SSM_P
SSM_N = 64
ATT_H = 16
ATT_DK = 64
ATT_DV = 2 * ATT_DK
D_FF = 4 * D_MODEL
Q_BLOCK = 128
ALPHA = (2 * DEPTH) ** 0.25
BETA = (8 * DEPTH) ** -0.25
LN_EPS = 1e-5
ROPE_BASE = 10000.0

kernel_name = "hybrid_diffusion_conv_s5_diffattn_step"


def layer_norm(x, g, b):
    x32 = x.astype(jnp.float32)
    mu = jnp.mean(x32, axis=-1, keepdims=True)
    var = jnp.mean(jnp.square(x32 - mu), axis=-1, keepdims=True)
    y = (x32 - mu) * lax.rsqrt(var + LN_EPS) * g.astype(jnp.float32) + b.astype(jnp.float32)
    return y.astype(x.dtype)


def modulate(x, shift, scale):
    return x * (1 + scale) + shift


def post_residual(x, y, gate, g, b):
    return layer_norm(ALPHA * x + gate * y, g, b)


def sq_relu_mlp(h, w1, w2):
    return jnp.square(jax.nn.relu(h @ w1)) @ w2


def conformer_conv(a_val, a_gate, w_dw, b_dw, ln_g, ln_b):
    u = a_val * jax.nn.sigmoid(a_gate)
    y = lax.conv_general_dilated(
        u, w_dw[:, None, :].astype(u.dtype), window_strides=(1,),
        padding=[(CONV_W // 2, CONV_W // 2)],
        dimension_numbers=("NWC", "WIO", "NWC"),
        feature_group_count=u.shape[-1]) + b_dw
    return jax.nn.silu(layer_norm(y, ln_g, ln_b))


def s5_discretize(lam_re, lam_im, log_dt, b_re, b_im):
    f32 = jnp.float32
    lam_re, lam_im = lam_re.astype(f32), lam_im.astype(f32)
    dt = jnp.exp(log_dt.astype(f32))[:, None]
    mag = jnp.exp(lam_re * dt)
    lb_re, lb_im = mag * jnp.cos(lam_im * dt), mag * jnp.sin(lam_im * dt)
    den = jnp.square(lam_re) + jnp.square(lam_im)
    coef_re = ((lb_re - 1) * lam_re + lb_im * lam_im) / den
    coef_im = (lb_im * lam_re - (lb_re - 1) * lam_im) / den
    b_re, b_im = b_re.astype(f32), b_im.astype(f32)
    bb_re = coef_re[..., None] * b_re - coef_im[..., None] * b_im
    bb_im = coef_re[..., None] * b_im + coef_im[..., None] * b_re
    return lb_re, lb_im, bb_re, bb_im


def complex_linear_combine(e1, e2):
    a1r, a1i, b1r, b1i = e1
    a2r, a2i, b2r, b2i = e2
    return (a1r * a2r - a1i * a2i, a1r * a2i + a1i * a2r,
            a2r * b1r - a2i * b1i + b2r, a2r * b1i + a2i * b1r + b2i)


def s5_scan(u, h0_re, h0_im, lam_re, lam_im, log_dt, b_re, b_im, c_re, c_im):
    f32 = jnp.float32
    lb_re, lb_im, bb_re, bb_im = s5_discretize(lam_re, lam_im, log_dt, b_re, b_im)
    bu_re = jnp.einsum("blgp,gnp->blgn", u, bb_re)
    bu_im = jnp.einsum("blgp,gnp->blgn", u, bb_im)
    h0_re, h0_im = h0_re.astype(f32), h0_im.astype(f32)
    bu_re = bu_re.at[:, 0].add(lb_re * h0_re - lb_im * h0_im)
    bu_im = bu_im.at[:, 0].add(lb_re * h0_im + lb_im * h0_re)
    a_re = jnp.broadcast_to(lb_re, bu_re.shape)
    a_im = jnp.broadcast_to(lb_im, bu_im.shape)
    _, _, h_re, h_im = lax.associative_scan(
        complex_linear_combine, (a_re, a_im, bu_re, bu_im), axis=1)
    c_re, c_im = c_re.astype(f32), c_im.astype(f32)
    y = jnp.einsum("gpn,blgn->blgp", c_re, h_re) - jnp.einsum("gpn,blgn->blgp", c_im, h_im)
    return y, h_re[:, -1], h_im[:, -1]


def conv_ssm_mixer(h, h0_re, h0_im, w_in, w_dw, b_dw, cln_g, cln_b, lam_re, lam_im,
                   log_dt, b_re, b_im, c_re, c_im, d_skip, w_glu, w_out):
    bsz, L, _ = h.shape
    a_val, a_gate, u = jnp.split(h @ w_in, [CONV_CH, 2 * CONV_CH], axis=-1)
    y_conv = conformer_conv(a_val, a_gate, w_dw, b_dw, cln_g, cln_b)
    u32 = u.astype(jnp.float32)
    ug = u32.reshape(bsz, L, SSM_G, SSM_P)
    y_f, hf_re, hf_im = s5_scan(ug, h0_re[:, 0], h0_im[:, 0], lam_re[0], lam_im[0], log_dt[0],
                                b_re[0], b_im[0], c_re[0], c_im[0])
    y_r, hb_re, hb_im = s5_scan(ug[:, ::-1], h0_re[:, 1], h0_im[:, 1], lam_re[1], lam_im[1],
                                log_dt[1], b_re[1], b_im[1], c_re[1], c_im[1])
    y_s = (y_f + y_r[:, ::-1]).reshape(bsz, L, SSM_CH) + d_skip.astype(jnp.float32) * u32
    y_s = jax.nn.gelu(y_s).astype(h.dtype)
    y_ssm = y_s * jax.nn.sigmoid(y_s @ w_glu)
    out = jnp.concatenate([y_conv, y_ssm], axis=-1) @ w_out
    return out, jnp.stack([hf_re, hb_re], axis=1), jnp.stack([hf_im, hb_im], axis=1)


def rope_axis(seg, pos):
    half = seg.shape[-1] // 2
    freqs = ROPE_BASE ** (-jnp.arange(half, dtype=jnp.float32) / half)
    ang = pos[:, None] * freqs
    cos, sin = jnp.cos(ang), jnp.sin(ang)
    s = seg.astype(jnp.float32)
    s1, s2 = s[..., :half], s[..., half:]
    return jnp.concatenate([s1 * cos - s2 * sin, s1 * sin + s2 * cos], axis=-1).astype(seg.dtype)


def rope_2d(x, rows):
    row = jnp.repeat(jnp.arange(rows, dtype=jnp.float32), GRID_W)
    col = jnp.tile(jnp.arange(GRID_W, dtype=jnp.float32), rows)
    ax = x.shape[-1] // 2
    return jnp.concatenate([rope_axis(x[..., :ax], row), rope_axis(x[..., ax:], col)], axis=-1)


def diff_attn_project(h, w_qkv):
    bsz, L, _ = h.shape
    qk = 2 * ATT_H * ATT_DK
    q, k, v = jnp.split(h @ w_qkv, [qk, 2 * qk], axis=-1)
    q = q.reshape(bsz, L, 2, ATT_H, ATT_DK).transpose(2, 0, 3, 1, 4)
    k = k.reshape(bsz, L, 2, ATT_H, ATT_DK).transpose(2, 0, 3, 1, 4)
    v = v.reshape(bsz, L, ATT_H, ATT_DV).transpose(0, 2, 1, 3)
    return q, k, v


def diff_lambda(lq1, lk1, lq2, lk2, lam_init):
    f32 = jnp.float32
    return (jnp.exp(jnp.sum(lq1.astype(f32) * lk1.astype(f32)))
            - jnp.exp(jnp.sum(lq2.astype(f32) * lk2.astype(f32))) + lam_init)


def diff_attend(q, k, v, lam):
    _, bsz, nh, lq, dk = q.shape
    nb = lq // Q_BLOCK
    qb = q.reshape(2, bsz, nh, nb, Q_BLOCK, dk).transpose(3, 0, 1, 2, 4, 5)
    scale = dk ** -0.5

    def one_block(qblk):
        s = jnp.einsum("mbhqd,mbhkd->mbhqk", qblk, k).astype(jnp.float32) * scale
        p = jax.nn.softmax(s, axis=-1)
        w = (p[0] - lam * p[1]).astype(v.dtype)
        return jnp.einsum("bhqk,bhkd->bhqd", w, v)

    o = lax.map(one_block, qb)
    return o.transpose(1, 2, 0, 3, 4).reshape(bsz, nh, lq, v.shape[-1])


def diff_attn_output(o, subln_g, lam_init, w_out):
    bsz, nh, L, dv = o.shape
    dt = o.dtype
    o32 = o.astype(jnp.float32)
    o32 = o32 * lax.rsqrt(jnp.mean(jnp.square(o32), axis=-1, keepdims=True) + LN_EPS)
    o32 = o32 * subln_g.astype(jnp.float32) * (1.0 - lam_init)
    o = o32.astype(dt).transpose(0, 2, 1, 3).reshape(bsz, L, nh * dv)
    return o @ w_out


def setup_inputs(seed: int = 0) -> dict:
    key = jax.random.key(seed)
    ks = jax.random.split(key, 40)
    f32 = jnp.float32

    def nrm(i, shape, s=1.0):
        return s * jax.random.normal(ks[i], shape, f32)

    lam_im_base = math.pi * jnp.arange(SSM_N, dtype=f32)
    return {
        "x_prompt": nrm(0, (BATCH, SEQ, D_MODEL)),
        "x_sample": nrm(1, (DEC_BATCH, DEC_SEQ, D_MODEL)),
        "state_s5_re": nrm(2, (DEC_BATCH, N_EVEN, 2, SSM_G, SSM_N), 0.3),
        "state_s5_im": nrm(3, (DEC_BATCH, N_EVEN, 2, SSM_G, SSM_N), 0.3),
        "cache_k": nrm(4, (DEC_BATCH, N_ODD, 2, ATT_H, PAST_LEN, ATT_DK)),
        "cache_v": nrm(5, (DEC_BATCH, N_ODD, ATT_H, PAST_LEN, ATT_DV)),
        "c": nrm(6, (DEC_BATCH, D_MODEL)),
        "c_ctx": nrm(7, (D_MODEL,)),
        "w_mod": nrm(8, (DEPTH, D_MODEL, 6 * D_MODEL), D_MODEL ** -0.5),
        "b_mod": nrm(9, (DEPTH, 6 * D_MODEL), 0.01),
        "ln_g": 1.0 + nrm(10, (DEPTH, 2, D_MODEL), 0.01),
        "ln_b": nrm(11, (DEPTH, 2, D_MODEL), 0.01),
        "w_in_ab": nrm(12, (N_EVEN, D_MODEL, 2 * CONV_CH + SSM_CH), D_MODEL ** -0.5),
        "w_dw": nrm(13, (N_EVEN, CONV_W, CONV_CH), CONV_W ** -0.5),
        "b_dw": nrm(14, (N_EVEN, CONV_CH), 0.01),
        "conv_ln_g": 1.0 + nrm(15, (N_EVEN, CONV_CH), 0.01),
        "conv_ln_b": nrm(16, (N_EVEN, CONV_CH), 0.01),
        "s5_lambda_re": -0.5 + nrm(17, (N_EVEN, 2, SSM_G, SSM_N), 0.01),
        "s5_lambda_im": lam_im_base + nrm(18, (N_EVEN, 2, SSM_G, SSM_N), 0.01),
        "s5_log_dt": jax.random.uniform(ks[19], (N_EVEN, 2, SSM_G), f32,
                                        minval=math.log(1e-3), maxval=math.log(1e-1)),
        "s5_b_re": nrm(20, (N_EVEN, 2, SSM_G, SSM_N, SSM_P), (2 * SSM_P) ** -0.5),
        "s5_b_im": nrm(21, (N_EVEN, 2, SSM_G, SSM_N, SSM_P), (2 * SSM_P) ** -0.5),
        "s5_c_re": nrm(22, (N_EVEN, 2, SSM_G, SSM_P, SSM_N), (2 * SSM_N) ** -0.5),
        "s5_c_im": nrm(23, (N_EVEN, 2, SSM_G, SSM_P, SSM_N), (2 * SSM_N) ** -0.5),
        "s5_d": 1.0 + nrm(24, (N_EVEN, SSM_CH), 0.1),
        "w_glu": nrm(25, (N_EVEN, SSM_CH, SSM_CH), SSM_CH ** -0.5),
        "w_out_ab": nrm(26, (N_EVEN, CONV_CH + SSM_CH, D_MODEL), BETA * (CONV_CH + SSM_CH) ** -0.5),
        "w_qkv": nrm(27, (N_ODD, D_MODEL, 4 * ATT_H * ATT_DK + ATT_H * ATT_DV), D_MODEL ** -0.5),
        "lam_q1": nrm(28, (N_ODD, ATT_DK), 0.1),
        "lam_k1": nrm(29, (N_ODD, ATT_DK), 0.1),
        "lam_q2": nrm(30, (N_ODD, ATT_DK), 0.1),
        "lam_k2": nrm(31, (N_ODD, ATT_DK), 0.1),
        "subln_g": 1.0 + nrm(32, (N_ODD, ATT_DV), 0.01),
        "w_out_c": nrm(33, (N_ODD, ATT_H * ATT_DV, D_MODEL), BETA * (ATT_H * ATT_DV) ** -0.5),
        "w_ff1": nrm(34, (DEPTH, D_MODEL, D_FF), D_MODEL ** -0.5),
        "w_ff2": nrm(35, (DEPTH, D_FF, D_MODEL), BETA * D_FF ** -0.5),
    }


def reference(x_prompt, x_sample, state_s5_re, state_s5_im, cache_k, cache_v, c, c_ctx,
              w_mod, b_mod, ln_g, ln_b, w_in_ab, w_dw, b_dw, conv_ln_g, conv_ln_b,
              s5_lambda_re, s5_lambda_im, s5_log_dt, s5_b_re, s5_b_im, s5_c_re, s5_c_im,
              s5_d, w_glu, w_out_ab, w_qkv, lam_q1, lam_k1, lam_q2, lam_k2, subln_g,
              w_out_c, w_ff1, w_ff2):
    silu_ctx = jax.nn.silu(c_ctx)[None, :]
    silu_c = jax.nn.silu(c)
    yp, ys = x_prompt, x_sample
    zero_state = jnp.zeros((x_prompt.shape[0], 2, SSM_G, SSM_N), jnp.float32)
    s_re_list, s_im_list, k_list, v_list = [], [], [], []
    for l in range(DEPTH):
        mod_p = jnp.split((silu_ctx @ w_mod[l] + b_mod[l])[:, None, :], 6, axis=-1)
        mod_s = jnp.split((silu_c @ w_mod[l] + b_mod[l])[:, None, :], 6, axis=-1)
        hp = modulate(yp, mod_p[0], mod_p[1])
        hs = modulate(ys, mod_s[0], mod_s[1])
        if l % 2 == 0:
            e = l // 2
            prm = (w_in_ab[e], w_dw[e], b_dw[e], conv_ln_g[e], conv_ln_b[e],
                   s5_lambda_re[e], s5_lambda_im[e], s5_log_dt[e], s5_b_re[e], s5_b_im[e],
                   s5_c_re[e], s5_c_im[e], s5_d[e], w_glu[e], w_out_ab[e])
            mp, st_re, st_im = conv_ssm_mixer(hp, zero_state, zero_state, *prm)
            ms, _, _ = conv_ssm_mixer(hs, state_s5_re[:, e], state_s5_im[:, e], *prm)
            s_re_list.append(st_re)
            s_im_list.append(st_im)
        else:
            o_i = l // 2
            lam_init = 0.8 - 0.6 * math.exp(-0.3 * l)
            lam = diff_lambda(lam_q1[o_i], lam_k1[o_i], lam_q2[o_i], lam_k2[o_i], lam_init)
            qp, kp, vp = diff_attn_project(hp, w_qkv[o_i])
            mp = diff_attn_output(diff_attend(qp, kp, vp, lam), subln_g[o_i], lam_init, w_out_c[o_i])
            k_list.append(jnp.swapaxes(kp, 0, 1))
            v_list.append(vp)
            qs, ks_, vs = diff_attn_project(hs, w_qkv[o_i])
            rows = hs.shape[1] // GRID_W
            qs = rope_2d(qs, rows)
            ks_ = rope_2d(ks_, rows)
            ck = jnp.swapaxes(cache_k[:, o_i], 0, 1).astype(ks_.dtype)
            ks_ = jnp.concatenate([ks_, ck], axis=3)
            vs = jnp.concatenate([vs, cache_v[:, o_i].astype(vs.dtype)], axis=2)
            ms = diff_attn_output(diff_attend(qs, ks_, vs, lam), subln_g[o_i], lam_init, w_out_c[o_i])
        yp = post_residual(yp, mp, mod_p[2], ln_g[l, 0], ln_b[l, 0])
        ys = post_residual(ys, ms, mod_s[2], ln_g[l, 0], ln_b[l, 0])
        fp = sq_relu_mlp(modulate(yp, mod_p[3], mod_p[4]), w_ff1[l], w_ff2[l])
        fs = sq_relu_mlp(modulate(ys, mod_s[3], mod_s[4]), w_ff1[l], w_ff2[l])
        yp = post_residual(yp, fp, mod_p[5], ln_g[l, 1], ln_b[l, 1])
        ys = post_residual(ys, fs, mod_s[5], ln_g[l, 1], ln_b[l, 1])
    new_state_s5_re = jnp.stack(s_re_list, axis=1)
    new_state_s5_im = jnp.stack(s_im_list, axis=1)
    new_cache_k = jnp.stack(k_list, axis=1)
    new_cache_v = jnp.stack(v_list, axis=1)
    return (yp, ys, new_state_s5_re, new_state_s5_im, new_cache_k, new_cache_v)
```

```python
import functools
import math

import jax
import jax.numpy as jnp
from jax import lax
from jax.experimental import pallas as pl
from jax.experimental.pallas import tpu as pltpu

F32 = jnp.float32
BF16 = jnp.bfloat16

LN_EPS = 1e-5
ROPE_BASE = 10000.0
LATENT_GRID_W = 64
MOD_ROWS = 8
V7X_VMEM_LIMIT = 56 * 1024 * 1024
LANES = 128


def _cparams(*sem):
    return pltpu.CompilerParams(dimension_semantics=sem, vmem_limit_bytes=V7X_VMEM_LIMIT)


def _layer_norm(z, g, b):
    mu = jnp.mean(z, axis=-1, keepdims=True)
    zc = z - mu
    var = jnp.mean(jnp.square(zc), axis=-1, keepdims=True)
    return zc * lax.rsqrt(var + LN_EPS) * g + b


def _group_of_tile(i, tiles_p, tiles_per_s):
    return jnp.where(i < tiles_p, 0, 1 + jnp.maximum(i - tiles_p, 0) // tiles_per_s)


def _mod_chunk(mod_ref, g, k, d):
    return mod_ref[pl.ds(g, 1), k * d:(k + 1) * d]


def _modvec_kernel(cv_ref, w_ref, b_ref, o_ref):
    cv = cv_ref[...]
    s = (cv * jax.nn.sigmoid(cv)).astype(BF16)
    o_ref[0] = jnp.dot(s, w_ref[0].astype(BF16), preferred_element_type=F32) + b_ref[0]


def _modvec(cvec, w_mod, b_mod, tn=1024):
    depth, d, n = w_mod.shape
    return pl.pallas_call(
        _modvec_kernel,
        grid=(depth, n // tn),
        in_specs=[pl.BlockSpec((MOD_ROWS, d), lambda l, j: (0, 0)),
                  pl.BlockSpec((1, d, tn), lambda l, j: (l, 0, j)),
                  pl.BlockSpec((1, 1, tn), lambda l, j: (l, 0, j))],
        out_specs=pl.BlockSpec((1, MOD_ROWS, tn), lambda l, j: (l, 0, j)),
        out_shape=jax.ShapeDtypeStruct((depth, MOD_ROWS, n), F32),
        compiler_params=_cparams("arbitrary", "arbitrary"),
        name="modvec",
    )(cvec, w_mod, b_mod.reshape(depth, 1, n))


def _inproj_kernel(x_ref, mod_ref, w_ref, ug_ref, u_ref, *, tiles_p, tiles_per_s, d, c):
    g = _group_of_tile(pl.program_id(0), tiles_p, tiles_per_s)
    shift = _mod_chunk(mod_ref, g, 0, d)
    scale = _mod_chunk(mod_ref, g, 1, d)
    h = (x_ref[...] * (1 + scale) + shift).astype(BF16)
    a_val = jnp.dot(h, w_ref[:, 0:c], preferred_element_type=F32)
    a_gate = jnp.dot(h, w_ref[:, c:2 * c], preferred_element_type=F32)
    ug_ref[...] = a_val * jax.nn.sigmoid(a_gate)
    u_ref[...] = jnp.dot(h, w_ref[:, 2 * c:3 * c], preferred_element_type=F32)


def _inproj(x, mod_l, w_in, tm, tiles_p, tiles_per_s):
    t, d = x.shape
    c = w_in.shape[1] // 3
    kern = functools.partial(_inproj_kernel, tiles_p=tiles_p, tiles_per_s=tiles_per_s, d=d, c=c)
    return pl.pallas_call(
        kern,
        grid=(t // tm,),
        in_specs=[pl.BlockSpec((tm, d), lambda i: (i, 0)),
                  pl.BlockSpec(mod_l.shape, lambda i: (0, 0)),
                  pl.BlockSpec(w_in.shape, lambda i: (0, 0))],
        out_specs=[pl.BlockSpec((tm, c), lambda i: (i, 0)),
                   pl.BlockSpec((tm, c), lambda i: (i, 0))],
        out_shape=[jax.ShapeDtypeStruct((t, c), F32), jax.ShapeDtypeStruct((t, c), F32)],
        compiler_params=_cparams("arbitrary"),
        name="inproj",
    )(x, mod_l, w_in)


CONV_HALO = 16
CONV_ROWS = 32
CONV_COLS = 256


def _conv_kernel(prev_ref, cur_ref, next_ref, w_ref, b_ref, g_ref, beta_ref, o_ref, pad_scr, conv_scr,
                 *, chunks_p, chunks_s, n_chunks_p, width, lc, c):
    i = pl.program_id(0)
    in_p = i < n_chunks_p
    k = jnp.where(in_p, i % chunks_p, jnp.maximum(i - n_chunks_p, 0) % chunks_s)
    last = jnp.where(in_p, chunks_p - 1, chunks_s - 1)
    has_prev = (k > 0).astype(F32)
    has_next = (k < last).astype(F32)
    pad_scr[0:CONV_HALO, :] = prev_ref[...] * has_prev
    pad_scr[CONV_HALO:CONV_HALO + lc, :] = cur_ref[...]
    pad_scr[CONV_HALO + lc:2 * CONV_HALO + lc, :] = next_ref[...] * has_next
    off = CONV_HALO - width // 2

    for r0 in range(0, lc, CONV_ROWS):
        for cb in range(c // CONV_COLS):
            cs = slice(cb * CONV_COLS, (cb + 1) * CONV_COLS)
            acc = jnp.zeros((CONV_ROWS, CONV_COLS), F32)
            for kk in range(width):
                acc = acc + pad_scr[r0 + kk + off:r0 + kk + off + CONV_ROWS, cs] * w_ref[kk:kk + 1, cs]
            conv_scr[r0:r0 + CONV_ROWS, cs] = acc + b_ref[:, cs]
    y = _layer_norm(conv_scr[...], g_ref[...], beta_ref[...])
    o_ref[...] = (y * jax.nn.sigmoid(y)).astype(o_ref.dtype)


def _conv_module(ug, w_dw, b_dw, ln_g, ln_b, lc, lp, ls, tp):
    t, c = ug.shape
    width = w_dw.shape[0]
    assert width // 2 < CONV_HALO and lc % CONV_HALO == 0 and lp % lc == 0 and ls % lc == 0
    hb = lc // CONV_HALO
    n_halo_blocks = t // CONV_HALO
    kern = functools.partial(_conv_kernel, chunks_p=lp // lc, chunks_s=ls // lc, n_chunks_p=tp // lc,
                             width=width, lc=lc, c=c)
    vec = lambda a: a.reshape(1, c)
    return pl.pallas_call(
        kern,
        grid=(t // lc,),
        in_specs=[pl.BlockSpec((CONV_HALO, c), lambda i: (jnp.maximum(i * hb - 1, 0), 0)),
                  pl.BlockSpec((lc, c), lambda i: (i, 0)),
                  pl.BlockSpec((CONV_HALO, c), lambda i: (jnp.minimum((i + 1) * hb, n_halo_blocks - 1), 0)),
                  pl.BlockSpec((width, c), lambda i: (0, 0)),
                  pl.BlockSpec((1, c), lambda i: (0, 0)),
                  pl.BlockSpec((1, c), lambda i: (0, 0)),
                  pl.BlockSpec((1, c), lambda i: (0, 0))],
        out_specs=pl.BlockSpec((lc, c), lambda i: (i, 0)),
        out_shape=jax.ShapeDtypeStruct((t, c), BF16),
        scratch_shapes=[pltpu.VMEM((lc + 2 * CONV_HALO, c), F32), pltpu.VMEM((lc, c), F32)],
        compiler_params=_cparams("arbitrary"),
        name="conv_module",
    )(ug, ug, ug, w_dw, vec(b_dw), vec(ln_g), vec(ln_b))


def _s5_prep_kernel(lre_ref, lim_ref, ldt_ref, bre_ref, bim_ref, lbre_ref, lbim_ref, bbre_ref, bbim_ref):
    lam_re, lam_im = lre_ref[...], lim_ref[...]
    dt = jnp.exp(ldt_ref[...])
    mag = jnp.exp(lam_re * dt)
    lb_re, lb_im = mag * jnp.cos(lam_im * dt), mag * jnp.sin(lam_im * dt)
    den = jnp.square(lam_re) + jnp.square(lam_im)
    coef_re = ((lb_re - 1) * lam_re + lb_im * lam_im) / den
    coef_im = (lb_im * lam_re - (lb_re - 1) * lam_im) / den
    b_re, b_im = bre_ref[...], bim_ref[...]
    lbre_ref[...] = lb_re
    lbim_ref[...] = lb_im
    bbre_ref[...] = coef_re * b_re - coef_im * b_im
    bbim_ref[...] = coef_re * b_im + coef_im * b_re


def _s5_prep(lam_re, lam_im, log_dt, b_re, b_im):
    shp = jax.ShapeDtypeStruct(b_re.shape, F32)
    return pl.pallas_call(_s5_prep_kernel, out_shape=[shp] * 4, name="s5_prep",
                          compiler_params=pltpu.CompilerParams(vmem_limit_bytes=V7X_VMEM_LIMIT),
                          )(lam_re, lam_im, log_dt, b_re, b_im)


S5_SLAB = 256


def _s5_kernel(u_ref, h0re_ref, h0im_ref, wb_ref, wc_ref, lb_ref, y_ref, hfre_ref, hfim_ref,
               hre_scr, him_scr, cre_scr, cim_scr, *, s, tc, n_t, lw, sn):
    d = pl.program_id(0)
    t = pl.program_id(2)

    @pl.when(t == 0)
    def _():
        cre_scr[...] = h0re_ref[0]
        cim_scr[...] = h0im_ref[0]

    ub = u_ref[...].astype(BF16)
    hre_scr[...] = jnp.dot(ub, wb_ref[0, 0, :, 0:sn], preferred_element_type=F32)
    him_scr[...] = jnp.dot(ub, wb_ref[0, 0, :, sn:2 * sn], preferred_element_type=F32)

    for lc in range(sn // lw):
        ls = slice(lc * lw, (lc + 1) * lw)
        a_re = jnp.broadcast_to(lb_ref[0, 0, 0:1, ls], (s, lw))
        a_im = jnp.broadcast_to(lb_ref[0, 0, 1:2, ls], (s, lw))

        def step(i, carry):
            h_re, h_im = carry
            row = pl.multiple_of(jnp.where(d == 0, i, tc - 1 - i) * s, 8)
            n_re = a_re * h_re - a_im * h_im + hre_scr[pl.ds(row, s), ls]
            n_im = a_re * h_im + a_im * h_re + him_scr[pl.ds(row, s), ls]
            hre_scr[pl.ds(row, s), ls] = n_re
            him_scr[pl.ds(row, s), ls] = n_im
            return n_re, n_im

        f_re, f_im = lax.fori_loop(0, tc, step, (cre_scr[:, ls], cim_scr[:, ls]))
        cre_scr[:, ls] = f_re
        cim_scr[:, ls] = f_im

    y_ref[0] = (jnp.dot(hre_scr[...].astype(BF16), wc_ref[0, 0, 0:sn, :], preferred_element_type=F32)
                + jnp.dot(him_scr[...].astype(BF16), wc_ref[0, 0, sn:2 * sn, :], preferred_element_type=F32))

    @pl.when(t == n_t - 1)
    def _():
        hfre_ref[0] = cre_scr[...]
        hfim_ref[0] = cim_scr[...]


def _s5_scan(u_tm, h0_re, h0_im, wb, wc, lb, s, tc):
    rows, c = u_tm.shape
    n_slab = c // S5_SLAB
    sn = wb.shape[-1] // 2
    n_t = rows // (tc * s)
    lw = max(LANES, (1024 * 8 // s) // LANES * LANES)
    lw = min(lw, sn)
    assert sn % lw == 0 and s % 8 == 0
    tblk = lambda d, t: jnp.where(d == 0, t, n_t - 1 - t)
    kern = functools.partial(_s5_kernel, s=s, tc=tc, n_t=n_t, lw=lw, sn=sn)
    gn = h0_re.shape[-1]
    return pl.pallas_call(
        kern,
        grid=(2, n_slab, n_t),
        in_specs=[pl.BlockSpec((tc * s, S5_SLAB), lambda d, j, t: (tblk(d, t), j)),
                  pl.BlockSpec((1, s, sn), lambda d, j, t: (d, 0, j)),
                  pl.BlockSpec((1, s, sn), lambda d, j, t: (d, 0, j)),
                  pl.BlockSpec((1, 1, S5_SLAB, 2 * sn), lambda d, j, t: (d, j, 0, 0)),
                  pl.BlockSpec((1, 1, 2 * sn, S5_SLAB), lambda d, j, t: (d, j, 0, 0)),
                  pl.BlockSpec((1, 1, 2, sn), lambda d, j, t: (d, j, 0, 0))],
        out_specs=[pl.BlockSpec((1, tc * s, S5_SLAB), lambda d, j, t: (d, tblk(d, t), j)),
                   pl.BlockSpec((1, s, sn), lambda d, j, t: (d, 0, j)),
                   pl.BlockSpec((1, s, sn), lambda d, j, t: (d, 0, j))],
        out_shape=[jax.ShapeDtypeStruct((2, rows, c), F32),
                   jax.ShapeDtypeStruct((2, s, gn), F32),
                   jax.ShapeDtypeStruct((2, s, gn), F32)],
        scratch_shapes=[pltpu.VMEM((tc * s, sn), F32), pltpu.VMEM((tc * s, sn), F32),
                        pltpu.VMEM((s, sn), F32), pltpu.VMEM((s, sn), F32)],
        compiler_params=_cparams("arbitrary", "arbitrary", "arbitrary"),
        name="s5_scan",
    )(u_tm, h0_re, h0_im, wb, wc, lb)


def _s5_matrices(lb_re, lb_im, bb_re, bb_im, c_re, c_im, p):
    _, g, n = lb_re.shape
    gs = S5_SLAB // p
    n_slab = g // gs
    eye = jnp.eye(gs, dtype=F32)

    def in_mat(bb):
        b5 = bb.reshape(2, n_slab, gs, n, p)
        return jnp.einsum("dsgnp,gh->dsgphn", b5, eye).reshape(2, n_slab, gs * p, gs * n)

    def out_mat(cc):
        c5 = cc.reshape(2, n_slab, gs, p, n)
        return jnp.einsum("dsgpn,gh->dsgnhp", c5, eye).reshape(2, n_slab, gs * n, gs * p)

    wb = jnp.concatenate([in_mat(bb_re), in_mat(bb_im)], axis=-1).astype(BF16)
    wc = jnp.concatenate([out_mat(c_re), out_mat(-c_im)], axis=-2).astype(BF16)
    lb = jnp.stack([lb_re.reshape(2, n_slab, gs * n), lb_im.reshape(2, n_slab, gs * n)], axis=2)
    return wb, wc, lb


def _post_residual(x, y, gate, g, b, alpha):
    return _layer_norm(alpha * x + gate * y, g, b)


def _outab_kernel(x_ref, mod_ref, yc_ref, yf_ref, yr_ref, u_ref, dsk_ref, wglu_ref, wout_ref, g_ref, b_ref, o_ref,
                  *, tiles_p, tiles_per_s, d, c, alpha):
    grp = _group_of_tile(pl.program_id(0), tiles_p, tiles_per_s)
    y_s = (yf_ref[...] + yr_ref[...]) + dsk_ref[...] * u_ref[...]
    y_s = jax.nn.gelu(y_s)
    z = jnp.dot(y_s.astype(BF16), wglu_ref[...], preferred_element_type=F32)
    y_ssm = y_s * jax.nn.sigmoid(z)
    out = (jnp.dot(yc_ref[...], wout_ref[0:c, :], preferred_element_type=F32)
           + jnp.dot(y_ssm.astype(BF16), wout_ref[c:2 * c, :], preferred_element_type=F32))
    gate = _mod_chunk(mod_ref, grp, 2, d)
    o_ref[...] = _post_residual(x_ref[...], out, gate, g_ref[...], b_ref[...], alpha)


def _outab(x, mod_l, y_conv, y_f, y_r, u, d_skip, w_glu, w_out, ln_g, ln_b, tm, tiles_p, tiles_per_s, alpha):
    t, d = x.shape
    c = u.shape[1]
    kern = functools.partial(_outab_kernel, tiles_p=tiles_p, tiles_per_s=tiles_per_s, d=d, c=c, alpha=alpha)
    row = lambda w: pl.BlockSpec((tm, w), lambda i: (i, 0))
    full = lambda a: pl.BlockSpec(a.shape, lambda i: (0,) * a.ndim)
    d_skip, ln_g, ln_b = d_skip.reshape(1, c), ln_g.reshape(1, d), ln_b.reshape(1, d)
    return pl.pallas_call(
        kern,
        grid=(t // tm,),
        in_specs=[row(d), full(mod_l), row(c), row(c), row(c), row(c), full(d_skip), full(w_glu), full(w_out),
                  full(ln_g), full(ln_b)],
        out_specs=row(d),
        out_shape=jax.ShapeDtypeStruct((t, d), F32),
        compiler_params=_cparams("arbitrary"),
        name="outproj_ab",
    )(x, mod_l, y_conv, y_f, y_r, u, d_skip, w_glu, w_out, ln_g, ln_b)


def _mlp_kernel(x_ref, mod_ref, w1_ref, w2_ref, g_ref, b_ref, o_ref, h_scr, acc_scr,
                *, tiles_p, tiles_per_s, d, n_f, alpha):
    f = pl.program_id(1)
    grp = _group_of_tile(pl.program_id(0), tiles_p, tiles_per_s)

    @pl.when(f == 0)
    def _():
        shift = _mod_chunk(mod_ref, grp, 3, d)
        scale = _mod_chunk(mod_ref, grp, 4, d)
        h_scr[...] = (x_ref[...] * (1 + scale) + shift).astype(BF16)
        acc_scr[...] = jnp.zeros_like(acc_scr)

    a = jnp.dot(h_scr[...], w1_ref[...], preferred_element_type=F32)
    a = jnp.square(jnp.maximum(a, 0.0)).astype(BF16)
    acc_scr[...] += jnp.dot(a, w2_ref[...], preferred_element_type=F32)

    @pl.when(f == n_f - 1)
    def _():
        gate = _mod_chunk(mod_ref, grp, 5, d)
        o_ref[...] = _post_residual(x_ref[...], acc_scr[...], gate, g_ref[...], b_ref[...], alpha)


def _mlp(x, mod_l, w1, w2, ln_g, ln_b, tm, tf, tiles_p, tiles_per_s, alpha):
    t, d = x.shape
    n_f = w1.shape[1] // tf
    kern = functools.partial(_mlp_kernel, tiles_p=tiles_p, tiles_per_s=tiles_per_s, d=d, n_f=n_f, alpha=alpha)
    ln_g, ln_b = ln_g.reshape(1, d), ln_b.reshape(1, d)
    return pl.pallas_call(
        kern,
        grid=(t // tm, n_f),
        in_specs=[pl.BlockSpec((tm, d), lambda i, f: (i, 0)),
                  pl.BlockSpec(mod_l.shape, lambda i, f: (0, 0)),
                  pl.BlockSpec((d, tf), lambda i, f: (0, f)),
                  pl.BlockSpec((tf, d), lambda i, f: (f, 0)),
                  pl.BlockSpec((1, d), lambda i, f: (0, 0)),
                  pl.BlockSpec((1, d), lambda i, f: (0, 0))],
        out_specs=pl.BlockSpec((tm, d), lambda i, f: (i, 0)),
        out_shape=jax.ShapeDtypeStruct((t, d), F32),
        scratch_shapes=[pltpu.VMEM((tm, d), BF16), pltpu.VMEM((tm, d), F32)],
        compiler_params=_cparams("arbitrary", "arbitrary"),
        name="mlp",
    )(x, mod_l, w1, w2, ln_g, ln_b)


def _rope_tables(n_pos, dk):
    ax = dk // 2
    half = ax // 2
    freqs = ROPE_BASE ** (-jnp.arange(half, dtype=F32) / half)
    pos = jnp.arange(n_pos)
    row = (pos // LATENT_GRID_W).astype(F32)
    col = (pos % LATENT_GRID_W).astype(F32)
    ang_r, ang_c = row[:, None] * freqs, col[:, None] * freqs
    cos = jnp.concatenate([jnp.cos(ang_r)] * 2 + [jnp.cos(ang_c)] * 2, axis=-1)
    sin = jnp.concatenate([-jnp.sin(ang_r), jnp.sin(ang_r), -jnp.sin(ang_c), jnp.sin(ang_c)], axis=-1)
    rep = LANES // dk
    return jnp.tile(cos, (1, rep)), jnp.tile(sin, (1, rep))


def _qkv_kernel(x_ref, mod_ref, w_ref, cos_ref, sin_ref, o_ref, *, tiles_p, tiles_per_s, d, quarter):
    n = pl.program_id(0)
    i = pl.program_id(1)
    g = _group_of_tile(i, tiles_p, tiles_per_s)
    shift = _mod_chunk(mod_ref, g, 0, d)
    scale = _mod_chunk(mod_ref, g, 1, d)
    h = (x_ref[...] * (1 + scale) + shift).astype(BF16)
    y = jnp.dot(h, w_ref[...], preferred_element_type=F32)
    rotate = jnp.logical_and(n < 2, i >= tiles_p)

    @pl.when(rotate)
    def _():
        cos, sin = cos_ref[...], sin_ref[...]
        lane = lax.broadcasted_iota(jnp.int32, cos.shape, 1)
        first = (lane % (2 * quarter)) < quarter
        for cb in range(y.shape[1] // LANES):
            yb = y[:, cb * LANES:(cb + 1) * LANES]
            partner = jnp.where(first, pltpu.roll(yb, LANES - quarter, 1), pltpu.roll(yb, quarter, 1))
            o_ref[0, :, cb * LANES:(cb + 1) * LANES] = yb * cos + partner * sin

    @pl.when(jnp.logical_not(rotate))
    def _():
        o_ref[0] = y


def _qkv(x, mod_l, w_qkv, cos_t, sin_t, tm, tiles_p, tiles_per_s, dk):
    t, d = x.shape
    n_out = w_qkv.shape[1]
    assert n_out % 3 == 0
    tn = n_out // 3
    kern = functools.partial(_qkv_kernel, tiles_p=tiles_p, tiles_per_s=tiles_per_s, d=d, quarter=dk // 4)
    pos_blk = lambda n, i: (jnp.maximum(i - tiles_p, 0) % tiles_per_s, 0)
    return pl.pallas_call(
        kern,
        grid=(3, t // tm),
        in_specs=[pl.BlockSpec((tm, d), lambda n, i: (i, 0)),
                  pl.BlockSpec(mod_l.shape, lambda n, i: (0, 0)),
                  pl.BlockSpec((d, tn), lambda n, i: (0, n)),
                  pl.BlockSpec((tm, LANES), pos_blk),
                  pl.BlockSpec((tm, LANES), pos_blk)],
        out_specs=pl.BlockSpec((1, tm, tn), lambda n, i: (n, i, 0)),
        out_shape=jax.ShapeDtypeStruct((3, t, tn), F32),
        compiler_params=_cparams("arbitrary", "arbitrary"),
        name="qkv_proj",
    )(x, mod_l, w_qkv, cos_t, sin_t)


def _attn_kernel(*refs, hp, dk, dv, scale, fold_scale, lam_init, has_cache):
    if has_cache:
        q1_ref, q2_ref, k1_ref, k2_ref, v_ref, ck_ref, cv_ref, lamv_ref, sg_ref, o_ref = refs
    else:
        q1_ref, q2_ref, k1_ref, k2_ref, v_ref, lamv_ref, sg_ref, o_ref = refs
        ck_ref = cv_ref = None
    lv = lamv_ref[...]
    lam = (jnp.exp(jnp.sum(lv[0:1] * lv[1:2], axis=-1, keepdims=True))
           - jnp.exp(jnp.sum(lv[2:3] * lv[3:4], axis=-1, keepdims=True)) + lam_init)
    nt = (((1,), (1,)), ((), ()))
    per_blk = LANES // dk

    def probs(q_ref, k_ref, m, head):
        cs = slice(head * dk, (head + 1) * dk)
        q = q_ref[:, cs]
        q = (q * scale).astype(BF16) if fold_scale else q.astype(BF16)
        s = lax.dot_general(q, k_ref[:, cs].astype(BF16), nt, preferred_element_type=F32)
        if not fold_scale:
            s = s * scale
        mx = jnp.max(s, axis=-1, keepdims=True)
        if has_cache:
            sc = lax.dot_general(q, ck_ref[0, 0, m, head].astype(BF16), nt, preferred_element_type=F32)
            if not fold_scale:
                sc = sc * scale
            mx = jnp.maximum(mx, jnp.max(sc, axis=-1, keepdims=True))
            ec = jnp.exp(sc - mx)
        e = jnp.exp(s - mx)
        den = jnp.sum(e, axis=-1, keepdims=True)
        if has_cache:
            den = den + jnp.sum(ec, axis=-1, keepdims=True)
            return e, ec, den
        return e, None, den

    for head in range(hp * per_blk):
        e1, e1c, d1 = probs(q1_ref, k1_ref, 0, head)
        e2, e2c, d2 = probs(q2_ref, k2_ref, 1, head)
        r1 = 1.0 / d1
        r2 = lam / d2
        vs = slice(head * dv, (head + 1) * dv)
        w = (e1 * r1 - e2 * r2).astype(BF16)
        o = jnp.dot(w, v_ref[:, vs].astype(BF16), preferred_element_type=F32)
        if has_cache:
            wc = (e1c * r1 - e2c * r2).astype(BF16)
            o = o + jnp.dot(wc, cv_ref[0, 0, head].astype(BF16), preferred_element_type=F32)
        o = o * lax.rsqrt(jnp.mean(jnp.square(o), axis=-1, keepdims=True) + LN_EPS)
        o = o * sg_ref[...] * (1.0 - lam_init)
        o_ref[:, vs] = o.astype(o_ref.dtype)


def _attention(qkv, lamv, subln_g, *, row0, n_seq, lq, tq, hp, n_heads, dk, dv, lam_init, cache=None):
    per_blk = LANES // dk
    heads_step = hp * per_blk
    n_hblk = n_heads // heads_step
    map2 = n_heads * dk // (hp * LANES)
    scale = dk ** -0.5
    fold_scale = math.frexp(scale)[0] == 0.5
    qb0, kb0 = row0 // tq, row0 // lq
    n_q = lq // tq
    q_spec = lambda off: pl.BlockSpec((None, tq, hp * LANES), lambda b, h, qi: (0, qb0 + b * n_q + qi, off + h))
    k_spec = lambda off: pl.BlockSpec((None, lq, hp * LANES), lambda b, h, qi: (1, kb0 + b, off + h))
    in_specs = [q_spec(0), q_spec(map2), k_spec(0), k_spec(map2),
                pl.BlockSpec((None, lq, heads_step * dv), lambda b, h, qi: (2, kb0 + b, h))]
    args = [qkv, qkv, qkv, qkv, qkv]
    if cache is not None:
        cache_k, cache_v, o_i = cache
        past = cache_k.shape[-2]
        in_specs += [pl.BlockSpec((1, 1, 2, heads_step, past, dk), lambda b, h, qi: (b, o_i, 0, h, 0, 0)),
                     pl.BlockSpec((1, 1, heads_step, past, dv), lambda b, h, qi: (b, o_i, h, 0, 0))]
        args += [cache_k, cache_v]
    in_specs += [pl.BlockSpec(lamv.shape, lambda b, h, qi: (0, 0)),
                 pl.BlockSpec((1, dv), lambda b, h, qi: (0, 0))]
    args += [lamv, subln_g.reshape(1, dv)]
    kern = functools.partial(_attn_kernel, hp=hp, dk=dk, dv=dv, scale=scale, fold_scale=fold_scale,
                             lam_init=lam_init, has_cache=cache is not None)
    return pl.pallas_call(
        kern,
        grid=(n_seq, n_hblk, n_q),
        in_specs=in_specs,
        out_specs=pl.BlockSpec((tq, heads_step * dv), lambda b, h, qi: (b * n_q + qi, h)),
        out_shape=jax.ShapeDtypeStruct((n_seq * lq, n_heads * dv), BF16),
        compiler_params=_cparams("arbitrary", "arbitrary", "arbitrary"),
        name="diff_attn_cache" if cache is not None else "diff_attn",
    )(*args)


def _outc_kernel(x_ref, mod_ref, o_ref_in, w_ref, g_ref, b_ref, o_ref, *, tiles_p, tiles_per_s, d, alpha):
    grp = _group_of_tile(pl.program_id(0), tiles_p, tiles_per_s)
    out = jnp.dot(o_ref_in[...], w_ref[...], preferred_element_type=F32)
    gate = _mod_chunk(mod_ref, grp, 2, d)
    o_ref[...] = _post_residual(x_ref[...], out, gate, g_ref[...], b_ref[...], alpha)


def _outc(x, mod_l, o_n, w_out, ln_g, ln_b, tm, tiles_p, tiles_per_s, alpha):
    t, d = x.shape
    kin = o_n.shape[1]
    kern = functools.partial(_outc_kernel, tiles_p=tiles_p, tiles_per_s=tiles_per_s, d=d, alpha=alpha)
    ln_g, ln_b = ln_g.reshape(1, d), ln_b.reshape(1, d)
    full = lambda a: pl.BlockSpec(a.shape, lambda i: (0,) * a.ndim)
    return pl.pallas_call(
        kern,
        grid=(t // tm,),
        in_specs=[pl.BlockSpec((tm, d), lambda i: (i, 0)), full(mod_l), pl.BlockSpec((tm, kin), lambda i: (i, 0)),
                  full(w_out), full(ln_g), full(ln_b)],
        out_specs=pl.BlockSpec((tm, d), lambda i: (i, 0)),
        out_shape=jax.ShapeDtypeStruct((t, d), F32),
        compiler_params=_cparams("arbitrary"),
        name="outproj_c",
    )(x, mod_l, o_n, w_out, ln_g, ln_b)


def _time_major(a, n_seq, length, pad_to):
    c = a.shape[-1]
    a = a.reshape(n_seq, length, c).transpose(1, 0, 2)
    if pad_to > n_seq:
        a = jnp.pad(a, ((0, 0), (0, pad_to - n_seq), (0, 0)))
    return a.reshape(length * pad_to, c)


def _seq_major(a, n_seq, length, pad_to):
    c = a.shape[-1]
    a = a.reshape(2, length, pad_to, c)[:, :, :n_seq].transpose(0, 2, 1, 3)
    return a.reshape(2, n_seq * length, c)


def kernel(x_prompt, x_sample, state_s5_re, state_s5_im, cache_k, cache_v, c, c_ctx, w_mod, b_mod, ln_g, ln_b, w_in_ab, w_dw, b_dw, conv_ln_g, conv_ln_b, s5_lambda_re, s5_lambda_im, s5_log_dt, s5_b_re, s5_b_im, s5_c_re, s5_c_im, s5_d, w_glu, w_out_ab, w_qkv, lam_q1, lam_k1, lam_q2, lam_k2, subln_g, w_out_c, w_ff1, w_ff2):
    bp, lp, d = x_prompt.shape
    bs, ls, _ = x_sample.shape
    depth = w_mod.shape[0]
    tp, ts = bp * lp, bs * ls
    alpha = (2 * depth) ** 0.25
    assert 1 + bs <= MOD_ROWS

    tm = 256
    tm_mlp, tf = 512, 512
    assert tp % tm_mlp == 0 and ls % tm_mlp == 0 and tp % ls == 0

    x = jnp.concatenate([x_prompt.reshape(tp, d), x_sample.reshape(ts, d)], axis=0)
    cvec = jnp.zeros((MOD_ROWS, d), F32).at[0].set(c_ctx).at[1:1 + bs].set(c)
    mod = _modvec(cvec, w_mod, b_mod)

    g_ssm, n_ssm, p_ssm = s5_b_re.shape[2:]
    dk = lam_q1.shape[-1]
    dv = subln_g.shape[-1]
    n_heads = w_out_c.shape[1] // dv
    s_pad = 8
    assert bs <= s_pad and bp % 8 == 0

    s_re, s_im, k_list, v_list = [], [], [], []
    for l in range(depth):
        mod_l = mod[l]
        if l % 2 == 0:
            e = l // 2
            ug, u = _inproj(x, mod_l, w_in_ab[e].astype(BF16), tm, tp // tm, ls // tm)
            y_conv = _conv_module(ug, w_dw[e], b_dw[e], conv_ln_g[e], conv_ln_b[e], min(lp, 256), lp, ls, tp)
            rep = lambda a: jnp.repeat(a.reshape(2 * g_ssm, -1), p_ssm, axis=-1)
            flat = lambda a: a.reshape(2 * g_ssm, n_ssm * p_ssm)
            ldt = jnp.broadcast_to(s5_log_dt[e].reshape(2 * g_ssm, 1), (2 * g_ssm, n_ssm * p_ssm))
            lb_re, lb_im, bb_re, bb_im = _s5_prep(rep(s5_lambda_re[e]), rep(s5_lambda_im[e]), ldt,
                                                  flat(s5_b_re[e]), flat(s5_b_im[e]))
            unrep = lambda a: a.reshape(2, g_ssm, n_ssm, p_ssm)[..., 0]
            full = lambda a: a.reshape(2, g_ssm, n_ssm, p_ssm)
            wb, wc, lb = _s5_matrices(unrep(lb_re), unrep(lb_im), full(bb_re), full(bb_im),
                                      s5_c_re[e], s5_c_im[e], p_ssm)
            gn = g_ssm * n_ssm
            zero_p = jnp.zeros((2, bp, gn), F32)
            yp_dir, hf_re, hf_im = _s5_scan(_time_major(u[:tp], bp, lp, bp), zero_p, zero_p, wb, wc, lb, bp, 32)
            h0 = lambda st: jnp.pad(st[:, e].reshape(bs, 2, gn).transpose(1, 0, 2), ((0, 0), (0, s_pad - bs), (0, 0)))
            ys_dir, _, _ = _s5_scan(_time_major(u[tp:], bs, ls, s_pad), h0(state_s5_re), h0(state_s5_im),
                                    wb, wc, lb, s_pad, 128)
            y_dir = jnp.concatenate([_seq_major(yp_dir, bp, lp, bp), _seq_major(ys_dir, bs, ls, s_pad)], axis=1)
            s_re.append(hf_re.reshape(2, bp, g_ssm, n_ssm).transpose(1, 0, 2, 3))
            s_im.append(hf_im.reshape(2, bp, g_ssm, n_ssm).transpose(1, 0, 2, 3))
            x = _outab(x, mod_l, y_conv, y_dir[0], y_dir[1], u, s5_d[e], w_glu[e].astype(BF16),
                       w_out_ab[e].astype(BF16), ln_g[l, 0], ln_b[l, 0], tm, tp // tm, ls // tm, alpha)
        else:
            o_i = l // 2
            lam_init = 0.8 - 0.6 * math.exp(-0.3 * l)
            cos_t, sin_t = _rope_tables(ls, dk)
            qkv = _qkv(x, mod_l, w_qkv[o_i].astype(BF16), cos_t, sin_t, tm, tp // tm, ls // tm, dk)
            lamv = jnp.stack([lam_q1[o_i], lam_k1[o_i], lam_q2[o_i], lam_k2[o_i]])
            geo = dict(n_heads=n_heads, dk=dk, dv=dv, lam_init=lam_init)
            o_p = _attention(qkv, lamv, subln_g[o_i], row0=0, n_seq=bp, lq=lp, tq=lp,
                             hp=n_heads * dk // LANES, **geo)
            o_s = _attention(qkv, lamv, subln_g[o_i], row0=tp, n_seq=bs, lq=ls, tq=256, hp=1,
                             cache=(cache_k, cache_v, o_i), **geo)
            o_n = jnp.concatenate([o_p, o_s], axis=0)
            kq = qkv[1, :tp].reshape(bp, lp, 2, n_heads, dk)
            k_list.append(kq.transpose(0, 2, 3, 1, 4))
            v_list.append(qkv[2, :tp].reshape(bp, lp, n_heads, dv).transpose(0, 2, 1, 3))
            x = _outc(x, mod_l, o_n, w_out_c[o_i].astype(BF16), ln_g[l, 0], ln_b[l, 0], tm, tp // tm, ls // tm, alpha)
        x = _mlp(x, mod_l, w_ff1[l].astype(BF16), w_ff2[l].astype(BF16), ln_g[l, 1], ln_b[l, 1],
                 tm_mlp, tf, tp // tm_mlp, ls // tm_mlp, alpha)

    return (x[:tp].reshape(bp, lp, d), x[tp:].reshape(bs, ls, d),
            jnp.stack(s_re, axis=1), jnp.stack(s_im, axis=1),
            jnp.stack(k_list, axis=1), jnp.stack(v_list, axis=1))
```

```python
import functools
import math

import jax
import jax.numpy as jnp
from jax import lax
from jax.experimental import pallas as pl
from jax.experimental.pallas import tpu as pltpu

F32 = jnp.float32
BF16 = jnp.bfloat16

LN_EPS = 1e-5
ROPE_BASE = 10000.0
LATENT_GRID_W = 64
MOD_ROWS = 8
V7X_VMEM_LIMIT = 56 * 1024 * 1024
LANES = 128


def _cparams(*sem):
    return pltpu.CompilerParams(dimension_semantics=sem, vmem_limit_bytes=V7X_VMEM_LIMIT)


def _layer_norm(z, g, b):
    mu = jnp.mean(z, axis=-1, keepdims=True)
    zc = z - mu
    var = jnp.mean(jnp.square(zc), axis=-1, keepdims=True)
    return zc * lax.rsqrt(var + LN_EPS) * g + b


def _group_of_tile(i, tiles_p, tiles_per_s):
    return jnp.where(i < tiles_p, 0, 1 + jnp.maximum(i - tiles_p, 0) // tiles_per_s)


def _mod_chunk(mod_ref, g, k, d):
    return mod_ref[pl.ds(g, 1), k * d:(k + 1) * d]


def _modvec_kernel(cv_ref, w_ref, b_ref, o_ref):
    cv = cv_ref[...]
    s = (cv * jax.nn.sigmoid(cv)).astype(BF16)
    o_ref[0] = jnp.dot(s, w_ref[0].astype(BF16), preferred_element_type=F32) + b_ref[0]


def _modvec(cvec, w_mod, b_mod, tn=1024):
    depth, d, n = w_mod.shape
    return pl.pallas_call(
        _modvec_kernel,
        grid=(depth, n // tn),
        in_specs=[pl.BlockSpec((MOD_ROWS, d), lambda l, j: (0, 0)),
                  pl.BlockSpec((1, d, tn), lambda l, j: (l, 0, j)),
                  pl.BlockSpec((1, 1, tn), lambda l, j: (l, 0, j))],
        out_specs=pl.BlockSpec((1, MOD_ROWS, tn), lambda l, j: (l, 0, j)),
        out_shape=jax.ShapeDtypeStruct((depth, MOD_ROWS, n), F32),
        compiler_params=_cparams("arbitrary", "arbitrary"),
        name="modvec",
    )(cvec, w_mod, b_mod.reshape(depth, 1, n))


def _inproj_kernel(x_ref, mod_ref, w_ref, ug_ref, u_ref, *, tiles_p, tiles_per_s, d, c):
    g = _group_of_tile(pl.program_id(0), tiles_p, tiles_per_s)
    shift = _mod_chunk(mod_ref, g, 0, d)
    scale = _mod_chunk(mod_ref, g, 1, d)
    h = (x_ref[...] * (1 + scale) + shift).astype(BF16)
    a_val = jnp.dot(h, w_ref[:, 0:c], preferred_element_type=F32)
    a_gate = jnp.dot(h, w_ref[:, c:2 * c], preferred_element_type=F32)
    ug_ref[...] = a_val * jax.nn.sigmoid(a_gate)
    u_ref[...] = jnp.dot(h, w_ref[:, 2 * c:3 * c], preferred_element_type=F32)


def _inproj(x, mod_l, w_in, tm, tiles_p, tiles_per_s):
    t, d = x.shape
    c = w_in.shape[1] // 3
    kern = functools.partial(_inproj_kernel, tiles_p=tiles_p, tiles_per_s=tiles_per_s, d=d, c=c)
    return pl.pallas_call(
        kern,
        grid=(t // tm,),
        in_specs=[pl.BlockSpec((tm, d), lambda i: (i, 0)),
                  pl.BlockSpec(mod_l.shape, lambda i: (0, 0)),
                  pl.BlockSpec(w_in.shape, lambda i: (0, 0))],
        out_specs=[pl.BlockSpec((tm, c), lambda i: (i, 0)),
                   pl.BlockSpec((tm, c), lambda i: (i, 0))],
        out_shape=[jax.ShapeDtypeStruct((t, c), F32), jax.ShapeDtypeStruct((t, c), F32)],
        compiler_params=_cparams("arbitrary"),
        name="inproj",
    )(x, mod_l, w_in)


CONV_HALO = 16
CONV_ROWS = 32
CONV_COLS = 256


def _conv_kernel(prev_ref, cur_ref, next_ref, w_ref, b_ref, g_ref, beta_ref, o_ref, pad_scr, conv_scr,
                 *, chunks_p, chunks_s, n_chunks_p, width, lc, c):
    i = pl.program_id(0)
    in_p = i < n_chunks_p
    k = jnp.where(in_p, i % chunks_p, jnp.maximum(i - n_chunks_p, 0) % chunks_s)
    last = jnp.where(in_p, chunks_p - 1, chunks_s - 1)
    has_prev = (k > 0).astype(F32)
    has_next = (k < last).astype(F32)
    pad_scr[0:CONV_HALO, :] = prev_ref[...] * has_prev
    pad_scr[CONV_HALO:CONV_HALO + lc, :] = cur_ref[...]
    pad_scr[CONV_HALO + lc:2 * CONV_HALO + lc, :] = next_ref[...] * has_next
    off = CONV_HALO - width // 2

    for r0 in range(0, lc, CONV_ROWS):
        for cb in range(c // CONV_COLS):
            cs = slice(cb * CONV_COLS, (cb + 1) * CONV_COLS)
            acc = jnp.zeros((CONV_ROWS, CONV_COLS), F32)
            for kk in range(width):
                acc = acc + pad_scr[r0 + kk + off:r0 + kk + off + CONV_ROWS, cs] * w_ref[kk:kk + 1, cs]
            conv_scr[r0:r0 + CONV_ROWS, cs] = acc + b_ref[:, cs]
    y = _layer_norm(conv_scr[...], g_ref[...], beta_ref[...])
    o_ref[...] = (y * jax.nn.sigmoid(y)).astype(o_ref.dtype)


def _conv_module(ug, w_dw, b_dw, ln_g, ln_b, lc, lp, ls, tp):
    t, c = ug.shape
    width = w_dw.shape[0]
    assert width // 2 < CONV_HALO and lc % CONV_HALO == 0 and lp % lc == 0 and ls % lc == 0
    hb = lc // CONV_HALO
    n_halo_blocks = t // CONV_HALO
    kern = functools.partial(_conv_kernel, chunks_p=lp // lc, chunks_s=ls // lc, n_chunks_p=tp // lc,
                             width=width, lc=lc, c=c)
    vec = lambda a: a.reshape(1, c)
    return pl.pallas_call(
        kern,
        grid=(t // lc,),
        in_specs=[pl.BlockSpec((CONV_HALO, c), lambda i: (jnp.maximum(i * hb - 1, 0), 0)),
                  pl.BlockSpec((lc, c), lambda i: (i, 0)),
                  pl.BlockSpec((CONV_HALO, c), lambda i: (jnp.minimum((i + 1) * hb, n_halo_blocks - 1), 0)),
                  pl.BlockSpec((width, c), lambda i: (0, 0)),
                  pl.BlockSpec((1, c), lambda i: (0, 0)),
                  pl.BlockSpec((1, c), lambda i: (0, 0)),
                  pl.BlockSpec((1, c), lambda i: (0, 0))],
        out_specs=pl.BlockSpec((lc, c), lambda i: (i, 0)),
        out_shape=jax.ShapeDtypeStruct((t, c), BF16),
        scratch_shapes=[pltpu.VMEM((lc + 2 * CONV_HALO, c), F32), pltpu.VMEM((lc, c), F32)],
        compiler_params=_cparams("arbitrary"),
        name="conv_module",
    )(ug, ug, ug, w_dw, vec(b_dw), vec(ln_g), vec(ln_b))


S5_T = 16


def _s5_prep_kernel(lxr_ref, lxi_ref, ldx_ref, lyr_ref, lyi_ref, ldy_ref, bre_ref, bim_ref, cre_ref, cim_ref,
                    vre_ref, vni_ref, wre_ref, wim_ref, are_ref, aim_ref):
    def lam_bar(lam_re, lam_im, log_dt):
        dt = jnp.exp(log_dt)
        mag = jnp.exp(lam_re * dt)
        return mag * jnp.cos(lam_im * dt), mag * jnp.sin(lam_im * dt)

    lam_re, lam_im = lyr_ref[...], lyi_ref[...]
    ar, ai = lam_bar(lam_re, lam_im, ldy_ref[...])
    den = jnp.square(lam_re) + jnp.square(lam_im)
    coef_re = ((ar - 1) * lam_re + ai * lam_im) / den
    coef_im = (ai * lam_re - (ar - 1) * lam_im) / den
    b_re, b_im = bre_ref[...], bim_ref[...]
    wr = coef_re * b_re - coef_im * b_im
    wi = coef_re * b_im + coef_im * b_re
    for k in range(S5_T):
        wre_ref[k] = wr
        wim_ref[k] = wi
        wr, wi = wr * ar - wi * ai, wr * ai + wi * ar

    ar, ai = lam_bar(lxr_ref[...], lxi_ref[...], ldx_ref[...])
    vr, vi = cre_ref[...], cim_ref[...]
    pr, pi = jnp.ones_like(ar), jnp.zeros_like(ar)
    for k in range(S5_T + 1):
        vre_ref[k] = vr
        vni_ref[k] = -vi
        if k < S5_T:
            vr, vi = vr * ar - vi * ai, vr * ai + vi * ar
            pr, pi = pr * ar - pi * ai, pr * ai + pi * ar
    are_ref[...] = pr
    aim_ref[...] = pi


def _s5_prep(lam_re, lam_im, log_dt, b_re, b_im, c_re, c_im):
    _, g, n, p = b_re.shape
    dg, w = 2 * g, n * p
    tile_x = lambda a: jnp.tile(a.reshape(dg, n), (1, p))
    rep_y = lambda a: jnp.repeat(a.reshape(dg, n), p, axis=-1)
    ldt = jnp.broadcast_to(log_dt.reshape(dg, 1), (dg, w))
    big = lambda k: jax.ShapeDtypeStruct((k, dg, w), F32)
    one = jax.ShapeDtypeStruct((dg, w), F32)
    return pl.pallas_call(
        _s5_prep_kernel, out_shape=[big(S5_T + 1), big(S5_T + 1), big(S5_T), big(S5_T), one, one], name="s5_prep",
        compiler_params=pltpu.CompilerParams(vmem_limit_bytes=V7X_VMEM_LIMIT),
    )(tile_x(lam_re), tile_x(lam_im), ldt, rep_y(lam_re), rep_y(lam_im), ldt,
      b_re.reshape(dg, w), b_im.reshape(dg, w), c_re.reshape(dg, w), c_im.reshape(dg, w))


S5_KMAT_GROUPS = 8


def _s5_kmat_kernel(lhs_ref, rhs_ref, o_ref):
    for i in range(S5_KMAT_GROUPS):
        o_ref[i] = jnp.dot(lhs_ref[i], rhs_ref[i], precision=lax.Precision.HIGHEST, preferred_element_type=F32)


def _s5_kmat(v_re, v_ni, w_re, w_im, n, p):
    dg = v_re.shape[1]
    rows = lambda v: v[:S5_T].reshape(S5_T, dg, p, n).transpose(1, 0, 2, 3).reshape(dg, S5_T * p, n)
    lhs = jnp.concatenate([rows(v_re), rows(v_ni)], axis=-1)
    rhs = jnp.concatenate([w_re[0].reshape(dg, n, p), w_im[0].reshape(dg, n, p)], axis=1)
    gb = S5_KMAT_GROUPS
    return pl.pallas_call(
        _s5_kmat_kernel,
        grid=(dg // gb,),
        in_specs=[pl.BlockSpec((gb, S5_T * p, 2 * n), lambda i: (i, 0, 0)),
                  pl.BlockSpec((gb, 2 * n, p), lambda i: (i, 0, 0))],
        out_specs=pl.BlockSpec((gb, S5_T * p, p), lambda i: (i, 0, 0)),
        out_shape=jax.ShapeDtypeStruct((dg, S5_T * p, p), F32),
        compiler_params=_cparams("arbitrary"),
        name="s5_kmat",
    )(lhs, rhs)


def _s5_weights(v_re, v_ni, w_re, w_im, kmat, a_re, a_im, g, n, p):
    t = S5_T
    k5 = kmat.reshape(2, g, t, p, p)
    jj, ii = jnp.meshgrid(jnp.arange(t), jnp.arange(t), indexing="ij")

    def toeplitz(kd, lag, valid):
        m = kd[:, jnp.clip(lag, 0, t - 1)] * valid.astype(F32)[None, :, :, None, None]
        return m.transpose(0, 1, 4, 2, 3).reshape(g, t * p, t * p)

    mcat = jnp.concatenate([toeplitz(k5[0], ii - jj, ii >= jj), toeplitz(k5[1], jj - ii, jj >= ii)], axis=-1)
    eye2 = jnp.eye(2, dtype=F32)

    def be(w, d, flip):
        a = w.reshape(t, 2, g, n, p)[:, d]
        a = a[::-1] if flip else a
        bx = a.transpose(1, 0, 3, 2).reshape(g // 2, 2, t * p, n)
        return jnp.einsum("hgrn,gk->hgrkn", bx, eye2).reshape(g // 2, 2 * t * p, 2 * n)

    be_pair = jnp.concatenate([be(w_re, 0, True), be(w_im, 0, True), be(w_re, 1, False), be(w_im, 1, False)], axis=-1)

    def cs(v, d, flip):
        a = v.reshape(t + 1, 2, g, p, n)[1:, d]
        a = a[::-1] if flip else a
        cx = a.transpose(1, 3, 0, 2).reshape(g // 2, 2, n, t * p)
        return jnp.einsum("hgnc,gk->hgnkc", cx, eye2).reshape(g // 2, 2 * n, 2 * t * p)

    cs_pair = jnp.concatenate([cs(v_re, 0, False), cs(v_ni, 0, False), cs(v_re, 1, True), cs(v_ni, 1, True)], axis=-2)
    nat = lambda a: a[:, :n].reshape(2, g * n)
    a4 = jnp.stack([nat(a_re)[0], nat(a_im)[0], nat(a_re)[1], nat(a_im)[1]])
    return mcat.astype(BF16), be_pair.astype(BF16), cs_pair.astype(BF16), a4


def _s5_chunk_kernel(u_ref, m_ref, be_ref, cs_ref, a_ref, h0_ref, y_ref, hf_ref, e_scr, *, gb, nseq, nc, pairs_per_loop):
    gp = gb // 2
    ew = e_scr.shape[1] // gp
    pw = ew // 4
    yw = y_ref.shape[-1]
    for pr in range(gp):
        u2 = jnp.concatenate([u_ref[2 * pr], u_ref[2 * pr + 1]], axis=-1)
        e_scr[:, pr * ew:(pr + 1) * ew] = jnp.dot(u2, be_ref[pr], preferred_element_type=F32)

    for p0 in range(0, gp, pairs_per_loop):
        prs = list(range(p0, min(p0 + pairs_per_loop, gp)))
        coef = [[jnp.broadcast_to(a_ref[r:r + 1, pr * pw:(pr + 1) * pw], (nseq, pw)) for r in range(4)] for pr in prs]
        init = tuple(tuple(h0_ref[r, :, pr * pw:(pr + 1) * pw] for r in range(4)) for pr in prs)

        def step(c, carry):
            row_f = pl.multiple_of(c * nseq, 8)
            row_r = pl.multiple_of((nc - 1 - c) * nseq, 8)
            out = []
            for idx, pr in enumerate(prs):
                sf_re, sf_im, sr_re, sr_im = carry[idx]
                af_re, af_im, ar_re, ar_im = coef[idx]
                col = lambda r: slice(pr * ew + r * pw, pr * ew + (r + 1) * pw)
                ef_re = e_scr[pl.ds(row_f, nseq), col(0)]
                ef_im = e_scr[pl.ds(row_f, nseq), col(1)]
                er_re = e_scr[pl.ds(row_r, nseq), col(2)]
                er_im = e_scr[pl.ds(row_r, nseq), col(3)]
                e_scr[pl.ds(row_f, nseq), col(0)] = sf_re
                e_scr[pl.ds(row_f, nseq), col(1)] = sf_im
                e_scr[pl.ds(row_r, nseq), col(2)] = sr_re
                e_scr[pl.ds(row_r, nseq), col(3)] = sr_im
                out.append((af_re * sf_re - af_im * sf_im + ef_re, af_re * sf_im + af_im * sf_re + ef_im,
                            ar_re * sr_re - ar_im * sr_im + er_re, ar_re * sr_im + ar_im * sr_re + er_im))
            return tuple(out)

        fin = lax.fori_loop(0, nc, step, init)
        for idx, pr in enumerate(prs):
            for r in range(4):
                hf_ref[r, :, pr * pw:(pr + 1) * pw] = fin[idx][r]

    for pr in range(gp):
        y2 = jnp.dot(e_scr[:, pr * ew:(pr + 1) * ew].astype(BF16), cs_ref[pr], preferred_element_type=F32)
        for sub in range(2):
            grp = 2 * pr + sub
            yi = jnp.dot(u_ref[grp], m_ref[grp], preferred_element_type=F32)
            y_ref[grp] = yi[:, 0:yw] + yi[:, yw:2 * yw] + y2[:, sub * yw:(sub + 1) * yw]


def _s5_chunked(u_rows, h0, mcat, be_pair, cs_pair, a4, nseq, gb):
    g, rows, w = u_rows.shape
    nc = rows // nseq
    gn = h0.shape[-1]
    sn = gn // g * gb
    pairs_per_loop = max(1, 32 // nseq)
    kern = functools.partial(_s5_chunk_kernel, gb=gb, nseq=nseq, nc=nc, pairs_per_loop=pairs_per_loop)
    gp = gb // 2
    return pl.pallas_call(
        kern,
        grid=(g // gb,),
        in_specs=[pl.BlockSpec((gb, rows, w), lambda i: (i, 0, 0)),
                  pl.BlockSpec((gb, w, 2 * w), lambda i: (i, 0, 0)),
                  pl.BlockSpec((gp,) + be_pair.shape[1:], lambda i: (i, 0, 0)),
                  pl.BlockSpec((gp,) + cs_pair.shape[1:], lambda i: (i, 0, 0)),
                  pl.BlockSpec((4, sn), lambda i: (0, i)),
                  pl.BlockSpec((4, nseq, sn), lambda i: (0, 0, i))],
        out_specs=[pl.BlockSpec((gb, rows, w), lambda i: (i, 0, 0)),
                   pl.BlockSpec((4, nseq, sn), lambda i: (0, 0, i))],
        out_shape=[jax.ShapeDtypeStruct((g, rows, w), F32), jax.ShapeDtypeStruct((4, nseq, gn), F32)],
        scratch_shapes=[pltpu.VMEM((rows, gp * be_pair.shape[-1]), F32)],
        compiler_params=_cparams("arbitrary"),
        name="s5_chunked",
    )(u_rows, mcat, be_pair, cs_pair, a4, h0)


def _to_chunk_rows(u_part, nseq, length, pad, g, p):
    a = u_part.astype(BF16).reshape(nseq, length // S5_T, S5_T, g, p).transpose(3, 1, 0, 2, 4)
    if pad > nseq:
        a = jnp.pad(a, ((0, 0), (0, 0), (0, pad - nseq), (0, 0), (0, 0)))
    return a.reshape(g, (length // S5_T) * pad, S5_T * p)


def _from_chunk_rows(y, nseq, length, pad, g, p):
    a = y.reshape(g, length // S5_T, pad, S5_T, p)[:, :, :nseq].transpose(2, 1, 3, 0, 4)
    return a.reshape(nseq * length, g * p)


def _post_residual(x, y, gate, g, b, alpha):
    return _layer_norm(alpha * x + gate * y, g, b)


def _outab_kernel(x_ref, mod_ref, yc_ref, ys_ref, u_ref, dsk_ref, wglu_ref, wout_ref, g_ref, b_ref, o_ref,
                  *, tiles_p, tiles_per_s, d, c, alpha):
    grp = _group_of_tile(pl.program_id(0), tiles_p, tiles_per_s)
    y_s = ys_ref[...] + dsk_ref[...] * u_ref[...]
    y_s = jax.nn.gelu(y_s)
    z = jnp.dot(y_s.astype(BF16), wglu_ref[...], preferred_element_type=F32)
    y_ssm = y_s * jax.nn.sigmoid(z)
    out = (jnp.dot(yc_ref[...], wout_ref[0:c, :], preferred_element_type=F32)
           + jnp.dot(y_ssm.astype(BF16), wout_ref[c:2 * c, :], preferred_element_type=F32))
    gate = _mod_chunk(mod_ref, grp, 2, d)
    o_ref[...] = _post_residual(x_ref[...], out, gate, g_ref[...], b_ref[...], alpha)


def _outab(x, mod_l, y_conv, y_scan, u, d_skip, w_glu, w_out, ln_g, ln_b, tm, tiles_p, tiles_per_s, alpha):
    t, d = x.shape
    c = u.shape[1]
    kern = functools.partial(_outab_kernel, tiles_p=tiles_p, tiles_per_s=tiles_per_s, d=d, c=c, alpha=alpha)
    row = lambda w: pl.BlockSpec((tm, w), lambda i: (i, 0))
    full = lambda a: pl.BlockSpec(a.shape, lambda i: (0,) * a.ndim)
    d_skip, ln_g, ln_b = d_skip.reshape(1, c), ln_g.reshape(1, d), ln_b.reshape(1, d)
    return pl.pallas_call(
        kern,
        grid=(t // tm,),
        in_specs=[row(d), full(mod_l), row(c), row(c), row(c), full(d_skip), full(w_glu), full(w_out),
                  full(ln_g), full(ln_b)],
        out_specs=row(d),
        out_shape=jax.ShapeDtypeStruct((t, d), F32),
        compiler_params=_cparams("arbitrary"),
        name="outproj_ab",
    )(x, mod_l, y_conv, y_scan, u, d_skip, w_glu, w_out, ln_g, ln_b)


def _mlp_kernel(x_ref, mod_ref, w1_ref, w2_ref, g_ref, b_ref, o_ref, h_scr, acc_scr,
                *, tiles_p, tiles_per_s, d, n_f, alpha):
    f = pl.program_id(1)
    grp = _group_of_tile(pl.program_id(0), tiles_p, tiles_per_s)

    @pl.when(f == 0)
    def _():
        shift = _mod_chunk(mod_ref, grp, 3, d)
        scale = _mod_chunk(mod_ref, grp, 4, d)
        h_scr[...] = (x_ref[...] * (1 + scale) + shift).astype(BF16)
        acc_scr[...] = jnp.zeros_like(acc_scr)

    a = jnp.dot(h_scr[...], w1_ref[...], preferred_element_type=F32)
    a = jnp.square(jnp.maximum(a, 0.0)).astype(BF16)
    acc_scr[...] += jnp.dot(a, w2_ref[...], preferred_element_type=F32)

    @pl.when(f == n_f - 1)
    def _():
        gate = _mod_chunk(mod_ref, grp, 5, d)
        o_ref[...] = _post_residual(x_ref[...], acc_scr[...], gate, g_ref[...], b_ref[...], alpha)


def _mlp(x, mod_l, w1, w2, ln_g, ln_b, tm, tf, tiles_p, tiles_per_s, alpha):
    t, d = x.shape
    n_f = w1.shape[1] // tf
    kern = functools.partial(_mlp_kernel, tiles_p=tiles_p, tiles_per_s=tiles_per_s, d=d, n_f=n_f, alpha=alpha)
    ln_g, ln_b = ln_g.reshape(1, d), ln_b.reshape(1, d)
    return pl.pallas_call(
        kern,
        grid=(t // tm, n_f),
        in_specs=[pl.BlockSpec((tm, d), lambda i, f: (i, 0)),
                  pl.BlockSpec(mod_l.shape, lambda i, f: (0, 0)),
                  pl.BlockSpec((d, tf), lambda i, f: (0, f)),
                  pl.BlockSpec((tf, d), lambda i, f: (f, 0)),
                  pl.BlockSpec((1, d), lambda i, f: (0, 0)),
                  pl.BlockSpec((1, d), lambda i, f: (0, 0))],
        out_specs=pl.BlockSpec((tm, d), lambda i, f: (i, 0)),
        out_shape=jax.ShapeDtypeStruct((t, d), F32),
        scratch_shapes=[pltpu.VMEM((tm, d), BF16), pltpu.VMEM((tm, d), F32)],
        compiler_params=_cparams("arbitrary", "arbitrary"),
        name="mlp",
    )(x, mod_l, w1, w2, ln_g, ln_b)


def _rope_tables(n_pos, dk):
    ax = dk // 2
    half = ax // 2
    freqs = ROPE_BASE ** (-jnp.arange(half, dtype=F32) / half)
    pos = jnp.arange(n_pos)
    row = (pos // LATENT_GRID_W).astype(F32)
    col = (pos % LATENT_GRID_W).astype(F32)
    ang_r, ang_c = row[:, None] * freqs, col[:, None] * freqs
    cos = jnp.concatenate([jnp.cos(ang_r)] * 2 + [jnp.cos(ang_c)] * 2, axis=-1)
    sin = jnp.concatenate([-jnp.sin(ang_r), jnp.sin(ang_r), -jnp.sin(ang_c), jnp.sin(ang_c)], axis=-1)
    rep = LANES // dk
    return jnp.tile(cos, (1, rep)), jnp.tile(sin, (1, rep))


def _qkv_kernel(x_ref, mod_ref, w_ref, cos_ref, sin_ref, o_ref, *, tiles_p, tiles_per_s, d, quarter):
    n = pl.program_id(0)
    i = pl.program_id(1)
    g = _group_of_tile(i, tiles_p, tiles_per_s)
    shift = _mod_chunk(mod_ref, g, 0, d)
    scale = _mod_chunk(mod_ref, g, 1, d)
    h = (x_ref[...] * (1 + scale) + shift).astype(BF16)
    y = jnp.dot(h, w_ref[...], preferred_element_type=F32)
    rotate = jnp.logical_and(n < 2, i >= tiles_p)

    @pl.when(rotate)
    def _():
        cos, sin = cos_ref[...], sin_ref[...]
        lane = lax.broadcasted_iota(jnp.int32, cos.shape, 1)
        first = (lane % (2 * quarter)) < quarter
        for cb in range(y.shape[1] // LANES):
            yb = y[:, cb * LANES:(cb + 1) * LANES]
            partner = jnp.where(first, pltpu.roll(yb, LANES - quarter, 1), pltpu.roll(yb, quarter, 1))
            o_ref[0, :, cb * LANES:(cb + 1) * LANES] = yb * cos + partner * sin

    @pl.when(jnp.logical_not(rotate))
    def _():
        o_ref[0] = y


def _qkv(x, mod_l, w_qkv, cos_t, sin_t, tm, tiles_p, tiles_per_s, dk):
    t, d = x.shape
    n_out = w_qkv.shape[1]
    assert n_out % 3 == 0
    tn = n_out // 3
    kern = functools.partial(_qkv_kernel, tiles_p=tiles_p, tiles_per_s=tiles_per_s, d=d, quarter=dk // 4)
    pos_blk = lambda n, i: (jnp.maximum(i - tiles_p, 0) % tiles_per_s, 0)
    return pl.pallas_call(
        kern,
        grid=(3, t // tm),
        in_specs=[pl.BlockSpec((tm, d), lambda n, i: (i, 0)),
                  pl.BlockSpec(mod_l.shape, lambda n, i: (0, 0)),
                  pl.BlockSpec((d, tn), lambda n, i: (0, n)),
                  pl.BlockSpec((tm, LANES), pos_blk),
                  pl.BlockSpec((tm, LANES), pos_blk)],
        out_specs=pl.BlockSpec((1, tm, tn), lambda n, i: (n, i, 0)),
        out_shape=jax.ShapeDtypeStruct((3, t, tn), F32),
        compiler_params=_cparams("arbitrary", "arbitrary"),
        name="qkv_proj",
    )(x, mod_l, w_qkv, cos_t, sin_t)


def _attn_kernel(*refs, hp, dk, dv, scale, fold_scale, lam_init, has_cache):
    if has_cache:
        q1_ref, q2_ref, k1_ref, k2_ref, v_ref, ck_ref, cv_ref, lamv_ref, sg_ref, o_ref = refs
    else:
        q1_ref, q2_ref, k1_ref, k2_ref, v_ref, lamv_ref, sg_ref, o_ref = refs
        ck_ref = cv_ref = None
    lv = lamv_ref[...]
    lam = (jnp.exp(jnp.sum(lv[0:1] * lv[1:2], axis=-1, keepdims=True))
           - jnp.exp(jnp.sum(lv[2:3] * lv[3:4], axis=-1, keepdims=True)) + lam_init)
    nt = (((1,), (1,)), ((), ()))
    per_blk = LANES // dk

    def probs(q_ref, k_ref, m, head):
        cs = slice(head * dk, (head + 1) * dk)
        q = q_ref[:, cs]
        q = (q * scale).astype(BF16) if fold_scale else q.astype(BF16)
        s = lax.dot_general(q, k_ref[:, cs].astype(BF16), nt, preferred_element_type=F32)
        if not fold_scale:
            s = s * scale
        mx = jnp.max(s, axis=-1, keepdims=True)
        if has_cache:
            sc = lax.dot_general(q, ck_ref[0, 0, m, head].astype(BF16), nt, preferred_element_type=F32)
            if not fold_scale:
                sc = sc * scale
            mx = jnp.maximum(mx, jnp.max(sc, axis=-1, keepdims=True))
            ec = jnp.exp(sc - mx)
        e = jnp.exp(s - mx)
        den = jnp.sum(e, axis=-1, keepdims=True)
        if has_cache:
            den = den + jnp.sum(ec, axis=-1, keepdims=True)
            return e, ec, den
        return e, None, den

    for head in range(hp * per_blk):
        e1, e1c, d1 = probs(q1_ref, k1_ref, 0, head)
        e2, e2c, d2 = probs(q2_ref, k2_ref, 1, head)
        r1 = 1.0 / d1
        r2 = lam / d2
        vs = slice(head * dv, (head + 1) * dv)
        w = (e1 * r1 - e2 * r2).astype(BF16)
        o = jnp.dot(w, v_ref[:, vs].astype(BF16), preferred_element_type=F32)
        if has_cache:
            wc = (e1c * r1 - e2c * r2).astype(BF16)
            o = o + jnp.dot(wc, cv_ref[0, 0, head].astype(BF16), preferred_element_type=F32)
        o = o * lax.rsqrt(jnp.mean(jnp.square(o), axis=-1, keepdims=True) + LN_EPS)
        o = o * sg_ref[...] * (1.0 - lam_init)
        o_ref[:, vs] = o.astype(o_ref.dtype)


def _attn_t_kernel(*refs, heads_step, dk, dv, scale, fold_scale, lam_init, has_cache):
    if has_cache:
        q1_ref, q2_ref, k1_ref, k2_ref, v_ref, ck_ref, cv_ref, lamv_ref, sg_ref, o_ref = refs
    else:
        q1_ref, q2_ref, k1_ref, k2_ref, v_ref, lamv_ref, sg_ref, o_ref = refs
        ck_ref = cv_ref = None
    lv = lamv_ref[...]
    lam = (jnp.exp(jnp.sum(lv[0:1] * lv[1:2], axis=-1, keepdims=True))
           - jnp.exp(jnp.sum(lv[2:3] * lv[3:4], axis=-1, keepdims=True)) + lam_init)
    nt = (((1,), (1,)), ((), ()))
    tn = (((0,), (0,)), ((), ()))
    per_blk = LANES // dk
    tq = q1_ref.shape[0]
    lane = lax.broadcasted_iota(jnp.int32, (tq, LANES), 1)

    def exps(kb, qb, m, head, sub):
        qm = jnp.where((lane >= sub * dk) & (lane < (sub + 1) * dk), qb, 0.0).astype(BF16)
        s = lax.dot_general(kb, qm, nt, preferred_element_type=F32)
        if not fold_scale:
            s = s * scale
        mx = jnp.max(s, axis=0, keepdims=True)
        if has_cache:
            qc = qb[:, sub * dk:(sub + 1) * dk].astype(BF16)
            sc = lax.dot_general(ck_ref[0, 0, m, head].astype(BF16), qc, nt, preferred_element_type=F32)
            if not fold_scale:
                sc = sc * scale
            mx = jnp.maximum(mx, jnp.max(sc, axis=0, keepdims=True))
            ec = jnp.exp(sc - mx)
        e = jnp.exp(s - mx)
        den = jnp.sum(e, axis=0, keepdims=True)
        if has_cache:
            return e, ec, den + jnp.sum(ec, axis=0, keepdims=True)
        return e, None, den

    for blk in range(heads_step // per_blk):
        bs = slice(blk * LANES, (blk + 1) * LANES)
        k1b, k2b = k1_ref[:, bs].astype(BF16), k2_ref[:, bs].astype(BF16)
        q1b, q2b = q1_ref[:, bs], q2_ref[:, bs]
        if fold_scale:
            q1b, q2b = q1b * scale, q2b * scale
        for sub in range(per_blk):
            head = blk * per_blk + sub
            e1, e1c, d1 = exps(k1b, q1b, 0, head, sub)
            e2, e2c, d2 = exps(k2b, q2b, 1, head, sub)
            r1 = 1.0 / d1
            r2 = lam / d2
            vs = slice(head * dv, (head + 1) * dv)
            w = (e1 * r1 - e2 * r2).astype(BF16)
            o_t = lax.dot_general(v_ref[:, vs].astype(BF16), w, tn, preferred_element_type=F32)
            if has_cache:
                wc = (e1c * r1 - e2c * r2).astype(BF16)
                o_t = o_t + lax.dot_general(cv_ref[0, 0, head].astype(BF16), wc, tn, preferred_element_type=F32)
            o_t = o_t * lax.rsqrt(jnp.mean(jnp.square(o_t), axis=0, keepdims=True) + LN_EPS)
            o = o_t.T * sg_ref[...] * (1.0 - lam_init)
            o_ref[:, vs] = o.astype(o_ref.dtype)


def _attention(qkv, lamv, subln_g, *, row0, n_seq, lq, tq, hp, n_heads, dk, dv, lam_init, cache=None,
               keys_on_sublanes=False):
    per_blk = LANES // dk
    heads_step = hp * per_blk
    n_hblk = n_heads // heads_step
    map2 = n_heads * dk // (hp * LANES)
    scale = dk ** -0.5
    fold_scale = math.frexp(scale)[0] == 0.5
    qb0, kb0 = row0 // tq, row0 // lq
    n_q = lq // tq
    q_spec = lambda off: pl.BlockSpec((None, tq, hp * LANES), lambda b, h, qi: (0, qb0 + b * n_q + qi, off + h))
    k_spec = lambda off: pl.BlockSpec((None, lq, hp * LANES), lambda b, h, qi: (1, kb0 + b, off + h))
    in_specs = [q_spec(0), q_spec(map2), k_spec(0), k_spec(map2),
                pl.BlockSpec((None, lq, heads_step * dv), lambda b, h, qi: (2, kb0 + b, h))]
    args = [qkv, qkv, qkv, qkv, qkv]
    if cache is not None:
        cache_k, cache_v, o_i = cache
        past = cache_k.shape[-2]
        in_specs += [pl.BlockSpec((1, 1, 2, heads_step, past, dk), lambda b, h, qi: (b, o_i, 0, h, 0, 0)),
                     pl.BlockSpec((1, 1, heads_step, past, dv), lambda b, h, qi: (b, o_i, h, 0, 0))]
        args += [cache_k, cache_v]
    in_specs += [pl.BlockSpec(lamv.shape, lambda b, h, qi: (0, 0)),
                 pl.BlockSpec((1, dv), lambda b, h, qi: (0, 0))]
    args += [lamv, subln_g.reshape(1, dv)]
    if keys_on_sublanes:
        kern = functools.partial(_attn_t_kernel, heads_step=heads_step, dk=dk, dv=dv, scale=scale,
                                 fold_scale=fold_scale, lam_init=lam_init, has_cache=cache is not None)
    else:
        kern = functools.partial(_attn_kernel, hp=hp, dk=dk, dv=dv, scale=scale, fold_scale=fold_scale,
                                 lam_init=lam_init, has_cache=cache is not None)
    return pl.pallas_call(
        kern,
        grid=(n_seq, n_hblk, n_q),
        in_specs=in_specs,
        out_specs=pl.BlockSpec((tq, heads_step * dv), lambda b, h, qi: (b * n_q + qi, h)),
        out_shape=jax.ShapeDtypeStruct((n_seq * lq, n_heads * dv), BF16),
        compiler_params=_cparams("arbitrary", "arbitrary", "arbitrary"),
        name="diff_attn_cache" if cache is not None else "diff_attn",
    )(*args)


def _outc_kernel(x_ref, mod_ref, o_ref_in, w_ref, g_ref, b_ref, o_ref, *, tiles_p, tiles_per_s, d, alpha):
    grp = _group_of_tile(pl.program_id(0), tiles_p, tiles_per_s)
    out = jnp.dot(o_ref_in[...], w_ref[...], preferred_element_type=F32)
    gate = _mod_chunk(mod_ref, grp, 2, d)
    o_ref[...] = _post_residual(x_ref[...], out, gate, g_ref[...], b_ref[...], alpha)


def _outc(x, mod_l, o_n, w_out, ln_g, ln_b, tm, tiles_p, tiles_per_s, alpha):
    t, d = x.shape
    kin = o_n.shape[1]
    kern = functools.partial(_outc_kernel, tiles_p=tiles_p, tiles_per_s=tiles_per_s, d=d, alpha=alpha)
    ln_g, ln_b = ln_g.reshape(1, d), ln_b.reshape(1, d)
    full = lambda a: pl.BlockSpec(a.shape, lambda i: (0,) * a.ndim)
    return pl.pallas_call(
        kern,
        grid=(t // tm,),
        in_specs=[pl.BlockSpec((tm, d), lambda i: (i, 0)), full(mod_l), pl.BlockSpec((tm, kin), lambda i: (i, 0)),
                  full(w_out), full(ln_g), full(ln_b)],
        out_specs=pl.BlockSpec((tm, d), lambda i: (i, 0)),
        out_shape=jax.ShapeDtypeStruct((t, d), F32),
        compiler_params=_cparams("arbitrary"),
        name="outproj_c",
    )(x, mod_l, o_n, w_out, ln_g, ln_b)


def kernel(x_prompt, x_sample, state_s5_re, state_s5_im, cache_k, cache_v, c, c_ctx, w_mod, b_mod, ln_g, ln_b, w_in_ab, w_dw, b_dw, conv_ln_g, conv_ln_b, s5_lambda_re, s5_lambda_im, s5_log_dt, s5_b_re, s5_b_im, s5_c_re, s5_c_im, s5_d, w_glu, w_out_ab, w_qkv, lam_q1, lam_k1, lam_q2, lam_k2, subln_g, w_out_c, w_ff1, w_ff2):
    bp, lp, d = x_prompt.shape
    bs, ls, _ = x_sample.shape
    depth = w_mod.shape[0]
    tp, ts = bp * lp, bs * ls
    alpha = (2 * depth) ** 0.25
    assert 1 + bs <= MOD_ROWS

    tm = 256
    tm_mlp, tf = 512, 512
    assert tp % tm_mlp == 0 and ls % tm_mlp == 0 and tp % ls == 0

    x = jnp.concatenate([x_prompt.reshape(tp, d), x_sample.reshape(ts, d)], axis=0)
    cvec = jnp.zeros((MOD_ROWS, d), F32).at[0].set(c_ctx).at[1:1 + bs].set(c)
    mod = _modvec(cvec, w_mod, b_mod)

    g_ssm, n_ssm, p_ssm = s5_b_re.shape[2:]
    dk = lam_q1.shape[-1]
    dv = subln_g.shape[-1]
    n_heads = w_out_c.shape[1] // dv
    s_pad = 8
    assert bs <= s_pad and bp % 8 == 0

    s_re, s_im, k_list, v_list = [], [], [], []
    for l in range(depth):
        mod_l = mod[l]
        if l % 2 == 0:
            e = l // 2
            ug, u = _inproj(x, mod_l, w_in_ab[e].astype(BF16), tm, tp // tm, ls // tm)
            y_conv = _conv_module(ug, w_dw[e], b_dw[e], conv_ln_g[e], conv_ln_b[e], min(lp, 256), lp, ls, tp)
            v_re, v_ni, w_re, w_im, a_re, a_im = _s5_prep(s5_lambda_re[e], s5_lambda_im[e], s5_log_dt[e],
                                                          s5_b_re[e], s5_b_im[e], s5_c_re[e], s5_c_im[e])
            kmat = _s5_kmat(v_re, v_ni, w_re, w_im, n_ssm, p_ssm)
            mcat, be_pair, cs_pair, a4 = _s5_weights(v_re, v_ni, w_re, w_im, kmat, a_re, a_im, g_ssm, n_ssm, p_ssm)
            gn = g_ssm * n_ssm
            yp_rows, hf = _s5_chunked(_to_chunk_rows(u[:tp], bp, lp, bp, g_ssm, p_ssm), jnp.zeros((4, bp, gn), F32),
                                      mcat, be_pair, cs_pair, a4, bp, 8)
            st = lambda a, dr: jnp.pad(a[:, e, dr].reshape(bs, gn), ((0, s_pad - bs), (0, 0)))
            h0 = jnp.stack([st(state_s5_re, 0), st(state_s5_im, 0), st(state_s5_re, 1), st(state_s5_im, 1)])
            ys_rows, _ = _s5_chunked(_to_chunk_rows(u[tp:], bs, ls, s_pad, g_ssm, p_ssm), h0,
                                     mcat, be_pair, cs_pair, a4, s_pad, 4)
            y_scan = jnp.concatenate([_from_chunk_rows(yp_rows, bp, lp, bp, g_ssm, p_ssm),
                                      _from_chunk_rows(ys_rows, bs, ls, s_pad, g_ssm, p_ssm)], axis=0)
            hf = hf.reshape(2, 2, bp, g_ssm, n_ssm)
            s_re.append(hf[:, 0].transpose(1, 0, 2, 3))
            s_im.append(hf[:, 1].transpose(1, 0, 2, 3))
            x = _outab(x, mod_l, y_conv, y_scan, u, s5_d[e], w_glu[e].astype(BF16),
                       w_out_ab[e].astype(BF16), ln_g[l, 0], ln_b[l, 0], tm, tp // tm, ls // tm, alpha)
        else:
            o_i = l // 2
            lam_init = 0.8 - 0.6 * math.exp(-0.3 * l)
            cos_t, sin_t = _rope_tables(ls, dk)
            qkv = _qkv(x, mod_l, w_qkv[o_i].astype(BF16), cos_t, sin_t, tm, tp // tm, ls // tm, dk)
            lamv = jnp.stack([lam_q1[o_i], lam_k1[o_i], lam_q2[o_i], lam_k2[o_i]])
            geo = dict(n_heads=n_heads, dk=dk, dv=dv, lam_init=lam_init)
            o_p = _attention(qkv, lamv, subln_g[o_i], row0=0, n_seq=bp, lq=lp, tq=lp,
                             hp=n_heads * dk // LANES, keys_on_sublanes=True, **geo)
            o_s = _attention(qkv, lamv, subln_g[o_i], row0=tp, n_seq=bs, lq=ls, tq=256, hp=1,
                             cache=(cache_k, cache_v, o_i), **geo)
            o_n = jnp.concatenate([o_p, o_s], axis=0)
            kq = qkv[1, :tp].reshape(bp, lp, 2, n_heads, dk)
            k_list.append(kq.transpose(0, 2, 3, 1, 4))
            v_list.append(qkv[2, :tp].reshape(bp, lp, n_heads, dv).transpose(0, 2, 1, 3))
            x = _outc(x, mod_l, o_n, w_out_c[o_i].astype(BF16), ln_g[l, 0], ln_b[l, 0], tm, tp // tm, ls // tm, alpha)
        x = _mlp(x, mod_l, w_ff1[l].astype(BF16), w_ff2[l].astype(BF16), ln_g[l, 1], ln_b[l, 1],
                 tm_mlp, tf, tp // tm_mlp, ls // tm_mlp, alpha)

    return (x[:tp].reshape(bp, lp, d), x[tp:].reshape(bs, ls, d),
            jnp.stack(s_re, axis=1), jnp.stack(s_im, axis=1),
            jnp.stack(k_list, axis=1), jnp.stack(v_list, axis=1))
```

```python
import functools
import math

import jax
import jax.numpy as jnp
from jax import lax
from jax.experimental import pallas as pl
from jax.experimental.pallas import tpu as pltpu

F32 = jnp.float32
BF16 = jnp.bfloat16

LN_EPS = 1e-5
ROPE_BASE = 10000.0
LATENT_GRID_W = 64
MOD_ROWS = 8
V7X_VMEM_LIMIT = 56 * 1024 * 1024
LANES = 128


def _cparams(*sem):
    return pltpu.CompilerParams(dimension_semantics=sem, vmem_limit_bytes=V7X_VMEM_LIMIT)


def _layer_norm(z, g, b):
    mu = jnp.mean(z, axis=-1, keepdims=True)
    zc = z - mu
    var = jnp.mean(jnp.square(zc), axis=-1, keepdims=True)
    return zc * lax.rsqrt(var + LN_EPS) * g + b


def _group_of_tile(i, tiles_p, tiles_per_s):
    return jnp.where(i < tiles_p, 0, 1 + jnp.maximum(i - tiles_p, 0) // tiles_per_s)


def _mod_chunk(mod_ref, g, k, d):
    return mod_ref[pl.ds(g, 1), k * d:(k + 1) * d]


def _modvec_kernel(cv_ref, w_ref, b_ref, o_ref):
    cv = cv_ref[...]
    s = (cv * jax.nn.sigmoid(cv)).astype(BF16)
    o_ref[0] = jnp.dot(s, w_ref[0].astype(BF16), preferred_element_type=F32) + b_ref[0]


def _modvec(cvec, w_mod, b_mod, tn=1024):
    depth, d, n = w_mod.shape
    return pl.pallas_call(
        _modvec_kernel,
        grid=(depth, n // tn),
        in_specs=[pl.BlockSpec((MOD_ROWS, d), lambda l, j: (0, 0)),
                  pl.BlockSpec((1, d, tn), lambda l, j: (l, 0, j)),
                  pl.BlockSpec((1, 1, tn), lambda l, j: (l, 0, j))],
        out_specs=pl.BlockSpec((1, MOD_ROWS, tn), lambda l, j: (l, 0, j)),
        out_shape=jax.ShapeDtypeStruct((depth, MOD_ROWS, n), F32),
        compiler_params=_cparams("arbitrary", "arbitrary"),
        name="modvec",
    )(cvec, w_mod, b_mod.reshape(depth, 1, n))


def _inproj_kernel(x_ref, mod_ref, w_ref, ug_ref, u_ref, *, tiles_p, tiles_per_s, d, c):
    g = _group_of_tile(pl.program_id(0), tiles_p, tiles_per_s)
    shift = _mod_chunk(mod_ref, g, 0, d)
    scale = _mod_chunk(mod_ref, g, 1, d)
    h = (x_ref[...] * (1 + scale) + shift).astype(BF16)
    a_val = jnp.dot(h, w_ref[:, 0:c], preferred_element_type=F32)
    a_gate = jnp.dot(h, w_ref[:, c:2 * c], preferred_element_type=F32)
    ug_ref[...] = a_val * jax.nn.sigmoid(a_gate)
    u_ref[...] = jnp.dot(h, w_ref[:, 2 * c:3 * c], preferred_element_type=F32)


def _inproj(x, mod_l, w_in, tm, tiles_p, tiles_per_s):
    t, d = x.shape
    c = w_in.shape[1] // 3
    kern = functools.partial(_inproj_kernel, tiles_p=tiles_p, tiles_per_s=tiles_per_s, d=d, c=c)
    return pl.pallas_call(
        kern,
        grid=(t // tm,),
        in_specs=[pl.BlockSpec((tm, d), lambda i: (i, 0)),
                  pl.BlockSpec(mod_l.shape, lambda i: (0, 0)),
                  pl.BlockSpec(w_in.shape, lambda i: (0, 0))],
        out_specs=[pl.BlockSpec((tm, c), lambda i: (i, 0)),
                   pl.BlockSpec((tm, c), lambda i: (i, 0))],
        out_shape=[jax.ShapeDtypeStruct((t, c), F32), jax.ShapeDtypeStruct((t, c), F32)],
        compiler_params=_cparams("arbitrary"),
        name="inproj",
    )(x, mod_l, w_in)


CONV_HALO = 16
CONV_ROWS = 32
CONV_COLS = 256


def _conv_kernel(prev_ref, cur_ref, next_ref, w_ref, b_ref, g_ref, beta_ref, o_ref, pad_scr, conv_scr,
                 *, chunks_p, chunks_s, n_chunks_p, width, lc, c):
    i = pl.program_id(0)
    in_p = i < n_chunks_p
    k = jnp.where(in_p, i % chunks_p, jnp.maximum(i - n_chunks_p, 0) % chunks_s)
    last = jnp.where(in_p, chunks_p - 1, chunks_s - 1)
    has_prev = (k > 0).astype(F32)
    has_next = (k < last).astype(F32)
    pad_scr[0:CONV_HALO, :] = prev_ref[...] * has_prev
    pad_scr[CONV_HALO:CONV_HALO + lc, :] = cur_ref[...]
    pad_scr[CONV_HALO + lc:2 * CONV_HALO + lc, :] = next_ref[...] * has_next
    off = CONV_HALO - width // 2

    for r0 in range(0, lc, CONV_ROWS):
        for cb in range(c // CONV_COLS):
            cs = slice(cb * CONV_COLS, (cb + 1) * CONV_COLS)
            acc = jnp.zeros((CONV_ROWS, CONV_COLS), F32)
            for kk in range(width):
                acc = acc + pad_scr[r0 + kk + off:r0 + kk + off + CONV_ROWS, cs] * w_ref[kk:kk + 1, cs]
            conv_scr[r0:r0 + CONV_ROWS, cs] = acc + b_ref[:, cs]
    y = _layer_norm(conv_scr[...], g_ref[...], beta_ref[...])
    o_ref[...] = (y * jax.nn.sigmoid(y)).astype(o_ref.dtype)


def _conv_module(ug, w_dw, b_dw, ln_g, ln_b, lc, lp, ls, tp):
    t, c = ug.shape
    width = w_dw.shape[0]
    assert width // 2 < CONV_HALO and lc % CONV_HALO == 0 and lp % lc == 0 and ls % lc == 0
    hb = lc // CONV_HALO
    n_halo_blocks = t // CONV_HALO
    kern = functools.partial(_conv_kernel, chunks_p=lp // lc, chunks_s=ls // lc, n_chunks_p=tp // lc,
                             width=width, lc=lc, c=c)
    vec = lambda a: a.reshape(1, c)
    return pl.pallas_call(
        kern,
        grid=(t // lc,),
        in_specs=[pl.BlockSpec((CONV_HALO, c), lambda i: (jnp.maximum(i * hb - 1, 0), 0)),
                  pl.BlockSpec((lc, c), lambda i: (i, 0)),
                  pl.BlockSpec((CONV_HALO, c), lambda i: (jnp.minimum((i + 1) * hb, n_halo_blocks - 1), 0)),
                  pl.BlockSpec((width, c), lambda i: (0, 0)),
                  pl.BlockSpec((1, c), lambda i: (0, 0)),
                  pl.BlockSpec((1, c), lambda i: (0, 0)),
                  pl.BlockSpec((1, c), lambda i: (0, 0))],
        out_specs=pl.BlockSpec((lc, c), lambda i: (i, 0)),
        out_shape=jax.ShapeDtypeStruct((t, c), BF16),
        scratch_shapes=[pltpu.VMEM((lc + 2 * CONV_HALO, c), F32), pltpu.VMEM((lc, c), F32)],
        compiler_params=_cparams("arbitrary"),
        name="conv_module",
    )(ug, ug, ug, w_dw, vec(b_dw), vec(ln_g), vec(ln_b))


S5_T = 16
S5_GB = 8


def _s5_prep_kernel(bt_re_ref, bt_im_ref, la_re_ref, la_im_ref, dta_ref, ct_re_ref, ct_im_ref, lb_re_ref, lb_im_ref,
                    dtb_ref, be_re_ref, be_im_ref, cs_re_ref, cs_ni_ref, kc_ref, a_re_ref, a_im_ref, *, n, p):
    t = S5_T
    fwd = pl.program_id(0) == 0

    lam_re, lam_im = la_re_ref[0, 0], la_im_ref[0, 0]
    dt = jnp.exp(dta_ref[0, 0])
    mag = jnp.exp(lam_re * dt)
    ar, ai = mag * jnp.cos(lam_im * dt), mag * jnp.sin(lam_im * dt)
    den = jnp.square(lam_re) + jnp.square(lam_im)
    coef_re = ((ar - 1) * lam_re + ai * lam_im) / den
    coef_im = (ai * lam_re - (ar - 1) * lam_im) / den
    b_re, b_im = bt_re_ref[0, 0], bt_im_ref[0, 0]
    bb_re = coef_re * b_re - coef_im * b_im
    bb_im = coef_re * b_im + coef_im * b_re
    rows = bb_re.shape[0]
    pw = []
    wr, wi = bb_re, bb_im
    pr, pi = jnp.ones_like(ar), jnp.zeros_like(ar)
    for k in range(t):
        pw.append((wr, wi))
        wr, wi = wr * ar - wi * ai, wr * ai + wi * ar
        pr, pi = pr * ar - pi * ai, pr * ai + pi * ar
    a_re_ref[0, 0] = pr
    a_im_ref[0, 0] = pi
    for j in range(t):
        be_re_ref[0, 0, j * rows:(j + 1) * rows, :] = jnp.where(fwd, pw[t - 1 - j][0], pw[j][0])
        be_im_ref[0, 0, j * rows:(j + 1) * rows, :] = jnp.where(fwd, pw[t - 1 - j][1], pw[j][1])

    ct_re, ct_im = ct_re_ref[0, 0], ct_im_ref[0, 0]
    shp = ct_re.shape
    lam_re = jnp.broadcast_to(lb_re_ref[0, 0], shp)
    lam_im = jnp.broadcast_to(lb_im_ref[0, 0], shp)
    dt = jnp.exp(jnp.broadcast_to(dtb_ref[0, 0], shp))
    blk = lax.broadcasted_iota(jnp.int32, shp, 1) // p
    k1 = jnp.where(fwd, blk + 1, t - blk).astype(F32)
    mag = jnp.exp(k1 * (lam_re * dt))
    ang = k1 * (lam_im * dt)
    qr, qi = mag * jnp.cos(ang), mag * jnp.sin(ang)
    v_re = ct_re * qr - ct_im * qi
    v_im = ct_re * qi + ct_im * qr
    cs_re_ref[0, 0] = v_re
    cs_ni_ref[0, 0] = -v_im
    lane = lax.broadcasted_iota(jnp.int32, shp, 1)
    w = shp[1]
    v0_re = jnp.where(fwd, jnp.where(lane < p, ct_re, pltpu.roll(v_re, p, 1)),
                      jnp.where(lane >= w - p, ct_re, pltpu.roll(v_re, w - p, 1)))
    v0_im = jnp.where(fwd, jnp.where(lane < p, ct_im, pltpu.roll(v_im, p, 1)),
                      jnp.where(lane >= w - p, ct_im, pltpu.roll(v_im, w - p, 1)))
    hi = lax.Precision.HIGHEST
    for g in range(S5_GB):
        ra, rb = slice(g * p, (g + 1) * p), slice(g * n, (g + 1) * n)
        kc_ref[0, 0, ra, :] = (jnp.dot(bb_re[ra], v0_re[rb], precision=hi, preferred_element_type=F32)
                               - jnp.dot(bb_im[ra], v0_im[rb], precision=hi, preferred_element_type=F32))


def _s5_prep(lam_re, lam_im, log_dt, b_re, b_im, c_re, c_im):
    _, g, n, p = b_re.shape
    t = S5_T
    nb = g // S5_GB
    ra, rb = S5_GB * p, S5_GB * n
    lay_a = lambda a: jnp.broadcast_to(a[:, :, None, :], (2, g, p, n)).reshape(2, nb, ra, n)
    lay_b = lambda a: a.reshape(2, nb, rb, 1)
    bt = lambda a: a.transpose(0, 1, 3, 2).reshape(2, nb, ra, n)
    ct = lambda a: jnp.broadcast_to(a.transpose(0, 1, 3, 2)[:, :, :, None, :], (2, g, n, t, p)).reshape(2, nb, rb, t * p)
    dt_g = jnp.broadcast_to(log_dt[:, :, None], (2, g, n))
    blk = lambda r, c: pl.BlockSpec((1, 1, r, c), lambda d, i: (d, i, 0, 0))
    shp = lambda r, c: jax.ShapeDtypeStruct((2, nb, r, c), F32)
    kern = functools.partial(_s5_prep_kernel, n=n, p=p)
    return pl.pallas_call(
        kern,
        grid=(2, nb),
        in_specs=[blk(ra, n)] * 5 + [blk(rb, t * p)] * 2 + [blk(rb, 1)] * 3,
        out_specs=[blk(t * ra, n), blk(t * ra, n), blk(rb, t * p), blk(rb, t * p), blk(ra, t * p), blk(ra, n), blk(ra, n)],
        out_shape=[shp(t * ra, n), shp(t * ra, n), shp(rb, t * p), shp(rb, t * p), shp(ra, t * p), shp(ra, n), shp(ra, n)],
        compiler_params=_cparams("arbitrary", "arbitrary"),
        name="s5_prep",
    )(bt(b_re), bt(b_im), lay_a(lam_re), lay_a(lam_im), lay_a(dt_g), ct(c_re), ct(c_im),
      lay_b(lam_re), lay_b(lam_im), lay_b(dt_g))


def _s5_expand(src, tile, row_div, row_mod, lane_div, lane_mod, precision=None):
    full = jnp.dot(src, tile, precision=precision, preferred_element_type=F32)
    r = lax.broadcasted_iota(jnp.int32, full.shape, 0) // row_div % row_mod
    l = lax.broadcasted_iota(jnp.int32, full.shape, 1) // lane_div % lane_mod
    return jnp.where(r == l, full, 0.0)


def _s5_chunk_kernel(u_ref, be_re_ref, be_im_ref, cs_re_ref, cs_ni_ref, kc_ref, tk_ref, tb_ref, a_ref, h0_ref,
                     y_ref, hf_ref, m8_scr, be8_scr, cs8_scr, e_scr, *, geoms, n_tiles_p, n, p):
    t, gb = S5_T, S5_GB
    cw = gb * p
    sw = gb * n
    rows = geoms[0][0] * geoms[0][1]
    tile_i = pl.program_id(1)

    @pl.when(tile_i == 0)
    def _():
        tk, tb = tk_ref[...].astype(BF16), tb_ref[...].astype(BF16)
        step = 4 * cw
        for r0 in range(0, t * cw, step):
            for part, (ref, d) in enumerate(((be_re_ref, 0), (be_im_ref, 0), (be_re_ref, 1), (be_im_ref, 1))):
                be8_scr[r0:r0 + step, part * sw:(part + 1) * sw] = _s5_expand(
                    ref[d, 0, r0:r0 + step, :].astype(BF16), tb, p, gb, n, gb).astype(BF16)
        for part, (ref, d) in enumerate(((cs_re_ref, 0), (cs_ni_ref, 0), (cs_re_ref, 1), (cs_ni_ref, 1))):
            cs8_scr[part * sw:(part + 1) * sw, :] = _s5_expand(ref[d, 0].astype(BF16), tk, n, gb, p, gb).astype(BF16)
        hi = lax.Precision.HIGHEST
        bd_f = _s5_expand(kc_ref[0, 0], tk_ref[...], p, gb, p, gb, hi)
        bd_r = _s5_expand(kc_ref[1, 0], tk_ref[...], p, gb, p, gb, hi)
        tile_f = lambda k: bd_f[:, k * cw:(k + 1) * cw]
        tile_r = lambda k: bd_r[:, (t - 1 - k) * cw:(t - k) * cw]
        for j in range(t):
            for i in range(t):
                blk = tile_f(i - j) if i > j else tile_r(j - i) if i < j else tile_f(0) + tile_r(0)
                m8_scr[j * cw:(j + 1) * cw, i * cw:(i + 1) * cw] = blk.astype(BF16)

    x = u_ref[...].reshape(rows, t * cw).astype(BF16)
    e = jnp.dot(x, be8_scr[...], preferred_element_type=F32)
    n_slab = e.shape[1] // LANES
    per_part = sw // LANES
    for k in range(n_slab):
        e_scr[k] = e[:, k * LANES:(k + 1) * LANES]

    def scan(nseq, nc):
        loops = [list(range(per_part))] if nseq < 8 else [[q] for q in range(per_part)]
        for prs in loops:
            coef = [[jnp.broadcast_to(a_ref[r:r + 1, q * LANES:(q + 1) * LANES], (nseq, LANES)) for r in range(4)]
                    for q in prs]
            init = tuple(tuple(h0_ref[0, r, 0:nseq, q * LANES:(q + 1) * LANES] for r in range(4)) for q in prs)

            def step(c, carry):
                out = []
                for idx, q in enumerate(prs):
                    sf_re, sf_im, sr_re, sr_im = carry[idx]
                    af_re, af_im, ar_re, ar_im = coef[idx]
                    at_f = pl.ds(c, nseq, stride=nc)
                    at_r = pl.ds(nc - 1 - c, nseq, stride=nc)
                    ef_re, ef_im = e_scr[q, at_f, :], e_scr[per_part + q, at_f, :]
                    er_re, er_im = e_scr[2 * per_part + q, at_r, :], e_scr[3 * per_part + q, at_r, :]
                    e_scr[q, at_f, :] = sf_re
                    e_scr[per_part + q, at_f, :] = sf_im
                    e_scr[2 * per_part + q, at_r, :] = sr_re
                    e_scr[3 * per_part + q, at_r, :] = sr_im
                    out.append((af_re * sf_re - af_im * sf_im + ef_re, af_re * sf_im + af_im * sf_re + ef_im,
                                ar_re * sr_re - ar_im * sr_im + er_re, ar_re * sr_im + ar_im * sr_re + er_im))
                return tuple(out)

            fin = lax.fori_loop(0, nc, step, init)
            for idx, q in enumerate(prs):
                for r in range(4):
                    hf_ref[0, r, 0:nseq, q * LANES:(q + 1) * LANES] = fin[idx][r]

    hf_ref[...] = jnp.zeros_like(hf_ref)

    @pl.when(tile_i < n_tiles_p)
    def _():
        scan(*geoms[0])

    @pl.when(tile_i >= n_tiles_p)
    def _():
        scan(*geoms[1])

    s = jnp.concatenate([e_scr[k] for k in range(n_slab)], axis=-1).astype(BF16)
    y = (jnp.dot(x, m8_scr[...], preferred_element_type=F32)
         + jnp.dot(s, cs8_scr[...], preferred_element_type=F32))
    y_ref[...] = y.reshape(rows * t, cw)


def _s5_chunked(u, h0, prep, tile_k, tile_b, a4, *, tok, geoms, n_tiles_p, n, p):
    be_re, be_im, cs_re, cs_ni, kc = prep
    t, gb = S5_T, S5_GB
    cw, sw = gb * p, gb * n
    nb = u.shape[1] // cw
    n_tiles = u.shape[0] // tok
    ms = h0.shape[2]
    both = lambda a: pl.BlockSpec((2, 1) + a.shape[2:], lambda b, i: (0, b, 0, 0), pipeline_mode=pl.Buffered(1))
    const = lambda a: pl.BlockSpec(a.shape, lambda b, i: (0, 0), pipeline_mode=pl.Buffered(1))
    kern = functools.partial(_s5_chunk_kernel, geoms=geoms, n_tiles_p=n_tiles_p, n=n, p=p)
    return pl.pallas_call(
        kern,
        grid=(nb, n_tiles),
        in_specs=[pl.BlockSpec((tok, cw), lambda b, i: (i, b)),
                  both(be_re), both(be_im), both(cs_re), both(cs_ni), both(kc), const(tile_k), const(tile_b),
                  pl.BlockSpec((4, sw), lambda b, i: (0, b)),
                  pl.BlockSpec((1, 4, ms, sw), lambda b, i: (i, 0, 0, b))],
        out_specs=[pl.BlockSpec((tok, cw), lambda b, i: (i, b)),
                   pl.BlockSpec((1, 4, ms, sw), lambda b, i: (i, 0, 0, b))],
        out_shape=[jax.ShapeDtypeStruct(u.shape, F32), jax.ShapeDtypeStruct(h0.shape, F32)],
        scratch_shapes=[pltpu.VMEM((t * cw, t * cw), BF16), pltpu.VMEM((t * cw, 4 * sw), BF16),
                        pltpu.VMEM((4 * sw, t * cw), BF16), pltpu.VMEM((4 * sw // LANES, tok // t, LANES), F32)],
        compiler_params=_cparams("arbitrary", "arbitrary"),
        name="s5_chunked",
    )(u, be_re, be_im, cs_re, cs_ni, kc, tile_k, tile_b, a4, h0)


def _s5_tiles(n, p):
    t, gb = S5_T, S5_GB
    eye = lambda k: jnp.eye(k, dtype=F32)
    tile_k = jnp.einsum("ab,pq->apbq", eye(t), eye(p))[:, :, :, None, :] * jnp.ones((1, 1, 1, gb, 1), F32)
    tile_b = eye(n)[:, None, :] * jnp.ones((1, gb, 1), F32)
    return tile_k.reshape(t * p, t * gb * p), tile_b.reshape(n, gb * n)


def _post_residual(x, y, gate, g, b, alpha):
    return _layer_norm(alpha * x + gate * y, g, b)


def _outab_kernel(x_ref, mod_ref, yc_ref, ys_ref, u_ref, dsk_ref, wglu_ref, wout_ref, g_ref, b_ref, o_ref,
                  *, tiles_p, tiles_per_s, d, c, alpha):
    grp = _group_of_tile(pl.program_id(0), tiles_p, tiles_per_s)
    y_s = ys_ref[...] + dsk_ref[...] * u_ref[...]
    y_s = jax.nn.gelu(y_s)
    z = jnp.dot(y_s.astype(BF16), wglu_ref[...], preferred_element_type=F32)
    y_ssm = y_s * jax.nn.sigmoid(z)
    out = (jnp.dot(yc_ref[...], wout_ref[0:c, :], preferred_element_type=F32)
           + jnp.dot(y_ssm.astype(BF16), wout_ref[c:2 * c, :], preferred_element_type=F32))
    gate = _mod_chunk(mod_ref, grp, 2, d)
    o_ref[...] = _post_residual(x_ref[...], out, gate, g_ref[...], b_ref[...], alpha)


def _outab(x, mod_l, y_conv, y_scan, u, d_skip, w_glu, w_out, ln_g, ln_b, tm, tiles_p, tiles_per_s, alpha):
    t, d = x.shape
    c = u.shape[1]
    kern = functools.partial(_outab_kernel, tiles_p=tiles_p, tiles_per_s=tiles_per_s, d=d, c=c, alpha=alpha)
    row = lambda w: pl.BlockSpec((tm, w), lambda i: (i, 0))
    full = lambda a: pl.BlockSpec(a.shape, lambda i: (0,) * a.ndim)
    d_skip, ln_g, ln_b = d_skip.reshape(1, c), ln_g.reshape(1, d), ln_b.reshape(1, d)
    return pl.pallas_call(
        kern,
        grid=(t // tm,),
        in_specs=[row(d), full(mod_l), row(c), row(c), row(c), full(d_skip), full(w_glu), full(w_out),
                  full(ln_g), full(ln_b)],
        out_specs=row(d),
        out_shape=jax.ShapeDtypeStruct((t, d), F32),
        compiler_params=_cparams("arbitrary"),
        name="outproj_ab",
    )(x, mod_l, y_conv, y_scan, u, d_skip, w_glu, w_out, ln_g, ln_b)


def _mlp_kernel(x_ref, mod_ref, w1_ref, w2_ref, g_ref, b_ref, o_ref, h_scr, acc_scr,
                *, tiles_p, tiles_per_s, d, n_f, alpha):
    f = pl.program_id(1)
    grp = _group_of_tile(pl.program_id(0), tiles_p, tiles_per_s)

    @pl.when(f == 0)
    def _():
        shift = _mod_chunk(mod_ref, grp, 3, d)
        scale = _mod_chunk(mod_ref, grp, 4, d)
        h_scr[...] = (x_ref[...] * (1 + scale) + shift).astype(BF16)
        acc_scr[...] = jnp.zeros_like(acc_scr)

    a = jnp.dot(h_scr[...], w1_ref[...], preferred_element_type=F32)
    a = jnp.square(jnp.maximum(a, 0.0)).astype(BF16)
    acc_scr[...] += jnp.dot(a, w2_ref[...], preferred_element_type=F32)

    @pl.when(f == n_f - 1)
    def _():
        gate = _mod_chunk(mod_ref, grp, 5, d)
        o_ref[...] = _post_residual(x_ref[...], acc_scr[...], gate, g_ref[...], b_ref[...], alpha)


def _mlp(x, mod_l, w1, w2, ln_g, ln_b, tm, tf, tiles_p, tiles_per_s, alpha):
    t, d = x.shape
    n_f = w1.shape[1] // tf
    kern = functools.partial(_mlp_kernel, tiles_p=tiles_p, tiles_per_s=tiles_per_s, d=d, n_f=n_f, alpha=alpha)
    ln_g, ln_b = ln_g.reshape(1, d), ln_b.reshape(1, d)
    return pl.pallas_call(
        kern,
        grid=(t // tm, n_f),
        in_specs=[pl.BlockSpec((tm, d), lambda i, f: (i, 0)),
                  pl.BlockSpec(mod_l.shape, lambda i, f: (0, 0)),
                  pl.BlockSpec((d, tf), lambda i, f: (0, f)),
                  pl.BlockSpec((tf, d), lambda i, f: (f, 0)),
                  pl.BlockSpec((1, d), lambda i, f: (0, 0)),
                  pl.BlockSpec((1, d), lambda i, f: (0, 0))],
        out_specs=pl.BlockSpec((tm, d), lambda i, f: (i, 0)),
        out_shape=jax.ShapeDtypeStruct((t, d), F32),
        scratch_shapes=[pltpu.VMEM((tm, d), BF16), pltpu.VMEM((tm, d), F32)],
        compiler_params=_cparams("arbitrary", "arbitrary"),
        name="mlp",
    )(x, mod_l, w1, w2, ln_g, ln_b)


def _rope_tables(n_pos, dk):
    ax = dk // 2
    half = ax // 2
    freqs = ROPE_BASE ** (-jnp.arange(half, dtype=F32) / half)
    pos = jnp.arange(n_pos)
    row = (pos // LATENT_GRID_W).astype(F32)
    col = (pos % LATENT_GRID_W).astype(F32)
    ang_r, ang_c = row[:, None] * freqs, col[:, None] * freqs
    cos = jnp.concatenate([jnp.cos(ang_r)] * 2 + [jnp.cos(ang_c)] * 2, axis=-1)
    sin = jnp.concatenate([-jnp.sin(ang_r), jnp.sin(ang_r), -jnp.sin(ang_c), jnp.sin(ang_c)], axis=-1)
    rep = LANES // dk
    return jnp.tile(cos, (1, rep)), jnp.tile(sin, (1, rep))


def _qkv_kernel(x_ref, mod_ref, w_ref, cos_ref, sin_ref, o_ref, kc_ref, vc_ref,
                *, tiles_p, tiles_per_s, d, dk, dv):
    n = pl.program_id(0)
    i = pl.program_id(1)
    g = _group_of_tile(i, tiles_p, tiles_per_s)
    shift = _mod_chunk(mod_ref, g, 0, d)
    scale = _mod_chunk(mod_ref, g, 1, d)
    h = (x_ref[...] * (1 + scale) + shift).astype(BF16)
    y = jnp.dot(h, w_ref[...], preferred_element_type=F32)
    rotate = jnp.logical_and(n < 2, i >= tiles_p)
    quarter = dk // 4

    @pl.when(rotate)
    def _():
        cos, sin = cos_ref[...], sin_ref[...]
        lane = lax.broadcasted_iota(jnp.int32, cos.shape, 1)
        first = (lane % (2 * quarter)) < quarter
        for cb in range(y.shape[1] // LANES):
            yb = y[:, cb * LANES:(cb + 1) * LANES]
            partner = jnp.where(first, pltpu.roll(yb, LANES - quarter, 1), pltpu.roll(yb, quarter, 1))
            o_ref[0, :, cb * LANES:(cb + 1) * LANES] = yb * cos + partner * sin

    @pl.when(jnp.logical_not(rotate))
    def _():
        o_ref[0] = y

    @pl.when(jnp.logical_and(n == 1, i < tiles_p))
    def _():
        n_heads = kc_ref.shape[2]
        for m in range(2):
            for hh in range(n_heads):
                c0 = (m * n_heads + hh) * dk
                kc_ref[0, m, hh] = y[:, c0:c0 + dk]

    @pl.when(jnp.logical_and(n == 2, i < tiles_p))
    def _():
        for hh in range(vc_ref.shape[1]):
            vc_ref[0, hh] = y[:, hh * dv:(hh + 1) * dv]


def _qkv(x, mod_l, w_qkv, cos_t, sin_t, tm, tiles_p, tiles_per_s, dk, dv, bp, lp):
    t, d = x.shape
    n_out = w_qkv.shape[1]
    assert n_out % 3 == 0 and lp % tm == 0
    tn = n_out // 3
    n_heads = tn // dv
    per_seq = lp // tm
    kern = functools.partial(_qkv_kernel, tiles_p=tiles_p, tiles_per_s=tiles_per_s, d=d, dk=dk, dv=dv)
    pos_blk = lambda n, i: (jnp.maximum(i - tiles_p, 0) % tiles_per_s, 0)

    def parked(own):
        def tile(n, i):
            return jnp.where(n < own, 0, jnp.where(n > own, tiles_p - 1, jnp.minimum(i, tiles_p - 1)))
        return tile

    k_tile, v_tile = parked(1), parked(2)
    return pl.pallas_call(
        kern,
        grid=(3, t // tm),
        in_specs=[pl.BlockSpec((tm, d), lambda n, i: (i, 0)),
                  pl.BlockSpec(mod_l.shape, lambda n, i: (0, 0)),
                  pl.BlockSpec((d, tn), lambda n, i: (0, n)),
                  pl.BlockSpec((tm, LANES), pos_blk),
                  pl.BlockSpec((tm, LANES), pos_blk)],
        out_specs=[pl.BlockSpec((1, tm, tn), lambda n, i: (n, i, 0)),
                   pl.BlockSpec((1, 2, n_heads, tm, dk),
                                lambda n, i: (k_tile(n, i) // per_seq, 0, 0, k_tile(n, i) % per_seq, 0)),
                   pl.BlockSpec((1, n_heads, tm, dv),
                                lambda n, i: (v_tile(n, i) // per_seq, 0, v_tile(n, i) % per_seq, 0))],
        out_shape=[jax.ShapeDtypeStruct((3, t, tn), F32),
                   jax.ShapeDtypeStruct((bp, 2, n_heads, lp, dk), F32),
                   jax.ShapeDtypeStruct((bp, n_heads, lp, dv), F32)],
        compiler_params=_cparams("arbitrary", "arbitrary"),
        name="qkv_proj",
    )(x, mod_l, w_qkv, cos_t, sin_t)


def _attn_kernel(*refs, hp, dk, dv, scale, fold_scale, lam_init, has_cache):
    if has_cache:
        q1_ref, q2_ref, k1_ref, k2_ref, v_ref, ck_ref, cv_ref, lamv_ref, sg_ref, o_ref = refs
    else:
        q1_ref, q2_ref, k1_ref, k2_ref, v_ref, lamv_ref, sg_ref, o_ref = refs
        ck_ref = cv_ref = None
    lv = lamv_ref[...]
    lam = (jnp.exp(jnp.sum(lv[0:1] * lv[1:2], axis=-1, keepdims=True))
           - jnp.exp(jnp.sum(lv[2:3] * lv[3:4], axis=-1, keepdims=True)) + lam_init)
    nt = (((1,), (1,)), ((), ()))
    per_blk = LANES // dk

    def probs(q_ref, k_ref, m, head):
        cs = slice(head * dk, (head + 1) * dk)
        q = q_ref[:, cs]
        q = (q * scale).astype(BF16) if fold_scale else q.astype(BF16)
        s = lax.dot_general(q, k_ref[:, cs].astype(BF16), nt, preferred_element_type=F32)
        if not fold_scale:
            s = s * scale
        mx = jnp.max(s, axis=-1, keepdims=True)
        if has_cache:
            sc = lax.dot_general(q, ck_ref[0, 0, m, head].astype(BF16), nt, preferred_element_type=F32)
            if not fold_scale:
                sc = sc * scale
            mx = jnp.maximum(mx, jnp.max(sc, axis=-1, keepdims=True))
            ec = jnp.exp(sc - mx)
        e = jnp.exp(s - mx)
        den = jnp.sum(e, axis=-1, keepdims=True)
        if has_cache:
            den = den + jnp.sum(ec, axis=-1, keepdims=True)
            return e, ec, den
        return e, None, den

    for head in range(hp * per_blk):
        e1, e1c, d1 = probs(q1_ref, k1_ref, 0, head)
        e2, e2c, d2 = probs(q2_ref, k2_ref, 1, head)
        r1 = 1.0 / d1
        r2 = lam / d2
        vs = slice(head * dv, (head + 1) * dv)
        w = (e1 * r1 - e2 * r2).astype(BF16)
        o = jnp.dot(w, v_ref[:, vs].astype(BF16), preferred_element_type=F32)
        if has_cache:
            wc = (e1c * r1 - e2c * r2).astype(BF16)
            o = o + jnp.dot(wc, cv_ref[0, 0, head].astype(BF16), preferred_element_type=F32)
        o = o * lax.rsqrt(jnp.mean(jnp.square(o), axis=-1, keepdims=True) + LN_EPS)
        o = o * sg_ref[...] * (1.0 - lam_init)
        o_ref[:, vs] = o.astype(o_ref.dtype)


def _attn_t_kernel(*refs, heads_step, dk, dv, scale, fold_scale, lam_init, has_cache):
    if has_cache:
        q1_ref, q2_ref, k1_ref, k2_ref, v_ref, ck_ref, cv_ref, lamv_ref, sg_ref, o_ref = refs
    else:
        q1_ref, q2_ref, k1_ref, k2_ref, v_ref, lamv_ref, sg_ref, o_ref = refs
        ck_ref = cv_ref = None
    lv = lamv_ref[...]
    lam = (jnp.exp(jnp.sum(lv[0:1] * lv[1:2], axis=-1, keepdims=True))
           - jnp.exp(jnp.sum(lv[2:3] * lv[3:4], axis=-1, keepdims=True)) + lam_init)
    nt = (((1,), (1,)), ((), ()))
    tn = (((0,), (0,)), ((), ()))
    per_blk = LANES // dk
    tq = q1_ref.shape[0]
    lane = lax.broadcasted_iota(jnp.int32, (tq, LANES), 1)

    def exps(kb, qb, m, head, sub):
        qm = jnp.where((lane >= sub * dk) & (lane < (sub + 1) * dk), qb, 0.0).astype(BF16)
        s = lax.dot_general(kb, qm, nt, preferred_element_type=F32)
        if not fold_scale:
            s = s * scale
        mx = jnp.max(s, axis=0, keepdims=True)
        if has_cache:
            qc = qb[:, sub * dk:(sub + 1) * dk].astype(BF16)
            sc = lax.dot_general(ck_ref[0, 0, m, head].astype(BF16), qc, nt, preferred_element_type=F32)
            if not fold_scale:
                sc = sc * scale
            mx = jnp.maximum(mx, jnp.max(sc, axis=0, keepdims=True))
            ec = jnp.exp(sc - mx)
        e = jnp.exp(s - mx)
        den = jnp.sum(e, axis=0, keepdims=True)
        if has_cache:
            return e, ec, den + jnp.sum(ec, axis=0, keepdims=True)
        return e, None, den

    for blk in range(heads_step // per_blk):
        bs = slice(blk * LANES, (blk + 1) * LANES)
        k1b, k2b = k1_ref[:, bs].astype(BF16), k2_ref[:, bs].astype(BF16)
        q1b, q2b = q1_ref[:, bs], q2_ref[:, bs]
        if fold_scale:
            q1b, q2b = q1b * scale, q2b * scale
        for sub in range(per_blk):
            head = blk * per_blk + sub
            e1, e1c, d1 = exps(k1b, q1b, 0, head, sub)
            e2, e2c, d2 = exps(k2b, q2b, 1, head, sub)
            r1 = 1.0 / d1
            r2 = lam / d2
            vs = slice(head * dv, (head + 1) * dv)
            w = (e1 * r1 - e2 * r2).astype(BF16)
            o_t = lax.dot_general(v_ref[:, vs].astype(BF16), w, tn, preferred_element_type=F32)
            if has_cache:
                wc = (e1c * r1 - e2c * r2).astype(BF16)
                o_t = o_t + lax.dot_general(cv_ref[0, 0, head].astype(BF16), wc, tn, preferred_element_type=F32)
            o_t = o_t * lax.rsqrt(jnp.mean(jnp.square(o_t), axis=0, keepdims=True) + LN_EPS)
            o = o_t.T * sg_ref[...] * (1.0 - lam_init)
            o_ref[:, vs] = o.astype(o_ref.dtype)


def _attention(qkv, lamv, subln_g, *, row0, n_seq, lq, tq, hp, n_heads, dk, dv, lam_init, cache=None,
               keys_on_sublanes=False):
    per_blk = LANES // dk
    heads_step = hp * per_blk
    n_hblk = n_heads // heads_step
    map2 = n_heads * dk // (hp * LANES)
    scale = dk ** -0.5
    fold_scale = math.frexp(scale)[0] == 0.5
    qb0, kb0 = row0 // tq, row0 // lq
    n_q = lq // tq
    q_spec = lambda off: pl.BlockSpec((None, tq, hp * LANES), lambda b, h, qi: (0, qb0 + b * n_q + qi, off + h))
    k_spec = lambda off: pl.BlockSpec((None, lq, hp * LANES), lambda b, h, qi: (1, kb0 + b, off + h))
    in_specs = [q_spec(0), q_spec(map2), k_spec(0), k_spec(map2),
                pl.BlockSpec((None, lq, heads_step * dv), lambda b, h, qi: (2, kb0 + b, h))]
    args = [qkv, qkv, qkv, qkv, qkv]
    if cache is not None:
        cache_k, cache_v, o_i = cache
        past = cache_k.shape[-2]
        in_specs += [pl.BlockSpec((1, 1, 2, heads_step, past, dk), lambda b, h, qi: (b, o_i, 0, h, 0, 0)),
                     pl.BlockSpec((1, 1, heads_step, past, dv), lambda b, h, qi: (b, o_i, h, 0, 0))]
        args += [cache_k, cache_v]
    in_specs += [pl.BlockSpec(lamv.shape, lambda b, h, qi: (0, 0)),
                 pl.BlockSpec((1, dv), lambda b, h, qi: (0, 0))]
    args += [lamv, subln_g.reshape(1, dv)]
    if keys_on_sublanes:
        kern = functools.partial(_attn_t_kernel, heads_step=heads_step, dk=dk, dv=dv, scale=scale,
                                 fold_scale=fold_scale, lam_init=lam_init, has_cache=cache is not None)
    else:
        kern = functools.partial(_attn_kernel, hp=hp, dk=dk, dv=dv, scale=scale, fold_scale=fold_scale,
                                 lam_init=lam_init, has_cache=cache is not None)
    return pl.pallas_call(
        kern,
        grid=(n_seq, n_hblk, n_q),
        in_specs=in_specs,
        out_specs=pl.BlockSpec((tq, heads_step * dv), lambda b, h, qi: (b * n_q + qi, h)),
        out_shape=jax.ShapeDtypeStruct((n_seq * lq, n_heads * dv), BF16),
        compiler_params=_cparams("arbitrary", "arbitrary", "arbitrary"),
        name="diff_attn_cache" if cache is not None else "diff_attn",
    )(*args)


def _outc_kernel(x_ref, mod_ref, o_ref_in, w_ref, g_ref, b_ref, o_ref, *, tiles_p, tiles_per_s, d, alpha):
    grp = _group_of_tile(pl.program_id(0), tiles_p, tiles_per_s)
    out = jnp.dot(o_ref_in[...], w_ref[...], preferred_element_type=F32)
    gate = _mod_chunk(mod_ref, grp, 2, d)
    o_ref[...] = _post_residual(x_ref[...], out, gate, g_ref[...], b_ref[...], alpha)


def _outc(x, mod_l, o_n, w_out, ln_g, ln_b, tm, tiles_p, tiles_per_s, alpha):
    t, d = x.shape
    kin = o_n.shape[1]
    kern = functools.partial(_outc_kernel, tiles_p=tiles_p, tiles_per_s=tiles_per_s, d=d, alpha=alpha)
    ln_g, ln_b = ln_g.reshape(1, d), ln_b.reshape(1, d)
    full = lambda a: pl.BlockSpec(a.shape, lambda i: (0,) * a.ndim)
    return pl.pallas_call(
        kern,
        grid=(t // tm,),
        in_specs=[pl.BlockSpec((tm, d), lambda i: (i, 0)), full(mod_l), pl.BlockSpec((tm, kin), lambda i: (i, 0)),
                  full(w_out), full(ln_g), full(ln_b)],
        out_specs=pl.BlockSpec((tm, d), lambda i: (i, 0)),
        out_shape=jax.ShapeDtypeStruct((t, d), F32),
        compiler_params=_cparams("arbitrary"),
        name="outproj_c",
    )(x, mod_l, o_n, w_out, ln_g, ln_b)


def kernel(x_prompt, x_sample, state_s5_re, state_s5_im, cache_k, cache_v, c, c_ctx, w_mod, b_mod, ln_g, ln_b, w_in_ab, w_dw, b_dw, conv_ln_g, conv_ln_b, s5_lambda_re, s5_lambda_im, s5_log_dt, s5_b_re, s5_b_im, s5_c_re, s5_c_im, s5_d, w_glu, w_out_ab, w_qkv, lam_q1, lam_k1, lam_q2, lam_k2, subln_g, w_out_c, w_ff1, w_ff2):
    bp, lp, d = x_prompt.shape
    bs, ls, _ = x_sample.shape
    depth = w_mod.shape[0]
    tp, ts = bp * lp, bs * ls
    alpha = (2 * depth) ** 0.25
    assert 1 + bs <= MOD_ROWS

    tm = 256
    tm_mlp, tf = 512, 512
    assert tp % tm_mlp == 0 and ls % tm_mlp == 0 and tp % ls == 0

    x = jnp.concatenate([x_prompt.reshape(tp, d), x_sample.reshape(ts, d)], axis=0)
    cvec = jnp.zeros((MOD_ROWS, d), F32).at[0].set(c_ctx).at[1:1 + bs].set(c)
    mod = _modvec(cvec, w_mod, b_mod)

    g_ssm, n_ssm, p_ssm = s5_b_re.shape[2:]
    dk = lam_q1.shape[-1]
    dv = subln_g.shape[-1]
    n_heads = w_out_c.shape[1] // dv

    s_re, s_im, k_list, v_list = [], [], [], []
    for l in range(depth):
        mod_l = mod[l]
        if l % 2 == 0:
            e = l // 2
            ug, u = _inproj(x, mod_l, w_in_ab[e].astype(BF16), tm, tp // tm, ls // tm)
            y_conv = _conv_module(ug, w_dw[e], b_dw[e], conv_ln_g[e], conv_ln_b[e], min(lp, 256), lp, ls, tp)
            prep = _s5_prep(s5_lambda_re[e], s5_lambda_im[e], s5_log_dt[e],
                            s5_b_re[e], s5_b_im[e], s5_c_re[e], s5_c_im[e])
            a_re, a_im = prep[5], prep[6]
            gn = g_ssm * n_ssm
            nat = lambda a: a.reshape(2, g_ssm, p_ssm, n_ssm)[:, :, 0].reshape(2, gn)
            a4 = jnp.stack([nat(a_re)[0], nat(a_im)[0], nat(a_re)[1], nat(a_im)[1]])
            tile_k, tile_b = _s5_tiles(n_ssm, p_ssm)
            tok = min(4096, tp, ts)
            assert tok % lp == 0 and tok % ls == 0 and tp % tok == 0 and ts % tok == 0
            seq_p, seq_s, tiles_p5 = tok // lp, tok // ls, tp // tok
            ms = max(seq_p, seq_s)
            st = lambda a, dr: jnp.pad(a[:, e, dr].reshape(ts // tok, seq_s, gn), ((0, 0), (0, ms - seq_s), (0, 0)))
            h0_s = jnp.stack([st(state_s5_re, 0), st(state_s5_im, 0), st(state_s5_re, 1), st(state_s5_im, 1)], axis=1)
            h0 = jnp.concatenate([jnp.zeros((tiles_p5, 4, ms, gn), F32), h0_s], axis=0)
            y_scan, hf = _s5_chunked(u, h0, prep[:5], tile_k, tile_b, a4, tok=tok,
                                     geoms=((seq_p, lp // S5_T), (seq_s, ls // S5_T)), n_tiles_p=tiles_p5,
                                     n=n_ssm, p=p_ssm)
            hf = hf[:tiles_p5, :, :seq_p].transpose(1, 0, 2, 3).reshape(2, 2, bp, g_ssm, n_ssm)
            s_re.append(hf[:, 0].transpose(1, 0, 2, 3))
            s_im.append(hf[:, 1].transpose(1, 0, 2, 3))
            x = _outab(x, mod_l, y_conv, y_scan, u, s5_d[e], w_glu[e].astype(BF16),
                       w_out_ab[e].astype(BF16), ln_g[l, 0], ln_b[l, 0], tm, tp // tm, ls // tm, alpha)
        else:
            o_i = l // 2
            lam_init = 0.8 - 0.6 * math.exp(-0.3 * l)
            cos_t, sin_t = _rope_tables(ls, dk)
            qkv, k_new, v_new = _qkv(x, mod_l, w_qkv[o_i].astype(BF16), cos_t, sin_t, tm, tp // tm, ls // tm,
                                     dk, dv, bp, lp)
            lamv = jnp.stack([lam_q1[o_i], lam_k1[o_i], lam_q2[o_i], lam_k2[o_i]])
            geo = dict(n_heads=n_heads, dk=dk, dv=dv, lam_init=lam_init)
            o_p = _attention(qkv, lamv, subln_g[o_i], row0=0, n_seq=bp, lq=lp, tq=lp,
                             hp=n_heads * dk // LANES, keys_on_sublanes=True, **geo)
            o_s = _attention(qkv, lamv, subln_g[o_i], row0=tp, n_seq=bs, lq=ls, tq=256, hp=1,
                             cache=(cache_k, cache_v, o_i), **geo)
            o_n = jnp.concatenate([o_p, o_s], axis=0)
            k_list.append(k_new)
            v_list.append(v_new)
            x = _outc(x, mod_l, o_n, w_out_c[o_i].astype(BF16), ln_g[l, 0], ln_b[l, 0], tm, tp // tm, ls // tm, alpha)
        x = _mlp(x, mod_l, w_ff1[l].astype(BF16), w_ff2[l].astype(BF16), ln_g[l, 1], ln_b[l, 1],
                 tm_mlp, tf, tp // tm_mlp, ls // tm_mlp, alpha)

    return (x[:tp].reshape(bp, lp, d), x[tp:].reshape(bs, ls, d),
            jnp.stack(s_re, axis=1), jnp.stack(s_im, axis=1),
            jnp.stack(k_list, axis=1), jnp.stack(v_list, axis=1))
```

```python
import functools
import math

import jax
import jax.numpy as jnp
from jax import lax
from jax.experimental import pallas as pl
from jax.experimental.pallas import tpu as pltpu

F32 = jnp.float32
BF16 = jnp.bfloat16

LN_EPS = 1e-5
ROPE_BASE = 10000.0
LATENT_GRID_W = 64
MOD_ROWS = 8
V7X_VMEM_LIMIT = 56 * 1024 * 1024
LANES = 128
SUBLANES = 8


def _cparams(*sem):
    return pltpu.CompilerParams(dimension_semantics=sem, vmem_limit_bytes=V7X_VMEM_LIMIT)


def _layer_norm(z, g, b):
    mu = jnp.mean(z, axis=-1, keepdims=True)
    zc = z - mu
    var = jnp.mean(jnp.square(zc), axis=-1, keepdims=True)
    return zc * lax.rsqrt(var + LN_EPS) * g + b


def _group_of_tile(i, tiles_p, tiles_per_s):
    return jnp.where(i < tiles_p, 0, 1 + jnp.maximum(i - tiles_p, 0) // tiles_per_s)


def _mod_chunk(mod_ref, g, k, d):
    return mod_ref[pl.ds(g, 1), k * d:(k + 1) * d]


def _pair_specs(tm, width, tiles_p, grid_rank=1):
    if grid_rank == 1:
        return [pl.BlockSpec((tm, width), lambda i: (jnp.minimum(i, tiles_p - 1), 0)),
                pl.BlockSpec((tm, width), lambda i: (jnp.maximum(i - tiles_p, 0), 0))]
    return [pl.BlockSpec((tm, width), lambda i, f: (jnp.minimum(i, tiles_p - 1), 0)),
            pl.BlockSpec((tm, width), lambda i, f: (jnp.maximum(i - tiles_p, 0), 0))]


def _pair_rows(i, tiles_p, p_ref, s_ref):
    return jnp.where(i < tiles_p, p_ref[...], s_ref[...])


def _modvec_kernel(cv_ref, w_ref, b_ref, o_ref):
    cv = cv_ref[...]
    s = (cv * jax.nn.sigmoid(cv)).astype(BF16)
    o_ref[0] = jnp.dot(s, w_ref[0].astype(BF16), preferred_element_type=F32) + b_ref[0]


def _modvec(cvec, w_mod, b_mod, tn=1024):
    depth, d, n = w_mod.shape
    return pl.pallas_call(
        _modvec_kernel,
        grid=(depth, n // tn),
        in_specs=[pl.BlockSpec((MOD_ROWS, d), lambda l, j: (0, 0)),
                  pl.BlockSpec((1, d, tn), lambda l, j: (l, 0, j)),
                  pl.BlockSpec((1, 1, tn), lambda l, j: (l, 0, j))],
        out_specs=pl.BlockSpec((1, MOD_ROWS, tn), lambda l, j: (l, 0, j)),
        out_shape=jax.ShapeDtypeStruct((depth, MOD_ROWS, n), F32),
        compiler_params=_cparams("arbitrary", "arbitrary"),
        name="modvec",
    )(cvec, w_mod, b_mod.reshape(depth, 1, n))


def _inproj_kernel(xp_ref, xs_ref, mod_ref, w_ref, ug_ref, u_ref, *, tiles_p, tiles_per_s, d, c):
    i = pl.program_id(0)
    g = _group_of_tile(i, tiles_p, tiles_per_s)
    shift = _mod_chunk(mod_ref, g, 0, d)
    scale = _mod_chunk(mod_ref, g, 1, d)
    h = (_pair_rows(i, tiles_p, xp_ref, xs_ref) * (1 + scale) + shift).astype(BF16)
    a_val = jnp.dot(h, w_ref[:, 0:c], preferred_element_type=F32)
    a_gate = jnp.dot(h, w_ref[:, c:2 * c], preferred_element_type=F32)
    ug_ref[...] = a_val * jax.nn.sigmoid(a_gate)
    u_ref[...] = jnp.dot(h, w_ref[:, 2 * c:3 * c], preferred_element_type=F32)


def _inproj(xp, xs, mod_l, w_in, tm, tiles_p, tiles_per_s):
    t, d = xp.shape[0] + xs.shape[0], xp.shape[1]
    c = w_in.shape[1] // 3
    kern = functools.partial(_inproj_kernel, tiles_p=tiles_p, tiles_per_s=tiles_per_s, d=d, c=c)
    return pl.pallas_call(
        kern,
        grid=(t // tm,),
        in_specs=_pair_specs(tm, d, tiles_p) + [pl.BlockSpec(mod_l.shape, lambda i: (0, 0)),
                                                pl.BlockSpec(w_in.shape, lambda i: (0, 0))],
        out_specs=[pl.BlockSpec((tm, c), lambda i: (i, 0)),
                   pl.BlockSpec((tm, c), lambda i: (i, 0))],
        out_shape=[jax.ShapeDtypeStruct((t, c), F32), jax.ShapeDtypeStruct((t, c), F32)],
        compiler_params=_cparams("arbitrary"),
        name="inproj",
    )(xp, xs, mod_l, w_in)


CONV_HALO = 16
CONV_ROWS = 32
CONV_COLS = 256


def _conv_kernel(prev_ref, cur_ref, next_ref, w_ref, b_ref, g_ref, beta_ref, o_ref, pad_scr, sh_scr, conv_scr,
                 *, chunks_p, chunks_s, n_chunks_p, width, lc, c):
    i = pl.program_id(0)
    in_p = i < n_chunks_p
    k = jnp.where(in_p, i % chunks_p, jnp.maximum(i - n_chunks_p, 0) % chunks_s)
    last = jnp.where(in_p, chunks_p - 1, chunks_s - 1)
    has_prev = (k > 0).astype(F32)
    has_next = (k < last).astype(F32)
    pad_scr[0:CONV_HALO, :] = prev_ref[...] * has_prev
    pad_scr[CONV_HALO:CONV_HALO + lc, :] = cur_ref[...]
    pad_scr[CONV_HALO + lc:2 * CONV_HALO + lc, :] = next_ref[...] * has_next
    off = CONV_HALO - width // 2
    span = lc + CONV_HALO + SUBLANES
    for sft in range(SUBLANES):
        sh_scr[sft] = pad_scr[sft:sft + span, :]

    for r0 in range(0, lc, CONV_ROWS):
        for cb in range(c // CONV_COLS):
            cs = slice(cb * CONV_COLS, (cb + 1) * CONV_COLS)
            acc = jnp.zeros((CONV_ROWS, CONV_COLS), F32)
            for kk in range(width):
                whole, sft = divmod(kk + off, SUBLANES)
                base = r0 + whole * SUBLANES
                acc = acc + sh_scr[sft, base:base + CONV_ROWS, cs] * w_ref[kk:kk + 1, cs]
            conv_scr[r0:r0 + CONV_ROWS, cs] = acc + b_ref[:, cs]
    y = _layer_norm(conv_scr[...], g_ref[...], beta_ref[...])
    o_ref[...] = (y * jax.nn.sigmoid(y)).astype(o_ref.dtype)


def _conv_module(ug, w_dw, b_dw, ln_g, ln_b, lc, lp, ls, tp):
    t, c = ug.shape
    width = w_dw.shape[0]
    assert width // 2 < CONV_HALO and lc % CONV_HALO == 0 and lp % lc == 0 and ls % lc == 0
    hb = lc // CONV_HALO
    n_halo_blocks = t // CONV_HALO
    kern = functools.partial(_conv_kernel, chunks_p=lp // lc, chunks_s=ls // lc, n_chunks_p=tp // lc,
                             width=width, lc=lc, c=c)
    vec = lambda a: a.reshape(1, c)
    return pl.pallas_call(
        kern,
        grid=(t // lc,),
        in_specs=[pl.BlockSpec((CONV_HALO, c), lambda i: (jnp.maximum(i * hb - 1, 0), 0)),
                  pl.BlockSpec((lc, c), lambda i: (i, 0)),
                  pl.BlockSpec((CONV_HALO, c), lambda i: (jnp.minimum((i + 1) * hb, n_halo_blocks - 1), 0)),
                  pl.BlockSpec((width, c), lambda i: (0, 0)),
                  pl.BlockSpec((1, c), lambda i: (0, 0)),
                  pl.BlockSpec((1, c), lambda i: (0, 0)),
                  pl.BlockSpec((1, c), lambda i: (0, 0))],
        out_specs=pl.BlockSpec((lc, c), lambda i: (i, 0)),
        out_shape=jax.ShapeDtypeStruct((t, c), BF16),
        scratch_shapes=[pltpu.VMEM((lc + 2 * CONV_HALO, c), F32),
                        pltpu.VMEM((SUBLANES, lc + CONV_HALO + SUBLANES, c), F32), pltpu.VMEM((lc, c), F32)],
        compiler_params=_cparams("arbitrary"),
        name="conv_module",
    )(ug, ug, ug, w_dw, vec(b_dw), vec(ln_g), vec(ln_b))


S5_T = 16
S5_GB = 8


def _s5_prep_kernel(bt_re_ref, bt_im_ref, la_re_ref, la_im_ref, dta_ref, ct_re_ref, ct_im_ref, lb_re_ref, lb_im_ref,
                    dtb_ref, be_re_ref, be_im_ref, cs_re_ref, cs_ni_ref, kc_ref, a_re_ref, a_im_ref, *, n, p):
    t = S5_T
    fwd = pl.program_id(0) == 0

    lam_re, lam_im = la_re_ref[0, 0], la_im_ref[0, 0]
    dt = jnp.exp(dta_ref[0, 0])
    mag = jnp.exp(lam_re * dt)
    ar, ai = mag * jnp.cos(lam_im * dt), mag * jnp.sin(lam_im * dt)
    den = jnp.square(lam_re) + jnp.square(lam_im)
    coef_re = ((ar - 1) * lam_re + ai * lam_im) / den
    coef_im = (ai * lam_re - (ar - 1) * lam_im) / den
    b_re, b_im = bt_re_ref[0, 0], bt_im_ref[0, 0]
    bb_re = coef_re * b_re - coef_im * b_im
    bb_im = coef_re * b_im + coef_im * b_re
    rows = bb_re.shape[0]
    pw = []
    wr, wi = bb_re, bb_im
    pr, pi = jnp.ones_like(ar), jnp.zeros_like(ar)
    for k in range(t):
        pw.append((wr, wi))
        wr, wi = wr * ar - wi * ai, wr * ai + wi * ar
        pr, pi = pr * ar - pi * ai, pr * ai + pi * ar
    a_re_ref[0, 0] = pr
    a_im_ref[0, 0] = pi
    for j in range(t):
        be_re_ref[0, 0, j * rows:(j + 1) * rows, :] = jnp.where(fwd, pw[t - 1 - j][0], pw[j][0])
        be_im_ref[0, 0, j * rows:(j + 1) * rows, :] = jnp.where(fwd, pw[t - 1 - j][1], pw[j][1])

    ct_re, ct_im = ct_re_ref[0, 0], ct_im_ref[0, 0]
    shp = ct_re.shape
    lam_re = jnp.broadcast_to(lb_re_ref[0, 0], shp)
    lam_im = jnp.broadcast_to(lb_im_ref[0, 0], shp)
    dt = jnp.exp(jnp.broadcast_to(dtb_ref[0, 0], shp))
    blk = lax.broadcasted_iota(jnp.int32, shp, 1) // p
    k1 = jnp.where(fwd, blk + 1, t - blk).astype(F32)
    mag = jnp.exp(k1 * (lam_re * dt))
    ang = k1 * (lam_im * dt)
    qr, qi = mag * jnp.cos(ang), mag * jnp.sin(ang)
    v_re = ct_re * qr - ct_im * qi
    v_im = ct_re * qi + ct_im * qr
    cs_re_ref[0, 0] = v_re
    cs_ni_ref[0, 0] = -v_im
    lane = lax.broadcasted_iota(jnp.int32, shp, 1)
    w = shp[1]
    v0_re = jnp.where(fwd, jnp.where(lane < p, ct_re, pltpu.roll(v_re, p, 1)),
                      jnp.where(lane >= w - p, ct_re, pltpu.roll(v_re, w - p, 1)))
    v0_im = jnp.where(fwd, jnp.where(lane < p, ct_im, pltpu.roll(v_im, p, 1)),
                      jnp.where(lane >= w - p, ct_im, pltpu.roll(v_im, w - p, 1)))
    hi = lax.Precision.HIGHEST
    for g in range(S5_GB):
        ra, rb = slice(g * p, (g + 1) * p), slice(g * n, (g + 1) * n)
        kc_ref[0, 0, ra, :] = (jnp.dot(bb_re[ra], v0_re[rb], precision=hi, preferred_element_type=F32)
                               - jnp.dot(bb_im[ra], v0_im[rb], precision=hi, preferred_element_type=F32))


def _s5_prep(lam_re, lam_im, log_dt, b_re, b_im, c_re, c_im):
    _, g, n, p = b_re.shape
    t = S5_T
    nb = g // S5_GB
    ra, rb = S5_GB * p, S5_GB * n
    lay_a = lambda a: jnp.broadcast_to(a[:, :, None, :], (2, g, p, n)).reshape(2, nb, ra, n)
    lay_b = lambda a: a.reshape(2, nb, rb, 1)
    bt = lambda a: a.transpose(0, 1, 3, 2).reshape(2, nb, ra, n)
    ct = lambda a: jnp.broadcast_to(a.transpose(0, 1, 3, 2)[:, :, :, None, :], (2, g, n, t, p)).reshape(2, nb, rb, t * p)
    dt_g = jnp.broadcast_to(log_dt[:, :, None], (2, g, n))
    blk = lambda r, c: pl.BlockSpec((1, 1, r, c), lambda d, i: (d, i, 0, 0))
    shp = lambda r, c: jax.ShapeDtypeStruct((2, nb, r, c), F32)
    kern = functools.partial(_s5_prep_kernel, n=n, p=p)
    return pl.pallas_call(
        kern,
        grid=(2, nb),
        in_specs=[blk(ra, n)] * 5 + [blk(rb, t * p)] * 2 + [blk(rb, 1)] * 3,
        out_specs=[blk(t * ra, n), blk(t * ra, n), blk(rb, t * p), blk(rb, t * p), blk(ra, t * p), blk(ra, n), blk(ra, n)],
        out_shape=[shp(t * ra, n), shp(t * ra, n), shp(rb, t * p), shp(rb, t * p), shp(ra, t * p), shp(ra, n), shp(ra, n)],
        compiler_params=_cparams("arbitrary", "arbitrary"),
        name="s5_prep",
    )(bt(b_re), bt(b_im), lay_a(lam_re), lay_a(lam_im), lay_a(dt_g), ct(c_re), ct(c_im),
      lay_b(lam_re), lay_b(lam_im), lay_b(dt_g))


def _s5_expand(src, tile, row_div, row_mod, lane_div, lane_mod, precision=None):
    full = jnp.dot(src, tile, precision=precision, preferred_element_type=F32)
    r = lax.broadcasted_iota(jnp.int32, full.shape, 0) // row_div % row_mod
    l = lax.broadcasted_iota(jnp.int32, full.shape, 1) // lane_div % lane_mod
    return jnp.where(r == l, full, 0.0)


def _s5_chunk_kernel(u_ref, be_re_ref, be_im_ref, cs_re_ref, cs_ni_ref, kc_ref, tk_ref, tb_ref, a_ref, h0_ref,
                     y_ref, hf_ref, m8_scr, be8_scr, cs8_scr, e_scr, *, geoms, n_tiles_p, n, p):
    t, gb = S5_T, S5_GB
    cw = gb * p
    sw = gb * n
    rows = geoms[0][0] * geoms[0][1]
    tile_i = pl.program_id(1)

    @pl.when(tile_i == 0)
    def _():
        tk, tb = tk_ref[...].astype(BF16), tb_ref[...].astype(BF16)
        step = 4 * cw
        for r0 in range(0, t * cw, step):
            for part, (ref, d) in enumerate(((be_re_ref, 0), (be_im_ref, 0), (be_re_ref, 1), (be_im_ref, 1))):
                be8_scr[r0:r0 + step, part * sw:(part + 1) * sw] = _s5_expand(
                    ref[d, 0, r0:r0 + step, :].astype(BF16), tb, p, gb, n, gb).astype(BF16)
        for part, (ref, d) in enumerate(((cs_re_ref, 0), (cs_ni_ref, 0), (cs_re_ref, 1), (cs_ni_ref, 1))):
            cs8_scr[part * sw:(part + 1) * sw, :] = _s5_expand(ref[d, 0].astype(BF16), tk, n, gb, p, gb).astype(BF16)
        hi = lax.Precision.HIGHEST
        bd_f = _s5_expand(kc_ref[0, 0], tk_ref[...], p, gb, p, gb, hi)
        bd_r = _s5_expand(kc_ref[1, 0], tk_ref[...], p, gb, p, gb, hi)
        tile_f = lambda k: bd_f[:, k * cw:(k + 1) * cw]
        tile_r = lambda k: bd_r[:, (t - 1 - k) * cw:(t - k) * cw]
        for j in range(t):
            for i in range(t):
                blk = tile_f(i - j) if i > j else tile_r(j - i) if i < j else tile_f(0) + tile_r(0)
                m8_scr[j * cw:(j + 1) * cw, i * cw:(i + 1) * cw] = blk.astype(BF16)

    x = u_ref[...].reshape(rows, t * cw).astype(BF16)
    e = jnp.dot(x, be8_scr[...], preferred_element_type=F32)
    n_slab = e.shape[1] // LANES
    per_part = sw // LANES
    for k in range(n_slab):
        e_scr[k] = e[:, k * LANES:(k + 1) * LANES]

    def scan(nseq, nc):
        loops = [list(range(per_part))] if nseq < 8 else [[q] for q in range(per_part)]
        for prs in loops:
            coef = [[jnp.broadcast_to(a_ref[r:r + 1, q * LANES:(q + 1) * LANES], (nseq, LANES)) for r in range(4)]
                    for q in prs]
            init = tuple(tuple(h0_ref[0, r, 0:nseq, q * LANES:(q + 1) * LANES] for r in range(4)) for q in prs)

            def step(c, carry):
                out = []
                for idx, q in enumerate(prs):
                    sf_re, sf_im, sr_re, sr_im = carry[idx]
                    af_re, af_im, ar_re, ar_im = coef[idx]
                    at_f = pl.ds(c, nseq, stride=nc)
                    at_r = pl.ds(nc - 1 - c, nseq, stride=nc)
                    ef_re, ef_im = e_scr[q, at_f, :], e_scr[per_part + q, at_f, :]
                    er_re, er_im = e_scr[2 * per_part + q, at_r, :], e_scr[3 * per_part + q, at_r, :]
                    e_scr[q, at_f, :] = sf_re
                    e_scr[per_part + q, at_f, :] = sf_im
                    e_scr[2 * per_part + q, at_r, :] = sr_re
                    e_scr[3 * per_part + q, at_r, :] = sr_im
                    out.append((af_re * sf_re - af_im * sf_im + ef_re, af_re * sf_im + af_im * sf_re + ef_im,
                                ar_re * sr_re - ar_im * sr_im + er_re, ar_re * sr_im + ar_im * sr_re + er_im))
                return tuple(out)

            fin = lax.fori_loop(0, nc, step, init)
            for idx, q in enumerate(prs):
                for r in range(4):
                    hf_ref[0, r, 0:nseq, q * LANES:(q + 1) * LANES] = fin[idx][r]

    hf_ref[...] = jnp.zeros_like(hf_ref)

    @pl.when(tile_i < n_tiles_p)
    def _():
        scan(*geoms[0])

    @pl.when(tile_i >= n_tiles_p)
    def _():
        scan(*geoms[1])

    s = jnp.concatenate([e_scr[k] for k in range(n_slab)], axis=-1).astype(BF16)
    y = (jnp.dot(x, m8_scr[...], preferred_element_type=F32)
         + jnp.dot(s, cs8_scr[...], preferred_element_type=F32))
    y_ref[...] = y.reshape(rows * t, cw)


def _s5_chunked(u, h0, prep, tile_k, tile_b, a4, *, tok, geoms, n_tiles_p, n, p):
    be_re, be_im, cs_re, cs_ni, kc = prep
    t, gb = S5_T, S5_GB
    cw, sw = gb * p, gb * n
    nb = u.shape[1] // cw
    n_tiles = u.shape[0] // tok
    ms = h0.shape[2]
    both = lambda a: pl.BlockSpec((2, 1) + a.shape[2:], lambda b, i: (0, b, 0, 0), pipeline_mode=pl.Buffered(1))
    const = lambda a: pl.BlockSpec(a.shape, lambda b, i: (0, 0), pipeline_mode=pl.Buffered(1))
    kern = functools.partial(_s5_chunk_kernel, geoms=geoms, n_tiles_p=n_tiles_p, n=n, p=p)
    return pl.pallas_call(
        kern,
        grid=(nb, n_tiles),
        in_specs=[pl.BlockSpec((tok, cw), lambda b, i: (i, b)),
                  both(be_re), both(be_im), both(cs_re), both(cs_ni), both(kc), const(tile_k), const(tile_b),
                  pl.BlockSpec((4, sw), lambda b, i: (0, b)),
                  pl.BlockSpec((1, 4, ms, sw), lambda b, i: (i, 0, 0, b))],
        out_specs=[pl.BlockSpec((tok, cw), lambda b, i: (i, b)),
                   pl.BlockSpec((1, 4, ms, sw), lambda b, i: (i, 0, 0, b))],
        out_shape=[jax.ShapeDtypeStruct(u.shape, F32), jax.ShapeDtypeStruct(h0.shape, F32)],
        scratch_shapes=[pltpu.VMEM((t * cw, t * cw), BF16), pltpu.VMEM((t * cw, 4 * sw), BF16),
                        pltpu.VMEM((4 * sw, t * cw), BF16), pltpu.VMEM((4 * sw // LANES, tok // t, LANES), F32)],
        compiler_params=_cparams("arbitrary", "arbitrary"),
        name="s5_chunked",
    )(u, be_re, be_im, cs_re, cs_ni, kc, tile_k, tile_b, a4, h0)


def _s5_tiles(n, p):
    t, gb = S5_T, S5_GB
    eye = lambda k: jnp.eye(k, dtype=F32)
    tile_k = jnp.einsum("ab,pq->apbq", eye(t), eye(p))[:, :, :, None, :] * jnp.ones((1, 1, 1, gb, 1), F32)
    tile_b = eye(n)[:, None, :] * jnp.ones((1, gb, 1), F32)
    return tile_k.reshape(t * p, t * gb * p), tile_b.reshape(n, gb * n)


def _post_residual(x, y, gate, g, b, alpha):
    return _layer_norm(alpha * x + gate * y, g, b)


def _outab_kernel(xp_ref, xs_ref, mod_ref, yc_ref, ys_ref, u_ref, dsk_ref, wglu_ref, wout_ref, g_ref, b_ref, o_ref,
                  *, tiles_p, tiles_per_s, d, c, alpha):
    i = pl.program_id(0)
    grp = _group_of_tile(i, tiles_p, tiles_per_s)
    y_s = ys_ref[...] + dsk_ref[...] * u_ref[...]
    y_s = jax.nn.gelu(y_s)
    z = jnp.dot(y_s.astype(BF16), wglu_ref[...], preferred_element_type=F32)
    y_ssm = y_s * jax.nn.sigmoid(z)
    out = (jnp.dot(yc_ref[...], wout_ref[0:c, :], preferred_element_type=F32)
           + jnp.dot(y_ssm.astype(BF16), wout_ref[c:2 * c, :], preferred_element_type=F32))
    gate = _mod_chunk(mod_ref, grp, 2, d)
    o_ref[...] = _post_residual(_pair_rows(i, tiles_p, xp_ref, xs_ref), out, gate, g_ref[...], b_ref[...], alpha)


def _outab(xp, xs, mod_l, y_conv, y_scan, u, d_skip, w_glu, w_out, ln_g, ln_b, tm, tiles_p, tiles_per_s, alpha):
    t, d = xp.shape[0] + xs.shape[0], xp.shape[1]
    c = u.shape[1]
    kern = functools.partial(_outab_kernel, tiles_p=tiles_p, tiles_per_s=tiles_per_s, d=d, c=c, alpha=alpha)
    row = lambda w: pl.BlockSpec((tm, w), lambda i: (i, 0))
    full = lambda a: pl.BlockSpec(a.shape, lambda i: (0,) * a.ndim)
    d_skip, ln_g, ln_b = d_skip.reshape(1, c), ln_g.reshape(1, d), ln_b.reshape(1, d)
    return pl.pallas_call(
        kern,
        grid=(t // tm,),
        in_specs=_pair_specs(tm, d, tiles_p) + [full(mod_l), row(c), row(c), row(c), full(d_skip), full(w_glu),
                                                full(w_out), full(ln_g), full(ln_b)],
        out_specs=row(d),
        out_shape=jax.ShapeDtypeStruct((t, d), F32),
        compiler_params=_cparams("arbitrary"),
        name="outproj_ab",
    )(xp, xs, mod_l, y_conv, y_scan, u, d_skip, w_glu, w_out, ln_g, ln_b)


def _mlp_kernel(x_ref, mod_ref, w1_ref, w2_ref, g_ref, b_ref, *rest, tiles_p, tiles_per_s, d, n_f, alpha, split):
    outs, (h_scr, acc_scr) = rest[:-2], rest[-2:]
    i = pl.program_id(0)
    f = pl.program_id(1)
    grp = _group_of_tile(i, tiles_p, tiles_per_s)

    @pl.when(f == 0)
    def _():
        shift = _mod_chunk(mod_ref, grp, 3, d)
        scale = _mod_chunk(mod_ref, grp, 4, d)
        h_scr[...] = (x_ref[...] * (1 + scale) + shift).astype(BF16)
        acc_scr[...] = jnp.zeros_like(acc_scr)

    a = jnp.dot(h_scr[...], w1_ref[...], preferred_element_type=F32)
    a = jnp.square(jnp.maximum(a, 0.0)).astype(BF16)
    acc_scr[...] += jnp.dot(a, w2_ref[...], preferred_element_type=F32)

    @pl.when(f == n_f - 1)
    def _():
        gate = _mod_chunk(mod_ref, grp, 5, d)
        res = _post_residual(x_ref[...], acc_scr[...], gate, g_ref[...], b_ref[...], alpha)
        if split:
            @pl.when(i < tiles_p)
            def _():
                outs[0][...] = res

            @pl.when(i >= tiles_p)
            def _():
                outs[1][...] = res
        else:
            outs[0][...] = res


def _mlp(x, mod_l, w1, w2, ln_g, ln_b, tm, tf, tiles_p, tiles_per_s, alpha, split):
    t, d = x.shape
    n_f = w1.shape[1] // tf
    kern = functools.partial(_mlp_kernel, tiles_p=tiles_p, tiles_per_s=tiles_per_s, d=d, n_f=n_f, alpha=alpha,
                             split=split)
    ln_g, ln_b = ln_g.reshape(1, d), ln_b.reshape(1, d)
    if split:
        out_specs = _pair_specs(tm, d, tiles_p, grid_rank=2)
        out_shape = [jax.ShapeDtypeStruct((tiles_p * tm, d), F32), jax.ShapeDtypeStruct((t - tiles_p * tm, d), F32)]
    else:
        out_specs = [pl.BlockSpec((tm, d), lambda i, f: (i, 0))]
        out_shape = [jax.ShapeDtypeStruct((t, d), F32)]
    return pl.pallas_call(
        kern,
        grid=(t // tm, n_f),
        in_specs=[pl.BlockSpec((tm, d), lambda i, f: (i, 0)),
                  pl.BlockSpec(mod_l.shape, lambda i, f: (0, 0)),
                  pl.BlockSpec((d, tf), lambda i, f: (0, f)),
                  pl.BlockSpec((tf, d), lambda i, f: (f, 0)),
                  pl.BlockSpec((1, d), lambda i, f: (0, 0)),
                  pl.BlockSpec((1, d), lambda i, f: (0, 0))],
        out_specs=out_specs,
        out_shape=out_shape,
        scratch_shapes=[pltpu.VMEM((tm, d), BF16), pltpu.VMEM((tm, d), F32)],
        compiler_params=_cparams("arbitrary", "arbitrary"),
        name="mlp",
    )(x, mod_l, w1, w2, ln_g, ln_b)


def _rope_tables(n_pos, dk):
    ax = dk // 2
    half = ax // 2
    freqs = ROPE_BASE ** (-jnp.arange(half, dtype=F32) / half)
    pos = jnp.arange(n_pos)
    row = (pos // LATENT_GRID_W).astype(F32)
    col = (pos % LATENT_GRID_W).astype(F32)
    ang_r, ang_c = row[:, None] * freqs, col[:, None] * freqs
    cos = jnp.concatenate([jnp.cos(ang_r)] * 2 + [jnp.cos(ang_c)] * 2, axis=-1)
    sin = jnp.concatenate([-jnp.sin(ang_r), jnp.sin(ang_r), -jnp.sin(ang_c), jnp.sin(ang_c)], axis=-1)
    rep = LANES // dk
    return jnp.tile(cos, (1, rep)), jnp.tile(sin, (1, rep))


def _qkv_kernel(x_ref, mod_ref, w_ref, cos_ref, sin_ref, o_ref, kc_ref, vc_ref,
                *, tiles_p, tiles_per_s, d, dk, dv):
    n = pl.program_id(0)
    i = pl.program_id(1)
    g = _group_of_tile(i, tiles_p, tiles_per_s)
    shift = _mod_chunk(mod_ref, g, 0, d)
    scale = _mod_chunk(mod_ref, g, 1, d)
    h = (x_ref[...] * (1 + scale) + shift).astype(BF16)
    y = jnp.dot(h, w_ref[...], preferred_element_type=F32)
    rotate = jnp.logical_and(n < 2, i >= tiles_p)
    quarter = dk // 4

    @pl.when(rotate)
    def _():
        cos, sin = cos_ref[...], sin_ref[...]
        lane = lax.broadcasted_iota(jnp.int32, cos.shape, 1)
        first = (lane % (2 * quarter)) < quarter
        for cb in range(y.shape[1] // LANES):
            yb = y[:, cb * LANES:(cb + 1) * LANES]
            partner = jnp.where(first, pltpu.roll(yb, LANES - quarter, 1), pltpu.roll(yb, quarter, 1))
            o_ref[0, :, cb * LANES:(cb + 1) * LANES] = yb * cos + partner * sin

    @pl.when(jnp.logical_not(rotate))
    def _():
        o_ref[0] = y

    @pl.when(jnp.logical_and(n == 1, i < tiles_p))
    def _():
        n_heads = kc_ref.shape[2]
        for m in range(2):
            for hh in range(n_heads):
                c0 = (m * n_heads + hh) * dk
                kc_ref[0, m, hh] = y[:, c0:c0 + dk]

    @pl.when(jnp.logical_and(n == 2, i < tiles_p))
    def _():
        for hh in range(vc_ref.shape[1]):
            vc_ref[0, hh] = y[:, hh * dv:(hh + 1) * dv]


def _qkv(x, mod_l, w_qkv, cos_t, sin_t, tm, tiles_p, tiles_per_s, dk, dv, bp, lp):
    t, d = x.shape
    n_out = w_qkv.shape[1]
    assert n_out % 3 == 0 and lp % tm == 0
    tn = n_out // 3
    n_heads = tn // dv
    per_seq = lp // tm
    kern = functools.partial(_qkv_kernel, tiles_p=tiles_p, tiles_per_s=tiles_per_s, d=d, dk=dk, dv=dv)
    pos_blk = lambda n, i: (jnp.maximum(i - tiles_p, 0) % tiles_per_s, 0)

    def parked(own):
        def tile(n, i):
            return jnp.where(n < own, 0, jnp.where(n > own, tiles_p - 1, jnp.minimum(i, tiles_p - 1)))
        return tile

    k_tile, v_tile = parked(1), parked(2)
    return pl.pallas_call(
        kern,
        grid=(3, t // tm),
        in_specs=[pl.BlockSpec((tm, d), lambda n, i: (i, 0)),
                  pl.BlockSpec(mod_l.shape, lambda n, i: (0, 0)),
                  pl.BlockSpec((d, tn), lambda n, i: (0, n)),
                  pl.BlockSpec((tm, LANES), pos_blk),
                  pl.BlockSpec((tm, LANES), pos_blk)],
        out_specs=[pl.BlockSpec((1, tm, tn), lambda n, i: (n, i, 0)),
                   pl.BlockSpec((1, 2, n_heads, tm, dk),
                                lambda n, i: (k_tile(n, i) // per_seq, 0, 0, k_tile(n, i) % per_seq, 0)),
                   pl.BlockSpec((1, n_heads, tm, dv),
                                lambda n, i: (v_tile(n, i) // per_seq, 0, v_tile(n, i) % per_seq, 0))],
        out_shape=[jax.ShapeDtypeStruct((3, t, tn), F32),
                   jax.ShapeDtypeStruct((bp, 2, n_heads, lp, dk), F32),
                   jax.ShapeDtypeStruct((bp, n_heads, lp, dv), F32)],
        compiler_params=_cparams("arbitrary", "arbitrary"),
        name="qkv_proj",
    )(x, mod_l, w_qkv, cos_t, sin_t)


def _attn_kernel(*refs, hp, dk, dv, scale, fold_scale, lam_init, has_cache):
    if has_cache:
        q1_ref, q2_ref, k1_ref, k2_ref, v_ref, ck_ref, cv_ref, lamv_ref, sg_ref, o_ref = refs
    else:
        q1_ref, q2_ref, k1_ref, k2_ref, v_ref, lamv_ref, sg_ref, o_ref = refs
        ck_ref = cv_ref = None
    lv = lamv_ref[...]
    lam = (jnp.exp(jnp.sum(lv[0:1] * lv[1:2], axis=-1, keepdims=True))
           - jnp.exp(jnp.sum(lv[2:3] * lv[3:4], axis=-1, keepdims=True)) + lam_init)
    nt = (((1,), (1,)), ((), ()))
    per_blk = LANES // dk

    def probs(q_ref, k_ref, m, head):
        cs = slice(head * dk, (head + 1) * dk)
        q = q_ref[:, cs]
        q = (q * scale).astype(BF16) if fold_scale else q.astype(BF16)
        s = lax.dot_general(q, k_ref[:, cs].astype(BF16), nt, preferred_element_type=F32)
        if not fold_scale:
            s = s * scale
        mx = jnp.max(s, axis=-1, keepdims=True)
        if has_cache:
            sc = lax.dot_general(q, ck_ref[0, 0, m, head].astype(BF16), nt, preferred_element_type=F32)
            if not fold_scale:
                sc = sc * scale
            mx = jnp.maximum(mx, jnp.max(sc, axis=-1, keepdims=True))
            ec = jnp.exp(sc - mx)
        e = jnp.exp(s - mx)
        den = jnp.sum(e, axis=-1, keepdims=True)
        if has_cache:
            den = den + jnp.sum(ec, axis=-1, keepdims=True)
            return e, ec, den
        return e, None, den

    for head in range(hp * per_blk):
        e1, e1c, d1 = probs(q1_ref, k1_ref, 0, head)
        e2, e2c, d2 = probs(q2_ref, k2_ref, 1, head)
        r1 = 1.0 / d1
        r2 = lam / d2
        vs = slice(head * dv, (head + 1) * dv)
        w = (e1 * r1 - e2 * r2).astype(BF16)
        o = jnp.dot(w, v_ref[:, vs].astype(BF16), preferred_element_type=F32)
        if has_cache:
            wc = (e1c * r1 - e2c * r2).astype(BF16)
            o = o + jnp.dot(wc, cv_ref[0, 0, head].astype(BF16), preferred_element_type=F32)
        o = o * lax.rsqrt(jnp.mean(jnp.square(o), axis=-1, keepdims=True) + LN_EPS)
        o = o * sg_ref[...] * (1.0 - lam_init)
        o_ref[:, vs] = o.astype(o_ref.dtype)


def _attn_t_kernel(*refs, heads_step, dk, dv, scale, fold_scale, lam_init, has_cache):
    if has_cache:
        q1_ref, q2_ref, k1_ref, k2_ref, v_ref, ck_ref, cv_ref, lamv_ref, sg_ref, o_ref = refs
    else:
        q1_ref, q2_ref, k1_ref, k2_ref, v_ref, lamv_ref, sg_ref, o_ref = refs
        ck_ref = cv_ref = None
    lv = lamv_ref[...]
    lam = (jnp.exp(jnp.sum(lv[0:1] * lv[1:2], axis=-1, keepdims=True))
           - jnp.exp(jnp.sum(lv[2:3] * lv[3:4], axis=-1, keepdims=True)) + lam_init)
    nt = (((1,), (1,)), ((), ()))
    tn = (((0,), (0,)), ((), ()))
    per_blk = LANES // dk
    tq = q1_ref.shape[0]
    lane = lax.broadcasted_iota(jnp.int32, (tq, LANES), 1)

    def exps(kb, qb, m, head, sub):
        qm = jnp.where((lane >= sub * dk) & (lane < (sub + 1) * dk), qb, 0.0).astype(BF16)
        s = lax.dot_general(kb, qm, nt, preferred_element_type=F32)
        if not fold_scale:
            s = s * scale
        mx = jnp.max(s, axis=0, keepdims=True)
        if has_cache:
            qc = qb[:, sub * dk:(sub + 1) * dk].astype(BF16)
            sc = lax.dot_general(ck_ref[0, 0, m, head].astype(BF16), qc, nt, preferred_element_type=F32)
            if not fold_scale:
                sc = sc * scale
            mx = jnp.maximum(mx, jnp.max(sc, axis=0, keepdims=True))
            ec = jnp.exp(sc - mx)
        e = jnp.exp(s - mx)
        den = jnp.sum(e, axis=0, keepdims=True)
        if has_cache:
            return e, ec, den + jnp.sum(ec, axis=0, keepdims=True)
        return e, None, den

    for blk in range(heads_step // per_blk):
        bs = slice(blk * LANES, (blk + 1) * LANES)
        k1b, k2b = k1_ref[:, bs].astype(BF16), k2_ref[:, bs].astype(BF16)
        q1b, q2b = q1_ref[:, bs], q2_ref[:, bs]
        if fold_scale:
            q1b, q2b = q1b * scale, q2b * scale
        for sub in range(per_blk):
            head = blk * per_blk + sub
            e1, e1c, d1 = exps(k1b, q1b, 0, head, sub)
            e2, e2c, d2 = exps(k2b, q2b, 1, head, sub)
            r1 = 1.0 / d1
            r2 = lam / d2
            vs = slice(head * dv, (head + 1) * dv)
            w = (e1 * r1 - e2 * r2).astype(BF16)
            o_t = lax.dot_general(v_ref[:, vs].astype(BF16), w, tn, preferred_element_type=F32)
            if has_cache:
                wc = (e1c * r1 - e2c * r2).astype(BF16)
                o_t = o_t + lax.dot_general(cv_ref[0, 0, head].astype(BF16), wc, tn, preferred_element_type=F32)
            o_t = o_t * lax.rsqrt(jnp.mean(jnp.square(o_t), axis=0, keepdims=True) + LN_EPS)
            o = o_t.T * sg_ref[...] * (1.0 - lam_init)
            o_ref[:, vs] = o.astype(o_ref.dtype)


def _attention(qkv, lamv, subln_g, *, row0, n_seq, lq, tq, hp, n_heads, dk, dv, lam_init, cache=None,
               keys_on_sublanes=False):
    per_blk = LANES // dk
    heads_step = hp * per_blk
    n_hblk = n_heads // heads_step
    map2 = n_heads * dk // (hp * LANES)
    scale = dk ** -0.5
    fold_scale = math.frexp(scale)[0] == 0.5
    qb0, kb0 = row0 // tq, row0 // lq
    n_q = lq // tq
    q_spec = lambda off: pl.BlockSpec((None, tq, hp * LANES), lambda b, h, qi: (0, qb0 + b * n_q + qi, off + h))
    k_spec = lambda off: pl.BlockSpec((None, lq, hp * LANES), lambda b, h, qi: (1, kb0 + b, off + h))
    in_specs = [q_spec(0), q_spec(map2), k_spec(0), k_spec(map2),
                pl.BlockSpec((None, lq, heads_step * dv), lambda b, h, qi: (2, kb0 + b, h))]
    args = [qkv, qkv, qkv, qkv, qkv]
    if cache is not None:
        cache_k, cache_v, o_i = cache
        past = cache_k.shape[-2]
        in_specs += [pl.BlockSpec((1, 1, 2, heads_step, past, dk), lambda b, h, qi: (b, o_i, 0, h, 0, 0)),
                     pl.BlockSpec((1, 1, heads_step, past, dv), lambda b, h, qi: (b, o_i, h, 0, 0))]
        args += [cache_k, cache_v]
    in_specs += [pl.BlockSpec(lamv.shape, lambda b, h, qi: (0, 0)),
                 pl.BlockSpec((1, dv), lambda b, h, qi: (0, 0))]
    args += [lamv, subln_g.reshape(1, dv)]
    if keys_on_sublanes:
        kern = functools.partial(_attn_t_kernel, heads_step=heads_step, dk=dk, dv=dv, scale=scale,
                                 fold_scale=fold_scale, lam_init=lam_init, has_cache=cache is not None)
    else:
        kern = functools.partial(_attn_kernel, hp=hp, dk=dk, dv=dv, scale=scale, fold_scale=fold_scale,
                                 lam_init=lam_init, has_cache=cache is not None)
    return pl.pallas_call(
        kern,
        grid=(n_seq, n_hblk, n_q),
        in_specs=in_specs,
        out_specs=pl.BlockSpec((tq, heads_step * dv), lambda b, h, qi: (b * n_q + qi, h)),
        out_shape=jax.ShapeDtypeStruct((n_seq * lq, n_heads * dv), BF16),
        compiler_params=_cparams("arbitrary", "arbitrary", "arbitrary"),
        name="diff_attn_cache" if cache is not None else "diff_attn",
    )(*args)


def _outc_kernel(x_ref, mod_ref, op_ref, os_ref, w_ref, g_ref, b_ref, o_ref, *, tiles_p, tiles_per_s, d, alpha):
    i = pl.program_id(0)
    grp = _group_of_tile(i, tiles_p, tiles_per_s)
    out = jnp.dot(_pair_rows(i, tiles_p, op_ref, os_ref), w_ref[...], preferred_element_type=F32)
    gate = _mod_chunk(mod_ref, grp, 2, d)
    o_ref[...] = _post_residual(x_ref[...], out, gate, g_ref[...], b_ref[...], alpha)


def _outc(x, mod_l, o_p, o_s, w_out, ln_g, ln_b, tm, tiles_p, tiles_per_s, alpha):
    t, d = x.shape
    kin = o_p.shape[1]
    kern = functools.partial(_outc_kernel, tiles_p=tiles_p, tiles_per_s=tiles_per_s, d=d, alpha=alpha)
    ln_g, ln_b = ln_g.reshape(1, d), ln_b.reshape(1, d)
    full = lambda a: pl.BlockSpec(a.shape, lambda i: (0,) * a.ndim)
    return pl.pallas_call(
        kern,
        grid=(t // tm,),
        in_specs=[pl.BlockSpec((tm, d), lambda i: (i, 0)), full(mod_l)] + _pair_specs(tm, kin, tiles_p)
        + [full(w_out), full(ln_g), full(ln_b)],
        out_specs=pl.BlockSpec((tm, d), lambda i: (i, 0)),
        out_shape=jax.ShapeDtypeStruct((t, d), F32),
        compiler_params=_cparams("arbitrary"),
        name="outproj_c",
    )(x, mod_l, o_p, o_s, w_out, ln_g, ln_b)


def kernel(x_prompt, x_sample, state_s5_re, state_s5_im, cache_k, cache_v, c, c_ctx, w_mod, b_mod, ln_g, ln_b, w_in_ab, w_dw, b_dw, conv_ln_g, conv_ln_b, s5_lambda_re, s5_lambda_im, s5_log_dt, s5_b_re, s5_b_im, s5_c_re, s5_c_im, s5_d, w_glu, w_out_ab, w_qkv, lam_q1, lam_k1, lam_q2, lam_k2, subln_g, w_out_c, w_ff1, w_ff2):
    bp, lp, d = x_prompt.shape
    bs, ls, _ = x_sample.shape
    depth = w_mod.shape[0]
    tp, ts = bp * lp, bs * ls
    alpha = (2 * depth) ** 0.25
    assert 1 + bs <= MOD_ROWS

    tm = 256
    tm_mlp, tf = 512, 1024
    assert tp % tm_mlp == 0 and ls % tm_mlp == 0 and tp % ls == 0

    xp, xs = x_prompt.reshape(tp, d), x_sample.reshape(ts, d)
    cvec = jnp.zeros((MOD_ROWS, d), F32).at[0].set(c_ctx).at[1:1 + bs].set(c)
    mod = _modvec(cvec, w_mod, b_mod)

    g_ssm, n_ssm, p_ssm = s5_b_re.shape[2:]
    dk = lam_q1.shape[-1]
    dv = subln_g.shape[-1]
    n_heads = w_out_c.shape[1] // dv

    s_re, s_im, k_list, v_list = [], [], [], []
    for l in range(depth):
        mod_l = mod[l]
        if l % 2 == 0:
            e = l // 2
            if l > 0:
                xp, xs = x[:tp], x[tp:]
            ug, u = _inproj(xp, xs, mod_l, w_in_ab[e].astype(BF16), tm, tp // tm, ls // tm)
            y_conv = _conv_module(ug, w_dw[e], b_dw[e], conv_ln_g[e], conv_ln_b[e], min(lp, 256), lp, ls, tp)
            prep = _s5_prep(s5_lambda_re[e], s5_lambda_im[e], s5_log_dt[e],
                            s5_b_re[e], s5_b_im[e], s5_c_re[e], s5_c_im[e])
            a_re, a_im = prep[5], prep[6]
            gn = g_ssm * n_ssm
            nat = lambda a: a.reshape(2, g_ssm, p_ssm, n_ssm)[:, :, 0].reshape(2, gn)
            a4 = jnp.stack([nat(a_re)[0], nat(a_im)[0], nat(a_re)[1], nat(a_im)[1]])
            tile_k, tile_b = _s5_tiles(n_ssm, p_ssm)
            tok = min(4096, tp, ts)
            assert tok % lp == 0 and tok % ls == 0 and tp % tok == 0 and ts % tok == 0
            seq_p, seq_s, tiles_p5 = tok // lp, tok // ls, tp // tok
            ms = max(seq_p, seq_s)
            st = lambda a, dr: jnp.pad(a[:, e, dr].reshape(ts // tok, seq_s, gn), ((0, 0), (0, ms - seq_s), (0, 0)))
            h0_s = jnp.stack([st(state_s5_re, 0), st(state_s5_im, 0), st(state_s5_re, 1), st(state_s5_im, 1)], axis=1)
            h0 = jnp.concatenate([jnp.zeros((tiles_p5, 4, ms, gn), F32), h0_s], axis=0)
            y_scan, hf = _s5_chunked(u, h0, prep[:5], tile_k, tile_b, a4, tok=tok,
                                     geoms=((seq_p, lp // S5_T), (seq_s, ls // S5_T)), n_tiles_p=tiles_p5,
                                     n=n_ssm, p=p_ssm)
            hf = hf[:tiles_p5, :, :seq_p].transpose(1, 0, 2, 3).reshape(2, 2, bp, g_ssm, n_ssm)
            s_re.append(hf[:, 0].transpose(1, 0, 2, 3))
            s_im.append(hf[:, 1].transpose(1, 0, 2, 3))
            x = _outab(xp, xs, mod_l, y_conv, y_scan, u, s5_d[e], w_glu[e].astype(BF16),
                       w_out_ab[e].astype(BF16), ln_g[l, 0], ln_b[l, 0], tm, tp // tm, ls // tm, alpha)
        else:
            o_i = l // 2
            lam_init = 0.8 - 0.6 * math.exp(-0.3 * l)
            cos_t, sin_t = _rope_tables(ls, dk)
            qkv, k_new, v_new = _qkv(x, mod_l, w_qkv[o_i].astype(BF16), cos_t, sin_t, tm, tp // tm, ls // tm,
                                     dk, dv, bp, lp)
            lamv = jnp.stack([lam_q1[o_i], lam_k1[o_i], lam_q2[o_i], lam_k2[o_i]])
            geo = dict(n_heads=n_heads, dk=dk, dv=dv, lam_init=lam_init)
            o_p = _attention(qkv, lamv, subln_g[o_i], row0=0, n_seq=bp, lq=lp, tq=lp,
                             hp=n_heads * dk // LANES, keys_on_sublanes=True, **geo)
            o_s = _attention(qkv, lamv, subln_g[o_i], row0=tp, n_seq=bs, lq=ls, tq=256, hp=1,
                             cache=(cache_k, cache_v, o_i), **geo)
            k_list.append(k_new)
            v_list.append(v_new)
            x = _outc(x, mod_l, o_p, o_s, w_out_c[o_i].astype(BF16), ln_g[l, 0], ln_b[l, 0], tm, tp // tm, ls // tm, alpha)
        res = _mlp(x, mod_l, w_ff1[l].astype(BF16), w_ff2[l].astype(BF16), ln_g[l, 1], ln_b[l, 1],
                   tm_mlp, tf, tp // tm_mlp, ls // tm_mlp, alpha, split=l == depth - 1)
        x = res[0]

    return (res[0].reshape(bp, lp, d), res[1].reshape(bs, ls, d),
            jnp.stack(s_re, axis=1), jnp.stack(s_im, axis=1),
            jnp.stack(k_list, axis=1), jnp.stack(v_list, axis=1))
```

```python
import functools
import math

import jax
import jax.numpy as jnp
import numpy as np
from jax import lax
from jax.experimental import pallas as pl
from jax.experimental.pallas import tpu as pltpu

F32 = jnp.float32
BF16 = jnp.bfloat16

LN_EPS = 1e-5
ROPE_BASE = 10000.0
LATENT_GRID_W = 64
MOD_ROWS = 8
V7X_VMEM_LIMIT = 56 * 1024 * 1024
LANES = 128
SUBLANES = 8


def _cparams(*sem):
    return pltpu.CompilerParams(dimension_semantics=sem, vmem_limit_bytes=V7X_VMEM_LIMIT)


def _layer_norm(z, g, b):
    mu = jnp.mean(z, axis=-1, keepdims=True)
    zc = z - mu
    var = jnp.mean(jnp.square(zc), axis=-1, keepdims=True)
    return zc * lax.rsqrt(var + LN_EPS) * g + b


def _group_of_tile(i, tiles_p, tiles_per_s):
    return jnp.where(i < tiles_p, 0, 1 + jnp.maximum(i - tiles_p, 0) // tiles_per_s)


def _mod_chunk(mod_ref, g, k, d):
    return mod_ref[pl.ds(g, 1), k * d:(k + 1) * d]


def _pair_specs(tm, width, tiles_p, grid_rank=1):
    if grid_rank == 1:
        return [pl.BlockSpec((tm, width), lambda i: (jnp.minimum(i, tiles_p - 1), 0)),
                pl.BlockSpec((tm, width), lambda i: (jnp.maximum(i - tiles_p, 0), 0))]
    return [pl.BlockSpec((tm, width), lambda i, f: (jnp.minimum(i, tiles_p - 1), 0)),
            pl.BlockSpec((tm, width), lambda i, f: (jnp.maximum(i - tiles_p, 0), 0))]


def _pair_rows(i, tiles_p, p_ref, s_ref):
    return jnp.where(i < tiles_p, p_ref[...], s_ref[...])


def _modvec_kernel(cv_ref, w_ref, b_ref, o_ref):
    cv = cv_ref[...]
    s = (cv * jax.nn.sigmoid(cv)).astype(BF16)
    o_ref[0] = jnp.dot(s, w_ref[0].astype(BF16), preferred_element_type=F32) + b_ref[0]


def _modvec(cvec, w_mod, b_mod, tn=1024):
    depth, d, n = w_mod.shape
    return pl.pallas_call(
        _modvec_kernel,
        grid=(depth, n // tn),
        in_specs=[pl.BlockSpec((MOD_ROWS, d), lambda l, j: (0, 0)),
                  pl.BlockSpec((1, d, tn), lambda l, j: (l, 0, j)),
                  pl.BlockSpec((1, 1, tn), lambda l, j: (l, 0, j))],
        out_specs=pl.BlockSpec((1, MOD_ROWS, tn), lambda l, j: (l, 0, j)),
        out_shape=jax.ShapeDtypeStruct((depth, MOD_ROWS, n), F32),
        compiler_params=_cparams("arbitrary", "arbitrary"),
        name="modvec",
    )(cvec, w_mod, b_mod.reshape(depth, 1, n))


def _inproj_kernel(xp_ref, xs_ref, mod_ref, w_ref, ug_ref, u_ref, *, tiles_p, tiles_per_s, d, c):
    i = pl.program_id(0)
    g = _group_of_tile(i, tiles_p, tiles_per_s)
    shift = _mod_chunk(mod_ref, g, 0, d)
    scale = _mod_chunk(mod_ref, g, 1, d)
    h = (_pair_rows(i, tiles_p, xp_ref, xs_ref) * (1 + scale) + shift).astype(BF16)
    a_val = jnp.dot(h, w_ref[:, 0:c], preferred_element_type=F32)
    a_gate = jnp.dot(h, w_ref[:, c:2 * c], preferred_element_type=F32)
    ug_ref[...] = a_val * jax.nn.sigmoid(a_gate)
    u_ref[...] = jnp.dot(h, w_ref[:, 2 * c:3 * c], preferred_element_type=F32)


def _inproj(xp, xs, mod_l, w_in, tm, tiles_p, tiles_per_s):
    t, d = xp.shape[0] + xs.shape[0], xp.shape[1]
    c = w_in.shape[1] // 3
    kern = functools.partial(_inproj_kernel, tiles_p=tiles_p, tiles_per_s=tiles_per_s, d=d, c=c)
    return pl.pallas_call(
        kern,
        grid=(t // tm,),
        in_specs=_pair_specs(tm, d, tiles_p) + [pl.BlockSpec(mod_l.shape, lambda i: (0, 0)),
                                                pl.BlockSpec(w_in.shape, lambda i: (0, 0))],
        out_specs=[pl.BlockSpec((tm, c), lambda i: (i, 0)),
                   pl.BlockSpec((tm, c), lambda i: (i, 0))],
        out_shape=[jax.ShapeDtypeStruct((t, c), F32), jax.ShapeDtypeStruct((t, c), F32)],
        compiler_params=_cparams("arbitrary"),
        name="inproj",
    )(xp, xs, mod_l, w_in)


CONV_HALO = 16
CONV_ROWS = 32
CONV_COLS = 256


def _conv_kernel(prev_ref, cur_ref, next_ref, w_ref, b_ref, g_ref, beta_ref, o_ref, pad_scr, sh_scr, conv_scr,
                 *, chunks_p, chunks_s, n_chunks_p, width, lc, c):
    i = pl.program_id(0)
    in_p = i < n_chunks_p
    k = jnp.where(in_p, i % chunks_p, jnp.maximum(i - n_chunks_p, 0) % chunks_s)
    last = jnp.where(in_p, chunks_p - 1, chunks_s - 1)
    has_prev = (k > 0).astype(F32)
    has_next = (k < last).astype(F32)
    pad_scr[0:CONV_HALO, :] = prev_ref[...] * has_prev
    pad_scr[CONV_HALO:CONV_HALO + lc, :] = cur_ref[...]
    pad_scr[CONV_HALO + lc:2 * CONV_HALO + lc, :] = next_ref[...] * has_next
    off = CONV_HALO - width // 2
    span = lc + CONV_HALO + SUBLANES
    for sft in range(SUBLANES):
        sh_scr[sft] = pad_scr[sft:sft + span, :]

    for r0 in range(0, lc, CONV_ROWS):
        for cb in range(c // CONV_COLS):
            cs = slice(cb * CONV_COLS, (cb + 1) * CONV_COLS)
            acc = jnp.zeros((CONV_ROWS, CONV_COLS), F32)
            for kk in range(width):
                whole, sft = divmod(kk + off, SUBLANES)
                base = r0 + whole * SUBLANES
                acc = acc + sh_scr[sft, base:base + CONV_ROWS, cs] * w_ref[kk:kk + 1, cs]
            conv_scr[r0:r0 + CONV_ROWS, cs] = acc + b_ref[:, cs]
    y = _layer_norm(conv_scr[...], g_ref[...], beta_ref[...])
    o_ref[...] = (y * jax.nn.sigmoid(y)).astype(o_ref.dtype)


def _conv_module(ug, w_dw, b_dw, ln_g, ln_b, lc, lp, ls, tp):
    t, c = ug.shape
    width = w_dw.shape[0]
    assert width // 2 < CONV_HALO and lc % CONV_HALO == 0 and lp % lc == 0 and ls % lc == 0
    hb = lc // CONV_HALO
    n_halo_blocks = t // CONV_HALO
    kern = functools.partial(_conv_kernel, chunks_p=lp // lc, chunks_s=ls // lc, n_chunks_p=tp // lc,
                             width=width, lc=lc, c=c)
    vec = lambda a: a.reshape(1, c)
    return pl.pallas_call(
        kern,
        grid=(t // lc,),
        in_specs=[pl.BlockSpec((CONV_HALO, c), lambda i: (jnp.maximum(i * hb - 1, 0), 0)),
                  pl.BlockSpec((lc, c), lambda i: (i, 0)),
                  pl.BlockSpec((CONV_HALO, c), lambda i: (jnp.minimum((i + 1) * hb, n_halo_blocks - 1), 0)),
                  pl.BlockSpec((width, c), lambda i: (0, 0)),
                  pl.BlockSpec((1, c), lambda i: (0, 0)),
                  pl.BlockSpec((1, c), lambda i: (0, 0)),
                  pl.BlockSpec((1, c), lambda i: (0, 0))],
        out_specs=pl.BlockSpec((lc, c), lambda i: (i, 0)),
        out_shape=jax.ShapeDtypeStruct((t, c), BF16),
        scratch_shapes=[pltpu.VMEM((lc + 2 * CONV_HALO, c), F32),
                        pltpu.VMEM((SUBLANES, lc + CONV_HALO + SUBLANES, c), F32), pltpu.VMEM((lc, c), F32)],
        compiler_params=_cparams("arbitrary"),
        name="conv_module",
    )(ug, ug, ug, w_dw, vec(b_dw), vec(ln_g), vec(ln_b))


S5_T = 16
S5_GB = 8


def _s5_prep_kernel(bt_re_ref, bt_im_ref, la_re_ref, la_im_ref, dta_ref, ct_re_ref, ct_im_ref, lb_re_ref, lb_im_ref,
                    dtb_ref, be_re_ref, be_im_ref, cs_re_ref, cs_ni_ref, kc_ref, a_re_ref, a_im_ref, *, n, p):
    t = S5_T
    fwd = pl.program_id(0) == 0

    lam_re, lam_im = la_re_ref[0, 0], la_im_ref[0, 0]
    dt = jnp.exp(dta_ref[0, 0])
    mag = jnp.exp(lam_re * dt)
    ar, ai = mag * jnp.cos(lam_im * dt), mag * jnp.sin(lam_im * dt)
    den = jnp.square(lam_re) + jnp.square(lam_im)
    coef_re = ((ar - 1) * lam_re + ai * lam_im) / den
    coef_im = (ai * lam_re - (ar - 1) * lam_im) / den
    b_re, b_im = bt_re_ref[0, 0], bt_im_ref[0, 0]
    bb_re = coef_re * b_re - coef_im * b_im
    bb_im = coef_re * b_im + coef_im * b_re
    rows = bb_re.shape[0]
    pw = []
    wr, wi = bb_re, bb_im
    pr, pi = jnp.ones_like(ar), jnp.zeros_like(ar)
    for k in range(t):
        pw.append((wr, wi))
        wr, wi = wr * ar - wi * ai, wr * ai + wi * ar
        pr, pi = pr * ar - pi * ai, pr * ai + pi * ar
    a_re_ref[0, 0] = pr
    a_im_ref[0, 0] = pi
    for j in range(t):
        be_re_ref[0, 0, j * rows:(j + 1) * rows, :] = jnp.where(fwd, pw[t - 1 - j][0], pw[j][0])
        be_im_ref[0, 0, j * rows:(j + 1) * rows, :] = jnp.where(fwd, pw[t - 1 - j][1], pw[j][1])

    ct_re, ct_im = ct_re_ref[0, 0], ct_im_ref[0, 0]
    shp = ct_re.shape
    lam_re = jnp.broadcast_to(lb_re_ref[0, 0], shp)
    lam_im = jnp.broadcast_to(lb_im_ref[0, 0], shp)
    dt = jnp.exp(jnp.broadcast_to(dtb_ref[0, 0], shp))
    blk = lax.broadcasted_iota(jnp.int32, shp, 1) // p
    k1 = jnp.where(fwd, blk + 1, t - blk).astype(F32)
    mag = jnp.exp(k1 * (lam_re * dt))
    ang = k1 * (lam_im * dt)
    qr, qi = mag * jnp.cos(ang), mag * jnp.sin(ang)
    v_re = ct_re * qr - ct_im * qi
    v_im = ct_re * qi + ct_im * qr
    cs_re_ref[0, 0] = v_re
    cs_ni_ref[0, 0] = -v_im
    lane = lax.broadcasted_iota(jnp.int32, shp, 1)
    w = shp[1]
    v0_re = jnp.where(fwd, jnp.where(lane < p, ct_re, pltpu.roll(v_re, p, 1)),
                      jnp.where(lane >= w - p, ct_re, pltpu.roll(v_re, w - p, 1)))
    v0_im = jnp.where(fwd, jnp.where(lane < p, ct_im, pltpu.roll(v_im, p, 1)),
                      jnp.where(lane >= w - p, ct_im, pltpu.roll(v_im, w - p, 1)))
    hi = lax.Precision.HIGHEST
    for g in range(S5_GB):
        ra, rb = slice(g * p, (g + 1) * p), slice(g * n, (g + 1) * n)
        kc_ref[0, 0, ra, :] = (jnp.dot(bb_re[ra], v0_re[rb], precision=hi, preferred_element_type=F32)
                               - jnp.dot(bb_im[ra], v0_im[rb], precision=hi, preferred_element_type=F32))


def _s5_prep(lam_re, lam_im, log_dt, b_re, b_im, c_re, c_im):
    _, g, n, p = b_re.shape
    t = S5_T
    nb = g // S5_GB
    ra, rb = S5_GB * p, S5_GB * n
    lay_a = lambda a: jnp.broadcast_to(a[:, :, None, :], (2, g, p, n)).reshape(2, nb, ra, n)
    lay_b = lambda a: a.reshape(2, nb, rb, 1)
    bt = lambda a: a.transpose(0, 1, 3, 2).reshape(2, nb, ra, n)
    ct = lambda a: jnp.broadcast_to(a.transpose(0, 1, 3, 2)[:, :, :, None, :], (2, g, n, t, p)).reshape(2, nb, rb, t * p)
    dt_g = jnp.broadcast_to(log_dt[:, :, None], (2, g, n))
    blk = lambda r, c: pl.BlockSpec((1, 1, r, c), lambda d, i: (d, i, 0, 0))
    shp = lambda r, c: jax.ShapeDtypeStruct((2, nb, r, c), F32)
    kern = functools.partial(_s5_prep_kernel, n=n, p=p)
    return pl.pallas_call(
        kern,
        grid=(2, nb),
        in_specs=[blk(ra, n)] * 5 + [blk(rb, t * p)] * 2 + [blk(rb, 1)] * 3,
        out_specs=[blk(t * ra, n), blk(t * ra, n), blk(rb, t * p), blk(rb, t * p), blk(ra, t * p), blk(ra, n), blk(ra, n)],
        out_shape=[shp(t * ra, n), shp(t * ra, n), shp(rb, t * p), shp(rb, t * p), shp(ra, t * p), shp(ra, n), shp(ra, n)],
        compiler_params=_cparams("arbitrary", "arbitrary"),
        name="s5_prep",
    )(bt(b_re), bt(b_im), lay_a(lam_re), lay_a(lam_im), lay_a(dt_g), ct(c_re), ct(c_im),
      lay_b(lam_re), lay_b(lam_im), lay_b(dt_g))


def _s5_expand(src, tile, row_div, row_mod, lane_div, lane_mod, precision=None):
    full = jnp.dot(src, tile, precision=precision, preferred_element_type=F32)
    r = lax.broadcasted_iota(jnp.int32, full.shape, 0) // row_div % row_mod
    l = lax.broadcasted_iota(jnp.int32, full.shape, 1) // lane_div % lane_mod
    return jnp.where(r == l, full, 0.0)


def _s5_chunk_kernel(u_ref, be_re_ref, be_im_ref, cs_re_ref, cs_ni_ref, kc_ref, tk_ref, tb_ref, a_ref, h0_ref,
                     y_ref, hf_ref, m8_scr, be8_scr, cs8_scr, e_scr, *, geoms, n_tiles_p, n, p):
    t, gb = S5_T, S5_GB
    cw = gb * p
    sw = gb * n
    rows = geoms[0][0] * geoms[0][1]
    tile_i = pl.program_id(1)

    @pl.when(tile_i == 0)
    def _():
        tk, tb = tk_ref[...].astype(BF16), tb_ref[...].astype(BF16)
        step = 4 * cw
        for r0 in range(0, t * cw, step):
            for part, (ref, d) in enumerate(((be_re_ref, 0), (be_im_ref, 0), (be_re_ref, 1), (be_im_ref, 1))):
                be8_scr[r0:r0 + step, part * sw:(part + 1) * sw] = _s5_expand(
                    ref[d, 0, r0:r0 + step, :].astype(BF16), tb, p, gb, n, gb).astype(BF16)
        for part, (ref, d) in enumerate(((cs_re_ref, 0), (cs_ni_ref, 0), (cs_re_ref, 1), (cs_ni_ref, 1))):
            cs8_scr[part * sw:(part + 1) * sw, :] = _s5_expand(ref[d, 0].astype(BF16), tk, n, gb, p, gb).astype(BF16)
        hi = lax.Precision.HIGHEST
        bd_f = _s5_expand(kc_ref[0, 0], tk_ref[...], p, gb, p, gb, hi)
        bd_r = _s5_expand(kc_ref[1, 0], tk_ref[...], p, gb, p, gb, hi)
        tile_f = lambda k: bd_f[:, k * cw:(k + 1) * cw]
        tile_r = lambda k: bd_r[:, (t - 1 - k) * cw:(t - k) * cw]
        for j in range(t):
            for i in range(t):
                blk = tile_f(i - j) if i > j else tile_r(j - i) if i < j else tile_f(0) + tile_r(0)
                m8_scr[j * cw:(j + 1) * cw, i * cw:(i + 1) * cw] = blk.astype(BF16)

    x = u_ref[...].reshape(rows, t * cw).astype(BF16)
    e = jnp.dot(x, be8_scr[...], preferred_element_type=F32)
    n_slab = e.shape[1] // LANES
    per_part = sw // LANES
    for k in range(n_slab):
        e_scr[k] = e[:, k * LANES:(k + 1) * LANES]

    def scan(nseq, nc):
        loops = [list(range(per_part))] if nseq < 8 else [[q] for q in range(per_part)]
        for prs in loops:
            coef = [[jnp.broadcast_to(a_ref[r:r + 1, q * LANES:(q + 1) * LANES], (nseq, LANES)) for r in range(4)]
                    for q in prs]
            init = tuple(tuple(h0_ref[0, r, 0:nseq, q * LANES:(q + 1) * LANES] for r in range(4)) for q in prs)

            def step(c, carry):
                out = []
                for idx, q in enumerate(prs):
                    sf_re, sf_im, sr_re, sr_im = carry[idx]
                    af_re, af_im, ar_re, ar_im = coef[idx]
                    at_f = pl.ds(c, nseq, stride=nc)
                    at_r = pl.ds(nc - 1 - c, nseq, stride=nc)
                    ef_re, ef_im = e_scr[q, at_f, :], e_scr[per_part + q, at_f, :]
                    er_re, er_im = e_scr[2 * per_part + q, at_r, :], e_scr[3 * per_part + q, at_r, :]
                    e_scr[q, at_f, :] = sf_re
                    e_scr[per_part + q, at_f, :] = sf_im
                    e_scr[2 * per_part + q, at_r, :] = sr_re
                    e_scr[3 * per_part + q, at_r, :] = sr_im
                    out.append((af_re * sf_re - af_im * sf_im + ef_re, af_re * sf_im + af_im * sf_re + ef_im,
                                ar_re * sr_re - ar_im * sr_im + er_re, ar_re * sr_im + ar_im * sr_re + er_im))
                return tuple(out)

            fin = lax.fori_loop(0, nc, step, init)
            for idx, q in enumerate(prs):
                for r in range(4):
                    hf_ref[0, r, 0:nseq, q * LANES:(q + 1) * LANES] = fin[idx][r]

    hf_ref[...] = jnp.zeros_like(hf_ref)

    @pl.when(tile_i < n_tiles_p)
    def _():
        scan(*geoms[0])

    @pl.when(tile_i >= n_tiles_p)
    def _():
        scan(*geoms[1])

    s = jnp.concatenate([e_scr[k] for k in range(n_slab)], axis=-1).astype(BF16)
    y = (jnp.dot(x, m8_scr[...], preferred_element_type=F32)
         + jnp.dot(s, cs8_scr[...], preferred_element_type=F32))
    y_ref[...] = y.reshape(rows * t, cw)


def _s5_chunked(u, h0, prep, tile_k, tile_b, a4, *, tok, geoms, n_tiles_p, n, p):
    be_re, be_im, cs_re, cs_ni, kc = prep
    t, gb = S5_T, S5_GB
    cw, sw = gb * p, gb * n
    nb = u.shape[1] // cw
    n_tiles = u.shape[0] // tok
    ms = h0.shape[2]
    both = lambda a: pl.BlockSpec((2, 1) + a.shape[2:], lambda b, i: (0, b, 0, 0), pipeline_mode=pl.Buffered(1))
    const = lambda a: pl.BlockSpec(a.shape, lambda b, i: (0, 0), pipeline_mode=pl.Buffered(1))
    kern = functools.partial(_s5_chunk_kernel, geoms=geoms, n_tiles_p=n_tiles_p, n=n, p=p)
    return pl.pallas_call(
        kern,
        grid=(nb, n_tiles),
        in_specs=[pl.BlockSpec((tok, cw), lambda b, i: (i, b)),
                  both(be_re), both(be_im), both(cs_re), both(cs_ni), both(kc), const(tile_k), const(tile_b),
                  pl.BlockSpec((4, sw), lambda b, i: (0, b)),
                  pl.BlockSpec((1, 4, ms, sw), lambda b, i: (i, 0, 0, b))],
        out_specs=[pl.BlockSpec((tok, cw), lambda b, i: (i, b)),
                   pl.BlockSpec((1, 4, ms, sw), lambda b, i: (i, 0, 0, b))],
        out_shape=[jax.ShapeDtypeStruct(u.shape, F32), jax.ShapeDtypeStruct(h0.shape, F32)],
        scratch_shapes=[pltpu.VMEM((t * cw, t * cw), BF16), pltpu.VMEM((t * cw, 4 * sw), BF16),
                        pltpu.VMEM((4 * sw, t * cw), BF16), pltpu.VMEM((4 * sw // LANES, tok // t, LANES), F32)],
        compiler_params=_cparams("arbitrary", "arbitrary"),
        name="s5_chunked",
    )(u, be_re, be_im, cs_re, cs_ni, kc, tile_k, tile_b, a4, h0)


def _s5_tiles(n, p):
    t, gb = S5_T, S5_GB
    eye = lambda k: np.eye(k, dtype=np.float32)
    tile_k = np.einsum("ab,pq->apbq", eye(t), eye(p))[:, :, :, None, :] * np.ones((1, 1, 1, gb, 1), np.float32)
    tile_b = eye(n)[:, None, :] * np.ones((1, gb, 1), np.float32)
    return jnp.asarray(tile_k.reshape(t * p, t * gb * p)), jnp.asarray(tile_b.reshape(n, gb * n))


def _post_residual(x, y, gate, g, b, alpha):
    return _layer_norm(alpha * x + gate * y, g, b)


def _outab_kernel(xp_ref, xs_ref, mod_ref, yc_ref, ys_ref, u_ref, dsk_ref, wglu_ref, wout_ref, g_ref, b_ref, o_ref,
                  *, tiles_p, tiles_per_s, d, c, alpha):
    i = pl.program_id(0)
    grp = _group_of_tile(i, tiles_p, tiles_per_s)
    y_s = ys_ref[...] + dsk_ref[...] * u_ref[...]
    y_s = jax.nn.gelu(y_s)
    z = jnp.dot(y_s.astype(BF16), wglu_ref[...], preferred_element_type=F32)
    y_ssm = y_s * jax.nn.sigmoid(z)
    out = (jnp.dot(yc_ref[...], wout_ref[0:c, :], preferred_element_type=F32)
           + jnp.dot(y_ssm.astype(BF16), wout_ref[c:2 * c, :], preferred_element_type=F32))
    gate = _mod_chunk(mod_ref, grp, 2, d)
    o_ref[...] = _post_residual(_pair_rows(i, tiles_p, xp_ref, xs_ref), out, gate, g_ref[...], b_ref[...], alpha)


def _outab(xp, xs, mod_l, y_conv, y_scan, u, d_skip, w_glu, w_out, ln_g, ln_b, tm, tiles_p, tiles_per_s, alpha):
    t, d = xp.shape[0] + xs.shape[0], xp.shape[1]
    c = u.shape[1]
    kern = functools.partial(_outab_kernel, tiles_p=tiles_p, tiles_per_s=tiles_per_s, d=d, c=c, alpha=alpha)
    row = lambda w: pl.BlockSpec((tm, w), lambda i: (i, 0))
    full = lambda a: pl.BlockSpec(a.shape, lambda i: (0,) * a.ndim)
    d_skip, ln_g, ln_b = d_skip.reshape(1, c), ln_g.reshape(1, d), ln_b.reshape(1, d)
    return pl.pallas_call(
        kern,
        grid=(t // tm,),
        in_specs=_pair_specs(tm, d, tiles_p) + [full(mod_l), row(c), row(c), row(c), full(d_skip), full(w_glu),
                                                full(w_out), full(ln_g), full(ln_b)],
        out_specs=row(d),
        out_shape=jax.ShapeDtypeStruct((t, d), F32),
        compiler_params=_cparams("arbitrary"),
        name="outproj_ab",
    )(xp, xs, mod_l, y_conv, y_scan, u, d_skip, w_glu, w_out, ln_g, ln_b)


def _mlp_kernel(x_ref, mod_ref, w1_ref, w2_ref, g_ref, b_ref, *rest, tiles_p, tiles_per_s, d, n_f, alpha, split):
    outs, (h_scr, acc_scr) = rest[:-2], rest[-2:]
    i = pl.program_id(0)
    f = pl.program_id(1)
    grp = _group_of_tile(i, tiles_p, tiles_per_s)

    @pl.when(f == 0)
    def _():
        shift = _mod_chunk(mod_ref, grp, 3, d)
        scale = _mod_chunk(mod_ref, grp, 4, d)
        h_scr[...] = (x_ref[...] * (1 + scale) + shift).astype(BF16)
        acc_scr[...] = jnp.zeros_like(acc_scr)

    a = jnp.dot(h_scr[...], w1_ref[...], preferred_element_type=F32)
    a = jnp.square(jnp.maximum(a, 0.0)).astype(BF16)
    acc_scr[...] += jnp.dot(a, w2_ref[...], preferred_element_type=F32)

    @pl.when(f == n_f - 1)
    def _():
        gate = _mod_chunk(mod_ref, grp, 5, d)
        res = _post_residual(x_ref[...], acc_scr[...], gate, g_ref[...], b_ref[...], alpha)
        if split:
            @pl.when(i < tiles_p)
            def _():
                outs[0][...] = res

            @pl.when(i >= tiles_p)
            def _():
                outs[1][...] = res
        else:
            outs[0][...] = res


def _mlp(x, mod_l, w1, w2, layer, ln_g, ln_b, tm, tf, tiles_p, tiles_per_s, alpha, split):
    t, d = x.shape
    n_f = w1.shape[2] // tf
    kern = functools.partial(_mlp_kernel, tiles_p=tiles_p, tiles_per_s=tiles_per_s, d=d, n_f=n_f, alpha=alpha,
                             split=split)
    ln_g, ln_b = ln_g.reshape(1, d), ln_b.reshape(1, d)
    if split:
        out_specs = _pair_specs(tm, d, tiles_p, grid_rank=2)
        out_shape = [jax.ShapeDtypeStruct((tiles_p * tm, d), F32), jax.ShapeDtypeStruct((t - tiles_p * tm, d), F32)]
    else:
        out_specs = [pl.BlockSpec((tm, d), lambda i, f: (i, 0))]
        out_shape = [jax.ShapeDtypeStruct((t, d), F32)]
    return pl.pallas_call(
        kern,
        grid=(t // tm, n_f),
        in_specs=[pl.BlockSpec((tm, d), lambda i, f: (i, 0)),
                  pl.BlockSpec(mod_l.shape, lambda i, f: (0, 0)),
                  pl.BlockSpec((None, d, tf), lambda i, f: (layer, 0, f)),
                  pl.BlockSpec((None, tf, d), lambda i, f: (layer, f, 0)),
                  pl.BlockSpec((1, d), lambda i, f: (0, 0)),
                  pl.BlockSpec((1, d), lambda i, f: (0, 0))],
        out_specs=out_specs,
        out_shape=out_shape,
        scratch_shapes=[pltpu.VMEM((tm, d), BF16), pltpu.VMEM((tm, d), F32)],
        compiler_params=_cparams("arbitrary", "arbitrary"),
        name="mlp",
    )(x, mod_l, w1, w2, ln_g, ln_b)


def _rope_tables(n_pos, dk):
    ax = dk // 2
    half = ax // 2
    freqs = ROPE_BASE ** (-jnp.arange(half, dtype=F32) / half)
    pos = jnp.arange(n_pos)
    row = (pos // LATENT_GRID_W).astype(F32)
    col = (pos % LATENT_GRID_W).astype(F32)
    ang_r, ang_c = row[:, None] * freqs, col[:, None] * freqs
    cos = jnp.concatenate([jnp.cos(ang_r)] * 2 + [jnp.cos(ang_c)] * 2, axis=-1)
    sin = jnp.concatenate([-jnp.sin(ang_r), jnp.sin(ang_r), -jnp.sin(ang_c), jnp.sin(ang_c)], axis=-1)
    rep = LANES // dk
    return jnp.tile(cos, (1, rep)), jnp.tile(sin, (1, rep))


def _qkv_kernel(x_ref, mod_ref, w_ref, cos_ref, sin_ref, o_ref, kc_ref, vc_ref, h_scr,
                *, tiles_p, tiles_per_s, d, dk, dv, q_scale):
    i = pl.program_id(0)
    n = pl.program_id(1)

    @pl.when(n == 0)
    def _():
        g = _group_of_tile(i, tiles_p, tiles_per_s)
        shift = _mod_chunk(mod_ref, g, 0, d)
        scale = _mod_chunk(mod_ref, g, 1, d)
        h_scr[...] = (x_ref[...] * (1 + scale) + shift).astype(BF16)

    y = jnp.dot(h_scr[...], w_ref[n], preferred_element_type=F32)
    rotate = jnp.logical_and(n < 2, i >= tiles_p)
    quarter = dk // 4
    out_scale = jnp.where(n == 0, q_scale, 1.0).astype(F32)

    @pl.when(rotate)
    def _():
        cos, sin = cos_ref[...], sin_ref[...]
        lane = lax.broadcasted_iota(jnp.int32, cos.shape, 1)
        first = (lane % (2 * quarter)) < quarter
        for cb in range(y.shape[1] // LANES):
            yb = y[:, cb * LANES:(cb + 1) * LANES]
            partner = jnp.where(first, pltpu.roll(yb, LANES - quarter, 1), pltpu.roll(yb, quarter, 1))
            o_ref[0, :, cb * LANES:(cb + 1) * LANES] = ((yb * cos + partner * sin) * out_scale).astype(o_ref.dtype)

    @pl.when(jnp.logical_not(rotate))
    def _():
        o_ref[0] = (y * out_scale).astype(o_ref.dtype)

    @pl.when(jnp.logical_and(n == 1, i < tiles_p))
    def _():
        n_heads = kc_ref.shape[2]
        for m in range(2):
            for hh in range(n_heads):
                c0 = (m * n_heads + hh) * dk
                kc_ref[0, m, hh] = y[:, c0:c0 + dk]

    @pl.when(jnp.logical_and(n == 2, i < tiles_p))
    def _():
        for hh in range(vc_ref.shape[1]):
            vc_ref[0, hh] = y[:, hh * dv:(hh + 1) * dv]


def _qkv(x, mod_l, w_qkv, cos_t, sin_t, tm, tiles_p, tiles_per_s, dk, dv, bp, lp, q_scale):
    t, d = x.shape
    n_out = w_qkv.shape[1]
    assert n_out % 3 == 0 and lp % tm == 0
    tn = n_out // 3
    n_heads = tn // dv
    per_seq = lp // tm
    w3 = w_qkv.reshape(d, 3, tn).transpose(1, 0, 2)
    kern = functools.partial(_qkv_kernel, tiles_p=tiles_p, tiles_per_s=tiles_per_s, d=d, dk=dk, dv=dv,
                             q_scale=q_scale)
    pos_blk = lambda i, n: (jnp.maximum(i - tiles_p, 0) % tiles_per_s, 0)
    c_tile = lambda i: jnp.minimum(i, tiles_p - 1)
    return pl.pallas_call(
        kern,
        grid=(t // tm, 3),
        in_specs=[pl.BlockSpec((tm, d), lambda i, n: (i, 0)),
                  pl.BlockSpec(mod_l.shape, lambda i, n: (0, 0)),
                  pl.BlockSpec((3, d, tn), lambda i, n: (0, 0, 0), pipeline_mode=pl.Buffered(1)),
                  pl.BlockSpec((tm, LANES), pos_blk),
                  pl.BlockSpec((tm, LANES), pos_blk)],
        out_specs=[pl.BlockSpec((1, tm, tn), lambda i, n: (n, i, 0)),
                   pl.BlockSpec((1, 2, n_heads, tm, dk),
                                lambda i, n: (c_tile(i) // per_seq, 0, 0, c_tile(i) % per_seq, 0)),
                   pl.BlockSpec((1, n_heads, tm, dv),
                                lambda i, n: (c_tile(i) // per_seq, 0, c_tile(i) % per_seq, 0))],
        out_shape=[jax.ShapeDtypeStruct((3, t, tn), BF16),
                   jax.ShapeDtypeStruct((bp, 2, n_heads, lp, dk), F32),
                   jax.ShapeDtypeStruct((bp, n_heads, lp, dv), F32)],
        scratch_shapes=[pltpu.VMEM((tm, d), BF16)],
        compiler_params=_cparams("arbitrary", "arbitrary"),
        name="qkv_proj",
    )(x, mod_l, w3, cos_t, sin_t)


def _attn_kernel(*refs, hp, dk, dv, scale, fold_scale, lam_init, has_cache):
    if has_cache:
        q1_ref, q2_ref, k1_ref, k2_ref, v_ref, ck_ref, cv_ref, lamv_ref, sg_ref, o_ref = refs
    else:
        q1_ref, q2_ref, k1_ref, k2_ref, v_ref, lamv_ref, sg_ref, o_ref = refs
        ck_ref = cv_ref = None
    lv = lamv_ref[...]
    lam = (jnp.exp(jnp.sum(lv[0:1] * lv[1:2], axis=-1, keepdims=True))
           - jnp.exp(jnp.sum(lv[2:3] * lv[3:4], axis=-1, keepdims=True)) + lam_init)
    nt = (((1,), (1,)), ((), ()))
    per_blk = LANES // dk

    def probs(q_ref, k_ref, m, head):
        cs = slice(head * dk, (head + 1) * dk)
        q = q_ref[:, cs]
        s = lax.dot_general(q, k_ref[:, cs], nt, preferred_element_type=F32)
        if not fold_scale:
            s = s * scale
        mx = jnp.max(s, axis=-1, keepdims=True)
        if has_cache:
            sc = lax.dot_general(q, ck_ref[0, 0, m, head].astype(BF16), nt, preferred_element_type=F32)
            if not fold_scale:
                sc = sc * scale
            mx = jnp.maximum(mx, jnp.max(sc, axis=-1, keepdims=True))
            ec = jnp.exp(sc - mx)
        e = jnp.exp(s - mx)
        den = jnp.sum(e, axis=-1, keepdims=True)
        if has_cache:
            den = den + jnp.sum(ec, axis=-1, keepdims=True)
            return e, ec, den
        return e, None, den

    for head in range(hp * per_blk):
        e1, e1c, d1 = probs(q1_ref, k1_ref, 0, head)
        e2, e2c, d2 = probs(q2_ref, k2_ref, 1, head)
        r1 = 1.0 / d1
        ratio = lam * d1 / d2
        vs = slice(head * dv, (head + 1) * dv)
        w = (e1 - e2 * ratio).astype(BF16)
        o = jnp.dot(w, v_ref[:, vs], preferred_element_type=F32)
        if has_cache:
            wc = (e1c - e2c * ratio).astype(BF16)
            o = o + jnp.dot(wc, cv_ref[0, 0, head].astype(BF16), preferred_element_type=F32)
        o = o * r1
        o = o * lax.rsqrt(jnp.mean(jnp.square(o), axis=-1, keepdims=True) + LN_EPS)
        o = o * sg_ref[...] * (1.0 - lam_init)
        o_ref[:, vs] = o.astype(o_ref.dtype)


def _attn_t_kernel(*refs, heads_step, dk, dv, scale, fold_scale, lam_init, has_cache):
    if has_cache:
        q1_ref, q2_ref, k1_ref, k2_ref, v_ref, ck_ref, cv_ref, lamv_ref, sg_ref, o_ref = refs
    else:
        q1_ref, q2_ref, k1_ref, k2_ref, v_ref, lamv_ref, sg_ref, o_ref = refs
        ck_ref = cv_ref = None
    lv = lamv_ref[...]
    lam = (jnp.exp(jnp.sum(lv[0:1] * lv[1:2], axis=-1, keepdims=True))
           - jnp.exp(jnp.sum(lv[2:3] * lv[3:4], axis=-1, keepdims=True)) + lam_init)
    nt = (((1,), (1,)), ((), ()))
    tn = (((0,), (0,)), ((), ()))
    per_blk = LANES // dk
    tq = q1_ref.shape[0]
    lane = lax.broadcasted_iota(jnp.int32, (tq, LANES), 1)

    def exps(kb, qb, m, head, sub):
        qm = jnp.where((lane >= sub * dk) & (lane < (sub + 1) * dk), qb, jnp.zeros_like(qb))
        s = lax.dot_general(kb, qm, nt, preferred_element_type=F32)
        if not fold_scale:
            s = s * scale
        mx = jnp.max(s, axis=0, keepdims=True)
        if has_cache:
            qc = qb[:, sub * dk:(sub + 1) * dk]
            sc = lax.dot_general(ck_ref[0, 0, m, head].astype(BF16), qc, nt, preferred_element_type=F32)
            if not fold_scale:
                sc = sc * scale
            mx = jnp.maximum(mx, jnp.max(sc, axis=0, keepdims=True))
            ec = jnp.exp(sc - mx)
        e = jnp.exp(s - mx)
        den = jnp.sum(e, axis=0, keepdims=True)
        if has_cache:
            return e, ec, den + jnp.sum(ec, axis=0, keepdims=True)
        return e, None, den

    for blk in range(heads_step // per_blk):
        bs = slice(blk * LANES, (blk + 1) * LANES)
        k1b, k2b = k1_ref[:, bs], k2_ref[:, bs]
        q1b, q2b = q1_ref[:, bs], q2_ref[:, bs]
        for sub in range(per_blk):
            head = blk * per_blk + sub
            e1, e1c, d1 = exps(k1b, q1b, 0, head, sub)
            e2, e2c, d2 = exps(k2b, q2b, 1, head, sub)
            r1 = 1.0 / d1
            ratio = lam * d1 / d2
            vs = slice(head * dv, (head + 1) * dv)
            w = (e1 - e2 * ratio).astype(BF16)
            o_t = lax.dot_general(v_ref[:, vs], w, tn, preferred_element_type=F32)
            if has_cache:
                wc = (e1c - e2c * ratio).astype(BF16)
                o_t = o_t + lax.dot_general(cv_ref[0, 0, head].astype(BF16), wc, tn, preferred_element_type=F32)
            o_t = o_t * r1
            o_t = o_t * lax.rsqrt(jnp.mean(jnp.square(o_t), axis=0, keepdims=True) + LN_EPS)
            o = o_t.T * sg_ref[...] * (1.0 - lam_init)
            o_ref[:, vs] = o.astype(o_ref.dtype)


def _softmax_scale(dk):
    scale = dk ** -0.5
    return scale, math.frexp(scale)[0] == 0.5


def _attention(qkv, lamv, subln_g, *, row0, n_seq, lq, tq, hp, n_heads, dk, dv, lam_init, cache=None,
               keys_on_sublanes=False):
    per_blk = LANES // dk
    heads_step = hp * per_blk
    n_hblk = n_heads // heads_step
    map2 = n_heads * dk // (hp * LANES)
    scale, fold_scale = _softmax_scale(dk)
    qb0, kb0 = row0 // tq, row0 // lq
    n_q = lq // tq
    q_spec = lambda off: pl.BlockSpec((None, tq, hp * LANES), lambda b, h, qi: (0, qb0 + b * n_q + qi, off + h))
    k_spec = lambda off: pl.BlockSpec((None, lq, hp * LANES), lambda b, h, qi: (1, kb0 + b, off + h))
    in_specs = [q_spec(0), q_spec(map2), k_spec(0), k_spec(map2),
                pl.BlockSpec((None, lq, heads_step * dv), lambda b, h, qi: (2, kb0 + b, h))]
    args = [qkv, qkv, qkv, qkv, qkv]
    if cache is not None:
        cache_k, cache_v, o_i = cache
        past = cache_k.shape[-2]
        in_specs += [pl.BlockSpec((1, 1, 2, heads_step, past, dk), lambda b, h, qi: (b, o_i, 0, h, 0, 0)),
                     pl.BlockSpec((1, 1, heads_step, past, dv), lambda b, h, qi: (b, o_i, h, 0, 0))]
        args += [cache_k, cache_v]
    in_specs += [pl.BlockSpec(lamv.shape, lambda b, h, qi: (0, 0)),
                 pl.BlockSpec((1, dv), lambda b, h, qi: (0, 0))]
    args += [lamv, subln_g.reshape(1, dv)]
    if keys_on_sublanes:
        kern = functools.partial(_attn_t_kernel, heads_step=heads_step, dk=dk, dv=dv, scale=scale,
                                 fold_scale=fold_scale, lam_init=lam_init, has_cache=cache is not None)
    else:
        kern = functools.partial(_attn_kernel, hp=hp, dk=dk, dv=dv, scale=scale, fold_scale=fold_scale,
                                 lam_init=lam_init, has_cache=cache is not None)
    return pl.pallas_call(
        kern,
        grid=(n_seq, n_hblk, n_q),
        in_specs=in_specs,
        out_specs=pl.BlockSpec((tq, heads_step * dv), lambda b, h, qi: (b * n_q + qi, h)),
        out_shape=jax.ShapeDtypeStruct((n_seq * lq, n_heads * dv), BF16),
        compiler_params=_cparams("arbitrary", "arbitrary", "arbitrary"),
        name="diff_attn_cache" if cache is not None else "diff_attn",
    )(*args)


def _outc_kernel(x_ref, mod_ref, op_ref, os_ref, w_ref, g_ref, b_ref, o_ref, *, tiles_p, tiles_per_s, d, alpha):
    i = pl.program_id(0)
    grp = _group_of_tile(i, tiles_p, tiles_per_s)
    out = jnp.dot(_pair_rows(i, tiles_p, op_ref, os_ref), w_ref[...], preferred_element_type=F32)
    gate = _mod_chunk(mod_ref, grp, 2, d)
    o_ref[...] = _post_residual(x_ref[...], out, gate, g_ref[...], b_ref[...], alpha)


def _outc(x, mod_l, o_p, o_s, w_out, ln_g, ln_b, tm, tiles_p, tiles_per_s, alpha):
    t, d = x.shape
    kin = o_p.shape[1]
    kern = functools.partial(_outc_kernel, tiles_p=tiles_p, tiles_per_s=tiles_per_s, d=d, alpha=alpha)
    ln_g, ln_b = ln_g.reshape(1, d), ln_b.reshape(1, d)
    full = lambda a: pl.BlockSpec(a.shape, lambda i: (0,) * a.ndim)
    return pl.pallas_call(
        kern,
        grid=(t // tm,),
        in_specs=[pl.BlockSpec((tm, d), lambda i: (i, 0)), full(mod_l)] + _pair_specs(tm, kin, tiles_p)
        + [full(w_out), full(ln_g), full(ln_b)],
        out_specs=pl.BlockSpec((tm, d), lambda i: (i, 0)),
        out_shape=jax.ShapeDtypeStruct((t, d), F32),
        compiler_params=_cparams("arbitrary"),
        name="outproj_c",
    )(x, mod_l, o_p, o_s, w_out, ln_g, ln_b)


def kernel(x_prompt, x_sample, state_s5_re, state_s5_im, cache_k, cache_v, c, c_ctx, w_mod, b_mod, ln_g, ln_b, w_in_ab, w_dw, b_dw, conv_ln_g, conv_ln_b, s5_lambda_re, s5_lambda_im, s5_log_dt, s5_b_re, s5_b_im, s5_c_re, s5_c_im, s5_d, w_glu, w_out_ab, w_qkv, lam_q1, lam_k1, lam_q2, lam_k2, subln_g, w_out_c, w_ff1, w_ff2):
    bp, lp, d = x_prompt.shape
    bs, ls, _ = x_sample.shape
    depth = w_mod.shape[0]
    tp, ts = bp * lp, bs * ls
    alpha = (2 * depth) ** 0.25
    assert 1 + bs <= MOD_ROWS

    tm = 256
    tm_mlp, tf = 512, 1024
    assert tp % tm_mlp == 0 and ls % tm_mlp == 0 and tp % ls == 0

    xp, xs = x_prompt.reshape(tp, d), x_sample.reshape(ts, d)
    cvec = jnp.zeros((MOD_ROWS, d), F32).at[0].set(c_ctx).at[1:1 + bs].set(c)
    mod = _modvec(cvec, w_mod, b_mod)

    g_ssm, n_ssm, p_ssm = s5_b_re.shape[2:]
    dk = lam_q1.shape[-1]
    dv = subln_g.shape[-1]
    n_heads = w_out_c.shape[1] // dv

    w_ff1_bf, w_ff2_bf = w_ff1.astype(BF16), w_ff2.astype(BF16)
    s_re, s_im, k_list, v_list = [], [], [], []
    for l in range(depth):
        mod_l = mod[l]
        if l % 2 == 0:
            e = l // 2
            if l > 0:
                xp, xs = x[:tp], x[tp:]
            ug, u = _inproj(xp, xs, mod_l, w_in_ab[e].astype(BF16), tm, tp // tm, ls // tm)
            y_conv = _conv_module(ug, w_dw[e], b_dw[e], conv_ln_g[e], conv_ln_b[e], min(lp, 256), lp, ls, tp)
            prep = _s5_prep(s5_lambda_re[e], s5_lambda_im[e], s5_log_dt[e],
                            s5_b_re[e], s5_b_im[e], s5_c_re[e], s5_c_im[e])
            a_re, a_im = prep[5], prep[6]
            gn = g_ssm * n_ssm
            nat = lambda a: a.reshape(2, g_ssm, p_ssm, n_ssm)[:, :, 0].reshape(2, gn)
            a4 = jnp.stack([nat(a_re)[0], nat(a_im)[0], nat(a_re)[1], nat(a_im)[1]])
            tile_k, tile_b = _s5_tiles(n_ssm, p_ssm)
            tok = min(4096, tp, ts)
            assert tok % lp == 0 and tok % ls == 0 and tp % tok == 0 and ts % tok == 0
            seq_p, seq_s, tiles_p5 = tok // lp, tok // ls, tp // tok
            ms = max(seq_p, seq_s)
            st = lambda a, dr: jnp.pad(a[:, e, dr].reshape(ts // tok, seq_s, gn), ((0, 0), (0, ms - seq_s), (0, 0)))
            h0_s = jnp.stack([st(state_s5_re, 0), st(state_s5_im, 0), st(state_s5_re, 1), st(state_s5_im, 1)], axis=1)
            h0 = jnp.concatenate([jnp.zeros((tiles_p5, 4, ms, gn), F32), h0_s], axis=0)
            y_scan, hf = _s5_chunked(u, h0, prep[:5], tile_k, tile_b, a4, tok=tok,
                                     geoms=((seq_p, lp // S5_T), (seq_s, ls // S5_T)), n_tiles_p=tiles_p5,
                                     n=n_ssm, p=p_ssm)
            hf = hf[:tiles_p5, :, :seq_p].transpose(1, 0, 2, 3).reshape(2, 2, bp, g_ssm, n_ssm)
            s_re.append(hf[:, 0].transpose(1, 0, 2, 3))
            s_im.append(hf[:, 1].transpose(1, 0, 2, 3))
            x = _outab(xp, xs, mod_l, y_conv, y_scan, u, s5_d[e], w_glu[e].astype(BF16),
                       w_out_ab[e].astype(BF16), ln_g[l, 0], ln_b[l, 0], tm, tp // tm, ls // tm, alpha)
        else:
            o_i = l // 2
            lam_init = 0.8 - 0.6 * math.exp(-0.3 * l)
            cos_t, sin_t = _rope_tables(ls, dk)
            scale, fold_scale = _softmax_scale(dk)
            qkv, k_new, v_new = _qkv(x, mod_l, w_qkv[o_i].astype(BF16), cos_t, sin_t, tm, tp // tm, ls // tm,
                                     dk, dv, bp, lp, scale if fold_scale else 1.0)
            lamv = jnp.stack([lam_q1[o_i], lam_k1[o_i], lam_q2[o_i], lam_k2[o_i]])
            geo = dict(n_heads=n_heads, dk=dk, dv=dv, lam_init=lam_init)
            o_p = _attention(qkv, lamv, subln_g[o_i], row0=0, n_seq=bp, lq=lp, tq=lp,
                             hp=n_heads * dk // LANES, keys_on_sublanes=True, **geo)
            o_s = _attention(qkv, lamv, subln_g[o_i], row0=tp, n_seq=bs, lq=ls, tq=256, hp=1,
                             cache=(cache_k, cache_v, o_i), **geo)
            k_list.append(k_new)
            v_list.append(v_new)
            x = _outc(x, mod_l, o_p, o_s, w_out_c[o_i].astype(BF16), ln_g[l, 0], ln_b[l, 0], tm, tp // tm, ls // tm, alpha)
        res = _mlp(x, mod_l, w_ff1_bf, w_ff2_bf, l, ln_g[l, 1], ln_b[l, 1],
                   tm_mlp, tf, tp // tm_mlp, ls // tm_mlp, alpha, split=l == depth - 1)
        x = res[0]

    return (res[0].reshape(bp, lp, d), res[1].reshape(bs, ls, d),
            jnp.stack(s_re, axis=1), jnp.stack(s_im, axis=1),
            jnp.stack(k_list, axis=1), jnp.stack(v_list, axis=1))
```

```python
import functools
import math

import jax
import jax.numpy as jnp
import numpy as np
from jax import lax
from jax.experimental import pallas as pl
from jax.experimental.pallas import tpu as pltpu

F32 = jnp.float32
BF16 = jnp.bfloat16

LN_EPS = 1e-5
ROPE_BASE = 10000.0
LATENT_GRID_W = 64
MOD_ROWS = 8
V7X_VMEM_LIMIT = 56 * 1024 * 1024
LANES = 128
SUBLANES = 8


def _cparams(*sem):
    return pltpu.CompilerParams(dimension_semantics=sem, vmem_limit_bytes=V7X_VMEM_LIMIT)


def _layer_norm(z, g, b):
    mu = jnp.mean(z, axis=-1, keepdims=True)
    zc = z - mu
    var = jnp.mean(jnp.square(zc), axis=-1, keepdims=True)
    return zc * lax.rsqrt(var + LN_EPS) * g + b


def _group_of_tile(i, tiles_p, tiles_per_s):
    return jnp.where(i < tiles_p, 0, 1 + jnp.maximum(i - tiles_p, 0) // tiles_per_s)


def _mod_chunk(mod_ref, g, k, d):
    return mod_ref[pl.ds(g, 1), k * d:(k + 1) * d]


def _pair_specs(tm, width, tiles_p, grid_rank=1):
    if grid_rank == 1:
        return [pl.BlockSpec((tm, width), lambda i: (jnp.minimum(i, tiles_p - 1), 0)),
                pl.BlockSpec((tm, width), lambda i: (jnp.maximum(i - tiles_p, 0), 0))]
    return [pl.BlockSpec((tm, width), lambda i, f: (jnp.minimum(i, tiles_p - 1), 0)),
            pl.BlockSpec((tm, width), lambda i, f: (jnp.maximum(i - tiles_p, 0), 0))]


def _pair_rows(i, tiles_p, p_ref, s_ref):
    return jnp.where(i < tiles_p, p_ref[...], s_ref[...])


def _modvec_kernel(cv_ref, w_ref, b_ref, o_ref):
    cv = cv_ref[...]
    s = (cv * jax.nn.sigmoid(cv)).astype(BF16)
    o_ref[0] = jnp.dot(s, w_ref[0].astype(BF16), preferred_element_type=F32) + b_ref[0]


def _modvec(cvec, w_mod, b_mod, tn=1024):
    depth, d, n = w_mod.shape
    return pl.pallas_call(
        _modvec_kernel,
        grid=(depth, n // tn),
        in_specs=[pl.BlockSpec((MOD_ROWS, d), lambda l, j: (0, 0)),
                  pl.BlockSpec((1, d, tn), lambda l, j: (l, 0, j)),
                  pl.BlockSpec((1, 1, tn), lambda l, j: (l, 0, j))],
        out_specs=pl.BlockSpec((1, MOD_ROWS, tn), lambda l, j: (l, 0, j)),
        out_shape=jax.ShapeDtypeStruct((depth, MOD_ROWS, n), F32),
        compiler_params=_cparams("arbitrary", "arbitrary"),
        name="modvec",
    )(cvec, w_mod, b_mod.reshape(depth, 1, n))


def _inproj_kernel(xp_ref, xs_ref, mod_ref, w_ref, ug_ref, u_ref, *, tiles_p, tiles_per_s, d, c):
    i = pl.program_id(0)
    g = _group_of_tile(i, tiles_p, tiles_per_s)
    shift = _mod_chunk(mod_ref, g, 0, d)
    scale = _mod_chunk(mod_ref, g, 1, d)
    h = (_pair_rows(i, tiles_p, xp_ref, xs_ref) * (1 + scale) + shift).astype(BF16)
    a_val = jnp.dot(h, w_ref[:, 0:c], preferred_element_type=F32)
    a_gate = jnp.dot(h, w_ref[:, c:2 * c], preferred_element_type=F32)
    ug_ref[...] = a_val * jax.nn.sigmoid(a_gate)
    u_ref[...] = jnp.dot(h, w_ref[:, 2 * c:3 * c], preferred_element_type=F32)


def _inproj(xp, xs, mod_l, w_in, tm, tiles_p, tiles_per_s):
    t, d = xp.shape[0] + xs.shape[0], xp.shape[1]
    c = w_in.shape[1] // 3
    kern = functools.partial(_inproj_kernel, tiles_p=tiles_p, tiles_per_s=tiles_per_s, d=d, c=c)
    return pl.pallas_call(
        kern,
        grid=(t // tm,),
        in_specs=_pair_specs(tm, d, tiles_p) + [pl.BlockSpec(mod_l.shape, lambda i: (0, 0)),
                                                pl.BlockSpec(w_in.shape, lambda i: (0, 0))],
        out_specs=[pl.BlockSpec((tm, c), lambda i: (i, 0)),
                   pl.BlockSpec((tm, c), lambda i: (i, 0))],
        out_shape=[jax.ShapeDtypeStruct((t, c), F32), jax.ShapeDtypeStruct((t, c), F32)],
        compiler_params=_cparams("arbitrary"),
        name="inproj",
    )(xp, xs, mod_l, w_in)


CONV_HALO = 16
CONV_ROWS = 32
CONV_COLS = 256


def _conv_kernel(prev_ref, cur_ref, next_ref, w_ref, b_ref, g_ref, beta_ref, o_ref, pad_scr, sh_scr, conv_scr,
                 *, chunks_p, chunks_s, n_chunks_p, width, lc, c):
    i = pl.program_id(0)
    in_p = i < n_chunks_p
    k = jnp.where(in_p, i % chunks_p, jnp.maximum(i - n_chunks_p, 0) % chunks_s)
    last = jnp.where(in_p, chunks_p - 1, chunks_s - 1)
    has_prev = (k > 0).astype(F32)
    has_next = (k < last).astype(F32)
    pad_scr[0:CONV_HALO, :] = prev_ref[...] * has_prev
    pad_scr[CONV_HALO:CONV_HALO + lc, :] = cur_ref[...]
    pad_scr[CONV_HALO + lc:2 * CONV_HALO + lc, :] = next_ref[...] * has_next
    off = CONV_HALO - width // 2
    span = lc + CONV_HALO + SUBLANES
    for sft in range(SUBLANES):
        sh_scr[sft] = pad_scr[sft:sft + span, :]

    for r0 in range(0, lc, CONV_ROWS):
        for cb in range(c // CONV_COLS):
            cs = slice(cb * CONV_COLS, (cb + 1) * CONV_COLS)
            acc = jnp.zeros((CONV_ROWS, CONV_COLS), F32)
            for kk in range(width):
                whole, sft = divmod(kk + off, SUBLANES)
                base = r0 + whole * SUBLANES
                acc = acc + sh_scr[sft, base:base + CONV_ROWS, cs] * w_ref[kk:kk + 1, cs]
            conv_scr[r0:r0 + CONV_ROWS, cs] = acc + b_ref[:, cs]
    y = _layer_norm(conv_scr[...], g_ref[...], beta_ref[...])
    o_ref[...] = (y * jax.nn.sigmoid(y)).astype(o_ref.dtype)


def _conv_module(ug, w_dw, b_dw, ln_g, ln_b, lc, lp, ls, tp):
    t, c = ug.shape
    width = w_dw.shape[0]
    assert width // 2 < CONV_HALO and lc % CONV_HALO == 0 and lp % lc == 0 and ls % lc == 0
    hb = lc // CONV_HALO
    n_halo_blocks = t // CONV_HALO
    kern = functools.partial(_conv_kernel, chunks_p=lp // lc, chunks_s=ls // lc, n_chunks_p=tp // lc,
                             width=width, lc=lc, c=c)
    vec = lambda a: a.reshape(1, c)
    return pl.pallas_call(
        kern,
        grid=(t // lc,),
        in_specs=[pl.BlockSpec((CONV_HALO, c), lambda i: (jnp.maximum(i * hb - 1, 0), 0)),
                  pl.BlockSpec((lc, c), lambda i: (i, 0)),
                  pl.BlockSpec((CONV_HALO, c), lambda i: (jnp.minimum((i + 1) * hb, n_halo_blocks - 1), 0)),
                  pl.BlockSpec((width, c), lambda i: (0, 0)),
                  pl.BlockSpec((1, c), lambda i: (0, 0)),
                  pl.BlockSpec((1, c), lambda i: (0, 0)),
                  pl.BlockSpec((1, c), lambda i: (0, 0))],
        out_specs=pl.BlockSpec((lc, c), lambda i: (i, 0)),
        out_shape=jax.ShapeDtypeStruct((t, c), BF16),
        scratch_shapes=[pltpu.VMEM((lc + 2 * CONV_HALO, c), F32),
                        pltpu.VMEM((SUBLANES, lc + CONV_HALO + SUBLANES, c), F32), pltpu.VMEM((lc, c), F32)],
        compiler_params=_cparams("arbitrary"),
        name="conv_module",
    )(ug, ug, ug, w_dw, vec(b_dw), vec(ln_g), vec(ln_b))


S5_T = 16
S5_GB = 8


def _s5_prep_kernel(bt_re_ref, bt_im_ref, la_re_ref, la_im_ref, dta_ref, ct_re_ref, ct_im_ref, lb_re_ref, lb_im_ref,
                    dtb_ref, be_re_ref, be_im_ref, cs_re_ref, cs_ni_ref, kc_ref, a_re_ref, a_im_ref, *, n, p):
    t = S5_T
    fwd = pl.program_id(0) == 0

    lam_re, lam_im = la_re_ref[0, 0], la_im_ref[0, 0]
    dt = jnp.exp(dta_ref[0, 0])
    mag = jnp.exp(lam_re * dt)
    ar, ai = mag * jnp.cos(lam_im * dt), mag * jnp.sin(lam_im * dt)
    den = jnp.square(lam_re) + jnp.square(lam_im)
    coef_re = ((ar - 1) * lam_re + ai * lam_im) / den
    coef_im = (ai * lam_re - (ar - 1) * lam_im) / den
    b_re, b_im = bt_re_ref[0, 0], bt_im_ref[0, 0]
    bb_re = coef_re * b_re - coef_im * b_im
    bb_im = coef_re * b_im + coef_im * b_re
    rows = bb_re.shape[0]
    pw = []
    wr, wi = bb_re, bb_im
    pr, pi = jnp.ones_like(ar), jnp.zeros_like(ar)
    for k in range(t):
        pw.append((wr, wi))
        wr, wi = wr * ar - wi * ai, wr * ai + wi * ar
        pr, pi = pr * ar - pi * ai, pr * ai + pi * ar
    a_re_ref[0, 0] = pr
    a_im_ref[0, 0] = pi
    for j in range(t):
        be_re_ref[0, 0, j * rows:(j + 1) * rows, :] = jnp.where(fwd, pw[t - 1 - j][0], pw[j][0])
        be_im_ref[0, 0, j * rows:(j + 1) * rows, :] = jnp.where(fwd, pw[t - 1 - j][1], pw[j][1])

    ct_re, ct_im = ct_re_ref[0, 0], ct_im_ref[0, 0]
    shp = ct_re.shape
    lam_re, lam_im = lb_re_ref[0, 0], lb_im_ref[0, 0]
    dt = jnp.exp(dtb_ref[0, 0])
    mag = jnp.exp(lam_re * dt)
    sq_re = jnp.broadcast_to(mag * jnp.cos(lam_im * dt), shp)
    sq_im = jnp.broadcast_to(mag * jnp.sin(lam_im * dt), shp)
    blk = lax.broadcasted_iota(jnp.int32, shp, 1) // p
    k1 = jnp.where(fwd, blk + 1, t - blk)
    qr, qi = jnp.ones(shp, F32), jnp.zeros(shp, F32)
    n_bits = t.bit_length()
    for bit in range(n_bits):
        take = ((k1 >> bit) & 1) == 1
        qr, qi = (jnp.where(take, qr * sq_re - qi * sq_im, qr), jnp.where(take, qr * sq_im + qi * sq_re, qi))
        if bit + 1 < n_bits:
            sq_re, sq_im = sq_re * sq_re - sq_im * sq_im, 2.0 * (sq_re * sq_im)
    v_re = ct_re * qr - ct_im * qi
    v_im = ct_re * qi + ct_im * qr
    cs_re_ref[0, 0] = v_re
    cs_ni_ref[0, 0] = -v_im
    lane = lax.broadcasted_iota(jnp.int32, shp, 1)
    w = shp[1]
    v0_re = jnp.where(fwd, jnp.where(lane < p, ct_re, pltpu.roll(v_re, p, 1)),
                      jnp.where(lane >= w - p, ct_re, pltpu.roll(v_re, w - p, 1)))
    v0_im = jnp.where(fwd, jnp.where(lane < p, ct_im, pltpu.roll(v_im, p, 1)),
                      jnp.where(lane >= w - p, ct_im, pltpu.roll(v_im, w - p, 1)))
    hi = lax.Precision.HIGHEST
    for g in range(S5_GB):
        ra, rb = slice(g * p, (g + 1) * p), slice(g * n, (g + 1) * n)
        kc_ref[0, 0, ra, :] = (jnp.dot(bb_re[ra], v0_re[rb], precision=hi, preferred_element_type=F32)
                               - jnp.dot(bb_im[ra], v0_im[rb], precision=hi, preferred_element_type=F32))


def _s5_prep(lam_re, lam_im, log_dt, b_re, b_im, c_re, c_im):
    _, g, n, p = b_re.shape
    t = S5_T
    nb = g // S5_GB
    ra, rb = S5_GB * p, S5_GB * n
    lay_a = lambda a: jnp.broadcast_to(a[:, :, None, :], (2, g, p, n)).reshape(2, nb, ra, n)
    lay_b = lambda a: a.reshape(2, nb, rb, 1)
    bt = lambda a: a.transpose(0, 1, 3, 2).reshape(2, nb, ra, n)
    ct = lambda a: jnp.broadcast_to(a.transpose(0, 1, 3, 2)[:, :, :, None, :], (2, g, n, t, p)).reshape(2, nb, rb, t * p)
    dt_g = jnp.broadcast_to(log_dt[:, :, None], (2, g, n))
    blk = lambda r, c: pl.BlockSpec((1, 1, r, c), lambda d, i: (d, i, 0, 0))
    shp = lambda r, c: jax.ShapeDtypeStruct((2, nb, r, c), F32)
    kern = functools.partial(_s5_prep_kernel, n=n, p=p)
    return pl.pallas_call(
        kern,
        grid=(2, nb),
        in_specs=[blk(ra, n)] * 5 + [blk(rb, t * p)] * 2 + [blk(rb, 1)] * 3,
        out_specs=[blk(t * ra, n), blk(t * ra, n), blk(rb, t * p), blk(rb, t * p), blk(ra, t * p), blk(ra, n), blk(ra, n)],
        out_shape=[shp(t * ra, n), shp(t * ra, n), shp(rb, t * p), shp(rb, t * p), shp(ra, t * p), shp(ra, n), shp(ra, n)],
        compiler_params=_cparams("arbitrary", "arbitrary"),
        name="s5_prep",
    )(bt(b_re), bt(b_im), lay_a(lam_re), lay_a(lam_im), lay_a(dt_g), ct(c_re), ct(c_im),
      lay_b(lam_re), lay_b(lam_im), lay_b(dt_g))


def _s5_expand(src, tile, row_div, row_mod, lane_div, lane_mod, precision=None):
    full = jnp.dot(src, tile, precision=precision, preferred_element_type=F32)
    r = lax.broadcasted_iota(jnp.int32, full.shape, 0) // row_div % row_mod
    l = lax.broadcasted_iota(jnp.int32, full.shape, 1) // lane_div % lane_mod
    return jnp.where(r == l, full, 0.0)


def _s5_chunk_kernel(u_ref, be_re_ref, be_im_ref, cs_re_ref, cs_ni_ref, kc_ref, tk_ref, tb_ref, a_ref, h0_ref,
                     y_ref, hf_ref, m8_scr, be8_scr, cs8_scr, e_scr, *, geoms, n_tiles_p, n, p):
    t, gb = S5_T, S5_GB
    cw = gb * p
    sw = gb * n
    rows = geoms[0][0] * geoms[0][1]
    tile_i = pl.program_id(1)

    @pl.when(tile_i == 0)
    def _():
        tk, tb = tk_ref[...].astype(BF16), tb_ref[...].astype(BF16)
        step = 4 * cw
        for r0 in range(0, t * cw, step):
            for part, (ref, d) in enumerate(((be_re_ref, 0), (be_im_ref, 0), (be_re_ref, 1), (be_im_ref, 1))):
                be8_scr[r0:r0 + step, part * sw:(part + 1) * sw] = _s5_expand(
                    ref[d, 0, r0:r0 + step, :].astype(BF16), tb, p, gb, n, gb).astype(BF16)
        for part, (ref, d) in enumerate(((cs_re_ref, 0), (cs_ni_ref, 0), (cs_re_ref, 1), (cs_ni_ref, 1))):
            cs8_scr[part * sw:(part + 1) * sw, :] = _s5_expand(ref[d, 0].astype(BF16), tk, n, gb, p, gb).astype(BF16)
        hi = lax.Precision.HIGHEST
        bd_f = _s5_expand(kc_ref[0, 0], tk_ref[...], p, gb, p, gb, hi)
        bd_r = _s5_expand(kc_ref[1, 0], tk_ref[...], p, gb, p, gb, hi)
        tile_f = lambda k: bd_f[:, k * cw:(k + 1) * cw]
        tile_r = lambda k: bd_r[:, (t - 1 - k) * cw:(t - k) * cw]
        for j in range(t):
            for i in range(t):
                blk = tile_f(i - j) if i > j else tile_r(j - i) if i < j else tile_f(0) + tile_r(0)
                m8_scr[j * cw:(j + 1) * cw, i * cw:(i + 1) * cw] = blk.astype(BF16)

    x = u_ref[...].reshape(rows, t * cw).astype(BF16)
    e = jnp.dot(x, be8_scr[...], preferred_element_type=F32)
    n_slab = e.shape[1] // LANES
    per_part = sw // LANES
    for k in range(n_slab):
        e_scr[k] = e[:, k * LANES:(k + 1) * LANES]

    def scan(nseq, nc):
        loops = [list(range(per_part))] if nseq < 8 else [[q] for q in range(per_part)]
        for prs in loops:
            coef = [[jnp.broadcast_to(a_ref[r:r + 1, q * LANES:(q + 1) * LANES], (nseq, LANES)) for r in range(4)]
                    for q in prs]
            init = tuple(tuple(h0_ref[0, r, 0:nseq, q * LANES:(q + 1) * LANES] for r in range(4)) for q in prs)

            def step(c, carry):
                out = []
                for idx, q in enumerate(prs):
                    sf_re, sf_im, sr_re, sr_im = carry[idx]
                    af_re, af_im, ar_re, ar_im = coef[idx]
                    at_f = pl.ds(c, nseq, stride=nc)
                    at_r = pl.ds(nc - 1 - c, nseq, stride=nc)
                    ef_re, ef_im = e_scr[q, at_f, :], e_scr[per_part + q, at_f, :]
                    er_re, er_im = e_scr[2 * per_part + q, at_r, :], e_scr[3 * per_part + q, at_r, :]
                    e_scr[q, at_f, :] = sf_re
                    e_scr[per_part + q, at_f, :] = sf_im
                    e_scr[2 * per_part + q, at_r, :] = sr_re
                    e_scr[3 * per_part + q, at_r, :] = sr_im
                    out.append((af_re * sf_re - af_im * sf_im + ef_re, af_re * sf_im + af_im * sf_re + ef_im,
                                ar_re * sr_re - ar_im * sr_im + er_re, ar_re * sr_im + ar_im * sr_re + er_im))
                return tuple(out)

            fin = lax.fori_loop(0, nc, step, init)
            for idx, q in enumerate(prs):
                for r in range(4):
                    hf_ref[0, r, 0:nseq, q * LANES:(q + 1) * LANES] = fin[idx][r]

    hf_ref[...] = jnp.zeros_like(hf_ref)

    @pl.when(tile_i < n_tiles_p)
    def _():
        scan(*geoms[0])

    @pl.when(tile_i >= n_tiles_p)
    def _():
        scan(*geoms[1])

    s = jnp.concatenate([e_scr[k] for k in range(n_slab)], axis=-1).astype(BF16)
    y = (jnp.dot(x, m8_scr[...], preferred_element_type=F32)
         + jnp.dot(s, cs8_scr[...], preferred_element_type=F32))
    y_ref[...] = y.reshape(rows * t, cw)


def _s5_chunked(u, h0, prep, tile_k, tile_b, a4, *, tok, geoms, n_tiles_p, n, p):
    be_re, be_im, cs_re, cs_ni, kc = prep
    t, gb = S5_T, S5_GB
    cw, sw = gb * p, gb * n
    nb = u.shape[1] // cw
    n_tiles = u.shape[0] // tok
    ms = h0.shape[2]
    both = lambda a: pl.BlockSpec((2, 1) + a.shape[2:], lambda b, i: (0, b, 0, 0), pipeline_mode=pl.Buffered(1))
    const = lambda a: pl.BlockSpec(a.shape, lambda b, i: (0, 0), pipeline_mode=pl.Buffered(1))
    kern = functools.partial(_s5_chunk_kernel, geoms=geoms, n_tiles_p=n_tiles_p, n=n, p=p)
    return pl.pallas_call(
        kern,
        grid=(nb, n_tiles),
        in_specs=[pl.BlockSpec((tok, cw), lambda b, i: (i, b)),
                  both(be_re), both(be_im), both(cs_re), both(cs_ni), both(kc), const(tile_k), const(tile_b),
                  pl.BlockSpec((4, sw), lambda b, i: (0, b)),
                  pl.BlockSpec((1, 4, ms, sw), lambda b, i: (i, 0, 0, b))],
        out_specs=[pl.BlockSpec((tok, cw), lambda b, i: (i, b)),
                   pl.BlockSpec((1, 4, ms, sw), lambda b, i: (i, 0, 0, b))],
        out_shape=[jax.ShapeDtypeStruct(u.shape, F32), jax.ShapeDtypeStruct(h0.shape, F32)],
        scratch_shapes=[pltpu.VMEM((t * cw, t * cw), BF16), pltpu.VMEM((t * cw, 4 * sw), BF16),
                        pltpu.VMEM((4 * sw, t * cw), BF16), pltpu.VMEM((4 * sw // LANES, tok // t, LANES), F32)],
        compiler_params=_cparams("arbitrary", "arbitrary"),
        name="s5_chunked",
    )(u, be_re, be_im, cs_re, cs_ni, kc, tile_k, tile_b, a4, h0)


def _s5_tiles(n, p):
    t, gb = S5_T, S5_GB
    eye = lambda k: np.eye(k, dtype=np.float32)
    tile_k = np.einsum("ab,pq->apbq", eye(t), eye(p))[:, :, :, None, :] * np.ones((1, 1, 1, gb, 1), np.float32)
    tile_b = eye(n)[:, None, :] * np.ones((1, gb, 1), np.float32)
    return jnp.asarray(tile_k.reshape(t * p, t * gb * p)), jnp.asarray(tile_b.reshape(n, gb * n))


def _post_residual(x, y, gate, g, b, alpha):
    return _layer_norm(alpha * x + gate * y, g, b)


def _outab_kernel(xp_ref, xs_ref, mod_ref, yc_ref, ys_ref, u_ref, dsk_ref, wglu_ref, wout_ref, g_ref, b_ref, o_ref,
                  *, tiles_p, tiles_per_s, d, c, alpha):
    i = pl.program_id(0)
    grp = _group_of_tile(i, tiles_p, tiles_per_s)
    y_s = ys_ref[...] + dsk_ref[...] * u_ref[...]
    y_s = jax.nn.gelu(y_s)
    z = jnp.dot(y_s.astype(BF16), wglu_ref[...], preferred_element_type=F32)
    y_ssm = y_s * jax.nn.sigmoid(z)
    out = (jnp.dot(yc_ref[...], wout_ref[0:c, :], preferred_element_type=F32)
           + jnp.dot(y_ssm.astype(BF16), wout_ref[c:2 * c, :], preferred_element_type=F32))
    gate = _mod_chunk(mod_ref, grp, 2, d)
    o_ref[...] = _post_residual(_pair_rows(i, tiles_p, xp_ref, xs_ref), out, gate, g_ref[...], b_ref[...], alpha)


def _outab(xp, xs, mod_l, y_conv, y_scan, u, d_skip, w_glu, w_out, ln_g, ln_b, tm, tiles_p, tiles_per_s, alpha):
    t, d = xp.shape[0] + xs.shape[0], xp.shape[1]
    c = u.shape[1]
    kern = functools.partial(_outab_kernel, tiles_p=tiles_p, tiles_per_s=tiles_per_s, d=d, c=c, alpha=alpha)
    row = lambda w: pl.BlockSpec((tm, w), lambda i: (i, 0))
    full = lambda a: pl.BlockSpec(a.shape, lambda i: (0,) * a.ndim)
    d_skip, ln_g, ln_b = d_skip.reshape(1, c), ln_g.reshape(1, d), ln_b.reshape(1, d)
    return pl.pallas_call(
        kern,
        grid=(t // tm,),
        in_specs=_pair_specs(tm, d, tiles_p) + [full(mod_l), row(c), row(c), row(c), full(d_skip), full(w_glu),
                                                full(w_out), full(ln_g), full(ln_b)],
        out_specs=row(d),
        out_shape=jax.ShapeDtypeStruct((t, d), F32),
        compiler_params=_cparams("arbitrary"),
        name="outproj_ab",
    )(xp, xs, mod_l, y_conv, y_scan, u, d_skip, w_glu, w_out, ln_g, ln_b)


def _mlp_kernel(x_ref, mod_ref, w1_ref, w2_ref, g_ref, b_ref, *rest, tiles_p, tiles_per_s, d, n_f, alpha, split):
    outs, (h_scr, acc_scr) = rest[:-2], rest[-2:]
    i = pl.program_id(0)
    f = pl.program_id(1)
    grp = _group_of_tile(i, tiles_p, tiles_per_s)
    tm = h_scr.shape[0]
    halves = [slice(0, tm // 2), slice(tm // 2, tm)]

    def ffn(rows):
        a = jnp.dot(h_scr[rows, :], w1_ref[...], preferred_element_type=F32)
        a = jnp.square(jnp.maximum(a, 0.0)).astype(BF16)
        return jnp.dot(a, w2_ref[...], preferred_element_type=F32)

    @pl.when(f == 0)
    def _():
        shift = _mod_chunk(mod_ref, grp, 3, d)
        scale = _mod_chunk(mod_ref, grp, 4, d)
        for rows in halves:
            h_scr[rows, :] = (x_ref[rows, :] * (1 + scale) + shift).astype(BF16)
            acc_scr[rows, :] = ffn(rows)

    @pl.when(jnp.logical_and(f > 0, f < n_f - 1))
    def _():
        acc_scr[...] += ffn(slice(None))

    @pl.when(f == n_f - 1)
    def _():
        gate = _mod_chunk(mod_ref, grp, 5, d)
        for rows in halves:
            res = _post_residual(x_ref[rows, :], acc_scr[rows, :] + ffn(rows), gate, g_ref[...], b_ref[...], alpha)
            if split:
                @pl.when(i < tiles_p)
                def _():
                    outs[0][rows, :] = res

                @pl.when(i >= tiles_p)
                def _():
                    outs[1][rows, :] = res
            else:
                outs[0][rows, :] = res


def _mlp(x, mod_l, w1, w2, layer, ln_g, ln_b, tm, tf, tiles_p, tiles_per_s, alpha, split):
    t, d = x.shape
    n_f = w1.shape[2] // tf
    assert n_f >= 2
    kern = functools.partial(_mlp_kernel, tiles_p=tiles_p, tiles_per_s=tiles_per_s, d=d, n_f=n_f, alpha=alpha,
                             split=split)
    ln_g, ln_b = ln_g.reshape(1, d), ln_b.reshape(1, d)
    if split:
        out_specs = _pair_specs(tm, d, tiles_p, grid_rank=2)
        out_shape = [jax.ShapeDtypeStruct((tiles_p * tm, d), F32), jax.ShapeDtypeStruct((t - tiles_p * tm, d), F32)]
    else:
        out_specs = [pl.BlockSpec((tm, d), lambda i, f: (i, 0))]
        out_shape = [jax.ShapeDtypeStruct((t, d), F32)]
    return pl.pallas_call(
        kern,
        grid=(t // tm, n_f),
        in_specs=[pl.BlockSpec((tm, d), lambda i, f: (i, 0)),
                  pl.BlockSpec(mod_l.shape, lambda i, f: (0, 0)),
                  pl.BlockSpec((None, d, tf), lambda i, f: (layer, 0, f)),
                  pl.BlockSpec((None, tf, d), lambda i, f: (layer, f, 0)),
                  pl.BlockSpec((1, d), lambda i, f: (0, 0)),
                  pl.BlockSpec((1, d), lambda i, f: (0, 0))],
        out_specs=out_specs,
        out_shape=out_shape,
        scratch_shapes=[pltpu.VMEM((tm, d), BF16), pltpu.VMEM((tm, d), F32)],
        compiler_params=_cparams("arbitrary", "arbitrary"),
        name="mlp",
    )(x, mod_l, w1, w2, ln_g, ln_b)


def _rope_tables(n_pos, dk):
    ax = dk // 2
    half = ax // 2
    freqs = ROPE_BASE ** (-jnp.arange(half, dtype=F32) / half)
    pos = jnp.arange(n_pos)
    row = (pos // LATENT_GRID_W).astype(F32)
    col = (pos % LATENT_GRID_W).astype(F32)
    ang_r, ang_c = row[:, None] * freqs, col[:, None] * freqs
    cos = jnp.concatenate([jnp.cos(ang_r)] * 2 + [jnp.cos(ang_c)] * 2, axis=-1)
    sin = jnp.concatenate([-jnp.sin(ang_r), jnp.sin(ang_r), -jnp.sin(ang_c), jnp.sin(ang_c)], axis=-1)
    rep = LANES // dk
    return jnp.tile(cos, (1, rep)), jnp.tile(sin, (1, rep))


def _qkv_kernel(x_ref, mod_ref, w_ref, cos_ref, sin_ref, o_ref, kc_ref, vc_ref, h_scr,
                *, tiles_p, tiles_per_s, d, dk, dv, q_scale):
    i = pl.program_id(0)
    n = pl.program_id(1)

    @pl.when(n == 0)
    def _():
        g = _group_of_tile(i, tiles_p, tiles_per_s)
        shift = _mod_chunk(mod_ref, g, 0, d)
        scale = _mod_chunk(mod_ref, g, 1, d)
        h_scr[...] = (x_ref[...] * (1 + scale) + shift).astype(BF16)

    tn = o_ref.shape[-1]
    y = jnp.dot(h_scr[...], w_ref[:, pl.ds(pl.multiple_of(n * tn, tn), tn)], preferred_element_type=F32)
    quarter = dk // 4
    out_scale = jnp.where(n == 0, q_scale, 1.0).astype(F32)
    rotate = jnp.logical_and(n < 2, i >= tiles_p)
    cos, sin = cos_ref[...], sin_ref[...]
    lane = lax.broadcasted_iota(jnp.int32, cos.shape, 1)
    first = (lane % (2 * quarter)) < quarter
    for cb in range(tn // LANES):
        yb = y[:, cb * LANES:(cb + 1) * LANES]
        partner = jnp.where(first, pltpu.roll(yb, LANES - quarter, 1), pltpu.roll(yb, quarter, 1))
        out = jnp.where(rotate, yb * cos + partner * sin, yb)
        o_ref[0, :, cb * LANES:(cb + 1) * LANES] = (out * out_scale).astype(o_ref.dtype)

    @pl.when(jnp.logical_and(n == 1, i < tiles_p))
    def _():
        n_heads = kc_ref.shape[2]
        for m in range(2):
            for hh in range(n_heads):
                c0 = (m * n_heads + hh) * dk
                kc_ref[0, m, hh] = y[:, c0:c0 + dk]

    @pl.when(jnp.logical_and(n == 2, i < tiles_p))
    def _():
        for hh in range(vc_ref.shape[1]):
            vc_ref[0, hh] = y[:, hh * dv:(hh + 1) * dv]


def _qkv(x, mod_l, w_qkv, cos_t, sin_t, tm, tiles_p, tiles_per_s, dk, dv, bp, lp, q_scale):
    t, d = x.shape
    n_out = w_qkv.shape[1]
    assert n_out % 3 == 0 and lp % tm == 0
    tn = n_out // 3
    n_heads = tn // dv
    per_seq = lp // tm
    kern = functools.partial(_qkv_kernel, tiles_p=tiles_p, tiles_per_s=tiles_per_s, d=d, dk=dk, dv=dv,
                             q_scale=q_scale)
    pos_blk = lambda i, n: (jnp.maximum(i - tiles_p, 0) % tiles_per_s, 0)
    c_tile = lambda i: jnp.minimum(i, tiles_p - 1)
    return pl.pallas_call(
        kern,
        grid=(t // tm, 3),
        in_specs=[pl.BlockSpec((tm, d), lambda i, n: (i, 0)),
                  pl.BlockSpec(mod_l.shape, lambda i, n: (0, 0)),
                  pl.BlockSpec((d, n_out), lambda i, n: (0, 0), pipeline_mode=pl.Buffered(1)),
                  pl.BlockSpec((tm, LANES), pos_blk),
                  pl.BlockSpec((tm, LANES), pos_blk)],
        out_specs=[pl.BlockSpec((1, tm, tn), lambda i, n: (n, i, 0)),
                   pl.BlockSpec((1, 2, n_heads, tm, dk),
                                lambda i, n: (c_tile(i) // per_seq, 0, 0, c_tile(i) % per_seq, 0)),
                   pl.BlockSpec((1, n_heads, tm, dv),
                                lambda i, n: (c_tile(i) // per_seq, 0, c_tile(i) % per_seq, 0))],
        out_shape=[jax.ShapeDtypeStruct((3, t, tn), BF16),
                   jax.ShapeDtypeStruct((bp, 2, n_heads, lp, dk), F32),
                   jax.ShapeDtypeStruct((bp, n_heads, lp, dv), F32)],
        scratch_shapes=[pltpu.VMEM((tm, d), BF16)],
        compiler_params=_cparams("arbitrary", "arbitrary"),
        name="qkv_proj",
    )(x, mod_l, w_qkv, cos_t, sin_t)


def _attn_kernel(*refs, hp, dk, dv, scale, fold_scale, lam_init, has_cache):
    if has_cache:
        q1_ref, q2_ref, k1_ref, k2_ref, v_ref, ck_ref, cv_ref, lamv_ref, sg_ref, o_ref = refs
    else:
        q1_ref, q2_ref, k1_ref, k2_ref, v_ref, lamv_ref, sg_ref, o_ref = refs
        ck_ref = cv_ref = None
    lv = lamv_ref[...]
    lam = (jnp.exp(jnp.sum(lv[0:1] * lv[1:2], axis=-1, keepdims=True))
           - jnp.exp(jnp.sum(lv[2:3] * lv[3:4], axis=-1, keepdims=True)) + lam_init)
    nt = (((1,), (1,)), ((), ()))
    per_blk = LANES // dk

    def probs(q_ref, k_ref, m, head):
        cs = slice(head * dk, (head + 1) * dk)
        q = q_ref[:, cs]
        s = lax.dot_general(q, k_ref[:, cs], nt, preferred_element_type=F32)
        if not fold_scale:
            s = s * scale
        mx = jnp.max(s, axis=-1, keepdims=True)
        if has_cache:
            sc = lax.dot_general(q, ck_ref[0, 0, m, head].astype(BF16), nt, preferred_element_type=F32)
            if not fold_scale:
                sc = sc * scale
            mx = jnp.maximum(mx, jnp.max(sc, axis=-1, keepdims=True))
            ec = jnp.exp(sc - mx)
        e = jnp.exp(s - mx)
        den = jnp.sum(e, axis=-1, keepdims=True)
        if has_cache:
            den = den + jnp.sum(ec, axis=-1, keepdims=True)
            return e, ec, den
        return e, None, den

    for head in range(hp * per_blk):
        e1, e1c, d1 = probs(q1_ref, k1_ref, 0, head)
        e2, e2c, d2 = probs(q2_ref, k2_ref, 1, head)
        r1 = 1.0 / d1
        ratio = lam * d1 / d2
        vs = slice(head * dv, (head + 1) * dv)
        w = (e1 - e2 * ratio).astype(BF16)
        o = jnp.dot(w, v_ref[:, vs], preferred_element_type=F32)
        if has_cache:
            wc = (e1c - e2c * ratio).astype(BF16)
            o = o + jnp.dot(wc, cv_ref[0, 0, head].astype(BF16), preferred_element_type=F32)
        o = o * r1
        o = o * lax.rsqrt(jnp.mean(jnp.square(o), axis=-1, keepdims=True) + LN_EPS)
        o = o * sg_ref[...] * (1.0 - lam_init)
        o_ref[:, vs] = o.astype(o_ref.dtype)


def _attn_t_kernel(*refs, heads_step, dk, dv, scale, fold_scale, lam_init, has_cache):
    if has_cache:
        q1_ref, q2_ref, k1_ref, k2_ref, v_ref, ck_ref, cv_ref, lamv_ref, sg_ref, o_ref = refs
    else:
        q1_ref, q2_ref, k1_ref, k2_ref, v_ref, lamv_ref, sg_ref, o_ref = refs
        ck_ref = cv_ref = None
    lv = lamv_ref[...]
    lam = (jnp.exp(jnp.sum(lv[0:1] * lv[1:2], axis=-1, keepdims=True))
           - jnp.exp(jnp.sum(lv[2:3] * lv[3:4], axis=-1, keepdims=True)) + lam_init)
    nt = (((1,), (1,)), ((), ()))
    tn = (((0,), (0,)), ((), ()))
    per_blk = LANES // dk
    tq = q1_ref.shape[0]
    lane = lax.broadcasted_iota(jnp.int32, (tq, LANES), 1)

    def exps(kb, qb, m, head, sub):
        qm = jnp.where((lane >= sub * dk) & (lane < (sub + 1) * dk), qb, jnp.zeros_like(qb))
        s = lax.dot_general(kb, qm, nt, preferred_element_type=F32)
        if not fold_scale:
            s = s * scale
        mx = jnp.max(s, axis=0, keepdims=True)
        if has_cache:
            qc = qb[:, sub * dk:(sub + 1) * dk]
            sc = lax.dot_general(ck_ref[0, 0, m, head].astype(BF16), qc, nt, preferred_element_type=F32)
            if not fold_scale:
                sc = sc * scale
            mx = jnp.maximum(mx, jnp.max(sc, axis=0, keepdims=True))
            ec = jnp.exp(sc - mx)
        e = jnp.exp(s - mx)
        den = jnp.sum(e, axis=0, keepdims=True)
        if has_cache:
            return e, ec, den + jnp.sum(ec, axis=0, keepdims=True)
        return e, None, den

    for blk in range(heads_step // per_blk):
        bs = slice(blk * LANES, (blk + 1) * LANES)
        k1b, k2b = k1_ref[:, bs], k2_ref[:, bs]
        q1b, q2b = q1_ref[:, bs], q2_ref[:, bs]
        for sub in range(per_blk):
            head = blk * per_blk + sub
            e1, e1c, d1 = exps(k1b, q1b, 0, head, sub)
            e2, e2c, d2 = exps(k2b, q2b, 1, head, sub)
            r1 = 1.0 / d1
            ratio = lam * d1 / d2
            vs = slice(head * dv, (head + 1) * dv)
            w = (e1 - e2 * ratio).astype(BF16)
            o_t = lax.dot_general(v_ref[:, vs], w, tn, preferred_element_type=F32)
            if has_cache:
                wc = (e1c - e2c * ratio).astype(BF16)
                o_t = o_t + lax.dot_general(cv_ref[0, 0, head].astype(BF16), wc, tn, preferred_element_type=F32)
            o_t = o_t * r1
            o_t = o_t * lax.rsqrt(jnp.mean(jnp.square(o_t), axis=0, keepdims=True) + LN_EPS)
            o = o_t.T * sg_ref[...] * (1.0 - lam_init)
            o_ref[:, vs] = o.astype(o_ref.dtype)


def _softmax_scale(dk):
    scale = dk ** -0.5
    return scale, math.frexp(scale)[0] == 0.5


def _attention(qkv, lamv, subln_g, *, row0, n_seq, lq, tq, hp, n_heads, dk, dv, lam_init, cache=None,
               keys_on_sublanes=False):
    per_blk = LANES // dk
    heads_step = hp * per_blk
    n_hblk = n_heads // heads_step
    map2 = n_heads * dk // (hp * LANES)
    scale, fold_scale = _softmax_scale(dk)
    qb0, kb0 = row0 // tq, row0 // lq
    n_q = lq // tq
    q_spec = lambda off: pl.BlockSpec((None, tq, hp * LANES), lambda b, h, qi: (0, qb0 + b * n_q + qi, off + h))
    k_spec = lambda off: pl.BlockSpec((None, lq, hp * LANES), lambda b, h, qi: (1, kb0 + b, off + h))
    in_specs = [q_spec(0), q_spec(map2), k_spec(0), k_spec(map2),
                pl.BlockSpec((None, lq, heads_step * dv), lambda b, h, qi: (2, kb0 + b, h))]
    args = [qkv, qkv, qkv, qkv, qkv]
    if cache is not None:
        cache_k, cache_v, o_i = cache
        past = cache_k.shape[-2]
        in_specs += [pl.BlockSpec((1, 1, 2, heads_step, past, dk), lambda b, h, qi: (b, o_i, 0, h, 0, 0)),
                     pl.BlockSpec((1, 1, heads_step, past, dv), lambda b, h, qi: (b, o_i, h, 0, 0))]
        args += [cache_k, cache_v]
    in_specs += [pl.BlockSpec(lamv.shape, lambda b, h, qi: (0, 0)),
                 pl.BlockSpec((1, dv), lambda b, h, qi: (0, 0))]
    args += [lamv, subln_g.reshape(1, dv)]
    if keys_on_sublanes:
        kern = functools.partial(_attn_t_kernel, heads_step=heads_step, dk=dk, dv=dv, scale=scale,
                                 fold_scale=fold_scale, lam_init=lam_init, has_cache=cache is not None)
    else:
        kern = functools.partial(_attn_kernel, hp=hp, dk=dk, dv=dv, scale=scale, fold_scale=fold_scale,
                                 lam_init=lam_init, has_cache=cache is not None)
    return pl.pallas_call(
        kern,
        grid=(n_seq, n_hblk, n_q),
        in_specs=in_specs,
        out_specs=pl.BlockSpec((tq, heads_step * dv), lambda b, h, qi: (b * n_q + qi, h)),
        out_shape=jax.ShapeDtypeStruct((n_seq * lq, n_heads * dv), BF16),
        compiler_params=_cparams("arbitrary", "arbitrary", "arbitrary"),
        name="diff_attn_cache" if cache is not None else "diff_attn",
    )(*args)


def _outc_kernel(x_ref, mod_ref, op_ref, os_ref, w_ref, g_ref, b_ref, o_ref, *, tiles_p, tiles_per_s, d, alpha):
    i = pl.program_id(0)
    grp = _group_of_tile(i, tiles_p, tiles_per_s)
    out = jnp.dot(_pair_rows(i, tiles_p, op_ref, os_ref), w_ref[...], preferred_element_type=F32)
    gate = _mod_chunk(mod_ref, grp, 2, d)
    o_ref[...] = _post_residual(x_ref[...], out, gate, g_ref[...], b_ref[...], alpha)


def _outc(x, mod_l, o_p, o_s, w_out, ln_g, ln_b, tm, tiles_p, tiles_per_s, alpha):
    t, d = x.shape
    kin = o_p.shape[1]
    kern = functools.partial(_outc_kernel, tiles_p=tiles_p, tiles_per_s=tiles_per_s, d=d, alpha=alpha)
    ln_g, ln_b = ln_g.reshape(1, d), ln_b.reshape(1, d)
    full = lambda a: pl.BlockSpec(a.shape, lambda i: (0,) * a.ndim)
    return pl.pallas_call(
        kern,
        grid=(t // tm,),
        in_specs=[pl.BlockSpec((tm, d), lambda i: (i, 0)), full(mod_l)] + _pair_specs(tm, kin, tiles_p)
        + [full(w_out), full(ln_g), full(ln_b)],
        out_specs=pl.BlockSpec((tm, d), lambda i: (i, 0)),
        out_shape=jax.ShapeDtypeStruct((t, d), F32),
        compiler_params=_cparams("arbitrary"),
        name="outproj_c",
    )(x, mod_l, o_p, o_s, w_out, ln_g, ln_b)


def kernel(x_prompt, x_sample, state_s5_re, state_s5_im, cache_k, cache_v, c, c_ctx, w_mod, b_mod, ln_g, ln_b, w_in_ab, w_dw, b_dw, conv_ln_g, conv_ln_b, s5_lambda_re, s5_lambda_im, s5_log_dt, s5_b_re, s5_b_im, s5_c_re, s5_c_im, s5_d, w_glu, w_out_ab, w_qkv, lam_q1, lam_k1, lam_q2, lam_k2, subln_g, w_out_c, w_ff1, w_ff2):
    bp, lp, d = x_prompt.shape
    bs, ls, _ = x_sample.shape
    depth = w_mod.shape[0]
    tp, ts = bp * lp, bs * ls
    alpha = (2 * depth) ** 0.25
    assert 1 + bs <= MOD_ROWS

    tm = 256
    tm_mlp, tf = 512, 1024
    assert tp % tm_mlp == 0 and ls % tm_mlp == 0 and tp % ls == 0

    xp, xs = x_prompt.reshape(tp, d), x_sample.reshape(ts, d)
    cvec = jnp.zeros((MOD_ROWS, d), F32).at[0].set(c_ctx).at[1:1 + bs].set(c)
    mod = _modvec(cvec, w_mod, b_mod)

    g_ssm, n_ssm, p_ssm = s5_b_re.shape[2:]
    dk = lam_q1.shape[-1]
    dv = subln_g.shape[-1]
    n_heads = w_out_c.shape[1] // dv

    w_ff1_bf, w_ff2_bf = w_ff1.astype(BF16), w_ff2.astype(BF16)
    s_re, s_im, k_list, v_list = [], [], [], []
    for l in range(depth):
        mod_l = mod[l]
        if l % 2 == 0:
            e = l // 2
            if l > 0:
                xp, xs = x[:tp], x[tp:]
            ug, u = _inproj(xp, xs, mod_l, w_in_ab[e].astype(BF16), tm, tp // tm, ls // tm)
            y_conv = _conv_module(ug, w_dw[e], b_dw[e], conv_ln_g[e], conv_ln_b[e], min(lp, 256), lp, ls, tp)
            prep = _s5_prep(s5_lambda_re[e], s5_lambda_im[e], s5_log_dt[e],
                            s5_b_re[e], s5_b_im[e], s5_c_re[e], s5_c_im[e])
            a_re, a_im = prep[5], prep[6]
            gn = g_ssm * n_ssm
            nat = lambda a: a.reshape(2, g_ssm, p_ssm, n_ssm)[:, :, 0].reshape(2, gn)
            a4 = jnp.stack([nat(a_re)[0], nat(a_im)[0], nat(a_re)[1], nat(a_im)[1]])
            tile_k, tile_b = _s5_tiles(n_ssm, p_ssm)
            tok = min(4096, tp, ts)
            assert tok % lp == 0 and tok % ls == 0 and tp % tok == 0 and ts % tok == 0
            seq_p, seq_s, tiles_p5 = tok // lp, tok // ls, tp // tok
            ms = max(seq_p, seq_s)
            st = lambda a, dr: jnp.pad(a[:, e, dr].reshape(ts // tok, seq_s, gn), ((0, 0), (0, ms - seq_s), (0, 0)))
            h0_s = jnp.stack([st(state_s5_re, 0), st(state_s5_im, 0), st(state_s5_re, 1), st(state_s5_im, 1)], axis=1)
            h0 = jnp.concatenate([jnp.zeros((tiles_p5, 4, ms, gn), F32), h0_s], axis=0)
            y_scan, hf = _s5_chunked(u, h0, prep[:5], tile_k, tile_b, a4, tok=tok,
                                     geoms=((seq_p, lp // S5_T), (seq_s, ls // S5_T)), n_tiles_p=tiles_p5,
                                     n=n_ssm, p=p_ssm)
            hf = hf[:tiles_p5, :, :seq_p].transpose(1, 0, 2, 3).reshape(2, 2, bp, g_ssm, n_ssm)
            s_re.append(hf[:, 0].transpose(1, 0, 2, 3))
            s_im.append(hf[:, 1].transpose(1, 0, 2, 3))
            x = _outab(xp, xs, mod_l, y_conv, y_scan, u, s5_d[e], w_glu[e].astype(BF16),
                       w_out_ab[e].astype(BF16), ln_g[l, 0], ln_b[l, 0], tm, tp // tm, ls // tm, alpha)
        else:
            o_i = l // 2
            lam_init = 0.8 - 0.6 * math.exp(-0.3 * l)
            cos_t, sin_t = _rope_tables(ls, dk)
            scale, fold_scale = _softmax_scale(dk)
            qkv, k_new, v_new = _qkv(x, mod_l, w_qkv[o_i].astype(BF16), cos_t, sin_t, tm, tp // tm, ls // tm,
                                     dk, dv, bp, lp, scale if fold_scale else 1.0)
            lamv = jnp.stack([lam_q1[o_i], lam_k1[o_i], lam_q2[o_i], lam_k2[o_i]])
            geo = dict(n_heads=n_heads, dk=dk, dv=dv, lam_init=lam_init)
            o_p = _attention(qkv, lamv, subln_g[o_i], row0=0, n_seq=bp, lq=lp, tq=lp,
                             hp=n_heads * dk // LANES, keys_on_sublanes=True, **geo)
            o_s = _attention(qkv, lamv, subln_g[o_i], row0=tp, n_seq=bs, lq=ls, tq=256, hp=1,
                             cache=(cache_k, cache_v, o_i), **geo)
            k_list.append(k_new)
            v_list.append(v_new)
            x = _outc(x, mod_l, o_p, o_s, w_out_c[o_i].astype(BF16), ln_g[l, 0], ln_b[l, 0], tm, tp // tm, ls // tm, alpha)
        res = _mlp(x, mod_l, w_ff1_bf, w_ff2_bf, l, ln_g[l, 1], ln_b[l, 1],
                   tm_mlp, tf, tp // tm_mlp, ls // tm_mlp, alpha, split=l == depth - 1)
        x = res[0]

    return (res[0].reshape(bp, lp, d), res[1].reshape(bs, ls, d),
            jnp.stack(s_re, axis=1), jnp.stack(s_im, axis=1),
            jnp.stack(k_list, axis=1), jnp.stack(v_list, axis=1))
```

```python
import functools
import math

import jax
import jax.numpy as jnp
import numpy as np
from jax import lax
from jax.experimental import pallas as pl
from jax.experimental.pallas import tpu as pltpu

F32 = jnp.float32
BF16 = jnp.bfloat16

LN_EPS = 1e-5
ROPE_BASE = 10000.0
LATENT_GRID_W = 64
MOD_ROWS = 8
V7X_VMEM_LIMIT = 56 * 1024 * 1024
LANES = 128
SUBLANES = 8


def _cparams(*sem):
    return pltpu.CompilerParams(dimension_semantics=sem, vmem_limit_bytes=V7X_VMEM_LIMIT)


def _layer_norm(z, g, b):
    mu = jnp.mean(z, axis=-1, keepdims=True)
    zc = z - mu
    var = jnp.mean(jnp.square(zc), axis=-1, keepdims=True)
    return zc * lax.rsqrt(var + LN_EPS) * g + b


def _group_of_tile(i, tiles_p, tiles_per_s):
    return jnp.where(i < tiles_p, 0, 1 + jnp.maximum(i - tiles_p, 0) // tiles_per_s)


def _mod_chunk(mod_ref, g, k, d):
    return mod_ref[pl.ds(g, 1), k * d:(k + 1) * d]


def _pair_specs(tm, width, tiles_p, grid_rank=1):
    if grid_rank == 1:
        return [pl.BlockSpec((tm, width), lambda i: (jnp.minimum(i, tiles_p - 1), 0)),
                pl.BlockSpec((tm, width), lambda i: (jnp.maximum(i - tiles_p, 0), 0))]
    return [pl.BlockSpec((tm, width), lambda i, f: (jnp.minimum(i, tiles_p - 1), 0)),
            pl.BlockSpec((tm, width), lambda i, f: (jnp.maximum(i - tiles_p, 0), 0))]


def _pair_rows(i, tiles_p, p_ref, s_ref):
    return jnp.where(i < tiles_p, p_ref[...], s_ref[...])


def _modvec_kernel(cv_ref, w_ref, b_ref, o_ref):
    cv = cv_ref[...]
    s = (cv * jax.nn.sigmoid(cv)).astype(BF16)
    o_ref[0] = jnp.dot(s, w_ref[0].astype(BF16), preferred_element_type=F32) + b_ref[0]


def _modvec(cvec, w_mod, b_mod, tn=1024):
    depth, d, n = w_mod.shape
    return pl.pallas_call(
        _modvec_kernel,
        grid=(depth, n // tn),
        in_specs=[pl.BlockSpec((MOD_ROWS, d), lambda l, j: (0, 0)),
                  pl.BlockSpec((1, d, tn), lambda l, j: (l, 0, j)),
                  pl.BlockSpec((1, 1, tn), lambda l, j: (l, 0, j))],
        out_specs=pl.BlockSpec((1, MOD_ROWS, tn), lambda l, j: (l, 0, j)),
        out_shape=jax.ShapeDtypeStruct((depth, MOD_ROWS, n), F32),
        compiler_params=_cparams("arbitrary", "arbitrary"),
        name="modvec",
    )(cvec, w_mod, b_mod.reshape(depth, 1, n))


def _inproj_kernel(xp_ref, xs_ref, mod_ref, w_ref, ug_ref, u_ref, *, tiles_p, tiles_per_s, d, c):
    i = pl.program_id(0)
    g = _group_of_tile(i, tiles_p, tiles_per_s)
    shift = _mod_chunk(mod_ref, g, 0, d)
    scale = _mod_chunk(mod_ref, g, 1, d)
    h = (_pair_rows(i, tiles_p, xp_ref, xs_ref) * (1 + scale) + shift).astype(BF16)
    a_val = jnp.dot(h, w_ref[:, 0:c], preferred_element_type=F32)
    a_gate = jnp.dot(h, w_ref[:, c:2 * c], preferred_element_type=F32)
    ug_ref[...] = a_val * jax.nn.sigmoid(a_gate)
    u_ref[...] = jnp.dot(h, w_ref[:, 2 * c:3 * c], preferred_element_type=F32)


def _inproj(xp, xs, mod_l, w_in, tm, tiles_p, tiles_per_s):
    t, d = xp.shape[0] + xs.shape[0], xp.shape[1]
    c = w_in.shape[1] // 3
    kern = functools.partial(_inproj_kernel, tiles_p=tiles_p, tiles_per_s=tiles_per_s, d=d, c=c)
    return pl.pallas_call(
        kern,
        grid=(t // tm,),
        in_specs=_pair_specs(tm, d, tiles_p) + [pl.BlockSpec(mod_l.shape, lambda i: (0, 0)),
                                                pl.BlockSpec(w_in.shape, lambda i: (0, 0))],
        out_specs=[pl.BlockSpec((tm, c), lambda i: (i, 0)),
                   pl.BlockSpec((tm, c), lambda i: (i, 0))],
        out_shape=[jax.ShapeDtypeStruct((t, c), F32), jax.ShapeDtypeStruct((t, c), F32)],
        compiler_params=_cparams("arbitrary"),
        name="inproj",
    )(xp, xs, mod_l, w_in)


CONV_HALO = 16
CONV_ROWS = 32
CONV_COLS = 256


def _conv_kernel(prev_ref, cur_ref, next_ref, w_ref, b_ref, g_ref, beta_ref, o_ref, pad_scr, sh_scr, conv_scr,
                 *, chunks_p, chunks_s, n_chunks_p, width, lc, c):
    i = pl.program_id(0)
    in_p = i < n_chunks_p
    k = jnp.where(in_p, i % chunks_p, jnp.maximum(i - n_chunks_p, 0) % chunks_s)
    last = jnp.where(in_p, chunks_p - 1, chunks_s - 1)
    has_prev = (k > 0).astype(F32)
    has_next = (k < last).astype(F32)
    pad_scr[0:CONV_HALO, :] = prev_ref[...] * has_prev
    pad_scr[CONV_HALO:CONV_HALO + lc, :] = cur_ref[...]
    pad_scr[CONV_HALO + lc:2 * CONV_HALO + lc, :] = next_ref[...] * has_next
    off = CONV_HALO - width // 2
    span = lc + CONV_HALO + SUBLANES
    for sft in range(SUBLANES):
        sh_scr[sft] = pad_scr[sft:sft + span, :]

    for r0 in range(0, lc, CONV_ROWS):
        for cb in range(c // CONV_COLS):
            cs = slice(cb * CONV_COLS, (cb + 1) * CONV_COLS)
            acc = jnp.zeros((CONV_ROWS, CONV_COLS), F32)
            for kk in range(width):
                whole, sft = divmod(kk + off, SUBLANES)
                base = r0 + whole * SUBLANES
                acc = acc + sh_scr[sft, base:base + CONV_ROWS, cs] * w_ref[kk:kk + 1, cs]
            conv_scr[r0:r0 + CONV_ROWS, cs] = acc + b_ref[:, cs]
    y = _layer_norm(conv_scr[...], g_ref[...], beta_ref[...])
    o_ref[...] = (y * jax.nn.sigmoid(y)).astype(o_ref.dtype)


def _conv_module(ug, w_dw, b_dw, ln_g, ln_b, lc, lp, ls, tp):
    t, c = ug.shape
    width = w_dw.shape[0]
    assert width // 2 < CONV_HALO and lc % CONV_HALO == 0 and lp % lc == 0 and ls % lc == 0
    hb = lc // CONV_HALO
    n_halo_blocks = t // CONV_HALO
    kern = functools.partial(_conv_kernel, chunks_p=lp // lc, chunks_s=ls // lc, n_chunks_p=tp // lc,
                             width=width, lc=lc, c=c)
    vec = lambda a: a.reshape(1, c)
    return pl.pallas_call(
        kern,
        grid=(t // lc,),
        in_specs=[pl.BlockSpec((CONV_HALO, c), lambda i: (jnp.maximum(i * hb - 1, 0), 0)),
                  pl.BlockSpec((lc, c), lambda i: (i, 0)),
                  pl.BlockSpec((CONV_HALO, c), lambda i: (jnp.minimum((i + 1) * hb, n_halo_blocks - 1), 0)),
                  pl.BlockSpec((width, c), lambda i: (0, 0)),
                  pl.BlockSpec((1, c), lambda i: (0, 0)),
                  pl.BlockSpec((1, c), lambda i: (0, 0)),
                  pl.BlockSpec((1, c), lambda i: (0, 0))],
        out_specs=pl.BlockSpec((lc, c), lambda i: (i, 0)),
        out_shape=jax.ShapeDtypeStruct((t, c), BF16),
        scratch_shapes=[pltpu.VMEM((lc + 2 * CONV_HALO, c), F32),
                        pltpu.VMEM((SUBLANES, lc + CONV_HALO + SUBLANES, c), F32), pltpu.VMEM((lc, c), F32)],
        compiler_params=_cparams("arbitrary"),
        name="conv_module",
    )(ug, ug, ug, w_dw, vec(b_dw), vec(ln_g), vec(ln_b))


S5_T = 16
S5_GB = 8


def _s5_prep_kernel(bt_re_ref, bt_im_ref, la_re_ref, la_im_ref, dta_ref, ct_re_ref, ct_im_ref, lb_re_ref, lb_im_ref,
                    dtb_ref, be_re_ref, be_im_ref, cs_re_ref, cs_ni_ref, kc_ref, a_re_ref, a_im_ref, *, n, p):
    t = S5_T
    fwd = pl.program_id(0) == 0

    lam_re, lam_im = la_re_ref[0, 0], la_im_ref[0, 0]
    dt = jnp.exp(dta_ref[0, 0])
    mag = jnp.exp(lam_re * dt)
    ar, ai = mag * jnp.cos(lam_im * dt), mag * jnp.sin(lam_im * dt)
    den = jnp.square(lam_re) + jnp.square(lam_im)
    coef_re = ((ar - 1) * lam_re + ai * lam_im) / den
    coef_im = (ai * lam_re - (ar - 1) * lam_im) / den
    b_re, b_im = bt_re_ref[0, 0], bt_im_ref[0, 0]
    bb_re = coef_re * b_re - coef_im * b_im
    bb_im = coef_re * b_im + coef_im * b_re
    rows = bb_re.shape[0]
    pw = []
    wr, wi = bb_re, bb_im
    pr, pi = jnp.ones_like(ar), jnp.zeros_like(ar)
    for k in range(t):
        pw.append((wr, wi))
        wr, wi = wr * ar - wi * ai, wr * ai + wi * ar
        pr, pi = pr * ar - pi * ai, pr * ai + pi * ar
    a_re_ref[0, 0] = pr
    a_im_ref[0, 0] = pi
    for j in range(t):
        be_re_ref[0, 0, j * rows:(j + 1) * rows, :] = jnp.where(fwd, pw[t - 1 - j][0], pw[j][0])
        be_im_ref[0, 0, j * rows:(j + 1) * rows, :] = jnp.where(fwd, pw[t - 1 - j][1], pw[j][1])

    ct_re, ct_im = ct_re_ref[0, 0], ct_im_ref[0, 0]
    shp = ct_re.shape
    lam_re, lam_im = lb_re_ref[0, 0], lb_im_ref[0, 0]
    dt = jnp.exp(dtb_ref[0, 0])
    mag = jnp.exp(lam_re * dt)
    sq_re = jnp.broadcast_to(mag * jnp.cos(lam_im * dt), shp)
    sq_im = jnp.broadcast_to(mag * jnp.sin(lam_im * dt), shp)
    blk = lax.broadcasted_iota(jnp.int32, shp, 1) // p
    k1 = jnp.where(fwd, blk + 1, t - blk)
    qr, qi = jnp.ones(shp, F32), jnp.zeros(shp, F32)
    n_bits = t.bit_length()
    for bit in range(n_bits):
        take = ((k1 >> bit) & 1) == 1
        qr, qi = (jnp.where(take, qr * sq_re - qi * sq_im, qr), jnp.where(take, qr * sq_im + qi * sq_re, qi))
        if bit + 1 < n_bits:
            sq_re, sq_im = sq_re * sq_re - sq_im * sq_im, 2.0 * (sq_re * sq_im)
    v_re = ct_re * qr - ct_im * qi
    v_im = ct_re * qi + ct_im * qr
    cs_re_ref[0, 0] = v_re
    cs_ni_ref[0, 0] = -v_im
    lane = lax.broadcasted_iota(jnp.int32, shp, 1)
    w = shp[1]
    v0_re = jnp.where(fwd, jnp.where(lane < p, ct_re, pltpu.roll(v_re, p, 1)),
                      jnp.where(lane >= w - p, ct_re, pltpu.roll(v_re, w - p, 1)))
    v0_im = jnp.where(fwd, jnp.where(lane < p, ct_im, pltpu.roll(v_im, p, 1)),
                      jnp.where(lane >= w - p, ct_im, pltpu.roll(v_im, w - p, 1)))
    hi = lax.Precision.HIGHEST
    for g in range(S5_GB):
        ra, rb = slice(g * p, (g + 1) * p), slice(g * n, (g + 1) * n)
        kc_ref[0, 0, ra, :] = (jnp.dot(bb_re[ra], v0_re[rb], precision=hi, preferred_element_type=F32)
                               - jnp.dot(bb_im[ra], v0_im[rb], precision=hi, preferred_element_type=F32))


def _s5_prep(lam_re, lam_im, log_dt, b_re, b_im, c_re, c_im):
    _, g, n, p = b_re.shape
    t = S5_T
    nb = g // S5_GB
    ra, rb = S5_GB * p, S5_GB * n
    lay_a = lambda a: jnp.broadcast_to(a[:, :, None, :], (2, g, p, n)).reshape(2, nb, ra, n)
    lay_b = lambda a: a.reshape(2, nb, rb, 1)
    bt = lambda a: a.transpose(0, 1, 3, 2).reshape(2, nb, ra, n)
    ct = lambda a: jnp.broadcast_to(a.transpose(0, 1, 3, 2)[:, :, :, None, :], (2, g, n, t, p)).reshape(2, nb, rb, t * p)
    dt_g = jnp.broadcast_to(log_dt[:, :, None], (2, g, n))
    blk = lambda r, c: pl.BlockSpec((1, 1, r, c), lambda d, i: (d, i, 0, 0))
    shp = lambda r, c: jax.ShapeDtypeStruct((2, nb, r, c), F32)
    kern = functools.partial(_s5_prep_kernel, n=n, p=p)
    return pl.pallas_call(
        kern,
        grid=(2, nb),
        in_specs=[blk(ra, n)] * 5 + [blk(rb, t * p)] * 2 + [blk(rb, 1)] * 3,
        out_specs=[blk(t * ra, n), blk(t * ra, n), blk(rb, t * p), blk(rb, t * p), blk(ra, t * p), blk(ra, n), blk(ra, n)],
        out_shape=[shp(t * ra, n), shp(t * ra, n), shp(rb, t * p), shp(rb, t * p), shp(ra, t * p), shp(ra, n), shp(ra, n)],
        compiler_params=_cparams("arbitrary", "arbitrary"),
        name="s5_prep",
    )(bt(b_re), bt(b_im), lay_a(lam_re), lay_a(lam_im), lay_a(dt_g), ct(c_re), ct(c_im),
      lay_b(lam_re), lay_b(lam_im), lay_b(dt_g))


def _s5_expand(src, tile, row_div, row_mod, lane_div, lane_mod, precision=None):
    full = jnp.dot(src, tile, precision=precision, preferred_element_type=F32)
    r = lax.broadcasted_iota(jnp.int32, full.shape, 0) // row_div % row_mod
    l = lax.broadcasted_iota(jnp.int32, full.shape, 1) // lane_div % lane_mod
    return jnp.where(r == l, full, 0.0)


def _s5_chunk_kernel(u_ref, be_re_ref, be_im_ref, cs_re_ref, cs_ni_ref, kc_ref, tk_ref, tb_ref, a_ref, h0_ref,
                     y_ref, hf_ref, m8_scr, be8_scr, cs8_scr, e_scr, *, geoms, n_tiles_p, n, p):
    t, gb = S5_T, S5_GB
    cw = gb * p
    sw = gb * n
    rows = geoms[0][0] * geoms[0][1]
    tile_i = pl.program_id(1)

    @pl.when(tile_i == 0)
    def _():
        tk, tb = tk_ref[...].astype(BF16), tb_ref[...].astype(BF16)
        step = 4 * cw
        for r0 in range(0, t * cw, step):
            for part, (ref, d) in enumerate(((be_re_ref, 0), (be_im_ref, 0), (be_re_ref, 1), (be_im_ref, 1))):
                be8_scr[r0:r0 + step, part * sw:(part + 1) * sw] = _s5_expand(
                    ref[d, 0, r0:r0 + step, :].astype(BF16), tb, p, gb, n, gb).astype(BF16)
        for part, (ref, d) in enumerate(((cs_re_ref, 0), (cs_ni_ref, 0), (cs_re_ref, 1), (cs_ni_ref, 1))):
            cs8_scr[part * sw:(part + 1) * sw, :] = _s5_expand(ref[d, 0].astype(BF16), tk, n, gb, p, gb).astype(BF16)
        hi = lax.Precision.HIGHEST
        bd_f = _s5_expand(kc_ref[0, 0], tk_ref[...], p, gb, p, gb, hi)
        bd_r = _s5_expand(kc_ref[1, 0], tk_ref[...], p, gb, p, gb, hi)
        tile_f = lambda k: bd_f[:, k * cw:(k + 1) * cw]
        tile_r = lambda k: bd_r[:, (t - 1 - k) * cw:(t - k) * cw]
        for j in range(t):
            for i in range(t):
                blk = tile_f(i - j) if i > j else tile_r(j - i) if i < j else tile_f(0) + tile_r(0)
                m8_scr[j * cw:(j + 1) * cw, i * cw:(i + 1) * cw] = blk.astype(BF16)

    x = u_ref[...].reshape(rows, t * cw).astype(BF16)
    e = jnp.dot(x, be8_scr[...], preferred_element_type=F32)
    n_slab = e.shape[1] // LANES
    per_part = sw // LANES
    for k in range(n_slab):
        e_scr[k] = e[:, k * LANES:(k + 1) * LANES]

    def scan(nseq, nc):
        loops = [list(range(per_part))] if nseq < 8 else [[q] for q in range(per_part)]
        for prs in loops:
            coef = [[jnp.broadcast_to(a_ref[r:r + 1, q * LANES:(q + 1) * LANES], (nseq, LANES)) for r in range(4)]
                    for q in prs]
            init = tuple(tuple(h0_ref[0, r, 0:nseq, q * LANES:(q + 1) * LANES] for r in range(4)) for q in prs)

            def step(c, carry):
                out = []
                for idx, q in enumerate(prs):
                    sf_re, sf_im, sr_re, sr_im = carry[idx]
                    af_re, af_im, ar_re, ar_im = coef[idx]
                    at_f = pl.ds(c, nseq, stride=nc)
                    at_r = pl.ds(nc - 1 - c, nseq, stride=nc)
                    ef_re, ef_im = e_scr[q, at_f, :], e_scr[per_part + q, at_f, :]
                    er_re, er_im = e_scr[2 * per_part + q, at_r, :], e_scr[3 * per_part + q, at_r, :]
                    e_scr[q, at_f, :] = sf_re
                    e_scr[per_part + q, at_f, :] = sf_im
                    e_scr[2 * per_part + q, at_r, :] = sr_re
                    e_scr[3 * per_part + q, at_r, :] = sr_im
                    out.append((af_re * sf_re - af_im * sf_im + ef_re, af_re * sf_im + af_im * sf_re + ef_im,
                                ar_re * sr_re - ar_im * sr_im + er_re, ar_re * sr_im + ar_im * sr_re + er_im))
                return tuple(out)

            fin = lax.fori_loop(0, nc, step, init)
            for idx, q in enumerate(prs):
                for r in range(4):
                    hf_ref[0, r, 0:nseq, q * LANES:(q + 1) * LANES] = fin[idx][r]

    hf_ref[...] = jnp.zeros_like(hf_ref)

    @pl.when(tile_i < n_tiles_p)
    def _():
        scan(*geoms[0])

    @pl.when(tile_i >= n_tiles_p)
    def _():
        scan(*geoms[1])

    s = jnp.concatenate([e_scr[k] for k in range(n_slab)], axis=-1).astype(BF16)
    y = (jnp.dot(x, m8_scr[...], preferred_element_type=F32)
         + jnp.dot(s, cs8_scr[...], preferred_element_type=F32))
    y_ref[...] = y.reshape(rows * t, cw)


def _s5_chunked(u, h0, prep, tile_k, tile_b, a4, *, tok, geoms, n_tiles_p, n, p):
    be_re, be_im, cs_re, cs_ni, kc = prep
    t, gb = S5_T, S5_GB
    cw, sw = gb * p, gb * n
    nb = u.shape[1] // cw
    n_tiles = u.shape[0] // tok
    ms = h0.shape[2]
    both = lambda a: pl.BlockSpec((2, 1) + a.shape[2:], lambda b, i: (0, b, 0, 0), pipeline_mode=pl.Buffered(1))
    const = lambda a: pl.BlockSpec(a.shape, lambda b, i: (0, 0), pipeline_mode=pl.Buffered(1))
    kern = functools.partial(_s5_chunk_kernel, geoms=geoms, n_tiles_p=n_tiles_p, n=n, p=p)
    return pl.pallas_call(
        kern,
        grid=(nb, n_tiles),
        in_specs=[pl.BlockSpec((tok, cw), lambda b, i: (i, b)),
                  both(be_re), both(be_im), both(cs_re), both(cs_ni), both(kc), const(tile_k), const(tile_b),
                  pl.BlockSpec((4, sw), lambda b, i: (0, b)),
                  pl.BlockSpec((1, 4, ms, sw), lambda b, i: (i, 0, 0, b))],
        out_specs=[pl.BlockSpec((tok, cw), lambda b, i: (i, b)),
                   pl.BlockSpec((1, 4, ms, sw), lambda b, i: (i, 0, 0, b))],
        out_shape=[jax.ShapeDtypeStruct(u.shape, F32), jax.ShapeDtypeStruct(h0.shape, F32)],
        scratch_shapes=[pltpu.VMEM((t * cw, t * cw), BF16), pltpu.VMEM((t * cw, 4 * sw), BF16),
                        pltpu.VMEM((4 * sw, t * cw), BF16), pltpu.VMEM((4 * sw // LANES, tok // t, LANES), F32)],
        compiler_params=_cparams("arbitrary", "arbitrary"),
        name="s5_chunked",
    )(u, be_re, be_im, cs_re, cs_ni, kc, tile_k, tile_b, a4, h0)


def _s5_tiles(n, p):
    t, gb = S5_T, S5_GB
    eye = lambda k: np.eye(k, dtype=np.float32)
    tile_k = np.einsum("ab,pq->apbq", eye(t), eye(p))[:, :, :, None, :] * np.ones((1, 1, 1, gb, 1), np.float32)
    tile_b = eye(n)[:, None, :] * np.ones((1, gb, 1), np.float32)
    return jnp.asarray(tile_k.reshape(t * p, t * gb * p)), jnp.asarray(tile_b.reshape(n, gb * n))


def _post_residual(x, y, gate, g, b, alpha):
    return _layer_norm(alpha * x + gate * y, g, b)


def _outab_kernel(xp_ref, xs_ref, mod_ref, yc_ref, ys_ref, u_ref, dsk_ref, wglu_ref, wout_ref, g_ref, b_ref, o_ref,
                  *, tiles_p, tiles_per_s, d, c, alpha):
    i = pl.program_id(0)
    grp = _group_of_tile(i, tiles_p, tiles_per_s)
    y_s = ys_ref[...] + dsk_ref[...] * u_ref[...]
    y_s = jax.nn.gelu(y_s)
    z = jnp.dot(y_s.astype(BF16), wglu_ref[...], preferred_element_type=F32)
    y_ssm = y_s * jax.nn.sigmoid(z)
    out = (jnp.dot(yc_ref[...], wout_ref[0:c, :], preferred_element_type=F32)
           + jnp.dot(y_ssm.astype(BF16), wout_ref[c:2 * c, :], preferred_element_type=F32))
    gate = _mod_chunk(mod_ref, grp, 2, d)
    o_ref[...] = _post_residual(_pair_rows(i, tiles_p, xp_ref, xs_ref), out, gate, g_ref[...], b_ref[...], alpha)


def _outab(xp, xs, mod_l, y_conv, y_scan, u, d_skip, w_glu, w_out, ln_g, ln_b, tm, tiles_p, tiles_per_s, alpha):
    t, d = xp.shape[0] + xs.shape[0], xp.shape[1]
    c = u.shape[1]
    kern = functools.partial(_outab_kernel, tiles_p=tiles_p, tiles_per_s=tiles_per_s, d=d, c=c, alpha=alpha)
    row = lambda w: pl.BlockSpec((tm, w), lambda i: (i, 0))
    full = lambda a: pl.BlockSpec(a.shape, lambda i: (0,) * a.ndim)
    d_skip, ln_g, ln_b = d_skip.reshape(1, c), ln_g.reshape(1, d), ln_b.reshape(1, d)
    return pl.pallas_call(
        kern,
        grid=(t // tm,),
        in_specs=_pair_specs(tm, d, tiles_p) + [full(mod_l), row(c), row(c), row(c), full(d_skip), full(w_glu),
                                                full(w_out), full(ln_g), full(ln_b)],
        out_specs=row(d),
        out_shape=jax.ShapeDtypeStruct((t, d), F32),
        compiler_params=_cparams("arbitrary"),
        name="outproj_ab",
    )(xp, xs, mod_l, y_conv, y_scan, u, d_skip, w_glu, w_out, ln_g, ln_b)


def _mlp_kernel(x_ref, mod_ref, w1_ref, w2_ref, g_ref, b_ref, *rest, tiles_p, tiles_per_s, d, n_f, alpha, split):
    outs, (h_scr, acc_scr) = rest[:-2], rest[-2:]
    i = pl.program_id(0)
    f = pl.program_id(1)
    grp = _group_of_tile(i, tiles_p, tiles_per_s)
    tm = h_scr.shape[0]
    halves = [slice(0, tm // 2), slice(tm // 2, tm)]

    def ffn(rows):
        a = jnp.dot(h_scr[rows, :], w1_ref[...], preferred_element_type=F32)
        a = jnp.square(jnp.maximum(a, 0.0)).astype(BF16)
        return jnp.dot(a, w2_ref[...], preferred_element_type=F32)

    @pl.when(f == 0)
    def _():
        shift = _mod_chunk(mod_ref, grp, 3, d)
        scale = _mod_chunk(mod_ref, grp, 4, d)
        for rows in halves:
            h_scr[rows, :] = (x_ref[rows, :] * (1 + scale) + shift).astype(BF16)
            acc_scr[rows, :] = ffn(rows)

    @pl.when(jnp.logical_and(f > 0, f < n_f - 1))
    def _():
        acc_scr[...] += ffn(slice(None))

    @pl.when(f == n_f - 1)
    def _():
        gate = _mod_chunk(mod_ref, grp, 5, d)
        for rows in halves:
            res = _post_residual(x_ref[rows, :], acc_scr[rows, :] + ffn(rows), gate, g_ref[...], b_ref[...], alpha)
            if split:
                @pl.when(i < tiles_p)
                def _():
                    outs[0][rows, :] = res

                @pl.when(i >= tiles_p)
                def _():
                    outs[1][rows, :] = res
            else:
                outs[0][rows, :] = res


def _mlp(x, mod_l, w1, w2, layer, ln_g, ln_b, tm, tf, tiles_p, tiles_per_s, alpha, split):
    t, d = x.shape
    n_f = w1.shape[2] // tf
    assert n_f >= 2
    kern = functools.partial(_mlp_kernel, tiles_p=tiles_p, tiles_per_s=tiles_per_s, d=d, n_f=n_f, alpha=alpha,
                             split=split)
    ln_g, ln_b = ln_g.reshape(1, d), ln_b.reshape(1, d)
    if split:
        out_specs = _pair_specs(tm, d, tiles_p, grid_rank=2)
        out_shape = [jax.ShapeDtypeStruct((tiles_p * tm, d), F32), jax.ShapeDtypeStruct((t - tiles_p * tm, d), F32)]
    else:
        out_specs = [pl.BlockSpec((tm, d), lambda i, f: (i, 0))]
        out_shape = [jax.ShapeDtypeStruct((t, d), F32)]
    return pl.pallas_call(
        kern,
        grid=(t // tm, n_f),
        in_specs=[pl.BlockSpec((tm, d), lambda i, f: (i, 0)),
                  pl.BlockSpec(mod_l.shape, lambda i, f: (0, 0)),
                  pl.BlockSpec((None, d, tf), lambda i, f: (layer, 0, f)),
                  pl.BlockSpec((None, tf, d), lambda i, f: (layer, f, 0)),
                  pl.BlockSpec((1, d), lambda i, f: (0, 0)),
                  pl.BlockSpec((1, d), lambda i, f: (0, 0))],
        out_specs=out_specs,
        out_shape=out_shape,
        scratch_shapes=[pltpu.VMEM((tm, d), BF16), pltpu.VMEM((tm, d), F32)],
        compiler_params=_cparams("arbitrary", "arbitrary"),
        name="mlp",
    )(x, mod_l, w1, w2, ln_g, ln_b)


def _rope_tables(n_pos, dk):
    ax = dk // 2
    half = ax // 2
    freqs = ROPE_BASE ** (-jnp.arange(half, dtype=F32) / half)
    pos = jnp.arange(n_pos)
    row = (pos // LATENT_GRID_W).astype(F32)
    col = (pos % LATENT_GRID_W).astype(F32)
    ang_r, ang_c = row[:, None] * freqs, col[:, None] * freqs
    cos = jnp.concatenate([jnp.cos(ang_r)] * 2 + [jnp.cos(ang_c)] * 2, axis=-1)
    sin = jnp.concatenate([-jnp.sin(ang_r), jnp.sin(ang_r), -jnp.sin(ang_c), jnp.sin(ang_c)], axis=-1)
    rep = LANES // dk
    return jnp.tile(cos, (1, rep)), jnp.tile(sin, (1, rep))


def _qkv_kernel(x_ref, mod_ref, w_ref, cos_ref, sin_ref, o_ref, kc_ref, vc_ref, h_scr,
                *, tiles_p, tiles_per_s, d, dk, dv, q_scale):
    i = pl.program_id(0)
    n = pl.program_id(1)

    @pl.when(n == 0)
    def _():
        g = _group_of_tile(i, tiles_p, tiles_per_s)
        shift = _mod_chunk(mod_ref, g, 0, d)
        scale = _mod_chunk(mod_ref, g, 1, d)
        h_scr[...] = (x_ref[...] * (1 + scale) + shift).astype(BF16)

    tn = o_ref.shape[-1]
    y = jnp.dot(h_scr[...], w_ref[:, pl.ds(pl.multiple_of(n * tn, tn), tn)], preferred_element_type=F32)
    quarter = dk // 4
    out_scale = jnp.where(n == 0, q_scale, 1.0).astype(F32)
    rotate = jnp.logical_and(n < 2, i >= tiles_p)
    cos, sin = cos_ref[...], sin_ref[...]
    lane = lax.broadcasted_iota(jnp.int32, cos.shape, 1)
    first = (lane % (2 * quarter)) < quarter
    for cb in range(tn // LANES):
        yb = y[:, cb * LANES:(cb + 1) * LANES]
        partner = jnp.where(first, pltpu.roll(yb, LANES - quarter, 1), pltpu.roll(yb, quarter, 1))
        out = jnp.where(rotate, yb * cos + partner * sin, yb)
        o_ref[0, :, cb * LANES:(cb + 1) * LANES] = (out * out_scale).astype(o_ref.dtype)

    @pl.when(jnp.logical_and(n == 1, i < tiles_p))
    def _():
        n_heads = kc_ref.shape[2]
        for m in range(2):
            for hh in range(n_heads):
                c0 = (m * n_heads + hh) * dk
                kc_ref[0, m, hh] = y[:, c0:c0 + dk]

    @pl.when(jnp.logical_and(n == 2, i < tiles_p))
    def _():
        for hh in range(vc_ref.shape[1]):
            vc_ref[0, hh] = y[:, hh * dv:(hh + 1) * dv]


def _qkv(x, mod_l, w_qkv, cos_t, sin_t, tm, tiles_p, tiles_per_s, dk, dv, bp, lp, q_scale):
    t, d = x.shape
    n_out = w_qkv.shape[1]
    assert n_out % 3 == 0 and lp % tm == 0
    tn = n_out // 3
    n_heads = tn // dv
    per_seq = lp // tm
    kern = functools.partial(_qkv_kernel, tiles_p=tiles_p, tiles_per_s=tiles_per_s, d=d, dk=dk, dv=dv,
                             q_scale=q_scale)
    pos_blk = lambda i, n: (jnp.maximum(i - tiles_p, 0) % tiles_per_s, 0)
    c_tile = lambda i: jnp.minimum(i, tiles_p - 1)
    return pl.pallas_call(
        kern,
        grid=(t // tm, 3),
        in_specs=[pl.BlockSpec((tm, d), lambda i, n: (i, 0)),
                  pl.BlockSpec(mod_l.shape, lambda i, n: (0, 0)),
                  pl.BlockSpec((d, n_out), lambda i, n: (0, 0), pipeline_mode=pl.Buffered(1)),
                  pl.BlockSpec((tm, LANES), pos_blk),
                  pl.BlockSpec((tm, LANES), pos_blk)],
        out_specs=[pl.BlockSpec((1, tm, tn), lambda i, n: (n, i, 0)),
                   pl.BlockSpec((1, 2, n_heads, tm, dk),
                                lambda i, n: (c_tile(i) // per_seq, 0, 0, c_tile(i) % per_seq, 0)),
                   pl.BlockSpec((1, n_heads, tm, dv),
                                lambda i, n: (c_tile(i) // per_seq, 0, c_tile(i) % per_seq, 0))],
        out_shape=[jax.ShapeDtypeStruct((3, t, tn), BF16),
                   jax.ShapeDtypeStruct((bp, 2, n_heads, lp, dk), F32),
                   jax.ShapeDtypeStruct((bp, n_heads, lp, dv), F32)],
        scratch_shapes=[pltpu.VMEM((tm, d), BF16)],
        compiler_params=_cparams("arbitrary", "arbitrary"),
        name="qkv_proj",
    )(x, mod_l, w_qkv, cos_t, sin_t)


def _attn_kernel(*refs, heads_step, dk, dv, scale, fold_scale, lam_init, has_cache):
    if has_cache:
        q1_ref, q2_ref, k1_ref, k2_ref, v_ref, ck_ref, cv_ref, lamv_ref, sg_ref, o_ref, s_scr, w_scr = refs
    else:
        q1_ref, q2_ref, k1_ref, k2_ref, v_ref, lamv_ref, sg_ref, o_ref, s_scr, w_scr = refs
        ck_ref = cv_ref = None
    lv = lamv_ref[...]
    lam = (jnp.exp(jnp.sum(lv[0:1] * lv[1:2], axis=-1, keepdims=True))
           - jnp.exp(jnp.sum(lv[2:3] * lv[3:4], axis=-1, keepdims=True)) + lam_init)
    nt = (((1,), (1,)), ((), ()))
    tn = (((0,), (0,)), ((), ()))
    per_blk = LANES // dk
    tq, lk = q1_ref.shape[0], k1_ref.shape[0]
    lane = lax.broadcasted_iota(jnp.int32, (tq, LANES), 1)

    for blk in range(heads_step // per_blk):
        bs = slice(blk * LANES, (blk + 1) * LANES)
        for m, (q_ref, k_ref) in enumerate(((q1_ref, k1_ref), (q2_ref, k2_ref))):
            kb, qb = k_ref[:, bs], q_ref[:, bs]
            for sub in range(per_blk):
                head = blk * per_blk + sub
                qm = jnp.where((lane >= sub * dk) & (lane < (sub + 1) * dk), qb, jnp.zeros_like(qb))
                s_scr[2 * head + m, 0:lk, :] = lax.dot_general(kb, qm, nt, preferred_element_type=F32)
                if has_cache:
                    qc = qb[:, sub * dk:(sub + 1) * dk]
                    s_scr[2 * head + m, lk:, :] = lax.dot_general(ck_ref[0, 0, m, head].astype(BF16), qc, nt,
                                                                  preferred_element_type=F32)
    dens = []
    for hm in range(2 * heads_step):
        s = s_scr[hm]
        if not fold_scale:
            s = s * scale
        e = jnp.exp(s - jnp.max(s, axis=0, keepdims=True))
        dens.append(jnp.sum(e, axis=0, keepdims=True))
        s_scr[hm] = e
    for head in range(heads_step):
        ratio = lam * dens[2 * head] / dens[2 * head + 1]
        w_scr[head] = (s_scr[2 * head] - s_scr[2 * head + 1] * ratio).astype(BF16)
    for head in range(heads_step):
        vs = slice(head * dv, (head + 1) * dv)
        o_t = lax.dot_general(v_ref[:, vs], w_scr[head, 0:lk, :], tn, preferred_element_type=F32)
        if has_cache:
            o_t = o_t + lax.dot_general(cv_ref[0, 0, head].astype(BF16), w_scr[head, lk:, :], tn,
                                        preferred_element_type=F32)
        o_t = o_t * (1.0 / dens[2 * head])
        o_t = o_t * lax.rsqrt(jnp.mean(jnp.square(o_t), axis=0, keepdims=True) + LN_EPS)
        o = o_t.T * sg_ref[...] * (1.0 - lam_init)
        o_ref[:, vs] = o.astype(o_ref.dtype)


def _softmax_scale(dk):
    scale = dk ** -0.5
    return scale, math.frexp(scale)[0] == 0.5


def _attention(qkv, lamv, subln_g, *, row0, n_seq, lq, tq, hp, n_heads, dk, dv, lam_init, cache=None):
    per_blk = LANES // dk
    heads_step = hp * per_blk
    n_hblk = n_heads // heads_step
    map2 = n_heads * dk // (hp * LANES)
    scale, fold_scale = _softmax_scale(dk)
    qb0, kb0 = row0 // tq, row0 // lq
    n_q = lq // tq
    q_spec = lambda off: pl.BlockSpec((None, tq, hp * LANES), lambda b, h, qi: (0, qb0 + b * n_q + qi, off + h))
    k_spec = lambda off: pl.BlockSpec((None, lq, hp * LANES), lambda b, h, qi: (1, kb0 + b, off + h))
    in_specs = [q_spec(0), q_spec(map2), k_spec(0), k_spec(map2),
                pl.BlockSpec((None, lq, heads_step * dv), lambda b, h, qi: (2, kb0 + b, h))]
    args = [qkv, qkv, qkv, qkv, qkv]
    if cache is not None:
        cache_k, cache_v, o_i = cache
        past = cache_k.shape[-2]
        in_specs += [pl.BlockSpec((1, 1, 2, heads_step, past, dk), lambda b, h, qi: (b, o_i, 0, h, 0, 0)),
                     pl.BlockSpec((1, 1, heads_step, past, dv), lambda b, h, qi: (b, o_i, h, 0, 0))]
        args += [cache_k, cache_v]
    in_specs += [pl.BlockSpec(lamv.shape, lambda b, h, qi: (0, 0)),
                 pl.BlockSpec((1, dv), lambda b, h, qi: (0, 0))]
    args += [lamv, subln_g.reshape(1, dv)]
    kern = functools.partial(_attn_kernel, heads_step=heads_step, dk=dk, dv=dv, scale=scale,
                             fold_scale=fold_scale, lam_init=lam_init, has_cache=cache is not None)
    lk_all = lq + (cache[0].shape[-2] if cache is not None else 0)
    scratch = [pltpu.VMEM((2 * heads_step, lk_all, tq), F32), pltpu.VMEM((heads_step, lk_all, tq), BF16)]
    return pl.pallas_call(
        kern,
        grid=(n_seq, n_hblk, n_q),
        in_specs=in_specs,
        scratch_shapes=scratch,
        out_specs=pl.BlockSpec((tq, heads_step * dv), lambda b, h, qi: (b * n_q + qi, h)),
        out_shape=jax.ShapeDtypeStruct((n_seq * lq, n_heads * dv), BF16),
        compiler_params=_cparams("arbitrary", "arbitrary", "arbitrary"),
        name="diff_attn_cache" if cache is not None else "diff_attn",
    )(*args)


def _outc_kernel(x_ref, mod_ref, op_ref, os_ref, w_ref, g_ref, b_ref, o_ref, *, tiles_p, tiles_per_s, d, alpha):
    i = pl.program_id(0)
    grp = _group_of_tile(i, tiles_p, tiles_per_s)
    out = jnp.dot(_pair_rows(i, tiles_p, op_ref, os_ref), w_ref[...], preferred_element_type=F32)
    gate = _mod_chunk(mod_ref, grp, 2, d)
    o_ref[...] = _post_residual(x_ref[...], out, gate, g_ref[...], b_ref[...], alpha)


def _outc(x, mod_l, o_p, o_s, w_out, ln_g, ln_b, tm, tiles_p, tiles_per_s, alpha):
    t, d = x.shape
    kin = o_p.shape[1]
    kern = functools.partial(_outc_kernel, tiles_p=tiles_p, tiles_per_s=tiles_per_s, d=d, alpha=alpha)
    ln_g, ln_b = ln_g.reshape(1, d), ln_b.reshape(1, d)
    full = lambda a: pl.BlockSpec(a.shape, lambda i: (0,) * a.ndim)
    return pl.pallas_call(
        kern,
        grid=(t // tm,),
        in_specs=[pl.BlockSpec((tm, d), lambda i: (i, 0)), full(mod_l)] + _pair_specs(tm, kin, tiles_p)
        + [full(w_out), full(ln_g), full(ln_b)],
        out_specs=pl.BlockSpec((tm, d), lambda i: (i, 0)),
        out_shape=jax.ShapeDtypeStruct((t, d), F32),
        compiler_params=_cparams("arbitrary"),
        name="outproj_c",
    )(x, mod_l, o_p, o_s, w_out, ln_g, ln_b)


def kernel(x_prompt, x_sample, state_s5_re, state_s5_im, cache_k, cache_v, c, c_ctx, w_mod, b_mod, ln_g, ln_b, w_in_ab, w_dw, b_dw, conv_ln_g, conv_ln_b, s5_lambda_re, s5_lambda_im, s5_log_dt, s5_b_re, s5_b_im, s5_c_re, s5_c_im, s5_d, w_glu, w_out_ab, w_qkv, lam_q1, lam_k1, lam_q2, lam_k2, subln_g, w_out_c, w_ff1, w_ff2):
    bp, lp, d = x_prompt.shape
    bs, ls, _ = x_sample.shape
    depth = w_mod.shape[0]
    tp, ts = bp * lp, bs * ls
    alpha = (2 * depth) ** 0.25
    assert 1 + bs <= MOD_ROWS

    tm = 256
    tm_mlp, tf = 512, 1024
    assert tp % tm_mlp == 0 and ls % tm_mlp == 0 and tp % ls == 0

    xp, xs = x_prompt.reshape(tp, d), x_sample.reshape(ts, d)
    cvec = jnp.zeros((MOD_ROWS, d), F32).at[0].set(c_ctx).at[1:1 + bs].set(c)
    mod = _modvec(cvec, w_mod, b_mod)

    g_ssm, n_ssm, p_ssm = s5_b_re.shape[2:]
    dk = lam_q1.shape[-1]
    dv = subln_g.shape[-1]
    n_heads = w_out_c.shape[1] // dv

    w_ff1_bf, w_ff2_bf = w_ff1.astype(BF16), w_ff2.astype(BF16)
    s_re, s_im, k_list, v_list = [], [], [], []
    for l in range(depth):
        mod_l = mod[l]
        if l % 2 == 0:
            e = l // 2
            if l > 0:
                xp, xs = x[:tp], x[tp:]
            ug, u = _inproj(xp, xs, mod_l, w_in_ab[e].astype(BF16), tm, tp // tm, ls // tm)
            y_conv = _conv_module(ug, w_dw[e], b_dw[e], conv_ln_g[e], conv_ln_b[e], min(lp, 256), lp, ls, tp)
            prep = _s5_prep(s5_lambda_re[e], s5_lambda_im[e], s5_log_dt[e],
                            s5_b_re[e], s5_b_im[e], s5_c_re[e], s5_c_im[e])
            a_re, a_im = prep[5], prep[6]
            gn = g_ssm * n_ssm
            nat = lambda a: a.reshape(2, g_ssm, p_ssm, n_ssm)[:, :, 0].reshape(2, gn)
            a4 = jnp.stack([nat(a_re)[0], nat(a_im)[0], nat(a_re)[1], nat(a_im)[1]])
            tile_k, tile_b = _s5_tiles(n_ssm, p_ssm)
            tok = min(4096, tp, ts)
            assert tok % lp == 0 and tok % ls == 0 and tp % tok == 0 and ts % tok == 0
            seq_p, seq_s, tiles_p5 = tok // lp, tok // ls, tp // tok
            ms = max(seq_p, seq_s)
            st = lambda a, dr: jnp.pad(a[:, e, dr].reshape(ts // tok, seq_s, gn), ((0, 0), (0, ms - seq_s), (0, 0)))
            h0_s = jnp.stack([st(state_s5_re, 0), st(state_s5_im, 0), st(state_s5_re, 1), st(state_s5_im, 1)], axis=1)
            h0 = jnp.concatenate([jnp.zeros((tiles_p5, 4, ms, gn), F32), h0_s], axis=0)
            y_scan, hf = _s5_chunked(u, h0, prep[:5], tile_k, tile_b, a4, tok=tok,
                                     geoms=((seq_p, lp // S5_T), (seq_s, ls // S5_T)), n_tiles_p=tiles_p5,
                                     n=n_ssm, p=p_ssm)
            hf = hf[:tiles_p5, :, :seq_p].transpose(1, 0, 2, 3).reshape(2, 2, bp, g_ssm, n_ssm)
            s_re.append(hf[:, 0].transpose(1, 0, 2, 3))
            s_im.append(hf[:, 1].transpose(1, 0, 2, 3))
            x = _outab(xp, xs, mod_l, y_conv, y_scan, u, s5_d[e], w_glu[e].astype(BF16),
                       w_out_ab[e].astype(BF16), ln_g[l, 0], ln_b[l, 0], tm, tp // tm, ls // tm, alpha)
        else:
            o_i = l // 2
            lam_init = 0.8 - 0.6 * math.exp(-0.3 * l)
            cos_t, sin_t = _rope_tables(ls, dk)
            scale, fold_scale = _softmax_scale(dk)
            qkv, k_new, v_new = _qkv(x, mod_l, w_qkv[o_i].astype(BF16), cos_t, sin_t, tm, tp // tm, ls // tm,
                                     dk, dv, bp, lp, scale if fold_scale else 1.0)
            lamv = jnp.stack([lam_q1[o_i], lam_k1[o_i], lam_q2[o_i], lam_k2[o_i]])
            geo = dict(n_heads=n_heads, dk=dk, dv=dv, lam_init=lam_init)
            o_p = _attention(qkv, lamv, subln_g[o_i], row0=0, n_seq=bp, lq=lp, tq=lp,
                             hp=n_heads * dk // LANES, **geo)
            o_s = _attention(qkv, lamv, subln_g[o_i], row0=tp, n_seq=bs, lq=ls, tq=256, hp=2,
                             cache=(cache_k, cache_v, o_i), **geo)
            k_list.append(k_new)
            v_list.append(v_new)
            x = _outc(x, mod_l, o_p, o_s, w_out_c[o_i].astype(BF16), ln_g[l, 0], ln_b[l, 0], tm, tp // tm, ls // tm, alpha)
        res = _mlp(x, mod_l, w_ff1_bf, w_ff2_bf, l, ln_g[l, 1], ln_b[l, 1],
                   tm_mlp, tf, tp // tm_mlp, ls // tm_mlp, alpha, split=l == depth - 1)
        x = res[0]

    return (res[0].reshape(bp, lp, d), res[1].reshape(bs, ls, d),
            jnp.stack(s_re, axis=1), jnp.stack(s_im, axis=1),
            jnp.stack(k_list, axis=1), jnp.stack(v_list, axis=1))
```

```python
import functools
import math

import jax
import jax.numpy as jnp
import numpy as np
from jax import lax
from jax.experimental import pallas as pl
from jax.experimental.pallas import tpu as pltpu

F32 = jnp.float32
BF16 = jnp.bfloat16

LN_EPS = 1e-5
ROPE_BASE = 10000.0
LATENT_GRID_W = 64
MOD_ROWS = 8
V7X_VMEM_LIMIT = 56 * 1024 * 1024
LANES = 128
SUBLANES = 8


def _cparams(*sem):
    return pltpu.CompilerParams(dimension_semantics=sem, vmem_limit_bytes=V7X_VMEM_LIMIT)


def _layer_norm(z, g, b):
    mu = jnp.mean(z, axis=-1, keepdims=True)
    zc = z - mu
    var = jnp.mean(jnp.square(zc), axis=-1, keepdims=True)
    return zc * lax.rsqrt(var + LN_EPS) * g + b


def _group_of_tile(i, tiles_p, tiles_per_s):
    return jnp.where(i < tiles_p, 0, 1 + jnp.maximum(i - tiles_p, 0) // tiles_per_s)


def _mod_chunk(mod_ref, g, k, d):
    return mod_ref[pl.ds(g, 1), k * d:(k + 1) * d]


def _pair_specs(tm, width, tiles_p, grid_rank=1):
    if grid_rank == 1:
        return [pl.BlockSpec((tm, width), lambda i: (jnp.minimum(i, tiles_p - 1), 0)),
                pl.BlockSpec((tm, width), lambda i: (jnp.maximum(i - tiles_p, 0), 0))]
    return [pl.BlockSpec((tm, width), lambda i, f: (jnp.minimum(i, tiles_p - 1), 0)),
            pl.BlockSpec((tm, width), lambda i, f: (jnp.maximum(i - tiles_p, 0), 0))]


def _pair_rows(i, tiles_p, p_ref, s_ref):
    return jnp.where(i < tiles_p, p_ref[...], s_ref[...])


def _modvec_kernel(cv_ref, w_ref, b_ref, o_ref):
    cv = cv_ref[...]
    s = (cv * jax.nn.sigmoid(cv)).astype(BF16)
    o_ref[0] = jnp.dot(s, w_ref[0].astype(BF16), preferred_element_type=F32) + b_ref[0]


def _modvec(cvec, w_mod, b_mod, tn=1024):
    depth, d, n = w_mod.shape
    return pl.pallas_call(
        _modvec_kernel,
        grid=(depth, n // tn),
        in_specs=[pl.BlockSpec((MOD_ROWS, d), lambda l, j: (0, 0)),
                  pl.BlockSpec((1, d, tn), lambda l, j: (l, 0, j)),
                  pl.BlockSpec((1, 1, tn), lambda l, j: (l, 0, j))],
        out_specs=pl.BlockSpec((1, MOD_ROWS, tn), lambda l, j: (l, 0, j)),
        out_shape=jax.ShapeDtypeStruct((depth, MOD_ROWS, n), F32),
        compiler_params=_cparams("arbitrary", "arbitrary"),
        name="modvec",
    )(cvec, w_mod, b_mod.reshape(depth, 1, n))


def _inproj_kernel(xp_ref, xs_ref, mod_ref, w_ref, ug_ref, u_ref, *, tiles_p, tiles_per_s, d, c):
    i = pl.program_id(0)
    g = _group_of_tile(i, tiles_p, tiles_per_s)
    shift = _mod_chunk(mod_ref, g, 0, d)
    scale = _mod_chunk(mod_ref, g, 1, d)
    h = (_pair_rows(i, tiles_p, xp_ref, xs_ref) * (1 + scale) + shift).astype(BF16)
    a_val = jnp.dot(h, w_ref[:, 0:c], preferred_element_type=F32)
    a_gate = jnp.dot(h, w_ref[:, c:2 * c], preferred_element_type=F32)
    ug_ref[...] = a_val * jax.nn.sigmoid(a_gate)
    u_ref[...] = jnp.dot(h, w_ref[:, 2 * c:3 * c], preferred_element_type=F32)


def _inproj(xp, xs, mod_l, w_in, tm, tiles_p, tiles_per_s):
    t, d = xp.shape[0] + xs.shape[0], xp.shape[1]
    c = w_in.shape[1] // 3
    kern = functools.partial(_inproj_kernel, tiles_p=tiles_p, tiles_per_s=tiles_per_s, d=d, c=c)
    return pl.pallas_call(
        kern,
        grid=(t // tm,),
        in_specs=_pair_specs(tm, d, tiles_p) + [pl.BlockSpec(mod_l.shape, lambda i: (0, 0)),
                                                pl.BlockSpec(w_in.shape, lambda i: (0, 0))],
        out_specs=[pl.BlockSpec((tm, c), lambda i: (i, 0)),
                   pl.BlockSpec((tm, c), lambda i: (i, 0))],
        out_shape=[jax.ShapeDtypeStruct((t, c), F32), jax.ShapeDtypeStruct((t, c), F32)],
        compiler_params=_cparams("arbitrary"),
        name="inproj",
    )(xp, xs, mod_l, w_in)


CONV_HALO = 16
CONV_ROWS = 32
CONV_COLS = 256


def _conv_kernel(prev_ref, cur_ref, next_ref, w_ref, b_ref, g_ref, beta_ref, o_ref, pad_scr, sh_scr, conv_scr,
                 *, chunks_p, chunks_s, n_chunks_p, width, lc, c):
    i = pl.program_id(0)
    in_p = i < n_chunks_p
    k = jnp.where(in_p, i % chunks_p, jnp.maximum(i - n_chunks_p, 0) % chunks_s)
    last = jnp.where(in_p, chunks_p - 1, chunks_s - 1)
    has_prev = (k > 0).astype(F32)
    has_next = (k < last).astype(F32)
    pad_scr[0:CONV_HALO, :] = prev_ref[...] * has_prev
    pad_scr[CONV_HALO:CONV_HALO + lc, :] = cur_ref[...]
    pad_scr[CONV_HALO + lc:2 * CONV_HALO + lc, :] = next_ref[...] * has_next
    off = CONV_HALO - width // 2
    span = lc + CONV_HALO + SUBLANES
    for sft in range(SUBLANES):
        sh_scr[sft] = pad_scr[sft:sft + span, :]

    for r0 in range(0, lc, CONV_ROWS):
        for cb in range(c // CONV_COLS):
            cs = slice(cb * CONV_COLS, (cb + 1) * CONV_COLS)
            acc = jnp.zeros((CONV_ROWS, CONV_COLS), F32)
            for kk in range(width):
                whole, sft = divmod(kk + off, SUBLANES)
                base = r0 + whole * SUBLANES
                acc = acc + sh_scr[sft, base:base + CONV_ROWS, cs] * w_ref[kk:kk + 1, cs]
            conv_scr[r0:r0 + CONV_ROWS, cs] = acc + b_ref[:, cs]
    y = _layer_norm(conv_scr[...], g_ref[...], beta_ref[...])
    o_ref[...] = (y * jax.nn.sigmoid(y)).astype(o_ref.dtype)


def _conv_module(ug, w_dw, b_dw, ln_g, ln_b, lc, lp, ls, tp):
    t, c = ug.shape
    width = w_dw.shape[0]
    assert width // 2 < CONV_HALO and lc % CONV_HALO == 0 and lp % lc == 0 and ls % lc == 0
    hb = lc // CONV_HALO
    n_halo_blocks = t // CONV_HALO
    kern = functools.partial(_conv_kernel, chunks_p=lp // lc, chunks_s=ls // lc, n_chunks_p=tp // lc,
                             width=width, lc=lc, c=c)
    vec = lambda a: a.reshape(1, c)
    return pl.pallas_call(
        kern,
        grid=(t // lc,),
        in_specs=[pl.BlockSpec((CONV_HALO, c), lambda i: (jnp.maximum(i * hb - 1, 0), 0)),
                  pl.BlockSpec((lc, c), lambda i: (i, 0)),
                  pl.BlockSpec((CONV_HALO, c), lambda i: (jnp.minimum((i + 1) * hb, n_halo_blocks - 1), 0)),
                  pl.BlockSpec((width, c), lambda i: (0, 0)),
                  pl.BlockSpec((1, c), lambda i: (0, 0)),
                  pl.BlockSpec((1, c), lambda i: (0, 0)),
                  pl.BlockSpec((1, c), lambda i: (0, 0))],
        out_specs=pl.BlockSpec((lc, c), lambda i: (i, 0)),
        out_shape=jax.ShapeDtypeStruct((t, c), BF16),
        scratch_shapes=[pltpu.VMEM((lc + 2 * CONV_HALO, c), F32),
                        pltpu.VMEM((SUBLANES, lc + CONV_HALO + SUBLANES, c), F32), pltpu.VMEM((lc, c), F32)],
        compiler_params=_cparams("arbitrary"),
        name="conv_module",
    )(ug, ug, ug, w_dw, vec(b_dw), vec(ln_g), vec(ln_b))


S5_T = 16
S5_GB = 8


def _s5_prep_kernel(bt_re_ref, bt_im_ref, la_re_ref, la_im_ref, dta_ref, ct_re_ref, ct_im_ref, lb_re_ref, lb_im_ref,
                    dtb_ref, be_re_ref, be_im_ref, cs_re_ref, cs_ni_ref, kc_ref, a_re_ref, a_im_ref, *, n, p):
    t = S5_T
    fwd = pl.program_id(0) == 0

    lam_re, lam_im = la_re_ref[0, 0], la_im_ref[0, 0]
    dt = jnp.exp(dta_ref[0, 0])
    mag = jnp.exp(lam_re * dt)
    ar, ai = mag * jnp.cos(lam_im * dt), mag * jnp.sin(lam_im * dt)
    den = jnp.square(lam_re) + jnp.square(lam_im)
    coef_re = ((ar - 1) * lam_re + ai * lam_im) / den
    coef_im = (ai * lam_re - (ar - 1) * lam_im) / den
    b_re, b_im = bt_re_ref[0, 0], bt_im_ref[0, 0]
    bb_re = coef_re * b_re - coef_im * b_im
    bb_im = coef_re * b_im + coef_im * b_re
    rows = bb_re.shape[0]
    pw = []
    wr, wi = bb_re, bb_im
    pr, pi = jnp.ones_like(ar), jnp.zeros_like(ar)
    for k in range(t):
        pw.append((wr, wi))
        wr, wi = wr * ar - wi * ai, wr * ai + wi * ar
        pr, pi = pr * ar - pi * ai, pr * ai + pi * ar
    a_re_ref[0, 0] = pr
    a_im_ref[0, 0] = pi
    for j in range(t):
        be_re_ref[0, 0, j * rows:(j + 1) * rows, :] = jnp.where(fwd, pw[t - 1 - j][0], pw[j][0])
        be_im_ref[0, 0, j * rows:(j + 1) * rows, :] = jnp.where(fwd, pw[t - 1 - j][1], pw[j][1])

    ct_re, ct_im = ct_re_ref[0, 0], ct_im_ref[0, 0]
    shp = ct_re.shape
    lam_re, lam_im = lb_re_ref[0, 0], lb_im_ref[0, 0]
    dt = jnp.exp(dtb_ref[0, 0])
    mag = jnp.exp(lam_re * dt)
    sq_re = jnp.broadcast_to(mag * jnp.cos(lam_im * dt), shp)
    sq_im = jnp.broadcast_to(mag * jnp.sin(lam_im * dt), shp)
    blk = lax.broadcasted_iota(jnp.int32, shp, 1) // p
    k1 = jnp.where(fwd, blk + 1, t - blk)
    qr, qi = jnp.ones(shp, F32), jnp.zeros(shp, F32)
    n_bits = t.bit_length()
    for bit in range(n_bits):
        take = ((k1 >> bit) & 1) == 1
        qr, qi = (jnp.where(take, qr * sq_re - qi * sq_im, qr), jnp.where(take, qr * sq_im + qi * sq_re, qi))
        if bit + 1 < n_bits:
            sq_re, sq_im = sq_re * sq_re - sq_im * sq_im, 2.0 * (sq_re * sq_im)
    v_re = ct_re * qr - ct_im * qi
    v_im = ct_re * qi + ct_im * qr
    cs_re_ref[0, 0] = v_re
    cs_ni_ref[0, 0] = -v_im
    lane = lax.broadcasted_iota(jnp.int32, shp, 1)
    w = shp[1]
    v0_re = jnp.where(fwd, jnp.where(lane < p, ct_re, pltpu.roll(v_re, p, 1)),
                      jnp.where(lane >= w - p, ct_re, pltpu.roll(v_re, w - p, 1)))
    v0_im = jnp.where(fwd, jnp.where(lane < p, ct_im, pltpu.roll(v_im, p, 1)),
                      jnp.where(lane >= w - p, ct_im, pltpu.roll(v_im, w - p, 1)))
    hi = lax.Precision.HIGHEST
    for g in range(S5_GB):
        ra, rb = slice(g * p, (g + 1) * p), slice(g * n, (g + 1) * n)
        kc_ref[0, 0, ra, :] = (jnp.dot(bb_re[ra], v0_re[rb], precision=hi, preferred_element_type=F32)
                               - jnp.dot(bb_im[ra], v0_im[rb], precision=hi, preferred_element_type=F32))


def _s5_prep(lam_re, lam_im, log_dt, b_re, b_im, c_re, c_im):
    _, g, n, p = b_re.shape
    t = S5_T
    nb = g // S5_GB
    ra, rb = S5_GB * p, S5_GB * n
    lay_a = lambda a: jnp.broadcast_to(a[:, :, None, :], (2, g, p, n)).reshape(2, nb, ra, n)
    lay_b = lambda a: a.reshape(2, nb, rb, 1)
    bt = lambda a: a.transpose(0, 1, 3, 2).reshape(2, nb, ra, n)
    ct = lambda a: jnp.broadcast_to(a.transpose(0, 1, 3, 2)[:, :, :, None, :], (2, g, n, t, p)).reshape(2, nb, rb, t * p)
    dt_g = jnp.broadcast_to(log_dt[:, :, None], (2, g, n))
    blk = lambda r, c: pl.BlockSpec((1, 1, r, c), lambda d, i: (d, i, 0, 0))
    shp = lambda r, c: jax.ShapeDtypeStruct((2, nb, r, c), F32)
    kern = functools.partial(_s5_prep_kernel, n=n, p=p)
    return pl.pallas_call(
        kern,
        grid=(2, nb),
        in_specs=[blk(ra, n)] * 5 + [blk(rb, t * p)] * 2 + [blk(rb, 1)] * 3,
        out_specs=[blk(t * ra, n), blk(t * ra, n), blk(rb, t * p), blk(rb, t * p), blk(ra, t * p), blk(ra, n), blk(ra, n)],
        out_shape=[shp(t * ra, n), shp(t * ra, n), shp(rb, t * p), shp(rb, t * p), shp(ra, t * p), shp(ra, n), shp(ra, n)],
        compiler_params=_cparams("arbitrary", "arbitrary"),
        name="s5_prep",
    )(bt(b_re), bt(b_im), lay_a(lam_re), lay_a(lam_im), lay_a(dt_g), ct(c_re), ct(c_im),
      lay_b(lam_re), lay_b(lam_im), lay_b(dt_g))


def _s5_expand(src, tile, row_div, row_mod, lane_div, lane_mod, precision=None):
    full = jnp.dot(src, tile, precision=precision, preferred_element_type=F32)
    r = lax.broadcasted_iota(jnp.int32, full.shape, 0) // row_div % row_mod
    l = lax.broadcasted_iota(jnp.int32, full.shape, 1) // lane_div % lane_mod
    return jnp.where(r == l, full, 0.0)


def _s5_chunk_kernel(u_ref, be_re_ref, be_im_ref, cs_re_ref, cs_ni_ref, kc_ref, tk_ref, tb_ref, a_ref, h0_ref,
                     y_ref, hf_ref, m8_scr, be8_scr, cs8_scr, e_scr, *, geoms, n_tiles_p, n, p):
    t, gb = S5_T, S5_GB
    cw = gb * p
    sw = gb * n
    rows = geoms[0][0] * geoms[0][1]
    tile_i = pl.program_id(1)

    @pl.when(tile_i == 0)
    def _():
        tk, tb = tk_ref[...].astype(BF16), tb_ref[...].astype(BF16)
        step = 4 * cw
        for r0 in range(0, t * cw, step):
            for part, (ref, d) in enumerate(((be_re_ref, 0), (be_im_ref, 0), (be_re_ref, 1), (be_im_ref, 1))):
                be8_scr[r0:r0 + step, part * sw:(part + 1) * sw] = _s5_expand(
                    ref[d, 0, r0:r0 + step, :].astype(BF16), tb, p, gb, n, gb).astype(BF16)
        for part, (ref, d) in enumerate(((cs_re_ref, 0), (cs_ni_ref, 0), (cs_re_ref, 1), (cs_ni_ref, 1))):
            cs8_scr[part * sw:(part + 1) * sw, :] = _s5_expand(ref[d, 0].astype(BF16), tk, n, gb, p, gb).astype(BF16)
        hi = lax.Precision.HIGHEST
        bd_f = _s5_expand(kc_ref[0, 0], tk_ref[...], p, gb, p, gb, hi)
        bd_r = _s5_expand(kc_ref[1, 0], tk_ref[...], p, gb, p, gb, hi)
        tile_f = lambda k: bd_f[:, k * cw:(k + 1) * cw]
        tile_r = lambda k: bd_r[:, (t - 1 - k) * cw:(t - k) * cw]
        for j in range(t):
            for i in range(t):
                blk = tile_f(i - j) if i > j else tile_r(j - i) if i < j else tile_f(0) + tile_r(0)
                m8_scr[j * cw:(j + 1) * cw, i * cw:(i + 1) * cw] = blk.astype(BF16)

    x = u_ref[...].reshape(rows, t * cw).astype(BF16)
    e = jnp.dot(x, be8_scr[...], preferred_element_type=F32)
    n_slab = e.shape[1] // LANES
    per_part = sw // LANES
    for k in range(n_slab):
        e_scr[k] = e[:, k * LANES:(k + 1) * LANES]

    def scan(nseq, nc):
        loops = [list(range(per_part))] if nseq < 8 else [[q] for q in range(per_part)]
        for prs in loops:
            coef = [[jnp.broadcast_to(a_ref[r:r + 1, q * LANES:(q + 1) * LANES], (nseq, LANES)) for r in range(4)]
                    for q in prs]
            init = tuple(tuple(h0_ref[0, r, 0:nseq, q * LANES:(q + 1) * LANES] for r in range(4)) for q in prs)

            def step(c, carry):
                out = []
                for idx, q in enumerate(prs):
                    sf_re, sf_im, sr_re, sr_im = carry[idx]
                    af_re, af_im, ar_re, ar_im = coef[idx]
                    at_f = pl.ds(c, nseq, stride=nc)
                    at_r = pl.ds(nc - 1 - c, nseq, stride=nc)
                    ef_re, ef_im = e_scr[q, at_f, :], e_scr[per_part + q, at_f, :]
                    er_re, er_im = e_scr[2 * per_part + q, at_r, :], e_scr[3 * per_part + q, at_r, :]
                    e_scr[q, at_f, :] = sf_re
                    e_scr[per_part + q, at_f, :] = sf_im
                    e_scr[2 * per_part + q, at_r, :] = sr_re
                    e_scr[3 * per_part + q, at_r, :] = sr_im
                    out.append((af_re * sf_re - af_im * sf_im + ef_re, af_re * sf_im + af_im * sf_re + ef_im,
                                ar_re * sr_re - ar_im * sr_im + er_re, ar_re * sr_im + ar_im * sr_re + er_im))
                return tuple(out)

            fin = lax.fori_loop(0, nc, step, init)
            for idx, q in enumerate(prs):
                for r in range(4):
                    hf_ref[0, r, 0:nseq, q * LANES:(q + 1) * LANES] = fin[idx][r]

    hf_ref[...] = jnp.zeros_like(hf_ref)

    @pl.when(tile_i < n_tiles_p)
    def _():
        scan(*geoms[0])

    @pl.when(tile_i >= n_tiles_p)
    def _():
        scan(*geoms[1])

    s = jnp.concatenate([e_scr[k] for k in range(n_slab)], axis=-1).astype(BF16)
    y = (jnp.dot(x, m8_scr[...], preferred_element_type=F32)
         + jnp.dot(s, cs8_scr[...], preferred_element_type=F32))
    y_ref[...] = y.reshape(rows * t, cw)


def _s5_chunked(u, h0, prep, tile_k, tile_b, a4, *, tok, geoms, n_tiles_p, n, p):
    be_re, be_im, cs_re, cs_ni, kc = prep
    t, gb = S5_T, S5_GB
    cw, sw = gb * p, gb * n
    nb = u.shape[1] // cw
    n_tiles = u.shape[0] // tok
    ms = h0.shape[2]
    both = lambda a: pl.BlockSpec((2, 1) + a.shape[2:], lambda b, i: (0, b, 0, 0), pipeline_mode=pl.Buffered(1))
    const = lambda a: pl.BlockSpec(a.shape, lambda b, i: (0, 0), pipeline_mode=pl.Buffered(1))
    kern = functools.partial(_s5_chunk_kernel, geoms=geoms, n_tiles_p=n_tiles_p, n=n, p=p)
    return pl.pallas_call(
        kern,
        grid=(nb, n_tiles),
        in_specs=[pl.BlockSpec((tok, cw), lambda b, i: (i, b)),
                  both(be_re), both(be_im), both(cs_re), both(cs_ni), both(kc), const(tile_k), const(tile_b),
                  pl.BlockSpec((4, sw), lambda b, i: (0, b)),
                  pl.BlockSpec((1, 4, ms, sw), lambda b, i: (i, 0, 0, b))],
        out_specs=[pl.BlockSpec((tok, cw), lambda b, i: (i, b)),
                   pl.BlockSpec((1, 4, ms, sw), lambda b, i: (i, 0, 0, b))],
        out_shape=[jax.ShapeDtypeStruct(u.shape, F32), jax.ShapeDtypeStruct(h0.shape, F32)],
        scratch_shapes=[pltpu.VMEM((t * cw, t * cw), BF16), pltpu.VMEM((t * cw, 4 * sw), BF16),
                        pltpu.VMEM((4 * sw, t * cw), BF16), pltpu.VMEM((4 * sw // LANES, tok // t, LANES), F32)],
        compiler_params=_cparams("arbitrary", "arbitrary"),
        name="s5_chunked",
    )(u, be_re, be_im, cs_re, cs_ni, kc, tile_k, tile_b, a4, h0)


def _s5_tiles(n, p):
    t, gb = S5_T, S5_GB
    eye = lambda k: np.eye(k, dtype=np.float32)
    tile_k = np.einsum("ab,pq->apbq", eye(t), eye(p))[:, :, :, None, :] * np.ones((1, 1, 1, gb, 1), np.float32)
    tile_b = eye(n)[:, None, :] * np.ones((1, gb, 1), np.float32)
    return jnp.asarray(tile_k.reshape(t * p, t * gb * p)), jnp.asarray(tile_b.reshape(n, gb * n))


def _post_residual(x, y, gate, g, b, alpha):
    return _layer_norm(alpha * x + gate * y, g, b)


def _outab_kernel(xp_ref, xs_ref, mod_ref, yc_ref, ys_ref, u_ref, dsk_ref, wglu_ref, wout_ref, g_ref, b_ref, o_ref,
                  *, tiles_p, tiles_per_s, d, c, alpha):
    i = pl.program_id(0)
    grp = _group_of_tile(i, tiles_p, tiles_per_s)
    y_s = ys_ref[...] + dsk_ref[...] * u_ref[...]
    y_s = jax.nn.gelu(y_s)
    z = jnp.dot(y_s.astype(BF16), wglu_ref[...], preferred_element_type=F32)
    y_ssm = y_s * jax.nn.sigmoid(z)
    out = (jnp.dot(yc_ref[...], wout_ref[0:c, :], preferred_element_type=F32)
           + jnp.dot(y_ssm.astype(BF16), wout_ref[c:2 * c, :], preferred_element_type=F32))
    gate = _mod_chunk(mod_ref, grp, 2, d)
    o_ref[...] = _post_residual(_pair_rows(i, tiles_p, xp_ref, xs_ref), out, gate, g_ref[...], b_ref[...], alpha)


def _outab(xp, xs, mod_l, y_conv, y_scan, u, d_skip, w_glu, w_out, ln_g, ln_b, tm, tiles_p, tiles_per_s, alpha):
    t, d = xp.shape[0] + xs.shape[0], xp.shape[1]
    c = u.shape[1]
    kern = functools.partial(_outab_kernel, tiles_p=tiles_p, tiles_per_s=tiles_per_s, d=d, c=c, alpha=alpha)
    row = lambda w: pl.BlockSpec((tm, w), lambda i: (i, 0))
    full = lambda a: pl.BlockSpec(a.shape, lambda i: (0,) * a.ndim)
    d_skip, ln_g, ln_b = d_skip.reshape(1, c), ln_g.reshape(1, d), ln_b.reshape(1, d)
    return pl.pallas_call(
        kern,
        grid=(t // tm,),
        in_specs=_pair_specs(tm, d, tiles_p) + [full(mod_l), row(c), row(c), row(c), full(d_skip), full(w_glu),
                                                full(w_out), full(ln_g), full(ln_b)],
        out_specs=row(d),
        out_shape=jax.ShapeDtypeStruct((t, d), F32),
        compiler_params=_cparams("arbitrary"),
        name="outproj_ab",
    )(xp, xs, mod_l, y_conv, y_scan, u, d_skip, w_glu, w_out, ln_g, ln_b)


def _mlp_kernel(x_ref, mod_ref, w1_ref, w2_ref, g_ref, b_ref, *rest, tiles_p, tiles_per_s, d, n_f, alpha, split):
    outs, (h_scr, acc_scr) = rest[:-2], rest[-2:]
    i = pl.program_id(0)
    f = pl.program_id(1)
    grp = _group_of_tile(i, tiles_p, tiles_per_s)
    tm = h_scr.shape[0]
    halves = [slice(0, tm // 2), slice(tm // 2, tm)]

    def ffn(rows):
        a = jnp.dot(h_scr[rows, :], w1_ref[...], preferred_element_type=F32)
        a = jnp.square(jnp.maximum(a, 0.0)).astype(BF16)
        return jnp.dot(a, w2_ref[...], preferred_element_type=F32)

    @pl.when(f == 0)
    def _():
        shift = _mod_chunk(mod_ref, grp, 3, d)
        scale = _mod_chunk(mod_ref, grp, 4, d)
        for rows in halves:
            h_scr[rows, :] = (x_ref[rows, :] * (1 + scale) + shift).astype(BF16)
            acc_scr[rows, :] = ffn(rows)

    @pl.when(jnp.logical_and(f > 0, f < n_f - 1))
    def _():
        acc_scr[...] += ffn(slice(None))

    @pl.when(f == n_f - 1)
    def _():
        gate = _mod_chunk(mod_ref, grp, 5, d)
        for rows in halves:
            res = _post_residual(x_ref[rows, :], acc_scr[rows, :] + ffn(rows), gate, g_ref[...], b_ref[...], alpha)
            if split:
                @pl.when(i < tiles_p)
                def _():
                    outs[0][rows, :] = res

                @pl.when(i >= tiles_p)
                def _():
                    outs[1][rows, :] = res
            else:
                outs[0][rows, :] = res


def _mlp(x, mod_l, w1, w2, layer, ln_g, ln_b, tm, tf, tiles_p, tiles_per_s, alpha, split):
    t, d = x.shape
    n_f = w1.shape[1]
    assert n_f >= 2 and w1.shape[3] == tf
    kern = functools.partial(_mlp_kernel, tiles_p=tiles_p, tiles_per_s=tiles_per_s, d=d, n_f=n_f, alpha=alpha,
                             split=split)
    ln_g, ln_b = ln_g.reshape(1, d), ln_b.reshape(1, d)
    if split:
        out_specs = _pair_specs(tm, d, tiles_p, grid_rank=2)
        out_shape = [jax.ShapeDtypeStruct((tiles_p * tm, d), F32), jax.ShapeDtypeStruct((t - tiles_p * tm, d), F32)]
    else:
        out_specs = [pl.BlockSpec((tm, d), lambda i, f: (i, 0))]
        out_shape = [jax.ShapeDtypeStruct((t, d), F32)]
    return pl.pallas_call(
        kern,
        grid=(t // tm, n_f),
        in_specs=[pl.BlockSpec((tm, d), lambda i, f: (i, 0)),
                  pl.BlockSpec(mod_l.shape, lambda i, f: (0, 0)),
                  pl.BlockSpec((None, None, d, tf), lambda i, f: (layer, f, 0, 0)),
                  pl.BlockSpec((None, tf, d), lambda i, f: (layer, f, 0)),
                  pl.BlockSpec((1, d), lambda i, f: (0, 0)),
                  pl.BlockSpec((1, d), lambda i, f: (0, 0))],
        out_specs=out_specs,
        out_shape=out_shape,
        scratch_shapes=[pltpu.VMEM((tm, d), BF16), pltpu.VMEM((tm, d), F32)],
        compiler_params=_cparams("arbitrary", "arbitrary"),
        name="mlp",
    )(x, mod_l, w1, w2, ln_g, ln_b)


def _rope_tables(n_pos, dk):
    ax = dk // 2
    half = ax // 2
    freqs = ROPE_BASE ** (-jnp.arange(half, dtype=F32) / half)
    pos = jnp.arange(n_pos)
    row = (pos // LATENT_GRID_W).astype(F32)
    col = (pos % LATENT_GRID_W).astype(F32)
    ang_r, ang_c = row[:, None] * freqs, col[:, None] * freqs
    cos = jnp.concatenate([jnp.cos(ang_r)] * 2 + [jnp.cos(ang_c)] * 2, axis=-1)
    sin = jnp.concatenate([-jnp.sin(ang_r), jnp.sin(ang_r), -jnp.sin(ang_c), jnp.sin(ang_c)], axis=-1)
    rep = LANES // dk
    return jnp.tile(cos, (1, rep)), jnp.tile(sin, (1, rep))


def _qkv_kernel(x_ref, mod_ref, w_ref, cos_ref, sin_ref, q_ref, k_ref, v_ref, kc_ref, vc_ref,
                *, tiles_p, tiles_per_s, d, dk, dv, q_scale):
    i = pl.program_id(0)
    g = _group_of_tile(i, tiles_p, tiles_per_s)
    shift = _mod_chunk(mod_ref, g, 0, d)
    scale = _mod_chunk(mod_ref, g, 1, d)
    h = (x_ref[...] * (1 + scale) + shift).astype(BF16)
    tn = q_ref.shape[-1]
    quarter = dk // 4
    latent = i >= tiles_p
    cos, sin = cos_ref[...], sin_ref[...]
    lane = lax.broadcasted_iota(jnp.int32, cos.shape, 1)
    first = (lane % (2 * quarter)) < quarter

    def rotated(y, o_ref, out_scale):
        for cb in range(tn // LANES):
            yb = y[:, cb * LANES:(cb + 1) * LANES]
            partner = jnp.where(first, pltpu.roll(yb, LANES - quarter, 1), pltpu.roll(yb, quarter, 1))
            out = jnp.where(latent, yb * cos + partner * sin, yb)
            o_ref[:, cb * LANES:(cb + 1) * LANES] = (out * out_scale).astype(o_ref.dtype)

    rotated(jnp.dot(h, w_ref[:, 0:tn], preferred_element_type=F32), q_ref, q_scale)
    y_k = jnp.dot(h, w_ref[:, tn:2 * tn], preferred_element_type=F32)
    rotated(y_k, k_ref, 1.0)
    y_v = jnp.dot(h, w_ref[:, 2 * tn:3 * tn], preferred_element_type=F32)
    v_ref[...] = y_v.astype(v_ref.dtype)

    @pl.when(i < tiles_p)
    def _():
        n_heads = kc_ref.shape[2]
        for m in range(2):
            for hh in range(n_heads):
                c0 = (m * n_heads + hh) * dk
                kc_ref[0, m, hh] = y_k[:, c0:c0 + dk]
        for hh in range(n_heads):
            vc_ref[0, hh] = y_v[:, hh * dv:(hh + 1) * dv]


def _qkv(x, mod_l, w_qkv, cos_t, sin_t, tm, tiles_p, tiles_per_s, dk, dv, bp, lp, q_scale):
    t, d = x.shape
    n_out = w_qkv.shape[1]
    assert n_out % 3 == 0 and lp % tm == 0
    tn = n_out // 3
    n_heads = tn // dv
    per_seq = lp // tm
    kern = functools.partial(_qkv_kernel, tiles_p=tiles_p, tiles_per_s=tiles_per_s, d=d, dk=dk, dv=dv,
                             q_scale=q_scale)
    pos_blk = lambda i: (jnp.maximum(i - tiles_p, 0) % tiles_per_s, 0)
    c_tile = lambda i: jnp.minimum(i, tiles_p - 1)
    row = pl.BlockSpec((tm, tn), lambda i: (i, 0))
    return pl.pallas_call(
        kern,
        grid=(t // tm,),
        in_specs=[pl.BlockSpec((tm, d), lambda i: (i, 0)),
                  pl.BlockSpec(mod_l.shape, lambda i: (0, 0)),
                  pl.BlockSpec((d, n_out), lambda i: (0, 0), pipeline_mode=pl.Buffered(1)),
                  pl.BlockSpec((tm, LANES), pos_blk),
                  pl.BlockSpec((tm, LANES), pos_blk)],
        out_specs=[row, row, row,
                   pl.BlockSpec((1, 2, n_heads, tm, dk), lambda i: (c_tile(i) // per_seq, 0, 0, c_tile(i) % per_seq, 0)),
                   pl.BlockSpec((1, n_heads, tm, dv), lambda i: (c_tile(i) // per_seq, 0, c_tile(i) % per_seq, 0))],
        out_shape=[jax.ShapeDtypeStruct((t, tn), BF16)] * 3
        + [jax.ShapeDtypeStruct((bp, 2, n_heads, lp, dk), F32), jax.ShapeDtypeStruct((bp, n_heads, lp, dv), F32)],
        compiler_params=_cparams("arbitrary"),
        name="qkv_proj",
    )(x, mod_l, w_qkv, cos_t, sin_t)


def _attn_kernel(*refs, heads_step, dk, dv, scale, fold_scale, lam_init, has_cache):
    if has_cache:
        q1_ref, q2_ref, k1_ref, k2_ref, v_ref, ck_ref, cv_ref, lamv_ref, sg_ref, o_ref, s_scr, w_scr = refs
    else:
        q1_ref, q2_ref, k1_ref, k2_ref, v_ref, lamv_ref, sg_ref, o_ref, s_scr, w_scr = refs
        ck_ref = cv_ref = None
    lv = lamv_ref[...]
    lam = (jnp.exp(jnp.sum(lv[0:1] * lv[1:2], axis=-1, keepdims=True))
           - jnp.exp(jnp.sum(lv[2:3] * lv[3:4], axis=-1, keepdims=True)) + lam_init)
    nt = (((1,), (1,)), ((), ()))
    tn = (((0,), (0,)), ((), ()))
    per_blk = LANES // dk
    tq, lk = q1_ref.shape[0], k1_ref.shape[0]
    lane = lax.broadcasted_iota(jnp.int32, (tq, LANES), 1)

    for blk in range(heads_step // per_blk):
        bs = slice(blk * LANES, (blk + 1) * LANES)
        for m, (q_ref, k_ref) in enumerate(((q1_ref, k1_ref), (q2_ref, k2_ref))):
            kb, qb = k_ref[:, bs], q_ref[:, bs]
            for sub in range(per_blk):
                head = blk * per_blk + sub
                qm = jnp.where((lane >= sub * dk) & (lane < (sub + 1) * dk), qb, jnp.zeros_like(qb))
                s_scr[2 * head + m, 0:lk, :] = lax.dot_general(kb, qm, nt, preferred_element_type=F32)
                if has_cache:
                    qc = qb[:, sub * dk:(sub + 1) * dk]
                    s_scr[2 * head + m, lk:, :] = lax.dot_general(ck_ref[0, 0, m, head].astype(BF16), qc, nt,
                                                                  preferred_element_type=F32)
    dens = []
    for hm in range(2 * heads_step):
        s = s_scr[hm]
        if not fold_scale:
            s = s * scale
        e = jnp.exp(s - jnp.max(s, axis=0, keepdims=True))
        dens.append(jnp.sum(e, axis=0, keepdims=True))
        s_scr[hm] = e
    for head in range(heads_step):
        ratio = lam * dens[2 * head] / dens[2 * head + 1]
        w_scr[head] = (s_scr[2 * head] - s_scr[2 * head + 1] * ratio).astype(BF16)
    for head in range(heads_step):
        vs = slice(head * dv, (head + 1) * dv)
        o_t = lax.dot_general(v_ref[:, vs], w_scr[head, 0:lk, :], tn, preferred_element_type=F32)
        if has_cache:
            o_t = o_t + lax.dot_general(cv_ref[0, 0, head].astype(BF16), w_scr[head, lk:, :], tn,
                                        preferred_element_type=F32)
        o_t = o_t * (1.0 / dens[2 * head])
        o_t = o_t * lax.rsqrt(jnp.mean(jnp.square(o_t), axis=0, keepdims=True) + LN_EPS)
        o = o_t.T * sg_ref[...] * (1.0 - lam_init)
        o_ref[:, vs] = o.astype(o_ref.dtype)


def _softmax_scale(dk):
    scale = dk ** -0.5
    return scale, math.frexp(scale)[0] == 0.5


def _attention(q, k, v, lamv, subln_g, *, row0, n_seq, lq, tq, hp, n_heads, dk, dv, lam_init, cache=None):
    per_blk = LANES // dk
    heads_step = hp * per_blk
    n_hblk = n_heads // heads_step
    map2 = n_heads * dk // (hp * LANES)
    scale, fold_scale = _softmax_scale(dk)
    qb0, kb0 = row0 // tq, row0 // lq
    n_q = lq // tq
    q_spec = lambda off: pl.BlockSpec((tq, hp * LANES), lambda b, h, qi: (qb0 + b * n_q + qi, off + h))
    k_spec = lambda off: pl.BlockSpec((lq, hp * LANES), lambda b, h, qi: (kb0 + b, off + h))
    in_specs = [q_spec(0), q_spec(map2), k_spec(0), k_spec(map2),
                pl.BlockSpec((lq, heads_step * dv), lambda b, h, qi: (kb0 + b, h))]
    args = [q, q, k, k, v]
    if cache is not None:
        cache_k, cache_v, o_i = cache
        past = cache_k.shape[-2]
        in_specs += [pl.BlockSpec((1, 1, 2, heads_step, past, dk), lambda b, h, qi: (b, o_i, 0, h, 0, 0)),
                     pl.BlockSpec((1, 1, heads_step, past, dv), lambda b, h, qi: (b, o_i, h, 0, 0))]
        args += [cache_k, cache_v]
    in_specs += [pl.BlockSpec(lamv.shape, lambda b, h, qi: (0, 0)),
                 pl.BlockSpec((1, dv), lambda b, h, qi: (0, 0))]
    args += [lamv, subln_g.reshape(1, dv)]
    kern = functools.partial(_attn_kernel, heads_step=heads_step, dk=dk, dv=dv, scale=scale,
                             fold_scale=fold_scale, lam_init=lam_init, has_cache=cache is not None)
    lk_all = lq + (cache[0].shape[-2] if cache is not None else 0)
    scratch = [pltpu.VMEM((2 * heads_step, lk_all, tq), F32), pltpu.VMEM((heads_step, lk_all, tq), BF16)]
    return pl.pallas_call(
        kern,
        grid=(n_seq, n_hblk, n_q),
        in_specs=in_specs,
        scratch_shapes=scratch,
        out_specs=pl.BlockSpec((tq, heads_step * dv), lambda b, h, qi: (b * n_q + qi, h)),
        out_shape=jax.ShapeDtypeStruct((n_seq * lq, n_heads * dv), BF16),
        compiler_params=_cparams("arbitrary", "arbitrary", "arbitrary"),
        name="diff_attn_cache" if cache is not None else "diff_attn",
    )(*args)


def _outc_kernel(x_ref, mod_ref, op_ref, os_ref, w_ref, g_ref, b_ref, o_ref, *, tiles_p, tiles_per_s, d, alpha):
    i = pl.program_id(0)
    grp = _group_of_tile(i, tiles_p, tiles_per_s)
    gate = _mod_chunk(mod_ref, grp, 2, d)
    tm = x_ref.shape[0]
    for rows in (slice(0, tm // 2), slice(tm // 2, tm)):
        o_in = jnp.where(i < tiles_p, op_ref[rows, :], os_ref[rows, :])
        out = jnp.dot(o_in, w_ref[...], preferred_element_type=F32)
        o_ref[rows, :] = _post_residual(x_ref[rows, :], out, gate, g_ref[...], b_ref[...], alpha)


def _outc(x, mod_l, o_p, o_s, w_out, ln_g, ln_b, tm, tiles_p, tiles_per_s, alpha):
    t, d = x.shape
    kin = o_p.shape[1]
    kern = functools.partial(_outc_kernel, tiles_p=tiles_p, tiles_per_s=tiles_per_s, d=d, alpha=alpha)
    ln_g, ln_b = ln_g.reshape(1, d), ln_b.reshape(1, d)
    full = lambda a: pl.BlockSpec(a.shape, lambda i: (0,) * a.ndim)
    return pl.pallas_call(
        kern,
        grid=(t // tm,),
        in_specs=[pl.BlockSpec((tm, d), lambda i: (i, 0)), full(mod_l)] + _pair_specs(tm, kin, tiles_p)
        + [full(w_out), full(ln_g), full(ln_b)],
        out_specs=pl.BlockSpec((tm, d), lambda i: (i, 0)),
        out_shape=jax.ShapeDtypeStruct((t, d), F32),
        compiler_params=_cparams("arbitrary"),
        name="outproj_c",
    )(x, mod_l, o_p, o_s, w_out, ln_g, ln_b)


def kernel(x_prompt, x_sample, state_s5_re, state_s5_im, cache_k, cache_v, c, c_ctx, w_mod, b_mod, ln_g, ln_b, w_in_ab, w_dw, b_dw, conv_ln_g, conv_ln_b, s5_lambda_re, s5_lambda_im, s5_log_dt, s5_b_re, s5_b_im, s5_c_re, s5_c_im, s5_d, w_glu, w_out_ab, w_qkv, lam_q1, lam_k1, lam_q2, lam_k2, subln_g, w_out_c, w_ff1, w_ff2):
    bp, lp, d = x_prompt.shape
    bs, ls, _ = x_sample.shape
    depth = w_mod.shape[0]
    tp, ts = bp * lp, bs * ls
    alpha = (2 * depth) ** 0.25
    assert 1 + bs <= MOD_ROWS

    tm = 256
    tm_mlp, tf = 512, 1024
    assert tp % tm_mlp == 0 and ls % tm_mlp == 0 and tp % ls == 0

    xp, xs = x_prompt.reshape(tp, d), x_sample.reshape(ts, d)
    cvec = jnp.zeros((MOD_ROWS, d), F32).at[0].set(c_ctx).at[1:1 + bs].set(c)
    mod = _modvec(cvec, w_mod, b_mod)

    g_ssm, n_ssm, p_ssm = s5_b_re.shape[2:]
    dk = lam_q1.shape[-1]
    dv = subln_g.shape[-1]
    n_heads = w_out_c.shape[1] // dv

    w_ff1_bf = w_ff1.astype(BF16).reshape(depth, d, w_ff1.shape[2] // tf, tf).transpose(0, 2, 1, 3)
    w_ff2_bf = w_ff2.astype(BF16)
    s_re, s_im, k_list, v_list = [], [], [], []
    for l in range(depth):
        mod_l = mod[l]
        if l % 2 == 0:
            e = l // 2
            if l > 0:
                xp, xs = x[:tp], x[tp:]
            ug, u = _inproj(xp, xs, mod_l, w_in_ab[e].astype(BF16), tm, tp // tm, ls // tm)
            y_conv = _conv_module(ug, w_dw[e], b_dw[e], conv_ln_g[e], conv_ln_b[e], min(lp, 256), lp, ls, tp)
            prep = _s5_prep(s5_lambda_re[e], s5_lambda_im[e], s5_log_dt[e],
                            s5_b_re[e], s5_b_im[e], s5_c_re[e], s5_c_im[e])
            a_re, a_im = prep[5], prep[6]
            gn = g_ssm * n_ssm
            nat = lambda a: a.reshape(2, g_ssm, p_ssm, n_ssm)[:, :, 0].reshape(2, gn)
            a4 = jnp.stack([nat(a_re)[0], nat(a_im)[0], nat(a_re)[1], nat(a_im)[1]])
            tile_k, tile_b = _s5_tiles(n_ssm, p_ssm)
            tok = min(4096, tp, ts)
            assert tok % lp == 0 and tok % ls == 0 and tp % tok == 0 and ts % tok == 0
            seq_p, seq_s, tiles_p5 = tok // lp, tok // ls, tp // tok
            ms = max(seq_p, seq_s)
            st = lambda a, dr: jnp.pad(a[:, e, dr].reshape(ts // tok, seq_s, gn), ((0, 0), (0, ms - seq_s), (0, 0)))
            h0_s = jnp.stack([st(state_s5_re, 0), st(state_s5_im, 0), st(state_s5_re, 1), st(state_s5_im, 1)], axis=1)
            h0 = jnp.concatenate([jnp.zeros((tiles_p5, 4, ms, gn), F32), h0_s], axis=0)
            y_scan, hf = _s5_chunked(u, h0, prep[:5], tile_k, tile_b, a4, tok=tok,
                                     geoms=((seq_p, lp // S5_T), (seq_s, ls // S5_T)), n_tiles_p=tiles_p5,
                                     n=n_ssm, p=p_ssm)
            hf = hf[:tiles_p5, :, :seq_p].transpose(1, 0, 2, 3).reshape(2, 2, bp, g_ssm, n_ssm)
            s_re.append(hf[:, 0].transpose(1, 0, 2, 3))
            s_im.append(hf[:, 1].transpose(1, 0, 2, 3))
            x = _outab(xp, xs, mod_l, y_conv, y_scan, u, s5_d[e], w_glu[e].astype(BF16),
                       w_out_ab[e].astype(BF16), ln_g[l, 0], ln_b[l, 0], tm, tp // tm, ls // tm, alpha)
        else:
            o_i = l // 2
            lam_init = 0.8 - 0.6 * math.exp(-0.3 * l)
            cos_t, sin_t = _rope_tables(ls, dk)
            scale, fold_scale = _softmax_scale(dk)
            q, k, v, k_new, v_new = _qkv(x, mod_l, w_qkv[o_i].astype(BF16), cos_t, sin_t, tm, tp // tm, ls // tm,
                                     dk, dv, bp, lp, scale if fold_scale else 1.0)
            lamv = jnp.stack([lam_q1[o_i], lam_k1[o_i], lam_q2[o_i], lam_k2[o_i]])
            geo = dict(n_heads=n_heads, dk=dk, dv=dv, lam_init=lam_init)
            o_p = _attention(q, k, v, lamv, subln_g[o_i], row0=0, n_seq=bp, lq=lp, tq=lp,
                             hp=n_heads * dk // LANES, **geo)
            o_s = _attention(q, k, v, lamv, subln_g[o_i], row0=tp, n_seq=bs, lq=ls, tq=256, hp=2,
                             cache=(cache_k, cache_v, o_i), **geo)
            k_list.append(k_new)
            v_list.append(v_new)
            x = _outc(x, mod_l, o_p, o_s, w_out_c[o_i].astype(BF16), ln_g[l, 0], ln_b[l, 0],
                      tm_mlp, tp // tm_mlp, ls // tm_mlp, alpha)
        res = _mlp(x, mod_l, w_ff1_bf, w_ff2_bf, l, ln_g[l, 1], ln_b[l, 1],
                   tm_mlp, tf, tp // tm_mlp, ls // tm_mlp, alpha, split=l == depth - 1)
        x = res[0]

    return (res[0].reshape(bp, lp, d), res[1].reshape(bs, ls, d),
            jnp.stack(s_re, axis=1), jnp.stack(s_im, axis=1),
            jnp.stack(k_list, axis=1), jnp.stack(v_list, axis=1))
```

```python
import functools
import math
from typing import NamedTuple

import jax
import jax.numpy as jnp
import numpy as np
from jax import lax
from jax.experimental import pallas as pl
from jax.experimental.pallas import tpu as pltpu

F32 = jnp.float32
BF16 = jnp.bfloat16

LN_EPS = 1e-5
ROPE_BASE = 10000.0
LATENT_GRID_W = 64
MOD_ROWS = 8
V7X_VMEM_LIMIT = 56 * 1024 * 1024
LANES = 128
SUBLANES = 8


def _cparams(*sem):
    return pltpu.CompilerParams(dimension_semantics=sem, vmem_limit_bytes=V7X_VMEM_LIMIT)


def _layer_norm(z, g, b):
    mu = jnp.mean(z, axis=-1, keepdims=True)
    zc = z - mu
    var = jnp.mean(jnp.square(zc), axis=-1, keepdims=True)
    return zc * lax.rsqrt(var + LN_EPS) * g + b


def _group_of_tile(i, tiles_p, tiles_per_s):
    return jnp.where(i < tiles_p, 0, 1 + jnp.maximum(i - tiles_p, 0) // tiles_per_s)


def _mod_chunk(mod_ref, g, k, d):
    return mod_ref[pl.ds(g, 1), k * d:(k + 1) * d]


def _pair_specs(tm, width, tiles_p, grid_rank=1):
    if grid_rank == 1:
        return [pl.BlockSpec((tm, width), lambda i: (jnp.minimum(i, tiles_p - 1), 0)),
                pl.BlockSpec((tm, width), lambda i: (jnp.maximum(i - tiles_p, 0), 0))]
    return [pl.BlockSpec((tm, width), lambda i, f: (jnp.minimum(i, tiles_p - 1), 0)),
            pl.BlockSpec((tm, width), lambda i, f: (jnp.maximum(i - tiles_p, 0), 0))]


def _pair_rows(i, tiles_p, p_ref, s_ref):
    return jnp.where(i < tiles_p, p_ref[...], s_ref[...])


def _modvec_kernel(cv_ref, w_ref, b_ref, o_ref):
    cv = cv_ref[...]
    s = (cv * jax.nn.sigmoid(cv)).astype(BF16)
    o_ref[0] = jnp.dot(s, w_ref[0].astype(BF16), preferred_element_type=F32) + b_ref[0]


def _modvec(cvec, w_mod, b_mod, tn=1024):
    depth, d, n = w_mod.shape
    return pl.pallas_call(
        _modvec_kernel,
        grid=(depth, n // tn),
        in_specs=[pl.BlockSpec((MOD_ROWS, d), lambda l, j: (0, 0)),
                  pl.BlockSpec((1, d, tn), lambda l, j: (l, 0, j)),
                  pl.BlockSpec((1, 1, tn), lambda l, j: (l, 0, j))],
        out_specs=pl.BlockSpec((1, MOD_ROWS, tn), lambda l, j: (l, 0, j)),
        out_shape=jax.ShapeDtypeStruct((depth, MOD_ROWS, n), F32),
        compiler_params=_cparams("arbitrary", "arbitrary"),
        name="modvec",
    )(cvec, w_mod, b_mod.reshape(depth, 1, n))


def _inproj_kernel(xp_ref, xs_ref, mod_ref, w_ref, ug_ref, u_ref, *, tiles_p, tiles_per_s, d, c):
    i = pl.program_id(0)
    g = _group_of_tile(i, tiles_p, tiles_per_s)
    shift = _mod_chunk(mod_ref, g, 0, d)
    scale = _mod_chunk(mod_ref, g, 1, d)
    h = (_pair_rows(i, tiles_p, xp_ref, xs_ref) * (1 + scale) + shift).astype(BF16)
    a_val = jnp.dot(h, w_ref[:, 0:c], preferred_element_type=F32)
    a_gate = jnp.dot(h, w_ref[:, c:2 * c], preferred_element_type=F32)
    ug_ref[...] = a_val * jax.nn.sigmoid(a_gate)
    u_ref[...] = jnp.dot(h, w_ref[:, 2 * c:3 * c], preferred_element_type=F32)


def _inproj(xp, xs, mod_l, w_in, tm, tiles_p, tiles_per_s):
    t, d = xp.shape[0] + xs.shape[0], xp.shape[1]
    c = w_in.shape[1] // 3
    kern = functools.partial(_inproj_kernel, tiles_p=tiles_p, tiles_per_s=tiles_per_s, d=d, c=c)
    return pl.pallas_call(
        kern,
        grid=(t // tm,),
        in_specs=_pair_specs(tm, d, tiles_p) + [pl.BlockSpec(mod_l.shape, lambda i: (0, 0)),
                                                pl.BlockSpec(w_in.shape, lambda i: (0, 0))],
        out_specs=[pl.BlockSpec((tm, c), lambda i: (i, 0)),
                   pl.BlockSpec((tm, c), lambda i: (i, 0))],
        out_shape=[jax.ShapeDtypeStruct((t, c), F32), jax.ShapeDtypeStruct((t, c), F32)],
        compiler_params=_cparams("arbitrary"),
        name="inproj",
    )(xp, xs, mod_l, w_in)


CONV_HALO = 16
CONV_ROWS = 64
CONV_COLS = 128


def _conv_kernel(prev_ref, cur_ref, next_ref, w_ref, b_ref, g_ref, beta_ref, o_ref, pad_scr, sh_scr, conv_scr,
                 *, chunks_p, chunks_s, n_chunks_p, width, lc, c):
    i = pl.program_id(0)
    in_p = i < n_chunks_p
    k = jnp.where(in_p, i % chunks_p, jnp.maximum(i - n_chunks_p, 0) % chunks_s)
    last = jnp.where(in_p, chunks_p - 1, chunks_s - 1)
    has_prev = (k > 0).astype(F32)
    has_next = (k < last).astype(F32)
    pad_scr[0:CONV_HALO, :] = prev_ref[...] * has_prev
    pad_scr[CONV_HALO:CONV_HALO + lc, :] = cur_ref[...]
    pad_scr[CONV_HALO + lc:2 * CONV_HALO + lc, :] = next_ref[...] * has_next
    off = CONV_HALO - width // 2
    span = lc + CONV_HALO + SUBLANES
    for sft in range(SUBLANES):
        sh_scr[sft] = pad_scr[sft:sft + span, :]

    for r0 in range(0, lc, CONV_ROWS):
        for cb in range(c // CONV_COLS):
            cs = slice(cb * CONV_COLS, (cb + 1) * CONV_COLS)
            acc = jnp.zeros((CONV_ROWS, CONV_COLS), F32)
            for kk in range(width):
                whole, sft = divmod(kk + off, SUBLANES)
                base = r0 + whole * SUBLANES
                acc = acc + sh_scr[sft, base:base + CONV_ROWS, cs] * w_ref[kk:kk + 1, cs]
            conv_scr[r0:r0 + CONV_ROWS, cs] = acc + b_ref[:, cs]
    y = _layer_norm(conv_scr[...], g_ref[...], beta_ref[...])
    o_ref[...] = (y * jax.nn.sigmoid(y)).astype(o_ref.dtype)


def _conv_module(ug, w_dw, b_dw, ln_g, ln_b, lc, lp, ls, tp):
    t, c = ug.shape
    width = w_dw.shape[0]
    assert width // 2 < CONV_HALO and lc % CONV_HALO == 0 and lp % lc == 0 and ls % lc == 0
    hb = lc // CONV_HALO
    n_halo_blocks = t // CONV_HALO
    kern = functools.partial(_conv_kernel, chunks_p=lp // lc, chunks_s=ls // lc, n_chunks_p=tp // lc,
                             width=width, lc=lc, c=c)
    vec = lambda a: a.reshape(1, c)
    return pl.pallas_call(
        kern,
        grid=(t // lc,),
        in_specs=[pl.BlockSpec((CONV_HALO, c), lambda i: (jnp.maximum(i * hb - 1, 0), 0)),
                  pl.BlockSpec((lc, c), lambda i: (i, 0)),
                  pl.BlockSpec((CONV_HALO, c), lambda i: (jnp.minimum((i + 1) * hb, n_halo_blocks - 1), 0)),
                  pl.BlockSpec((width, c), lambda i: (0, 0)),
                  pl.BlockSpec((1, c), lambda i: (0, 0)),
                  pl.BlockSpec((1, c), lambda i: (0, 0)),
                  pl.BlockSpec((1, c), lambda i: (0, 0))],
        out_specs=pl.BlockSpec((lc, c), lambda i: (i, 0)),
        out_shape=jax.ShapeDtypeStruct((t, c), BF16),
        scratch_shapes=[pltpu.VMEM((lc + 2 * CONV_HALO, c), F32),
                        pltpu.VMEM((SUBLANES, lc + CONV_HALO + SUBLANES, c), F32), pltpu.VMEM((lc, c), F32)],
        compiler_params=_cparams("arbitrary"),
        name="conv_module",
    )(ug, ug, ug, w_dw, vec(b_dw), vec(ln_g), vec(ln_b))


S5_T = 16
S5_GB = 8


def _s5_prep_kernel(bt_re_ref, bt_im_ref, la_re_ref, la_im_ref, dta_ref, ct_re_ref, ct_im_ref, lb_re_ref, lb_im_ref,
                    dtb_ref, be_re_ref, be_im_ref, cs_re_ref, cs_ni_ref, kc_ref, a_re_ref, a_im_ref, *, n, p):
    t = S5_T
    fwd = pl.program_id(0) == 0

    lam_re, lam_im = la_re_ref[0, 0], la_im_ref[0, 0]
    dt = jnp.exp(dta_ref[0, 0])
    mag = jnp.exp(lam_re * dt)
    ar, ai = mag * jnp.cos(lam_im * dt), mag * jnp.sin(lam_im * dt)
    den = jnp.square(lam_re) + jnp.square(lam_im)
    coef_re = ((ar - 1) * lam_re + ai * lam_im) / den
    coef_im = (ai * lam_re - (ar - 1) * lam_im) / den
    b_re, b_im = bt_re_ref[0, 0], bt_im_ref[0, 0]
    bb_re = coef_re * b_re - coef_im * b_im
    bb_im = coef_re * b_im + coef_im * b_re
    rows = bb_re.shape[0]
    pw = []
    wr, wi = bb_re, bb_im
    pr, pi = jnp.ones_like(ar), jnp.zeros_like(ar)
    for k in range(t):
        pw.append((wr, wi))
        wr, wi = wr * ar - wi * ai, wr * ai + wi * ar
        pr, pi = pr * ar - pi * ai, pr * ai + pi * ar
    a_re_ref[0, 0] = pr
    a_im_ref[0, 0] = pi
    for j in range(t):
        be_re_ref[0, 0, j * rows:(j + 1) * rows, :] = jnp.where(fwd, pw[t - 1 - j][0], pw[j][0])
        be_im_ref[0, 0, j * rows:(j + 1) * rows, :] = jnp.where(fwd, pw[t - 1 - j][1], pw[j][1])

    ct_re, ct_im = ct_re_ref[0, 0], ct_im_ref[0, 0]
    shp = ct_re.shape
    lam_re, lam_im = lb_re_ref[0, 0], lb_im_ref[0, 0]
    dt = jnp.exp(dtb_ref[0, 0])
    mag = jnp.exp(lam_re * dt)
    sq_re = jnp.broadcast_to(mag * jnp.cos(lam_im * dt), shp)
    sq_im = jnp.broadcast_to(mag * jnp.sin(lam_im * dt), shp)
    blk = lax.broadcasted_iota(jnp.int32, shp, 1) // p
    k1 = jnp.where(fwd, blk + 1, t - blk)
    qr, qi = jnp.ones(shp, F32), jnp.zeros(shp, F32)
    n_bits = t.bit_length()
    for bit in range(n_bits):
        take = ((k1 >> bit) & 1) == 1
        qr, qi = (jnp.where(take, qr * sq_re - qi * sq_im, qr), jnp.where(take, qr * sq_im + qi * sq_re, qi))
        if bit + 1 < n_bits:
            sq_re, sq_im = sq_re * sq_re - sq_im * sq_im, 2.0 * (sq_re * sq_im)
    v_re = ct_re * qr - ct_im * qi
    v_im = ct_re * qi + ct_im * qr
    cs_re_ref[0, 0] = v_re
    cs_ni_ref[0, 0] = -v_im
    lane = lax.broadcasted_iota(jnp.int32, shp, 1)
    w = shp[1]
    v0_re = jnp.where(fwd, jnp.where(lane < p, ct_re, pltpu.roll(v_re, p, 1)),
                      jnp.where(lane >= w - p, ct_re, pltpu.roll(v_re, w - p, 1)))
    v0_im = jnp.where(fwd, jnp.where(lane < p, ct_im, pltpu.roll(v_im, p, 1)),
                      jnp.where(lane >= w - p, ct_im, pltpu.roll(v_im, w - p, 1)))
    hi = lax.Precision.HIGHEST
    for g in range(S5_GB):
        ra, rb = slice(g * p, (g + 1) * p), slice(g * n, (g + 1) * n)
        kc_ref[0, 0, ra, :] = (jnp.dot(bb_re[ra], v0_re[rb], precision=hi, preferred_element_type=F32)
                               - jnp.dot(bb_im[ra], v0_im[rb], precision=hi, preferred_element_type=F32))


def _s5_prep(lam_re, lam_im, log_dt, b_re, b_im, c_re, c_im):
    _, g, n, p = b_re.shape
    t = S5_T
    nb = g // S5_GB
    ra, rb = S5_GB * p, S5_GB * n
    lay_a = lambda a: jnp.broadcast_to(a[:, :, None, :], (2, g, p, n)).reshape(2, nb, ra, n)
    lay_b = lambda a: a.reshape(2, nb, rb, 1)
    bt = lambda a: a.transpose(0, 1, 3, 2).reshape(2, nb, ra, n)
    ct = lambda a: jnp.broadcast_to(a.transpose(0, 1, 3, 2)[:, :, :, None, :], (2, g, n, t, p)).reshape(2, nb, rb, t * p)
    dt_g = jnp.broadcast_to(log_dt[:, :, None], (2, g, n))
    blk = lambda r, c: pl.BlockSpec((1, 1, r, c), lambda d, i: (d, i, 0, 0))
    shp = lambda r, c: jax.ShapeDtypeStruct((2, nb, r, c), F32)
    kern = functools.partial(_s5_prep_kernel, n=n, p=p)
    return pl.pallas_call(
        kern,
        grid=(2, nb),
        in_specs=[blk(ra, n)] * 5 + [blk(rb, t * p)] * 2 + [blk(rb, 1)] * 3,
        out_specs=[blk(t * ra, n), blk(t * ra, n), blk(rb, t * p), blk(rb, t * p), blk(ra, t * p), blk(ra, n), blk(ra, n)],
        out_shape=[shp(t * ra, n), shp(t * ra, n), shp(rb, t * p), shp(rb, t * p), shp(ra, t * p), shp(ra, n), shp(ra, n)],
        compiler_params=_cparams("arbitrary", "arbitrary"),
        name="s5_prep",
    )(bt(b_re), bt(b_im), lay_a(lam_re), lay_a(lam_im), lay_a(dt_g), ct(c_re), ct(c_im),
      lay_b(lam_re), lay_b(lam_im), lay_b(dt_g))


def _s5_expand(src, tile, row_div, row_mod, lane_div, lane_mod, precision=None):
    full = jnp.dot(src, tile, precision=precision, preferred_element_type=F32)
    r = lax.broadcasted_iota(jnp.int32, full.shape, 0) // row_div % row_mod
    l = lax.broadcasted_iota(jnp.int32, full.shape, 1) // lane_div % lane_mod
    return jnp.where(r == l, full, 0.0)


def _s5_chunk_kernel(u_ref, be_re_ref, be_im_ref, cs_re_ref, cs_ni_ref, kc_ref, tk_ref, tb_ref, a_ref, h0_ref,
                     y_ref, hf_ref, m8_scr, be8_scr, cs8_scr, e_scr, *, geoms, n_tiles_p, n, p):
    t, gb = S5_T, S5_GB
    cw = gb * p
    sw = gb * n
    rows = geoms[0][0] * geoms[0][1]
    tile_i = pl.program_id(1)

    @pl.when(tile_i == 0)
    def _():
        tk, tb = tk_ref[...].astype(BF16), tb_ref[...].astype(BF16)
        step = 4 * cw
        for r0 in range(0, t * cw, step):
            for part, (ref, d) in enumerate(((be_re_ref, 0), (be_im_ref, 0), (be_re_ref, 1), (be_im_ref, 1))):
                be8_scr[r0:r0 + step, part * sw:(part + 1) * sw] = _s5_expand(
                    ref[d, 0, r0:r0 + step, :].astype(BF16), tb, p, gb, n, gb).astype(BF16)
        for part, (ref, d) in enumerate(((cs_re_ref, 0), (cs_ni_ref, 0), (cs_re_ref, 1), (cs_ni_ref, 1))):
            cs8_scr[part * sw:(part + 1) * sw, :] = _s5_expand(ref[d, 0].astype(BF16), tk, n, gb, p, gb).astype(BF16)
        hi = lax.Precision.HIGHEST
        bd_f = _s5_expand(kc_ref[0, 0], tk_ref[...], p, gb, p, gb, hi)
        bd_r = _s5_expand(kc_ref[1, 0], tk_ref[...], p, gb, p, gb, hi)
        tile_f = lambda k: bd_f[:, k * cw:(k + 1) * cw]
        tile_r = lambda k: bd_r[:, (t - 1 - k) * cw:(t - k) * cw]
        for j in range(t):
            for i in range(t):
                blk = tile_f(i - j) if i > j else tile_r(j - i) if i < j else tile_f(0) + tile_r(0)
                m8_scr[j * cw:(j + 1) * cw, i * cw:(i + 1) * cw] = blk.astype(BF16)

    x = u_ref[...].reshape(rows, t * cw).astype(BF16)
    e = jnp.dot(x, be8_scr[...], preferred_element_type=F32)
    n_slab = e.shape[1] // LANES
    per_part = sw // LANES
    for k in range(n_slab):
        e_scr[k] = e[:, k * LANES:(k + 1) * LANES]

    def scan(nseq, nc):
        loops = [list(range(per_part))] if nseq <= 16 else [[q] for q in range(per_part)]
        for prs in loops:
            coef = [[jnp.broadcast_to(a_ref[r:r + 1, q * LANES:(q + 1) * LANES], (nseq, LANES)) for r in range(4)]
                    for q in prs]
            init = tuple(tuple(h0_ref[0, r, 0:nseq, q * LANES:(q + 1) * LANES] for r in range(4)) for q in prs)

            def step(c, carry):
                out = []
                for idx, q in enumerate(prs):
                    sf_re, sf_im, sr_re, sr_im = carry[idx]
                    af_re, af_im, ar_re, ar_im = coef[idx]
                    at_f = pl.ds(c, nseq, stride=nc)
                    at_r = pl.ds(nc - 1 - c, nseq, stride=nc)
                    ef_re, ef_im = e_scr[q, at_f, :], e_scr[per_part + q, at_f, :]
                    er_re, er_im = e_scr[2 * per_part + q, at_r, :], e_scr[3 * per_part + q, at_r, :]
                    e_scr[q, at_f, :] = sf_re
                    e_scr[per_part + q, at_f, :] = sf_im
                    e_scr[2 * per_part + q, at_r, :] = sr_re
                    e_scr[3 * per_part + q, at_r, :] = sr_im
                    out.append((af_re * sf_re - af_im * sf_im + ef_re, af_re * sf_im + af_im * sf_re + ef_im,
                                ar_re * sr_re - ar_im * sr_im + er_re, ar_re * sr_im + ar_im * sr_re + er_im))
                return tuple(out)

            fin = lax.fori_loop(0, nc, step, init)
            for idx, q in enumerate(prs):
                for r in range(4):
                    hf_ref[0, r, 0:nseq, q * LANES:(q + 1) * LANES] = fin[idx][r]

    hf_ref[...] = jnp.zeros_like(hf_ref)

    @pl.when(tile_i < n_tiles_p)
    def _():
        scan(*geoms[0])

    @pl.when(tile_i >= n_tiles_p)
    def _():
        scan(*geoms[1])

    s = jnp.concatenate([e_scr[k] for k in range(n_slab)], axis=-1).astype(BF16)
    y = (jnp.dot(x, m8_scr[...], preferred_element_type=F32)
         + jnp.dot(s, cs8_scr[...], preferred_element_type=F32))
    y_ref[...] = y.reshape(rows * t, cw)


def _s5_chunked(u, h0, prep, tile_k, tile_b, a4, *, tok, geoms, n_tiles_p, n, p):
    be_re, be_im, cs_re, cs_ni, kc = prep
    t, gb = S5_T, S5_GB
    cw, sw = gb * p, gb * n
    nb = u.shape[1] // cw
    n_tiles = u.shape[0] // tok
    ms = h0.shape[2]
    both = lambda a: pl.BlockSpec((2, 1) + a.shape[2:], lambda b, i: (0, b, 0, 0), pipeline_mode=pl.Buffered(1))
    const = lambda a: pl.BlockSpec(a.shape, lambda b, i: (0, 0), pipeline_mode=pl.Buffered(1))
    kern = functools.partial(_s5_chunk_kernel, geoms=geoms, n_tiles_p=n_tiles_p, n=n, p=p)
    return pl.pallas_call(
        kern,
        grid=(nb, n_tiles),
        in_specs=[pl.BlockSpec((tok, cw), lambda b, i: (i, b)),
                  both(be_re), both(be_im), both(cs_re), both(cs_ni), both(kc), const(tile_k), const(tile_b),
                  pl.BlockSpec((4, sw), lambda b, i: (0, b)),
                  pl.BlockSpec((1, 4, ms, sw), lambda b, i: (i, 0, 0, b))],
        out_specs=[pl.BlockSpec((tok, cw), lambda b, i: (i, b)),
                   pl.BlockSpec((1, 4, ms, sw), lambda b, i: (i, 0, 0, b))],
        out_shape=[jax.ShapeDtypeStruct(u.shape, F32), jax.ShapeDtypeStruct(h0.shape, F32)],
        scratch_shapes=[pltpu.VMEM((t * cw, t * cw), BF16), pltpu.VMEM((t * cw, 4 * sw), BF16),
                        pltpu.VMEM((4 * sw, t * cw), BF16), pltpu.VMEM((4 * sw // LANES, tok // t, LANES), F32)],
        compiler_params=_cparams("arbitrary", "arbitrary"),
        name="s5_chunked",
    )(u, be_re, be_im, cs_re, cs_ni, kc, tile_k, tile_b, a4, h0)


def _s5_tiles(n, p):
    t, gb = S5_T, S5_GB
    eye = lambda k: np.eye(k, dtype=np.float32)
    tile_k = np.einsum("ab,pq->apbq", eye(t), eye(p))[:, :, :, None, :] * np.ones((1, 1, 1, gb, 1), np.float32)
    tile_b = eye(n)[:, None, :] * np.ones((1, gb, 1), np.float32)
    return jnp.asarray(tile_k.reshape(t * p, t * gb * p)), jnp.asarray(tile_b.reshape(n, gb * n))


def _post_residual(x, y, gate, g, b, alpha):
    return _layer_norm(alpha * x + gate * y, g, b)


def _outab_kernel(xp_ref, xs_ref, mod_ref, yc_ref, ys_ref, u_ref, dsk_ref, wglu_ref, wout_ref, g_ref, b_ref, o_ref,
                  *, tiles_p, tiles_per_s, d, c, alpha):
    i = pl.program_id(0)
    grp = _group_of_tile(i, tiles_p, tiles_per_s)
    y_s = ys_ref[...] + dsk_ref[...] * u_ref[...]
    y_s = jax.nn.gelu(y_s)
    z = jnp.dot(y_s.astype(BF16), wglu_ref[...], preferred_element_type=F32)
    y_ssm = y_s * jax.nn.sigmoid(z)
    out = (jnp.dot(yc_ref[...], wout_ref[0:c, :], preferred_element_type=F32)
           + jnp.dot(y_ssm.astype(BF16), wout_ref[c:2 * c, :], preferred_element_type=F32))
    gate = _mod_chunk(mod_ref, grp, 2, d)
    o_ref[...] = _post_residual(_pair_rows(i, tiles_p, xp_ref, xs_ref), out, gate, g_ref[...], b_ref[...], alpha)


def _outab(xp, xs, mod_l, y_conv, y_scan, u, d_skip, w_glu, w_out, ln_g, ln_b, tm, tiles_p, tiles_per_s, alpha):
    t, d = xp.shape[0] + xs.shape[0], xp.shape[1]
    c = u.shape[1]
    kern = functools.partial(_outab_kernel, tiles_p=tiles_p, tiles_per_s=tiles_per_s, d=d, c=c, alpha=alpha)
    row = lambda w: pl.BlockSpec((tm, w), lambda i: (i, 0))
    full = lambda a: pl.BlockSpec(a.shape, lambda i: (0,) * a.ndim)
    d_skip, ln_g, ln_b = d_skip.reshape(1, c), ln_g.reshape(1, d), ln_b.reshape(1, d)
    return pl.pallas_call(
        kern,
        grid=(t // tm,),
        in_specs=_pair_specs(tm, d, tiles_p) + [full(mod_l), row(c), row(c), row(c), full(d_skip), full(w_glu),
                                                full(w_out), full(ln_g), full(ln_b)],
        out_specs=row(d),
        out_shape=jax.ShapeDtypeStruct((t, d), F32),
        compiler_params=_cparams("arbitrary"),
        name="outproj_ab",
    )(xp, xs, mod_l, y_conv, y_scan, u, d_skip, w_glu, w_out, ln_g, ln_b)


def _mlp_kernel(x_ref, mod_ref, w1_ref, w2_ref, g_ref, b_ref, *rest, tiles_p, tiles_per_s, d, n_f, alpha, split):
    outs, (h_scr, acc_scr) = rest[:-2], rest[-2:]
    i = pl.program_id(0)
    f = pl.program_id(1)
    grp = _group_of_tile(i, tiles_p, tiles_per_s)
    tm = h_scr.shape[0]
    halves = [slice(0, tm // 2), slice(tm // 2, tm)]

    def ffn(rows):
        a = jnp.dot(h_scr[rows, :], w1_ref[...], preferred_element_type=F32)
        a = jnp.square(jnp.maximum(a, 0.0)).astype(BF16)
        return jnp.dot(a, w2_ref[...], preferred_element_type=F32)

    @pl.when(f == 0)
    def _():
        shift = _mod_chunk(mod_ref, grp, 3, d)
        scale = _mod_chunk(mod_ref, grp, 4, d)
        for rows in halves:
            h_scr[rows, :] = (x_ref[rows, :] * (1 + scale) + shift).astype(BF16)
            acc_scr[rows, :] = ffn(rows)

    @pl.when(jnp.logical_and(f > 0, f < n_f - 1))
    def _():
        acc_scr[...] += ffn(slice(None))

    @pl.when(f == n_f - 1)
    def _():
        gate = _mod_chunk(mod_ref, grp, 5, d)
        for rows in halves:
            res = _post_residual(x_ref[rows, :], acc_scr[rows, :] + ffn(rows), gate, g_ref[...], b_ref[...], alpha)
            if split:
                @pl.when(i < tiles_p)
                def _():
                    outs[0][rows, :] = res

                @pl.when(i >= tiles_p)
                def _():
                    outs[1][rows, :] = res
            else:
                outs[0][rows, :] = res


def _mlp(x, mod_l, w1, w2, layer, ln_g, ln_b, tm, tf, tiles_p, tiles_per_s, alpha, split):
    t, d = x.shape
    n_f = w1.shape[2] // tf
    assert n_f >= 2
    kern = functools.partial(_mlp_kernel, tiles_p=tiles_p, tiles_per_s=tiles_per_s, d=d, n_f=n_f, alpha=alpha,
                             split=split)
    ln_g, ln_b = ln_g.reshape(1, d), ln_b.reshape(1, d)
    if split:
        out_specs = _pair_specs(tm, d, tiles_p, grid_rank=2)
        out_shape = [jax.ShapeDtypeStruct((tiles_p * tm, d), F32), jax.ShapeDtypeStruct((t - tiles_p * tm, d), F32)]
    else:
        out_specs = [pl.BlockSpec((tm, d), lambda i, f: (i, 0))]
        out_shape = [jax.ShapeDtypeStruct((t, d), F32)]
    return pl.pallas_call(
        kern,
        grid=(t // tm, n_f),
        in_specs=[pl.BlockSpec((tm, d), lambda i, f: (i, 0)),
                  pl.BlockSpec(mod_l.shape, lambda i, f: (0, 0)),
                  pl.BlockSpec((None, d, tf), lambda i, f: (layer, 0, f)),
                  pl.BlockSpec((None, tf, d), lambda i, f: (layer, f, 0)),
                  pl.BlockSpec((1, d), lambda i, f: (0, 0)),
                  pl.BlockSpec((1, d), lambda i, f: (0, 0))],
        out_specs=out_specs,
        out_shape=out_shape,
        scratch_shapes=[pltpu.VMEM((tm, d), BF16), pltpu.VMEM((tm, d), F32)],
        compiler_params=_cparams("arbitrary", "arbitrary"),
        name="mlp",
    )(x, mod_l, w1, w2, ln_g, ln_b)


def _rope_tables(n_pos, dk):
    ax = dk // 2
    half = ax // 2
    freqs = ROPE_BASE ** (-jnp.arange(half, dtype=F32) / half)
    pos = jnp.arange(n_pos)
    row = (pos // LATENT_GRID_W).astype(F32)
    col = (pos % LATENT_GRID_W).astype(F32)
    ang_r, ang_c = row[:, None] * freqs, col[:, None] * freqs
    cos = jnp.concatenate([jnp.cos(ang_r)] * 2 + [jnp.cos(ang_c)] * 2, axis=-1)
    sin = jnp.concatenate([-jnp.sin(ang_r), jnp.sin(ang_r), -jnp.sin(ang_c), jnp.sin(ang_c)], axis=-1)
    rep = LANES // dk
    return jnp.tile(cos, (1, rep)), jnp.tile(sin, (1, rep))


def _qkv_kernel(x_ref, mod_ref, w_ref, cos_ref, sin_ref, q_ref, k_ref, v_ref, kc_ref, vc_ref,
                *, tiles_p, tiles_per_s, d, dk, dv, q_scale):
    i = pl.program_id(0)
    g = _group_of_tile(i, tiles_p, tiles_per_s)
    shift = _mod_chunk(mod_ref, g, 0, d)
    scale = _mod_chunk(mod_ref, g, 1, d)
    h = (x_ref[...] * (1 + scale) + shift).astype(BF16)
    tn = q_ref.shape[-1]
    quarter = dk // 4
    latent = i >= tiles_p
    cos, sin = cos_ref[...], sin_ref[...]
    lane = lax.broadcasted_iota(jnp.int32, cos.shape, 1)
    first = (lane % (2 * quarter)) < quarter

    def rotated(y, o_ref, out_scale):
        for cb in range(tn // LANES):
            yb = y[:, cb * LANES:(cb + 1) * LANES]
            partner = jnp.where(first, pltpu.roll(yb, LANES - quarter, 1), pltpu.roll(yb, quarter, 1))
            out = jnp.where(latent, yb * cos + partner * sin, yb)
            o_ref[:, cb * LANES:(cb + 1) * LANES] = (out * out_scale).astype(o_ref.dtype)

    rotated(jnp.dot(h, w_ref[:, 0:tn], preferred_element_type=F32), q_ref, q_scale)
    y_k = jnp.dot(h, w_ref[:, tn:2 * tn], preferred_element_type=F32)
    rotated(y_k, k_ref, 1.0)
    y_v = jnp.dot(h, w_ref[:, 2 * tn:3 * tn], preferred_element_type=F32)
    v_ref[...] = y_v.astype(v_ref.dtype)

    @pl.when(i < tiles_p)
    def _():
        n_heads = kc_ref.shape[2]
        for m in range(2):
            for hh in range(n_heads):
                c0 = (m * n_heads + hh) * dk
                kc_ref[0, m, hh] = y_k[:, c0:c0 + dk]
        for hh in range(n_heads):
            vc_ref[0, hh] = y_v[:, hh * dv:(hh + 1) * dv]


def _qkv(x, mod_l, w_qkv, cos_t, sin_t, tm, tiles_p, tiles_per_s, dk, dv, bp, lp, q_scale):
    t, d = x.shape
    n_out = w_qkv.shape[1]
    assert n_out % 3 == 0 and lp % tm == 0
    tn = n_out // 3
    n_heads = tn // dv
    per_seq = lp // tm
    kern = functools.partial(_qkv_kernel, tiles_p=tiles_p, tiles_per_s=tiles_per_s, d=d, dk=dk, dv=dv,
                             q_scale=q_scale)
    pos_blk = lambda i: (jnp.maximum(i - tiles_p, 0) % tiles_per_s, 0)
    c_tile = lambda i: jnp.minimum(i, tiles_p - 1)
    row = pl.BlockSpec((tm, tn), lambda i: (i, 0))
    return pl.pallas_call(
        kern,
        grid=(t // tm,),
        in_specs=[pl.BlockSpec((tm, d), lambda i: (i, 0)),
                  pl.BlockSpec(mod_l.shape, lambda i: (0, 0)),
                  pl.BlockSpec((d, n_out), lambda i: (0, 0), pipeline_mode=pl.Buffered(1)),
                  pl.BlockSpec((tm, LANES), pos_blk),
                  pl.BlockSpec((tm, LANES), pos_blk)],
        out_specs=[row, row, row,
                   pl.BlockSpec((1, 2, n_heads, tm, dk), lambda i: (c_tile(i) // per_seq, 0, 0, c_tile(i) % per_seq, 0)),
                   pl.BlockSpec((1, n_heads, tm, dv), lambda i: (c_tile(i) // per_seq, 0, c_tile(i) % per_seq, 0))],
        out_shape=[jax.ShapeDtypeStruct((t, tn), BF16)] * 3
        + [jax.ShapeDtypeStruct((bp, 2, n_heads, lp, dk), F32), jax.ShapeDtypeStruct((bp, n_heads, lp, dv), F32)],
        compiler_params=_cparams("arbitrary"),
        name="qkv_proj",
    )(x, mod_l, w_qkv, cos_t, sin_t)


def _attn_kernel(*refs, heads_step, dk, dv, scale, fold_scale, lam_init, has_cache):
    if has_cache:
        q1_ref, q2_ref, k1_ref, k2_ref, v_ref, ck_ref, cv_ref, lamv_ref, sg_ref, o_ref, s_scr, w_scr = refs
    else:
        q1_ref, q2_ref, k1_ref, k2_ref, v_ref, lamv_ref, sg_ref, o_ref, s_scr, w_scr = refs
        ck_ref = cv_ref = None
    lv = lamv_ref[...]
    lam = (jnp.exp(jnp.sum(lv[0:1] * lv[1:2], axis=-1, keepdims=True))
           - jnp.exp(jnp.sum(lv[2:3] * lv[3:4], axis=-1, keepdims=True)) + lam_init)
    nt = (((1,), (1,)), ((), ()))
    tn = (((0,), (0,)), ((), ()))
    per_blk = LANES // dk
    tq, lk = q1_ref.shape[0], k1_ref.shape[0]
    lane = lax.broadcasted_iota(jnp.int32, (tq, LANES), 1)

    for blk in range(heads_step // per_blk):
        bs = slice(blk * LANES, (blk + 1) * LANES)
        for m, (q_ref, k_ref) in enumerate(((q1_ref, k1_ref), (q2_ref, k2_ref))):
            kb, qb = k_ref[:, bs], q_ref[:, bs]
            for sub in range(per_blk):
                head = blk * per_blk + sub
                qm = jnp.where((lane >= sub * dk) & (lane < (sub + 1) * dk), qb, jnp.zeros_like(qb))
                s_scr[2 * head + m, 0:lk, :] = lax.dot_general(kb, qm, nt, preferred_element_type=F32)
                if has_cache:
                    qc = qb[:, sub * dk:(sub + 1) * dk]
                    s_scr[2 * head + m, lk:, :] = lax.dot_general(ck_ref[0, 0, m, head].astype(BF16), qc, nt,
                                                                  preferred_element_type=F32)
    dens = []
    for hm in range(2 * heads_step):
        s = s_scr[hm]
        if not fold_scale:
            s = s * scale
        e = jnp.exp(s - jnp.max(s, axis=0, keepdims=True))
        dens.append(jnp.sum(e, axis=0, keepdims=True))
        s_scr[hm] = e
    for head in range(heads_step):
        ratio = lam * dens[2 * head] / dens[2 * head + 1]
        w_scr[head] = (s_scr[2 * head] - s_scr[2 * head + 1] * ratio).astype(BF16)
    for head in range(heads_step):
        vs = slice(head * dv, (head + 1) * dv)
        o_t = lax.dot_general(v_ref[:, vs], w_scr[head, 0:lk, :], tn, preferred_element_type=F32)
        if has_cache:
            o_t = o_t + lax.dot_general(cv_ref[0, 0, head].astype(BF16), w_scr[head, lk:, :], tn,
                                        preferred_element_type=F32)
        o_t = o_t * (1.0 / dens[2 * head])
        o_t = o_t * lax.rsqrt(jnp.mean(jnp.square(o_t), axis=0, keepdims=True) + LN_EPS)
        o = o_t.T * sg_ref[...] * (1.0 - lam_init)
        o_ref[:, vs] = o.astype(o_ref.dtype)


def _softmax_scale(dk):
    scale = dk ** -0.5
    return scale, math.frexp(scale)[0] == 0.5


def _attention(q, k, v, lamv, subln_g, *, row0, n_seq, lq, tq, hp, n_heads, dk, dv, lam_init, cache=None):
    per_blk = LANES // dk
    heads_step = hp * per_blk
    n_hblk = n_heads // heads_step
    map2 = n_heads * dk // (hp * LANES)
    scale, fold_scale = _softmax_scale(dk)
    qb0, kb0 = row0 // tq, row0 // lq
    n_q = lq // tq
    q_spec = lambda off: pl.BlockSpec((tq, hp * LANES), lambda b, h, qi: (qb0 + b * n_q + qi, off + h))
    k_spec = lambda off: pl.BlockSpec((lq, hp * LANES), lambda b, h, qi: (kb0 + b, off + h))
    in_specs = [q_spec(0), q_spec(map2), k_spec(0), k_spec(map2),
                pl.BlockSpec((lq, heads_step * dv), lambda b, h, qi: (kb0 + b, h))]
    args = [q, q, k, k, v]
    if cache is not None:
        cache_k, cache_v, o_i = cache
        past = cache_k.shape[-2]
        in_specs += [pl.BlockSpec((1, 1, 2, heads_step, past, dk), lambda b, h, qi: (b, o_i, 0, h, 0, 0)),
                     pl.BlockSpec((1, 1, heads_step, past, dv), lambda b, h, qi: (b, o_i, h, 0, 0))]
        args += [cache_k, cache_v]
    in_specs += [pl.BlockSpec(lamv.shape, lambda b, h, qi: (0, 0)),
                 pl.BlockSpec((1, dv), lambda b, h, qi: (0, 0))]
    args += [lamv, subln_g.reshape(1, dv)]
    kern = functools.partial(_attn_kernel, heads_step=heads_step, dk=dk, dv=dv, scale=scale,
                             fold_scale=fold_scale, lam_init=lam_init, has_cache=cache is not None)
    lk_all = lq + (cache[0].shape[-2] if cache is not None else 0)
    scratch = [pltpu.VMEM((2 * heads_step, lk_all, tq), F32), pltpu.VMEM((heads_step, lk_all, tq), BF16)]
    return pl.pallas_call(
        kern,
        grid=(n_seq, n_hblk, n_q),
        in_specs=in_specs,
        scratch_shapes=scratch,
        out_specs=pl.BlockSpec((tq, heads_step * dv), lambda b, h, qi: (b * n_q + qi, h)),
        out_shape=jax.ShapeDtypeStruct((n_seq * lq, n_heads * dv), BF16),
        compiler_params=_cparams("arbitrary", "arbitrary", "arbitrary"),
        name="diff_attn_cache" if cache is not None else "diff_attn",
    )(*args)


def _outc_kernel(x_ref, mod_ref, op_ref, os_ref, w_ref, g_ref, b_ref, o_ref, *, tiles_p, tiles_per_s, d, alpha):
    i = pl.program_id(0)
    grp = _group_of_tile(i, tiles_p, tiles_per_s)
    gate = _mod_chunk(mod_ref, grp, 2, d)
    tm = x_ref.shape[0]
    for rows in (slice(0, tm // 2), slice(tm // 2, tm)):
        o_in = jnp.where(i < tiles_p, op_ref[rows, :], os_ref[rows, :])
        out = jnp.dot(o_in, w_ref[...], preferred_element_type=F32)
        o_ref[rows, :] = _post_residual(x_ref[rows, :], out, gate, g_ref[...], b_ref[...], alpha)


def _outc(x, mod_l, o_p, o_s, w_out, ln_g, ln_b, tm, tiles_p, tiles_per_s, alpha):
    t, d = x.shape
    kin = o_p.shape[1]
    kern = functools.partial(_outc_kernel, tiles_p=tiles_p, tiles_per_s=tiles_per_s, d=d, alpha=alpha)
    ln_g, ln_b = ln_g.reshape(1, d), ln_b.reshape(1, d)
    full = lambda a: pl.BlockSpec(a.shape, lambda i: (0,) * a.ndim)
    return pl.pallas_call(
        kern,
        grid=(t // tm,),
        in_specs=[pl.BlockSpec((tm, d), lambda i: (i, 0)), full(mod_l)] + _pair_specs(tm, kin, tiles_p)
        + [full(w_out), full(ln_g), full(ln_b)],
        out_specs=pl.BlockSpec((tm, d), lambda i: (i, 0)),
        out_shape=jax.ShapeDtypeStruct((t, d), F32),
        compiler_params=_cparams("arbitrary"),
        name="outproj_c",
    )(x, mod_l, o_p, o_s, w_out, ln_g, ln_b)


class _Tiles(NamedTuple):
    rows: int
    rows_wide: int
    hidden: int
    s5_tokens: int
    conv_chunk: int
    attn_q: int
    attn_blocks: int


def _plan_tiles(tp, ts, lp, ls):
    tiles = _Tiles(rows=256, rows_wide=512, hidden=1024, s5_tokens=min(4096, tp, ts), conv_chunk=min(lp, 256),
                   attn_q=256, attn_blocks=2)
    for tm in (tiles.rows, tiles.rows_wide):
        assert tp % tm == 0 and ls % tm == 0
    tok = tiles.s5_tokens
    assert tok % lp == 0 and tok % ls == 0 and tp % tok == 0 and ts % tok == 0 and tp % ls == 0
    return tiles


def kernel(x_prompt, x_sample, state_s5_re, state_s5_im, cache_k, cache_v, c, c_ctx, w_mod, b_mod, ln_g, ln_b, w_in_ab, w_dw, b_dw, conv_ln_g, conv_ln_b, s5_lambda_re, s5_lambda_im, s5_log_dt, s5_b_re, s5_b_im, s5_c_re, s5_c_im, s5_d, w_glu, w_out_ab, w_qkv, lam_q1, lam_k1, lam_q2, lam_k2, subln_g, w_out_c, w_ff1, w_ff2):
    bp, lp, d = x_prompt.shape
    bs, ls, _ = x_sample.shape
    depth = w_mod.shape[0]
    tp, ts = bp * lp, bs * ls
    alpha = (2 * depth) ** 0.25
    assert 1 + bs <= MOD_ROWS

    tiles = _plan_tiles(tp, ts, lp, ls)
    tm, tm_mlp, tf = tiles.rows, tiles.rows_wide, tiles.hidden

    xp, xs = x_prompt.reshape(tp, d), x_sample.reshape(ts, d)
    cvec = jnp.zeros((MOD_ROWS, d), F32).at[0].set(c_ctx).at[1:1 + bs].set(c)
    mod = _modvec(cvec, w_mod, b_mod)

    g_ssm, n_ssm, p_ssm = s5_b_re.shape[2:]
    dk = lam_q1.shape[-1]
    dv = subln_g.shape[-1]
    n_heads = w_out_c.shape[1] // dv

    w_ff1_bf, w_ff2_bf = w_ff1.astype(BF16), w_ff2.astype(BF16)
    s_re, s_im, k_list, v_list = [], [], [], []
    for l in range(depth):
        mod_l = mod[l]
        if l % 2 == 0:
            e = l // 2
            if l > 0:
                xp, xs = x[:tp], x[tp:]
            ug, u = _inproj(xp, xs, mod_l, w_in_ab[e].astype(BF16), tm, tp // tm, ls // tm)
            y_conv = _conv_module(ug, w_dw[e], b_dw[e], conv_ln_g[e], conv_ln_b[e], tiles.conv_chunk, lp, ls, tp)
            prep = _s5_prep(s5_lambda_re[e], s5_lambda_im[e], s5_log_dt[e],
                            s5_b_re[e], s5_b_im[e], s5_c_re[e], s5_c_im[e])
            a_re, a_im = prep[5], prep[6]
            gn = g_ssm * n_ssm
            nat = lambda a: a.reshape(2, g_ssm, p_ssm, n_ssm)[:, :, 0].reshape(2, gn)
            a4 = jnp.stack([nat(a_re)[0], nat(a_im)[0], nat(a_re)[1], nat(a_im)[1]])
            tile_k, tile_b = _s5_tiles(n_ssm, p_ssm)
            tok = tiles.s5_tokens
            seq_p, seq_s, tiles_p5 = tok // lp, tok // ls, tp // tok
            ms = max(seq_p, seq_s)
            st = lambda a, dr: jnp.pad(a[:, e, dr].reshape(ts // tok, seq_s, gn), ((0, 0), (0, ms - seq_s), (0, 0)))
            h0_s = jnp.stack([st(state_s5_re, 0), st(state_s5_im, 0), st(state_s5_re, 1), st(state_s5_im, 1)], axis=1)
            h0 = jnp.concatenate([jnp.zeros((tiles_p5, 4, ms, gn), F32), h0_s], axis=0)
            y_scan, hf = _s5_chunked(u, h0, prep[:5], tile_k, tile_b, a4, tok=tok,
                                     geoms=((seq_p, lp // S5_T), (seq_s, ls // S5_T)), n_tiles_p=tiles_p5,
                                     n=n_ssm, p=p_ssm)
            hf = hf[:tiles_p5, :, :seq_p].transpose(1, 0, 2, 3).reshape(2, 2, bp, g_ssm, n_ssm)
            s_re.append(hf[:, 0].transpose(1, 0, 2, 3))
            s_im.append(hf[:, 1].transpose(1, 0, 2, 3))
            x = _outab(xp, xs, mod_l, y_conv, y_scan, u, s5_d[e], w_glu[e].astype(BF16),
                       w_out_ab[e].astype(BF16), ln_g[l, 0], ln_b[l, 0], tm, tp // tm, ls // tm, alpha)
        else:
            o_i = l // 2
            lam_init = 0.8 - 0.6 * math.exp(-0.3 * l)
            cos_t, sin_t = _rope_tables(ls, dk)
            scale, fold_scale = _softmax_scale(dk)
            q, k, v, k_new, v_new = _qkv(x, mod_l, w_qkv[o_i].astype(BF16), cos_t, sin_t, tm, tp // tm, ls // tm,
                                     dk, dv, bp, lp, scale if fold_scale else 1.0)
            lamv = jnp.stack([lam_q1[o_i], lam_k1[o_i], lam_q2[o_i], lam_k2[o_i]])
            geo = dict(n_heads=n_heads, dk=dk, dv=dv, lam_init=lam_init)
            o_p = _attention(q, k, v, lamv, subln_g[o_i], row0=0, n_seq=bp, lq=lp, tq=lp,
                             hp=n_heads * dk // LANES, **geo)
            o_s = _attention(q, k, v, lamv, subln_g[o_i], row0=tp, n_seq=bs, lq=ls,
                             tq=tiles.attn_q, hp=tiles.attn_blocks,
                             cache=(cache_k, cache_v, o_i), **geo)
            k_list.append(k_new)
            v_list.append(v_new)
            x = _outc(x, mod_l, o_p, o_s, w_out_c[o_i].astype(BF16), ln_g[l, 0], ln_b[l, 0],
                      tm_mlp, tp // tm_mlp, ls // tm_mlp, alpha)
        res = _mlp(x, mod_l, w_ff1_bf, w_ff2_bf, l, ln_g[l, 1], ln_b[l, 1],
                   tm_mlp, tf, tp // tm_mlp, ls // tm_mlp, alpha, split=l == depth - 1)
        x = res[0]

    return (res[0].reshape(bp, lp, d), res[1].reshape(bs, ls, d),
            jnp.stack(s_re, axis=1), jnp.stack(s_im, axis=1),
            jnp.stack(k_list, axis=1), jnp.stack(v_list, axis=1))
```

```python
import functools
import math
from typing import NamedTuple

import jax
import jax.numpy as jnp
import numpy as np
from jax import lax
from jax.experimental import pallas as pl
from jax.experimental.pallas import tpu as pltpu

F32 = jnp.float32
BF16 = jnp.bfloat16

LN_EPS = 1e-5
ROPE_BASE = 10000.0
LATENT_GRID_W = 64
MOD_ROWS = 8
V7X_VMEM_LIMIT = 56 * 1024 * 1024
LANES = 128
SUBLANES = 8


def _cparams(*sem):
    return pltpu.CompilerParams(dimension_semantics=sem, vmem_limit_bytes=V7X_VMEM_LIMIT)


def _layer_norm(z, g, b):
    mu = jnp.mean(z, axis=-1, keepdims=True)
    zc = z - mu
    var = jnp.mean(jnp.square(zc), axis=-1, keepdims=True)
    return zc * lax.rsqrt(var + LN_EPS) * g + b


def _group_of_tile(i, tiles_p, tiles_per_s):
    return jnp.where(i < tiles_p, 0, 1 + jnp.maximum(i - tiles_p, 0) // tiles_per_s)


def _mod_chunk(mod_ref, g, k, d):
    return mod_ref[pl.ds(g, 1), k * d:(k + 1) * d]


def _pair_specs(tm, width, tiles_p, grid_rank=1):
    if grid_rank == 1:
        return [pl.BlockSpec((tm, width), lambda i: (jnp.minimum(i, tiles_p - 1), 0)),
                pl.BlockSpec((tm, width), lambda i: (jnp.maximum(i - tiles_p, 0), 0))]
    return [pl.BlockSpec((tm, width), lambda i, f: (jnp.minimum(i, tiles_p - 1), 0)),
            pl.BlockSpec((tm, width), lambda i, f: (jnp.maximum(i - tiles_p, 0), 0))]


def _pair_rows(i, tiles_p, p_ref, s_ref):
    return jnp.where(i < tiles_p, p_ref[...], s_ref[...])


def _modvec_kernel(cv_ref, w_ref, b_ref, o_ref):
    cv = cv_ref[...]
    s = (cv * jax.nn.sigmoid(cv)).astype(BF16)
    o_ref[0] = jnp.dot(s, w_ref[0].astype(BF16), preferred_element_type=F32) + b_ref[0]


def _modvec(cvec, w_mod, b_mod, tn=1024):
    depth, d, n = w_mod.shape
    return pl.pallas_call(
        _modvec_kernel,
        grid=(depth, n // tn),
        in_specs=[pl.BlockSpec((MOD_ROWS, d), lambda l, j: (0, 0)),
                  pl.BlockSpec((1, d, tn), lambda l, j: (l, 0, j)),
                  pl.BlockSpec((1, 1, tn), lambda l, j: (l, 0, j))],
        out_specs=pl.BlockSpec((1, MOD_ROWS, tn), lambda l, j: (l, 0, j)),
        out_shape=jax.ShapeDtypeStruct((depth, MOD_ROWS, n), F32),
        compiler_params=_cparams("arbitrary", "arbitrary"),
        name="modvec",
    )(cvec, w_mod, b_mod.reshape(depth, 1, n))


def _inproj_kernel(xp_ref, xs_ref, mod_ref, w_ref, ug_ref, u_ref, *, tiles_p, tiles_per_s, d, c):
    i = pl.program_id(0)
    g = _group_of_tile(i, tiles_p, tiles_per_s)
    shift = _mod_chunk(mod_ref, g, 0, d)
    scale = _mod_chunk(mod_ref, g, 1, d)
    h = (_pair_rows(i, tiles_p, xp_ref, xs_ref) * (1 + scale) + shift).astype(BF16)
    a_val = jnp.dot(h, w_ref[:, 0:c], preferred_element_type=F32)
    a_gate = jnp.dot(h, w_ref[:, c:2 * c], preferred_element_type=F32)
    ug_ref[...] = a_val * jax.nn.sigmoid(a_gate)
    u_ref[...] = jnp.dot(h, w_ref[:, 2 * c:3 * c], preferred_element_type=F32)


def _inproj(xp, xs, mod_l, w_in, tm, tiles_p, tiles_per_s):
    t, d = xp.shape[0] + xs.shape[0], xp.shape[1]
    c = w_in.shape[1] // 3
    kern = functools.partial(_inproj_kernel, tiles_p=tiles_p, tiles_per_s=tiles_per_s, d=d, c=c)
    return pl.pallas_call(
        kern,
        grid=(t // tm,),
        in_specs=_pair_specs(tm, d, tiles_p) + [pl.BlockSpec(mod_l.shape, lambda i: (0, 0)),
                                                pl.BlockSpec(w_in.shape, lambda i: (0, 0))],
        out_specs=[pl.BlockSpec((tm, c), lambda i: (i, 0)),
                   pl.BlockSpec((tm, c), lambda i: (i, 0))],
        out_shape=[jax.ShapeDtypeStruct((t, c), F32), jax.ShapeDtypeStruct((t, c), F32)],
        compiler_params=_cparams("arbitrary"),
        name="inproj",
    )(xp, xs, mod_l, w_in)


CONV_HALO = 16
CONV_ROWS = 64
CONV_COLS = 128


def _conv_kernel(prev_ref, cur_ref, next_ref, w_ref, b_ref, g_ref, beta_ref, o_ref, pad_scr, sh_scr, conv_scr,
                 *, chunks_p, chunks_s, n_chunks_p, width, lc, c):
    i = pl.program_id(0)
    in_p = i < n_chunks_p
    k = jnp.where(in_p, i % chunks_p, jnp.maximum(i - n_chunks_p, 0) % chunks_s)
    last = jnp.where(in_p, chunks_p - 1, chunks_s - 1)
    has_prev = (k > 0).astype(F32)
    has_next = (k < last).astype(F32)
    pad_scr[0:CONV_HALO, :] = prev_ref[...] * has_prev
    pad_scr[CONV_HALO:CONV_HALO + lc, :] = cur_ref[...]
    pad_scr[CONV_HALO + lc:2 * CONV_HALO + lc, :] = next_ref[...] * has_next
    off = CONV_HALO - width // 2
    span = lc + CONV_HALO + SUBLANES
    for sft in range(SUBLANES):
        sh_scr[sft] = pad_scr[sft:sft + span, :]

    for r0 in range(0, lc, CONV_ROWS):
        for cb in range(c // CONV_COLS):
            cs = slice(cb * CONV_COLS, (cb + 1) * CONV_COLS)
            acc = jnp.zeros((CONV_ROWS, CONV_COLS), F32)
            for kk in range(width):
                whole, sft = divmod(kk + off, SUBLANES)
                base = r0 + whole * SUBLANES
                acc = acc + sh_scr[sft, base:base + CONV_ROWS, cs] * w_ref[kk:kk + 1, cs]
            conv_scr[r0:r0 + CONV_ROWS, cs] = acc + b_ref[:, cs]
    y = _layer_norm(conv_scr[...], g_ref[...], beta_ref[...])
    o_ref[...] = (y * jax.nn.sigmoid(y)).astype(o_ref.dtype)


def _conv_module(ug, w_dw, b_dw, ln_g, ln_b, lc, lp, ls, tp):
    t, c = ug.shape
    width = w_dw.shape[0]
    assert width // 2 < CONV_HALO and lc % CONV_HALO == 0 and lp % lc == 0 and ls % lc == 0
    hb = lc // CONV_HALO
    n_halo_blocks = t // CONV_HALO
    kern = functools.partial(_conv_kernel, chunks_p=lp // lc, chunks_s=ls // lc, n_chunks_p=tp // lc,
                             width=width, lc=lc, c=c)
    vec = lambda a: a.reshape(1, c)
    return pl.pallas_call(
        kern,
        grid=(t // lc,),
        in_specs=[pl.BlockSpec((CONV_HALO, c), lambda i: (jnp.maximum(i * hb - 1, 0), 0)),
                  pl.BlockSpec((lc, c), lambda i: (i, 0)),
                  pl.BlockSpec((CONV_HALO, c), lambda i: (jnp.minimum((i + 1) * hb, n_halo_blocks - 1), 0)),
                  pl.BlockSpec((width, c), lambda i: (0, 0)),
                  pl.BlockSpec((1, c), lambda i: (0, 0)),
                  pl.BlockSpec((1, c), lambda i: (0, 0)),
                  pl.BlockSpec((1, c), lambda i: (0, 0))],
        out_specs=pl.BlockSpec((lc, c), lambda i: (i, 0)),
        out_shape=jax.ShapeDtypeStruct((t, c), BF16),
        scratch_shapes=[pltpu.VMEM((lc + 2 * CONV_HALO, c), F32),
                        pltpu.VMEM((SUBLANES, lc + CONV_HALO + SUBLANES, c), F32), pltpu.VMEM((lc, c), F32)],
        compiler_params=_cparams("arbitrary"),
        name="conv_module",
    )(ug, ug, ug, w_dw, vec(b_dw), vec(ln_g), vec(ln_b))


S5_T = 16
S5_GB = 8


def _s5_prep_kernel(bt_re_ref, bt_im_ref, la_re_ref, la_im_ref, dta_ref, ct_re_ref, ct_im_ref, lb_re_ref, lb_im_ref,
                    dtb_ref, be_re_ref, be_im_ref, cs_re_ref, cs_ni_ref, kc_ref, a_re_ref, a_im_ref, *, n, p):
    t = S5_T
    fwd = pl.program_id(0) == 0

    lam_re, lam_im = la_re_ref[0, 0], la_im_ref[0, 0]
    dt = jnp.exp(dta_ref[0, 0])
    mag = jnp.exp(lam_re * dt)
    ar, ai = mag * jnp.cos(lam_im * dt), mag * jnp.sin(lam_im * dt)
    den = jnp.square(lam_re) + jnp.square(lam_im)
    coef_re = ((ar - 1) * lam_re + ai * lam_im) / den
    coef_im = (ai * lam_re - (ar - 1) * lam_im) / den
    b_re, b_im = bt_re_ref[0, 0], bt_im_ref[0, 0]
    bb_re = coef_re * b_re - coef_im * b_im
    bb_im = coef_re * b_im + coef_im * b_re
    rows = bb_re.shape[0]
    pw = []
    wr, wi = bb_re, bb_im
    pr, pi = jnp.ones_like(ar), jnp.zeros_like(ar)
    for k in range(t):
        pw.append((wr, wi))
        wr, wi = wr * ar - wi * ai, wr * ai + wi * ar
        pr, pi = pr * ar - pi * ai, pr * ai + pi * ar
    a_re_ref[0, 0] = pr
    a_im_ref[0, 0] = pi
    for j in range(t):
        be_re_ref[0, 0, j * rows:(j + 1) * rows, :] = jnp.where(fwd, pw[t - 1 - j][0], pw[j][0])
        be_im_ref[0, 0, j * rows:(j + 1) * rows, :] = jnp.where(fwd, pw[t - 1 - j][1], pw[j][1])

    ct_re, ct_im = ct_re_ref[0, 0], ct_im_ref[0, 0]
    shp = ct_re.shape
    lam_re, lam_im = lb_re_ref[0, 0], lb_im_ref[0, 0]
    dt = jnp.exp(dtb_ref[0, 0])
    mag = jnp.exp(lam_re * dt)
    sq_re = jnp.broadcast_to(mag * jnp.cos(lam_im * dt), shp)
    sq_im = jnp.broadcast_to(mag * jnp.sin(lam_im * dt), shp)
    blk = lax.broadcasted_iota(jnp.int32, shp, 1) // p
    k1 = jnp.where(fwd, blk + 1, t - blk)
    qr, qi = jnp.ones(shp, F32), jnp.zeros(shp, F32)
    n_bits = t.bit_length()
    for bit in range(n_bits):
        take = ((k1 >> bit) & 1) == 1
        qr, qi = (jnp.where(take, qr * sq_re - qi * sq_im, qr), jnp.where(take, qr * sq_im + qi * sq_re, qi))
        if bit + 1 < n_bits:
            sq_re, sq_im = sq_re * sq_re - sq_im * sq_im, 2.0 * (sq_re * sq_im)
    v_re = ct_re * qr - ct_im * qi
    v_im = ct_re * qi + ct_im * qr
    cs_re_ref[0, 0] = v_re
    cs_ni_ref[0, 0] = -v_im
    lane = lax.broadcasted_iota(jnp.int32, shp, 1)
    w = shp[1]
    v0_re = jnp.where(fwd, jnp.where(lane < p, ct_re, pltpu.roll(v_re, p, 1)),
                      jnp.where(lane >= w - p, ct_re, pltpu.roll(v_re, w - p, 1)))
    v0_im = jnp.where(fwd, jnp.where(lane < p, ct_im, pltpu.roll(v_im, p, 1)),
                      jnp.where(lane >= w - p, ct_im, pltpu.roll(v_im, w - p, 1)))
    hi = lax.Precision.HIGHEST
    for g in range(S5_GB):
        ra, rb = slice(g * p, (g + 1) * p), slice(g * n, (g + 1) * n)
        kc_ref[0, 0, ra, :] = (jnp.dot(bb_re[ra], v0_re[rb], precision=hi, preferred_element_type=F32)
                               - jnp.dot(bb_im[ra], v0_im[rb], precision=hi, preferred_element_type=F32))


def _s5_prep(lam_re, lam_im, log_dt, b_re, b_im, c_re, c_im):
    _, g, n, p = b_re.shape
    t = S5_T
    nb = g // S5_GB
    ra, rb = S5_GB * p, S5_GB * n
    lay_a = lambda a: jnp.broadcast_to(a[:, :, None, :], (2, g, p, n)).reshape(2, nb, ra, n)
    lay_b = lambda a: a.reshape(2, nb, rb, 1)
    bt = lambda a: a.transpose(0, 1, 3, 2).reshape(2, nb, ra, n)
    ct = lambda a: jnp.broadcast_to(a.transpose(0, 1, 3, 2)[:, :, :, None, :], (2, g, n, t, p)).reshape(2, nb, rb, t * p)
    dt_g = jnp.broadcast_to(log_dt[:, :, None], (2, g, n))
    blk = lambda r, c: pl.BlockSpec((1, 1, r, c), lambda d, i: (d, i, 0, 0))
    shp = lambda r, c: jax.ShapeDtypeStruct((2, nb, r, c), F32)
    kern = functools.partial(_s5_prep_kernel, n=n, p=p)
    return pl.pallas_call(
        kern,
        grid=(2, nb),
        in_specs=[blk(ra, n)] * 5 + [blk(rb, t * p)] * 2 + [blk(rb, 1)] * 3,
        out_specs=[blk(t * ra, n), blk(t * ra, n), blk(rb, t * p), blk(rb, t * p), blk(ra, t * p), blk(ra, n), blk(ra, n)],
        out_shape=[shp(t * ra, n), shp(t * ra, n), shp(rb, t * p), shp(rb, t * p), shp(ra, t * p), shp(ra, n), shp(ra, n)],
        compiler_params=_cparams("arbitrary", "arbitrary"),
        name="s5_prep",
    )(bt(b_re), bt(b_im), lay_a(lam_re), lay_a(lam_im), lay_a(dt_g), ct(c_re), ct(c_im),
      lay_b(lam_re), lay_b(lam_im), lay_b(dt_g))


def _s5_expand(src, tile, row_div, row_mod, lane_div, lane_mod, precision=None):
    full = jnp.dot(src, tile, precision=precision, preferred_element_type=F32)
    r = lax.broadcasted_iota(jnp.int32, full.shape, 0) // row_div % row_mod
    l = lax.broadcasted_iota(jnp.int32, full.shape, 1) // lane_div % lane_mod
    return jnp.where(r == l, full, 0.0)


def _s5_chunk_kernel(u_ref, be_re_ref, be_im_ref, cs_re_ref, cs_ni_ref, kc_ref, tk_ref, tb_ref, a_ref, h0_ref,
                     y_ref, hf_ref, m8_scr, be8_scr, cs8_scr, e_scr, *, geoms, n_tiles_p, n, p):
    t, gb = S5_T, S5_GB
    cw = gb * p
    sw = gb * n
    rows = geoms[0][0] * geoms[0][1]
    tile_i = pl.program_id(1)

    @pl.when(tile_i == 0)
    def _():
        tk, tb = tk_ref[...].astype(BF16), tb_ref[...].astype(BF16)
        step = 4 * cw
        for r0 in range(0, t * cw, step):
            for part, (ref, d) in enumerate(((be_re_ref, 0), (be_im_ref, 0), (be_re_ref, 1), (be_im_ref, 1))):
                be8_scr[r0:r0 + step, part * sw:(part + 1) * sw] = _s5_expand(
                    ref[d, 0, r0:r0 + step, :].astype(BF16), tb, p, gb, n, gb).astype(BF16)
        for part, (ref, d) in enumerate(((cs_re_ref, 0), (cs_ni_ref, 0), (cs_re_ref, 1), (cs_ni_ref, 1))):
            cs8_scr[part * sw:(part + 1) * sw, :] = _s5_expand(ref[d, 0].astype(BF16), tk, n, gb, p, gb).astype(BF16)
        hi = lax.Precision.HIGHEST
        bd_f = _s5_expand(kc_ref[0, 0], tk_ref[...], p, gb, p, gb, hi)
        bd_r = _s5_expand(kc_ref[1, 0], tk_ref[...], p, gb, p, gb, hi)
        tile_f = lambda k: bd_f[:, k * cw:(k + 1) * cw]
        tile_r = lambda k: bd_r[:, (t - 1 - k) * cw:(t - k) * cw]
        for j in range(t):
            for i in range(t):
                blk = tile_f(i - j) if i > j else tile_r(j - i) if i < j else tile_f(0) + tile_r(0)
                m8_scr[j * cw:(j + 1) * cw, i * cw:(i + 1) * cw] = blk.astype(BF16)

    x = u_ref[...].reshape(rows, t * cw).astype(BF16)
    e = jnp.dot(x, be8_scr[...], preferred_element_type=F32)
    n_slab = e.shape[1] // LANES
    per_part = sw // LANES
    for k in range(n_slab):
        e_scr[k] = e[:, k * LANES:(k + 1) * LANES]

    def scan(nseq, nc):
        loops = [list(range(per_part))] if nseq <= 16 else [[q] for q in range(per_part)]
        for prs in loops:
            coef = [[jnp.broadcast_to(a_ref[r:r + 1, q * LANES:(q + 1) * LANES], (nseq, LANES)) for r in range(4)]
                    for q in prs]
            init = tuple(tuple(h0_ref[0, r, 0:nseq, q * LANES:(q + 1) * LANES] for r in range(4)) for q in prs)

            def step(c, carry):
                out = []
                for idx, q in enumerate(prs):
                    sf_re, sf_im, sr_re, sr_im = carry[idx]
                    af_re, af_im, ar_re, ar_im = coef[idx]
                    at_f = pl.ds(c, nseq, stride=nc)
                    at_r = pl.ds(nc - 1 - c, nseq, stride=nc)
                    ef_re, ef_im = e_scr[q, at_f, :], e_scr[per_part + q, at_f, :]
                    er_re, er_im = e_scr[2 * per_part + q, at_r, :], e_scr[3 * per_part + q, at_r, :]
                    e_scr[q, at_f, :] = sf_re
                    e_scr[per_part + q, at_f, :] = sf_im
                    e_scr[2 * per_part + q, at_r, :] = sr_re
                    e_scr[3 * per_part + q, at_r, :] = sr_im
                    out.append((af_re * sf_re - af_im * sf_im + ef_re, af_re * sf_im + af_im * sf_re + ef_im,
                                ar_re * sr_re - ar_im * sr_im + er_re, ar_re * sr_im + ar_im * sr_re + er_im))
                return tuple(out)

            fin = lax.fori_loop(0, nc, step, init)
            for idx, q in enumerate(prs):
                for r in range(4):
                    hf_ref[0, r, 0:nseq, q * LANES:(q + 1) * LANES] = fin[idx][r]

    hf_ref[...] = jnp.zeros_like(hf_ref)

    @pl.when(tile_i < n_tiles_p)
    def _():
        scan(*geoms[0])

    @pl.when(tile_i >= n_tiles_p)
    def _():
        scan(*geoms[1])

    s = jnp.concatenate([e_scr[k] for k in range(n_slab)], axis=-1).astype(BF16)
    y = (jnp.dot(x, m8_scr[...], preferred_element_type=F32)
         + jnp.dot(s, cs8_scr[...], preferred_element_type=F32))
    y_ref[...] = y.reshape(rows * t, cw)


def _s5_chunked(u, h0, prep, tile_k, tile_b, a4, *, tok, geoms, n_tiles_p, n, p):
    be_re, be_im, cs_re, cs_ni, kc = prep
    t, gb = S5_T, S5_GB
    cw, sw = gb * p, gb * n
    nb = u.shape[1] // cw
    n_tiles = u.shape[0] // tok
    ms = h0.shape[2]
    both = lambda a: pl.BlockSpec((2, 1) + a.shape[2:], lambda b, i: (0, b, 0, 0), pipeline_mode=pl.Buffered(1))
    const = lambda a: pl.BlockSpec(a.shape, lambda b, i: (0, 0), pipeline_mode=pl.Buffered(1))
    kern = functools.partial(_s5_chunk_kernel, geoms=geoms, n_tiles_p=n_tiles_p, n=n, p=p)
    return pl.pallas_call(
        kern,
        grid=(nb, n_tiles),
        in_specs=[pl.BlockSpec((tok, cw), lambda b, i: (i, b)),
                  both(be_re), both(be_im), both(cs_re), both(cs_ni), both(kc), const(tile_k), const(tile_b),
                  pl.BlockSpec((4, sw), lambda b, i: (0, b)),
                  pl.BlockSpec((1, 4, ms, sw), lambda b, i: (i, 0, 0, b))],
        out_specs=[pl.BlockSpec((tok, cw), lambda b, i: (i, b)),
                   pl.BlockSpec((1, 4, ms, sw), lambda b, i: (i, 0, 0, b))],
        out_shape=[jax.ShapeDtypeStruct(u.shape, F32), jax.ShapeDtypeStruct(h0.shape, F32)],
        scratch_shapes=[pltpu.VMEM((t * cw, t * cw), BF16), pltpu.VMEM((t * cw, 4 * sw), BF16),
                        pltpu.VMEM((4 * sw, t * cw), BF16), pltpu.VMEM((4 * sw // LANES, tok // t, LANES), F32)],
        compiler_params=_cparams("arbitrary", "arbitrary"),
        name="s5_chunked",
    )(u, be_re, be_im, cs_re, cs_ni, kc, tile_k, tile_b, a4, h0)


def _s5_tiles(n, p):
    t, gb = S5_T, S5_GB
    eye = lambda k: np.eye(k, dtype=np.float32)
    tile_k = np.einsum("ab,pq->apbq", eye(t), eye(p))[:, :, :, None, :] * np.ones((1, 1, 1, gb, 1), np.float32)
    tile_b = eye(n)[:, None, :] * np.ones((1, gb, 1), np.float32)
    return jnp.asarray(tile_k.reshape(t * p, t * gb * p)), jnp.asarray(tile_b.reshape(n, gb * n))


def _post_residual(x, y, gate, g, b, alpha):
    return _layer_norm(alpha * x + gate * y, g, b)


def _outab_kernel(xp_ref, xs_ref, mod_ref, yc_ref, ys_ref, u_ref, dsk_ref, wglu_ref, wout_ref, g_ref, b_ref, o_ref,
                  *, tiles_p, tiles_per_s, d, c, alpha):
    i = pl.program_id(0)
    grp = _group_of_tile(i, tiles_p, tiles_per_s)
    y_s = ys_ref[...] + dsk_ref[...] * u_ref[...]
    y_s = jax.nn.gelu(y_s)
    z = jnp.dot(y_s.astype(BF16), wglu_ref[...], preferred_element_type=F32)
    y_ssm = y_s * jax.nn.sigmoid(z)
    out = (jnp.dot(yc_ref[...], wout_ref[0:c, :], preferred_element_type=F32)
           + jnp.dot(y_ssm.astype(BF16), wout_ref[c:2 * c, :], preferred_element_type=F32))
    gate = _mod_chunk(mod_ref, grp, 2, d)
    o_ref[...] = _post_residual(_pair_rows(i, tiles_p, xp_ref, xs_ref), out, gate, g_ref[...], b_ref[...], alpha)


def _outab(xp, xs, mod_l, y_conv, y_scan, u, d_skip, w_glu, w_out, ln_g, ln_b, tm, tiles_p, tiles_per_s, alpha):
    t, d = xp.shape[0] + xs.shape[0], xp.shape[1]
    c = u.shape[1]
    kern = functools.partial(_outab_kernel, tiles_p=tiles_p, tiles_per_s=tiles_per_s, d=d, c=c, alpha=alpha)
    row = lambda w: pl.BlockSpec((tm, w), lambda i: (i, 0))
    full = lambda a: pl.BlockSpec(a.shape, lambda i: (0,) * a.ndim)
    d_skip, ln_g, ln_b = d_skip.reshape(1, c), ln_g.reshape(1, d), ln_b.reshape(1, d)
    return pl.pallas_call(
        kern,
        grid=(t // tm,),
        in_specs=_pair_specs(tm, d, tiles_p) + [full(mod_l), row(c), row(c), row(c), full(d_skip), full(w_glu),
                                                full(w_out), full(ln_g), full(ln_b)],
        out_specs=row(d),
        out_shape=jax.ShapeDtypeStruct((t, d), F32),
        compiler_params=_cparams("arbitrary"),
        name="outproj_ab",
    )(xp, xs, mod_l, y_conv, y_scan, u, d_skip, w_glu, w_out, ln_g, ln_b)


def _mlp_kernel(x_ref, mod_ref, w1_ref, w2_ref, g_ref, b_ref, *rest, tiles_p, tiles_per_s, d, n_f, alpha, split):
    outs, (h_scr, acc_scr) = rest[:-2], rest[-2:]
    i = pl.program_id(0)
    f = pl.program_id(1)
    grp = _group_of_tile(i, tiles_p, tiles_per_s)
    tm = h_scr.shape[0]
    halves = [slice(0, tm // 2), slice(tm // 2, tm)]

    def ffn(rows):
        a = jnp.dot(h_scr[rows, :], w1_ref[...], preferred_element_type=F32)
        a = jnp.square(jnp.maximum(a, 0.0)).astype(BF16)
        return jnp.dot(a, w2_ref[...], preferred_element_type=F32)

    @pl.when(f == 0)
    def _():
        shift = _mod_chunk(mod_ref, grp, 3, d)
        scale = _mod_chunk(mod_ref, grp, 4, d)
        for rows in halves:
            h_scr[rows, :] = (x_ref[rows, :] * (1 + scale) + shift).astype(BF16)
            acc_scr[rows, :] = ffn(rows)

    @pl.when(jnp.logical_and(f > 0, f < n_f - 1))
    def _():
        acc_scr[...] += ffn(slice(None))

    @pl.when(f == n_f - 1)
    def _():
        gate = _mod_chunk(mod_ref, grp, 5, d)
        for rows in halves:
            res = _post_residual(x_ref[rows, :], acc_scr[rows, :] + ffn(rows), gate, g_ref[...], b_ref[...], alpha)
            if split:
                @pl.when(i < tiles_p)
                def _():
                    outs[0][rows, :] = res

                @pl.when(i >= tiles_p)
                def _():
                    outs[1][rows, :] = res
            else:
                outs[0][rows, :] = res


def _mlp(x, mod_l, w1, w2, layer, ln_g, ln_b, tm, tf, tiles_p, tiles_per_s, alpha, split):
    t, d = x.shape
    n_f = w1.shape[2] // tf
    assert n_f >= 2
    kern = functools.partial(_mlp_kernel, tiles_p=tiles_p, tiles_per_s=tiles_per_s, d=d, n_f=n_f, alpha=alpha,
                             split=split)
    ln_g, ln_b = ln_g.reshape(1, d), ln_b.reshape(1, d)
    if split:
        out_specs = _pair_specs(tm, d, tiles_p, grid_rank=2)
        out_shape = [jax.ShapeDtypeStruct((tiles_p * tm, d), F32), jax.ShapeDtypeStruct((t - tiles_p * tm, d), F32)]
    else:
        out_specs = [pl.BlockSpec((tm, d), lambda i, f: (i, 0))]
        out_shape = [jax.ShapeDtypeStruct((t, d), F32)]
    return pl.pallas_call(
        kern,
        grid=(t // tm, n_f),
        in_specs=[pl.BlockSpec((tm, d), lambda i, f: (i, 0)),
                  pl.BlockSpec(mod_l.shape, lambda i, f: (0, 0)),
                  pl.BlockSpec((None, d, tf), lambda i, f: (layer, 0, f)),
                  pl.BlockSpec((None, tf, d), lambda i, f: (layer, f, 0)),
                  pl.BlockSpec((1, d), lambda i, f: (0, 0)),
                  pl.BlockSpec((1, d), lambda i, f: (0, 0))],
        out_specs=out_specs,
        out_shape=out_shape,
        scratch_shapes=[pltpu.VMEM((tm, d), BF16), pltpu.VMEM((tm, d), F32)],
        compiler_params=_cparams("arbitrary", "arbitrary"),
        name="mlp",
    )(x, mod_l, w1, w2, ln_g, ln_b)


def _rope_tables(n_pos, dk):
    ax = dk // 2
    half = ax // 2
    freqs = ROPE_BASE ** (-jnp.arange(half, dtype=F32) / half)
    pos = jnp.arange(n_pos)
    row = (pos // LATENT_GRID_W).astype(F32)
    col = (pos % LATENT_GRID_W).astype(F32)
    ang_r, ang_c = row[:, None] * freqs, col[:, None] * freqs
    cos = jnp.concatenate([jnp.cos(ang_r)] * 2 + [jnp.cos(ang_c)] * 2, axis=-1)
    sin = jnp.concatenate([-jnp.sin(ang_r), jnp.sin(ang_r), -jnp.sin(ang_c), jnp.sin(ang_c)], axis=-1)
    rep = LANES // dk
    return jnp.tile(cos, (1, rep)), jnp.tile(sin, (1, rep))


def _qkv_kernel(x_ref, mod_ref, w_ref, cos_ref, sin_ref, q_ref, k_ref, v_ref, kc_ref, vc_ref,
                *, tiles_p, tiles_per_s, d, dk, dv, q_scale):
    i = pl.program_id(0)
    g = _group_of_tile(i, tiles_p, tiles_per_s)
    shift = _mod_chunk(mod_ref, g, 0, d)
    scale = _mod_chunk(mod_ref, g, 1, d)
    h = (x_ref[...] * (1 + scale) + shift).astype(BF16)
    tn = q_ref.shape[-1]
    quarter = dk // 4
    latent = i >= tiles_p
    cos, sin = cos_ref[...], sin_ref[...]
    lane = lax.broadcasted_iota(jnp.int32, cos.shape, 1)
    first = (lane % (2 * quarter)) < quarter

    def rotated(y, o_ref, out_scale):
        for cb in range(tn // LANES):
            yb = y[:, cb * LANES:(cb + 1) * LANES]
            partner = jnp.where(first, pltpu.roll(yb, LANES - quarter, 1), pltpu.roll(yb, quarter, 1))
            out = jnp.where(latent, yb * cos + partner * sin, yb)
            o_ref[:, cb * LANES:(cb + 1) * LANES] = (out * out_scale).astype(o_ref.dtype)

    rotated(jnp.dot(h, w_ref[:, 0:tn], preferred_element_type=F32), q_ref, q_scale)
    y_k = jnp.dot(h, w_ref[:, tn:2 * tn], preferred_element_type=F32)
    rotated(y_k, k_ref, 1.0)
    y_v = jnp.dot(h, w_ref[:, 2 * tn:3 * tn], preferred_element_type=F32)
    v_ref[...] = y_v.astype(v_ref.dtype)

    @pl.when(i < tiles_p)
    def _():
        n_heads = kc_ref.shape[2]
        for m in range(2):
            for hh in range(n_heads):
                c0 = (m * n_heads + hh) * dk
                kc_ref[0, m, hh] = y_k[:, c0:c0 + dk]
        for hh in range(n_heads):
            vc_ref[0, hh] = y_v[:, hh * dv:(hh + 1) * dv]


def _qkv(x, mod_l, w_qkv, cos_t, sin_t, tm, tiles_p, tiles_per_s, dk, dv, bp, lp, q_scale):
    t, d = x.shape
    n_out = w_qkv.shape[1]
    assert n_out % 3 == 0 and lp % tm == 0
    tn = n_out // 3
    n_heads = tn // dv
    per_seq = lp // tm
    kern = functools.partial(_qkv_kernel, tiles_p=tiles_p, tiles_per_s=tiles_per_s, d=d, dk=dk, dv=dv,
                             q_scale=q_scale)
    pos_blk = lambda i: (jnp.maximum(i - tiles_p, 0) % tiles_per_s, 0)
    c_tile = lambda i: jnp.minimum(i, tiles_p - 1)
    row = pl.BlockSpec((tm, tn), lambda i: (i, 0))
    return pl.pallas_call(
        kern,
        grid=(t // tm,),
        in_specs=[pl.BlockSpec((tm, d), lambda i: (i, 0)),
                  pl.BlockSpec(mod_l.shape, lambda i: (0, 0)),
                  pl.BlockSpec((d, n_out), lambda i: (0, 0), pipeline_mode=pl.Buffered(1)),
                  pl.BlockSpec((tm, LANES), pos_blk),
                  pl.BlockSpec((tm, LANES), pos_blk)],
        out_specs=[row, row, row,
                   pl.BlockSpec((1, 2, n_heads, tm, dk), lambda i: (c_tile(i) // per_seq, 0, 0, c_tile(i) % per_seq, 0)),
                   pl.BlockSpec((1, n_heads, tm, dv), lambda i: (c_tile(i) // per_seq, 0, c_tile(i) % per_seq, 0))],
        out_shape=[jax.ShapeDtypeStruct((t, tn), BF16)] * 3
        + [jax.ShapeDtypeStruct((bp, 2, n_heads, lp, dk), F32), jax.ShapeDtypeStruct((bp, n_heads, lp, dv), F32)],
        compiler_params=_cparams("arbitrary"),
        name="qkv_proj",
    )(x, mod_l, w_qkv, cos_t, sin_t)


def _attn_kernel(*refs, heads_step, dk, dv, scale, fold_scale, lam_init, has_cache):
    if has_cache:
        q1_ref, q2_ref, k1_ref, k2_ref, v_ref, ck_ref, cv_ref, lamv_ref, sg_ref, o_ref, s_scr, w_scr = refs
    else:
        q1_ref, q2_ref, k1_ref, k2_ref, v_ref, lamv_ref, sg_ref, o_ref, s_scr, w_scr = refs
        ck_ref = cv_ref = None
    lv = lamv_ref[...]
    lam = (jnp.exp(jnp.sum(lv[0:1] * lv[1:2], axis=-1, keepdims=True))
           - jnp.exp(jnp.sum(lv[2:3] * lv[3:4], axis=-1, keepdims=True)) + lam_init)
    nt = (((1,), (1,)), ((), ()))
    tn = (((0,), (0,)), ((), ()))
    per_blk = LANES // dk
    tq, lk = q1_ref.shape[0], k1_ref.shape[0]
    lane = lax.broadcasted_iota(jnp.int32, (tq, LANES), 1)

    for blk in range(heads_step // per_blk):
        bs = slice(blk * LANES, (blk + 1) * LANES)
        for m, (q_ref, k_ref) in enumerate(((q1_ref, k1_ref), (q2_ref, k2_ref))):
            kb, qb = k_ref[:, bs], q_ref[:, bs]
            for sub in range(per_blk):
                head = blk * per_blk + sub
                qm = jnp.where((lane >= sub * dk) & (lane < (sub + 1) * dk), qb, jnp.zeros_like(qb))
                s_scr[2 * head + m, 0:lk, :] = lax.dot_general(kb, qm, nt, preferred_element_type=F32)
                if has_cache:
                    qc = qb[:, sub * dk:(sub + 1) * dk]
                    s_scr[2 * head + m, lk:, :] = lax.dot_general(ck_ref[0, 0, m, head].astype(BF16), qc, nt,
                                                                  preferred_element_type=F32)
    dens = []
    for hm in range(2 * heads_step):
        s = s_scr[hm]
        if not fold_scale:
            s = s * scale
        e = jnp.exp(s - jnp.max(s, axis=0, keepdims=True))
        dens.append(jnp.sum(e, axis=0, keepdims=True))
        s_scr[hm] = e
    for head in range(heads_step):
        ratio = lam * dens[2 * head] / dens[2 * head + 1]
        w_scr[head] = (s_scr[2 * head] - s_scr[2 * head + 1] * ratio).astype(BF16)
    for head in range(heads_step):
        vs = slice(head * dv, (head + 1) * dv)
        o_t = lax.dot_general(v_ref[:, vs], w_scr[head, 0:lk, :], tn, preferred_element_type=F32)
        if has_cache:
            o_t = o_t + lax.dot_general(cv_ref[0, 0, head].astype(BF16), w_scr[head, lk:, :], tn,
                                        preferred_element_type=F32)
        o_t = o_t * (1.0 / dens[2 * head])
        o_t = o_t * lax.rsqrt(jnp.mean(jnp.square(o_t), axis=0, keepdims=True) + LN_EPS)
        o = o_t.T * sg_ref[...] * (1.0 - lam_init)
        o_ref[:, vs] = o.astype(o_ref.dtype)


def _softmax_scale(dk):
    scale = dk ** -0.5
    return scale, math.frexp(scale)[0] == 0.5


def _attention(q, k, v, lamv, subln_g, *, row0, n_seq, lq, tq, hp, n_heads, dk, dv, lam_init, cache=None):
    per_blk = LANES // dk
    heads_step = hp * per_blk
    n_hblk = n_heads // heads_step
    map2 = n_heads * dk // (hp * LANES)
    scale, fold_scale = _softmax_scale(dk)
    qb0, kb0 = row0 // tq, row0 // lq
    n_q = lq // tq
    q_spec = lambda off: pl.BlockSpec((tq, hp * LANES), lambda b, h, qi: (qb0 + b * n_q + qi, off + h))
    k_spec = lambda off: pl.BlockSpec((lq, hp * LANES), lambda b, h, qi: (kb0 + b, off + h))
    in_specs = [q_spec(0), q_spec(map2), k_spec(0), k_spec(map2),
                pl.BlockSpec((lq, heads_step * dv), lambda b, h, qi: (kb0 + b, h))]
    args = [q, q, k, k, v]
    if cache is not None:
        cache_k, cache_v, o_i = cache
        past = cache_k.shape[-2]
        in_specs += [pl.BlockSpec((1, 1, 2, heads_step, past, dk), lambda b, h, qi: (b, o_i, 0, h, 0, 0)),
                     pl.BlockSpec((1, 1, heads_step, past, dv), lambda b, h, qi: (b, o_i, h, 0, 0))]
        args += [cache_k, cache_v]
    in_specs += [pl.BlockSpec(lamv.shape, lambda b, h, qi: (0, 0)),
                 pl.BlockSpec((1, dv), lambda b, h, qi: (0, 0))]
    args += [lamv, subln_g.reshape(1, dv)]
    kern = functools.partial(_attn_kernel, heads_step=heads_step, dk=dk, dv=dv, scale=scale,
                             fold_scale=fold_scale, lam_init=lam_init, has_cache=cache is not None)
    lk_all = lq + (cache[0].shape[-2] if cache is not None else 0)
    scratch = [pltpu.VMEM((2 * heads_step, lk_all, tq), F32), pltpu.VMEM((heads_step, lk_all, tq), BF16)]
    return pl.pallas_call(
        kern,
        grid=(n_seq, n_hblk, n_q),
        in_specs=in_specs,
        scratch_shapes=scratch,
        out_specs=pl.BlockSpec((tq, heads_step * dv), lambda b, h, qi: (b * n_q + qi, h)),
        out_shape=jax.ShapeDtypeStruct((n_seq * lq, n_heads * dv), BF16),
        compiler_params=_cparams("arbitrary", "arbitrary", "arbitrary"),
        name="diff_attn_cache" if cache is not None else "diff_attn",
    )(*args)


def _outc_kernel(x_ref, mod_ref, op_ref, os_ref, w_ref, g_ref, b_ref, o_ref, *, tiles_p, tiles_per_s, d, alpha):
    i = pl.program_id(0)
    grp = _group_of_tile(i, tiles_p, tiles_per_s)
    gate = _mod_chunk(mod_ref, grp, 2, d)
    tm = x_ref.shape[0]
    for rows in (slice(0, tm // 2), slice(tm // 2, tm)):
        o_in = jnp.where(i < tiles_p, op_ref[rows, :], os_ref[rows, :])
        out = jnp.dot(o_in, w_ref[...], preferred_element_type=F32)
        o_ref[rows, :] = _post_residual(x_ref[rows, :], out, gate, g_ref[...], b_ref[...], alpha)


def _outc(x, mod_l, o_p, o_s, w_out, ln_g, ln_b, tm, tiles_p, tiles_per_s, alpha):
    t, d = x.shape
    kin = o_p.shape[1]
    kern = functools.partial(_outc_kernel, tiles_p=tiles_p, tiles_per_s=tiles_per_s, d=d, alpha=alpha)
    ln_g, ln_b = ln_g.reshape(1, d), ln_b.reshape(1, d)
    full = lambda a: pl.BlockSpec(a.shape, lambda i: (0,) * a.ndim)
    return pl.pallas_call(
        kern,
        grid=(t // tm,),
        in_specs=[pl.BlockSpec((tm, d), lambda i: (i, 0)), full(mod_l)] + _pair_specs(tm, kin, tiles_p)
        + [full(w_out), full(ln_g), full(ln_b)],
        out_specs=pl.BlockSpec((tm, d), lambda i: (i, 0)),
        out_shape=jax.ShapeDtypeStruct((t, d), F32),
        compiler_params=_cparams("arbitrary"),
        name="outproj_c",
    )(x, mod_l, o_p, o_s, w_out, ln_g, ln_b)


class _Tiles(NamedTuple):
    rows: int
    rows_wide: int
    hidden: int
    s5_tokens: int
    conv_chunk: int
    attn_q: int
    attn_blocks: int


def _plan_tiles(tp, ts, lp, ls):
    tiles = _Tiles(rows=256, rows_wide=512, hidden=1024, s5_tokens=min(4096, tp, ts), conv_chunk=min(lp, 256),
                   attn_q=min(ls, 512), attn_blocks=2)
    for tm in (tiles.rows, tiles.rows_wide):
        assert tp % tm == 0 and ls % tm == 0
    tok = tiles.s5_tokens
    assert tok % lp == 0 and tok % ls == 0 and tp % tok == 0 and ts % tok == 0 and tp % ls == 0
    return tiles


def kernel(x_prompt, x_sample, state_s5_re, state_s5_im, cache_k, cache_v, c, c_ctx, w_mod, b_mod, ln_g, ln_b, w_in_ab, w_dw, b_dw, conv_ln_g, conv_ln_b, s5_lambda_re, s5_lambda_im, s5_log_dt, s5_b_re, s5_b_im, s5_c_re, s5_c_im, s5_d, w_glu, w_out_ab, w_qkv, lam_q1, lam_k1, lam_q2, lam_k2, subln_g, w_out_c, w_ff1, w_ff2):
    bp, lp, d = x_prompt.shape
    bs, ls, _ = x_sample.shape
    depth = w_mod.shape[0]
    tp, ts = bp * lp, bs * ls
    alpha = (2 * depth) ** 0.25
    assert 1 + bs <= MOD_ROWS

    tiles = _plan_tiles(tp, ts, lp, ls)
    tm, tm_mlp, tf = tiles.rows, tiles.rows_wide, tiles.hidden

    xp, xs = x_prompt.reshape(tp, d), x_sample.reshape(ts, d)
    cvec = jnp.zeros((MOD_ROWS, d), F32).at[0].set(c_ctx).at[1:1 + bs].set(c)
    mod = _modvec(cvec, w_mod, b_mod)

    g_ssm, n_ssm, p_ssm = s5_b_re.shape[2:]
    dk = lam_q1.shape[-1]
    dv = subln_g.shape[-1]
    n_heads = w_out_c.shape[1] // dv

    w_ff1_bf, w_ff2_bf = w_ff1.astype(BF16), w_ff2.astype(BF16)
    s_re, s_im, k_list, v_list = [], [], [], []
    for l in range(depth):
        mod_l = mod[l]
        if l % 2 == 0:
            e = l // 2
            if l > 0:
                xp, xs = x[:tp], x[tp:]
            ug, u = _inproj(xp, xs, mod_l, w_in_ab[e].astype(BF16), tm, tp // tm, ls // tm)
            y_conv = _conv_module(ug, w_dw[e], b_dw[e], conv_ln_g[e], conv_ln_b[e], tiles.conv_chunk, lp, ls, tp)
            prep = _s5_prep(s5_lambda_re[e], s5_lambda_im[e], s5_log_dt[e],
                            s5_b_re[e], s5_b_im[e], s5_c_re[e], s5_c_im[e])
            a_re, a_im = prep[5], prep[6]
            gn = g_ssm * n_ssm
            nat = lambda a: a.reshape(2, g_ssm, p_ssm, n_ssm)[:, :, 0].reshape(2, gn)
            a4 = jnp.stack([nat(a_re)[0], nat(a_im)[0], nat(a_re)[1], nat(a_im)[1]])
            tile_k, tile_b = _s5_tiles(n_ssm, p_ssm)
            tok = tiles.s5_tokens
            seq_p, seq_s, tiles_p5 = tok // lp, tok // ls, tp // tok
            ms = max(seq_p, seq_s)
            st = lambda a, dr: jnp.pad(a[:, e, dr].reshape(ts // tok, seq_s, gn), ((0, 0), (0, ms - seq_s), (0, 0)))
            h0_s = jnp.stack([st(state_s5_re, 0), st(state_s5_im, 0), st(state_s5_re, 1), st(state_s5_im, 1)], axis=1)
            h0 = jnp.concatenate([jnp.zeros((tiles_p5, 4, ms, gn), F32), h0_s], axis=0)
            y_scan, hf = _s5_chunked(u, h0, prep[:5], tile_k, tile_b, a4, tok=tok,
                                     geoms=((seq_p, lp // S5_T), (seq_s, ls // S5_T)), n_tiles_p=tiles_p5,
                                     n=n_ssm, p=p_ssm)
            hf = hf[:tiles_p5, :, :seq_p].transpose(1, 0, 2, 3).reshape(2, 2, bp, g_ssm, n_ssm)
            s_re.append(hf[:, 0].transpose(1, 0, 2, 3))
            s_im.append(hf[:, 1].transpose(1, 0, 2, 3))
            x = _outab(xp, xs, mod_l, y_conv, y_scan, u, s5_d[e], w_glu[e].astype(BF16),
                       w_out_ab[e].astype(BF16), ln_g[l, 0], ln_b[l, 0], tm, tp // tm, ls // tm, alpha)
        else:
            o_i = l // 2
            lam_init = 0.8 - 0.6 * math.exp(-0.3 * l)
            cos_t, sin_t = _rope_tables(ls, dk)
            scale, fold_scale = _softmax_scale(dk)
            q, k, v, k_new, v_new = _qkv(x, mod_l, w_qkv[o_i].astype(BF16), cos_t, sin_t, tm, tp // tm, ls // tm,
                                     dk, dv, bp, lp, scale if fold_scale else 1.0)
            lamv = jnp.stack([lam_q1[o_i], lam_k1[o_i], lam_q2[o_i], lam_k2[o_i]])
            geo = dict(n_heads=n_heads, dk=dk, dv=dv, lam_init=lam_init)
            o_p = _attention(q, k, v, lamv, subln_g[o_i], row0=0, n_seq=bp, lq=lp, tq=lp,
                             hp=n_heads * dk // LANES, **geo)
            o_s = _attention(q, k, v, lamv, subln_g[o_i], row0=tp, n_seq=bs, lq=ls,
                             tq=tiles.attn_q, hp=tiles.attn_blocks,
                             cache=(cache_k, cache_v, o_i), **geo)
            k_list.append(k_new)
            v_list.append(v_new)
            x = _outc(x, mod_l, o_p, o_s, w_out_c[o_i].astype(BF16), ln_g[l, 0], ln_b[l, 0],
                      tm_mlp, tp // tm_mlp, ls // tm_mlp, alpha)
        res = _mlp(x, mod_l, w_ff1_bf, w_ff2_bf, l, ln_g[l, 1], ln_b[l, 1],
                   tm_mlp, tf, tp // tm_mlp, ls // tm_mlp, alpha, split=l == depth - 1)
        x = res[0]

    return (res[0].reshape(bp, lp, d), res[1].reshape(bs, ls, d),
            jnp.stack(s_re, axis=1), jnp.stack(s_im, axis=1),
            jnp.stack(k_list, axis=1), jnp.stack(v_list, axis=1))
```

```python
import functools
import math
from typing import NamedTuple

import jax
import jax.numpy as jnp
import numpy as np
from jax import lax
from jax.experimental import pallas as pl
from jax.experimental.pallas import tpu as pltpu

F32 = jnp.float32
BF16 = jnp.bfloat16

LN_EPS = 1e-5
ROPE_BASE = 10000.0
LATENT_GRID_W = 64
MOD_ROWS = 8
V7X_VMEM_LIMIT = 56 * 1024 * 1024
LANES = 128
SUBLANES = 8


def _cparams(*sem):
    return pltpu.CompilerParams(dimension_semantics=sem, vmem_limit_bytes=V7X_VMEM_LIMIT)


def _layer_norm(z, g, b):
    mu = jnp.mean(z, axis=-1, keepdims=True)
    zc = z - mu
    var = jnp.mean(jnp.square(zc), axis=-1, keepdims=True)
    return zc * lax.rsqrt(var + LN_EPS) * g + b


def _group_of_tile(i, tiles_p, tiles_per_s):
    return jnp.where(i < tiles_p, 0, 1 + jnp.maximum(i - tiles_p, 0) // tiles_per_s)


def _mod_chunk(mod_ref, g, k, d):
    return mod_ref[pl.ds(g, 1), k * d:(k + 1) * d]


def _pair_specs(tm, width, tiles_p, grid_rank=1):
    if grid_rank == 1:
        return [pl.BlockSpec((tm, width), lambda i: (jnp.minimum(i, tiles_p - 1), 0)),
                pl.BlockSpec((tm, width), lambda i: (jnp.maximum(i - tiles_p, 0), 0))]
    return [pl.BlockSpec((tm, width), lambda i, f: (jnp.minimum(i, tiles_p - 1), 0)),
            pl.BlockSpec((tm, width), lambda i, f: (jnp.maximum(i - tiles_p, 0), 0))]


def _pair_rows(i, tiles_p, p_ref, s_ref):
    return jnp.where(i < tiles_p, p_ref[...], s_ref[...])


def _modvec_kernel(cv_ref, w_ref, b_ref, o_ref):
    cv = cv_ref[...]
    s = (cv * jax.nn.sigmoid(cv)).astype(BF16)
    o_ref[0] = jnp.dot(s, w_ref[0].astype(BF16), preferred_element_type=F32) + b_ref[0]


def _modvec(cvec, w_mod, b_mod, tn=1024):
    depth, d, n = w_mod.shape
    return pl.pallas_call(
        _modvec_kernel,
        grid=(depth, n // tn),
        in_specs=[pl.BlockSpec((MOD_ROWS, d), lambda l, j: (0, 0)),
                  pl.BlockSpec((1, d, tn), lambda l, j: (l, 0, j)),
                  pl.BlockSpec((1, 1, tn), lambda l, j: (l, 0, j))],
        out_specs=pl.BlockSpec((1, MOD_ROWS, tn), lambda l, j: (l, 0, j)),
        out_shape=jax.ShapeDtypeStruct((depth, MOD_ROWS, n), F32),
        compiler_params=_cparams("arbitrary", "arbitrary"),
        name="modvec",
    )(cvec, w_mod, b_mod.reshape(depth, 1, n))


def _inproj_kernel(xp_ref, xs_ref, mod_ref, w_ref, ug_ref, u_ref, *, tiles_p, tiles_per_s, d, c):
    i = pl.program_id(0)
    g = _group_of_tile(i, tiles_p, tiles_per_s)
    shift = _mod_chunk(mod_ref, g, 0, d)
    scale = _mod_chunk(mod_ref, g, 1, d)
    h = (_pair_rows(i, tiles_p, xp_ref, xs_ref) * (1 + scale) + shift).astype(BF16)
    a_val = jnp.dot(h, w_ref[:, 0:c], preferred_element_type=F32)
    a_gate = jnp.dot(h, w_ref[:, c:2 * c], preferred_element_type=F32)
    ug_ref[...] = a_val * jax.nn.sigmoid(a_gate)
    u_ref[...] = jnp.dot(h, w_ref[:, 2 * c:3 * c], preferred_element_type=F32)


def _inproj(xp, xs, mod_l, w_in, tm, tiles_p, tiles_per_s):
    t, d = xp.shape[0] + xs.shape[0], xp.shape[1]
    c = w_in.shape[1] // 3
    kern = functools.partial(_inproj_kernel, tiles_p=tiles_p, tiles_per_s=tiles_per_s, d=d, c=c)
    return pl.pallas_call(
        kern,
        grid=(t // tm,),
        in_specs=_pair_specs(tm, d, tiles_p) + [pl.BlockSpec(mod_l.shape, lambda i: (0, 0)),
                                                pl.BlockSpec(w_in.shape, lambda i: (0, 0))],
        out_specs=[pl.BlockSpec((tm, c), lambda i: (i, 0)),
                   pl.BlockSpec((tm, c), lambda i: (i, 0))],
        out_shape=[jax.ShapeDtypeStruct((t, c), F32), jax.ShapeDtypeStruct((t, c), F32)],
        compiler_params=_cparams("arbitrary"),
        name="inproj",
    )(xp, xs, mod_l, w_in)


CONV_HALO = 16
CONV_ROWS = 64
CONV_COLS = 128


def _conv_kernel(prev_ref, cur_ref, next_ref, w_ref, b_ref, g_ref, beta_ref, o_ref, pad_scr, sh_scr, conv_scr,
                 *, chunks_p, chunks_s, n_chunks_p, width, lc, c):
    i = pl.program_id(0)
    in_p = i < n_chunks_p
    k = jnp.where(in_p, i % chunks_p, jnp.maximum(i - n_chunks_p, 0) % chunks_s)
    last = jnp.where(in_p, chunks_p - 1, chunks_s - 1)
    has_prev = (k > 0).astype(F32)
    has_next = (k < last).astype(F32)
    pad_scr[0:CONV_HALO, :] = prev_ref[...] * has_prev
    pad_scr[CONV_HALO:CONV_HALO + lc, :] = cur_ref[...]
    pad_scr[CONV_HALO + lc:2 * CONV_HALO + lc, :] = next_ref[...] * has_next
    off = CONV_HALO - width // 2
    span = lc + CONV_HALO + SUBLANES
    for sft in range(SUBLANES):
        sh_scr[sft] = pad_scr[sft:sft + span, :]

    for r0 in range(0, lc, CONV_ROWS):
        for cb in range(c // CONV_COLS):
            cs = slice(cb * CONV_COLS, (cb + 1) * CONV_COLS)
            acc = jnp.zeros((CONV_ROWS, CONV_COLS), F32)
            for kk in range(width):
                whole, sft = divmod(kk + off, SUBLANES)
                base = r0 + whole * SUBLANES
                acc = acc + sh_scr[sft, base:base + CONV_ROWS, cs] * w_ref[kk:kk + 1, cs]
            conv_scr[r0:r0 + CONV_ROWS, cs] = acc + b_ref[:, cs]
    y = _layer_norm(conv_scr[...], g_ref[...], beta_ref[...])
    o_ref[...] = (y * jax.nn.sigmoid(y)).astype(o_ref.dtype)


def _conv_module(ug, w_dw, b_dw, ln_g, ln_b, lc, lp, ls, tp):
    t, c = ug.shape
    width = w_dw.shape[0]
    assert width // 2 < CONV_HALO and lc % CONV_HALO == 0 and lp % lc == 0 and ls % lc == 0
    hb = lc // CONV_HALO
    n_halo_blocks = t // CONV_HALO
    kern = functools.partial(_conv_kernel, chunks_p=lp // lc, chunks_s=ls // lc, n_chunks_p=tp // lc,
                             width=width, lc=lc, c=c)
    vec = lambda a: a.reshape(1, c)
    return pl.pallas_call(
        kern,
        grid=(t // lc,),
        in_specs=[pl.BlockSpec((CONV_HALO, c), lambda i: (jnp.maximum(i * hb - 1, 0), 0)),
                  pl.BlockSpec((lc, c), lambda i: (i, 0)),
                  pl.BlockSpec((CONV_HALO, c), lambda i: (jnp.minimum((i + 1) * hb, n_halo_blocks - 1), 0)),
                  pl.BlockSpec((width, c), lambda i: (0, 0)),
                  pl.BlockSpec((1, c), lambda i: (0, 0)),
                  pl.BlockSpec((1, c), lambda i: (0, 0)),
                  pl.BlockSpec((1, c), lambda i: (0, 0))],
        out_specs=pl.BlockSpec((lc, c), lambda i: (i, 0)),
        out_shape=jax.ShapeDtypeStruct((t, c), BF16),
        scratch_shapes=[pltpu.VMEM((lc + 2 * CONV_HALO, c), F32),
                        pltpu.VMEM((SUBLANES, lc + CONV_HALO + SUBLANES, c), F32), pltpu.VMEM((lc, c), F32)],
        compiler_params=_cparams("arbitrary"),
        name="conv_module",
    )(ug, ug, ug, w_dw, vec(b_dw), vec(ln_g), vec(ln_b))


S5_T = 16
S5_GB = 8


def _s5_prep_kernel(bt_re_ref, bt_im_ref, la_re_ref, la_im_ref, dta_ref, ct_re_ref, ct_im_ref, rep_ref, lb_re_ref,
                    lb_im_ref, dtb_ref, be_re_ref, be_im_ref, cs_re_ref, cs_ni_ref, kc_ref, a_re_ref, a_im_ref, *, n, p):
    t = S5_T
    fwd = pl.program_id(0) == 0

    lam_re, lam_im = la_re_ref[0, 0], la_im_ref[0, 0]
    dt = jnp.exp(dta_ref[0, 0])
    mag = jnp.exp(lam_re * dt)
    ar, ai = mag * jnp.cos(lam_im * dt), mag * jnp.sin(lam_im * dt)
    den = jnp.square(lam_re) + jnp.square(lam_im)
    coef_re = ((ar - 1) * lam_re + ai * lam_im) / den
    coef_im = (ai * lam_re - (ar - 1) * lam_im) / den
    b_re, b_im = bt_re_ref[0, 0], bt_im_ref[0, 0]
    bb_re = coef_re * b_re - coef_im * b_im
    bb_im = coef_re * b_im + coef_im * b_re
    rows = bb_re.shape[0]
    pw = []
    wr, wi = bb_re, bb_im
    pr, pi = jnp.ones_like(ar), jnp.zeros_like(ar)
    for k in range(t):
        pw.append((wr, wi))
        wr, wi = wr * ar - wi * ai, wr * ai + wi * ar
        pr, pi = pr * ar - pi * ai, pr * ai + pi * ar
    a_re_ref[0, 0] = pr
    a_im_ref[0, 0] = pi
    for j in range(t):
        be_re_ref[0, 0, j * rows:(j + 1) * rows, :] = jnp.where(fwd, pw[t - 1 - j][0], pw[j][0])
        be_im_ref[0, 0, j * rows:(j + 1) * rows, :] = jnp.where(fwd, pw[t - 1 - j][1], pw[j][1])

    hi = lax.Precision.HIGHEST
    ct_re = jnp.dot(ct_re_ref[0, 0], rep_ref[...], precision=hi, preferred_element_type=F32)
    ct_im = jnp.dot(ct_im_ref[0, 0], rep_ref[...], precision=hi, preferred_element_type=F32)
    shp = ct_re.shape
    lam_re, lam_im = lb_re_ref[0, 0], lb_im_ref[0, 0]
    dt = jnp.exp(dtb_ref[0, 0])
    mag = jnp.exp(lam_re * dt)
    sq_re = jnp.broadcast_to(mag * jnp.cos(lam_im * dt), shp)
    sq_im = jnp.broadcast_to(mag * jnp.sin(lam_im * dt), shp)
    blk = lax.broadcasted_iota(jnp.int32, shp, 1) // p
    k1 = jnp.where(fwd, blk + 1, t - blk)
    qr, qi = jnp.ones(shp, F32), jnp.zeros(shp, F32)
    n_bits = t.bit_length()
    for bit in range(n_bits):
        take = ((k1 >> bit) & 1) == 1
        qr, qi = (jnp.where(take, qr * sq_re - qi * sq_im, qr), jnp.where(take, qr * sq_im + qi * sq_re, qi))
        if bit + 1 < n_bits:
            sq_re, sq_im = sq_re * sq_re - sq_im * sq_im, 2.0 * (sq_re * sq_im)
    v_re = ct_re * qr - ct_im * qi
    v_im = ct_re * qi + ct_im * qr
    cs_re_ref[0, 0] = v_re
    cs_ni_ref[0, 0] = -v_im
    lane = lax.broadcasted_iota(jnp.int32, shp, 1)
    w = shp[1]
    v0_re = jnp.where(fwd, jnp.where(lane < p, ct_re, pltpu.roll(v_re, p, 1)),
                      jnp.where(lane >= w - p, ct_re, pltpu.roll(v_re, w - p, 1)))
    v0_im = jnp.where(fwd, jnp.where(lane < p, ct_im, pltpu.roll(v_im, p, 1)),
                      jnp.where(lane >= w - p, ct_im, pltpu.roll(v_im, w - p, 1)))
    for g in range(S5_GB):
        ra, rb = slice(g * p, (g + 1) * p), slice(g * n, (g + 1) * n)
        kc_ref[0, 0, ra, :] = (jnp.dot(bb_re[ra], v0_re[rb], precision=hi, preferred_element_type=F32)
                               - jnp.dot(bb_im[ra], v0_im[rb], precision=hi, preferred_element_type=F32))


def _s5_prep(lam_re, lam_im, log_dt, b_re, b_im, c_re, c_im):
    _, g, n, p = b_re.shape
    t = S5_T
    nb = g // S5_GB
    ra, rb = S5_GB * p, S5_GB * n
    lay_a = lambda a: jnp.broadcast_to(a[:, :, None, :], (2, g, p, n)).reshape(2, nb, ra, n)
    lay_b = lambda a: a.reshape(2, nb, rb, 1)
    bt = lambda a: a.transpose(0, 1, 3, 2).reshape(2, nb, ra, n)
    ct = lambda a: a.transpose(0, 1, 3, 2).reshape(2, nb, rb, p)
    rep_i = jnp.asarray(np.tile(np.eye(p, dtype=np.float32), (1, t)))
    dt_g = jnp.broadcast_to(log_dt[:, :, None], (2, g, n))
    blk = lambda r, c: pl.BlockSpec((1, 1, r, c), lambda d, i: (d, i, 0, 0))
    shp = lambda r, c: jax.ShapeDtypeStruct((2, nb, r, c), F32)
    kern = functools.partial(_s5_prep_kernel, n=n, p=p)
    return pl.pallas_call(
        kern,
        grid=(2, nb),
        in_specs=[blk(ra, n)] * 5 + [blk(rb, p)] * 2 + [pl.BlockSpec((p, t * p), lambda d, i: (0, 0))] + [blk(rb, 1)] * 3,
        out_specs=[blk(t * ra, n), blk(t * ra, n), blk(rb, t * p), blk(rb, t * p), blk(ra, t * p), blk(ra, n), blk(ra, n)],
        out_shape=[shp(t * ra, n), shp(t * ra, n), shp(rb, t * p), shp(rb, t * p), shp(ra, t * p), shp(ra, n), shp(ra, n)],
        compiler_params=_cparams("arbitrary", "arbitrary"),
        name="s5_prep",
    )(bt(b_re), bt(b_im), lay_a(lam_re), lay_a(lam_im), lay_a(dt_g), ct(c_re), ct(c_im), rep_i,
      lay_b(lam_re), lay_b(lam_im), lay_b(dt_g))


def _s5_expand(src, tile, row_div, row_mod, lane_div, lane_mod, precision=None):
    full = jnp.dot(src, tile, precision=precision, preferred_element_type=F32)
    r = lax.broadcasted_iota(jnp.int32, full.shape, 0) // row_div % row_mod
    l = lax.broadcasted_iota(jnp.int32, full.shape, 1) // lane_div % lane_mod
    return jnp.where(r == l, full, 0.0)


def _s5_chunk_kernel(u_ref, be_re_ref, be_im_ref, cs_re_ref, cs_ni_ref, kc_ref, tk_ref, tb_ref, a_ref, h0_ref,
                     y_ref, hf_ref, m8_scr, be8_scr, cs8_scr, e_scr, *, geoms, n_tiles_p, n, p):
    t, gb = S5_T, S5_GB
    cw = gb * p
    sw = gb * n
    rows = geoms[0][0] * geoms[0][1]
    tile_i = pl.program_id(1)

    @pl.when(tile_i == 0)
    def _():
        tk, tb = tk_ref[...].astype(BF16), tb_ref[...].astype(BF16)
        step = 4 * cw
        for r0 in range(0, t * cw, step):
            for part, (ref, d) in enumerate(((be_re_ref, 0), (be_im_ref, 0), (be_re_ref, 1), (be_im_ref, 1))):
                be8_scr[r0:r0 + step, part * sw:(part + 1) * sw] = _s5_expand(
                    ref[d, 0, r0:r0 + step, :].astype(BF16), tb, p, gb, n, gb).astype(BF16)
        for part, (ref, d) in enumerate(((cs_re_ref, 0), (cs_ni_ref, 0), (cs_re_ref, 1), (cs_ni_ref, 1))):
            cs8_scr[part * sw:(part + 1) * sw, :] = _s5_expand(ref[d, 0].astype(BF16), tk, n, gb, p, gb).astype(BF16)
        hi = lax.Precision.HIGHEST
        bd_f = _s5_expand(kc_ref[0, 0], tk_ref[...], p, gb, p, gb, hi)
        bd_r = _s5_expand(kc_ref[1, 0], tk_ref[...], p, gb, p, gb, hi)
        tile_f = lambda k: bd_f[:, k * cw:(k + 1) * cw]
        tile_r = lambda k: bd_r[:, (t - 1 - k) * cw:(t - k) * cw]
        for j in range(t):
            for i in range(t):
                blk = tile_f(i - j) if i > j else tile_r(j - i) if i < j else tile_f(0) + tile_r(0)
                m8_scr[j * cw:(j + 1) * cw, i * cw:(i + 1) * cw] = blk.astype(BF16)

    x = u_ref[...].reshape(rows, t * cw).astype(BF16)
    e = jnp.dot(x, be8_scr[...], preferred_element_type=F32)
    n_slab = e.shape[1] // LANES
    per_part = sw // LANES
    for k in range(n_slab):
        e_scr[k] = e[:, k * LANES:(k + 1) * LANES]

    def scan(nseq, nc):
        loops = [list(range(per_part))] if nseq <= 16 else [[q] for q in range(per_part)]
        for prs in loops:
            coef = [[jnp.broadcast_to(a_ref[r:r + 1, q * LANES:(q + 1) * LANES], (nseq, LANES)) for r in range(4)]
                    for q in prs]
            init = tuple(tuple(h0_ref[0, r, 0:nseq, q * LANES:(q + 1) * LANES] for r in range(4)) for q in prs)

            def step(c, carry):
                out = []
                for idx, q in enumerate(prs):
                    sf_re, sf_im, sr_re, sr_im = carry[idx]
                    af_re, af_im, ar_re, ar_im = coef[idx]
                    at_f = pl.ds(c, nseq, stride=nc)
                    at_r = pl.ds(nc - 1 - c, nseq, stride=nc)
                    ef_re, ef_im = e_scr[q, at_f, :], e_scr[per_part + q, at_f, :]
                    er_re, er_im = e_scr[2 * per_part + q, at_r, :], e_scr[3 * per_part + q, at_r, :]
                    e_scr[q, at_f, :] = sf_re
                    e_scr[per_part + q, at_f, :] = sf_im
                    e_scr[2 * per_part + q, at_r, :] = sr_re
                    e_scr[3 * per_part + q, at_r, :] = sr_im
                    out.append((af_re * sf_re - af_im * sf_im + ef_re, af_re * sf_im + af_im * sf_re + ef_im,
                                ar_re * sr_re - ar_im * sr_im + er_re, ar_re * sr_im + ar_im * sr_re + er_im))
                return tuple(out)

            fin = lax.fori_loop(0, nc, step, init)
            for idx, q in enumerate(prs):
                for r in range(4):
                    hf_ref[0, r, 0:nseq, q * LANES:(q + 1) * LANES] = fin[idx][r]

    hf_ref[...] = jnp.zeros_like(hf_ref)

    @pl.when(tile_i < n_tiles_p)
    def _():
        scan(*geoms[0])

    @pl.when(tile_i >= n_tiles_p)
    def _():
        scan(*geoms[1])

    s = jnp.concatenate([e_scr[k] for k in range(n_slab)], axis=-1).astype(BF16)
    y = (jnp.dot(x, m8_scr[...], preferred_element_type=F32)
         + jnp.dot(s, cs8_scr[...], preferred_element_type=F32))
    y_ref[...] = y.reshape(rows * t, cw)


def _s5_chunked(u, h0, prep, tile_k, tile_b, a4, *, tok, geoms, n_tiles_p, n, p):
    be_re, be_im, cs_re, cs_ni, kc = prep
    t, gb = S5_T, S5_GB
    cw, sw = gb * p, gb * n
    nb = u.shape[1] // cw
    n_tiles = u.shape[0] // tok
    ms = h0.shape[2]
    both = lambda a: pl.BlockSpec((2, 1) + a.shape[2:], lambda b, i: (0, b, 0, 0), pipeline_mode=pl.Buffered(1))
    const = lambda a: pl.BlockSpec(a.shape, lambda b, i: (0, 0), pipeline_mode=pl.Buffered(1))
    kern = functools.partial(_s5_chunk_kernel, geoms=geoms, n_tiles_p=n_tiles_p, n=n, p=p)
    return pl.pallas_call(
        kern,
        grid=(nb, n_tiles),
        in_specs=[pl.BlockSpec((tok, cw), lambda b, i: (i, b)),
                  both(be_re), both(be_im), both(cs_re), both(cs_ni), both(kc), const(tile_k), const(tile_b),
                  pl.BlockSpec((4, sw), lambda b, i: (0, b)),
                  pl.BlockSpec((1, 4, ms, sw), lambda b, i: (i, 0, 0, b))],
        out_specs=[pl.BlockSpec((tok, cw), lambda b, i: (i, b)),
                   pl.BlockSpec((1, 4, ms, sw), lambda b, i: (i, 0, 0, b))],
        out_shape=[jax.ShapeDtypeStruct(u.shape, F32), jax.ShapeDtypeStruct(h0.shape, F32)],
        scratch_shapes=[pltpu.VMEM((t * cw, t * cw), BF16), pltpu.VMEM((t * cw, 4 * sw), BF16),
                        pltpu.VMEM((4 * sw, t * cw), BF16), pltpu.VMEM((4 * sw // LANES, tok // t, LANES), F32)],
        compiler_params=_cparams("arbitrary", "arbitrary"),
        name="s5_chunked",
    )(u, be_re, be_im, cs_re, cs_ni, kc, tile_k, tile_b, a4, h0)


def _s5_tiles(n, p):
    t, gb = S5_T, S5_GB
    eye = lambda k: np.eye(k, dtype=np.float32)
    tile_k = np.einsum("ab,pq->apbq", eye(t), eye(p))[:, :, :, None, :] * np.ones((1, 1, 1, gb, 1), np.float32)
    tile_b = eye(n)[:, None, :] * np.ones((1, gb, 1), np.float32)
    return jnp.asarray(tile_k.reshape(t * p, t * gb * p)), jnp.asarray(tile_b.reshape(n, gb * n))


def _post_residual(x, y, gate, g, b, alpha):
    return _layer_norm(alpha * x + gate * y, g, b)


def _outab_kernel(xp_ref, xs_ref, mod_ref, yc_ref, ys_ref, u_ref, dsk_ref, wglu_ref, wout_ref, g_ref, b_ref, o_ref,
                  *, tiles_p, tiles_per_s, d, c, alpha):
    i = pl.program_id(0)
    grp = _group_of_tile(i, tiles_p, tiles_per_s)
    y_s = ys_ref[...] + dsk_ref[...] * u_ref[...]
    y_s = jax.nn.gelu(y_s)
    z = jnp.dot(y_s.astype(BF16), wglu_ref[...], preferred_element_type=F32)
    y_ssm = y_s * jax.nn.sigmoid(z)
    out = (jnp.dot(yc_ref[...], wout_ref[0:c, :], preferred_element_type=F32)
           + jnp.dot(y_ssm.astype(BF16), wout_ref[c:2 * c, :], preferred_element_type=F32))
    gate = _mod_chunk(mod_ref, grp, 2, d)
    o_ref[...] = _post_residual(_pair_rows(i, tiles_p, xp_ref, xs_ref), out, gate, g_ref[...], b_ref[...], alpha)


def _outab(xp, xs, mod_l, y_conv, y_scan, u, d_skip, w_glu, w_out, ln_g, ln_b, tm, tiles_p, tiles_per_s, alpha):
    t, d = xp.shape[0] + xs.shape[0], xp.shape[1]
    c = u.shape[1]
    kern = functools.partial(_outab_kernel, tiles_p=tiles_p, tiles_per_s=tiles_per_s, d=d, c=c, alpha=alpha)
    row = lambda w: pl.BlockSpec((tm, w), lambda i: (i, 0))
    full = lambda a: pl.BlockSpec(a.shape, lambda i: (0,) * a.ndim)
    d_skip, ln_g, ln_b = d_skip.reshape(1, c), ln_g.reshape(1, d), ln_b.reshape(1, d)
    return pl.pallas_call(
        kern,
        grid=(t // tm,),
        in_specs=_pair_specs(tm, d, tiles_p) + [full(mod_l), row(c), row(c), row(c), full(d_skip), full(w_glu),
                                                full(w_out), full(ln_g), full(ln_b)],
        out_specs=row(d),
        out_shape=jax.ShapeDtypeStruct((t, d), F32),
        compiler_params=_cparams("arbitrary"),
        name="outproj_ab",
    )(xp, xs, mod_l, y_conv, y_scan, u, d_skip, w_glu, w_out, ln_g, ln_b)


def _mlp_kernel(x_ref, mod_ref, w1_ref, w2_ref, g_ref, b_ref, *rest, tiles_p, tiles_per_s, d, n_f, alpha, split):
    outs, (h_scr, acc_scr) = rest[:-2], rest[-2:]
    i = pl.program_id(0)
    f = pl.program_id(1)
    grp = _group_of_tile(i, tiles_p, tiles_per_s)
    tm = h_scr.shape[0]
    halves = [slice(0, tm // 2), slice(tm // 2, tm)]

    def ffn(rows):
        a = jnp.dot(h_scr[rows, :], w1_ref[...], preferred_element_type=F32)
        a = jnp.square(jnp.maximum(a, 0.0)).astype(BF16)
        return jnp.dot(a, w2_ref[...], preferred_element_type=F32)

    @pl.when(f == 0)
    def _():
        shift = _mod_chunk(mod_ref, grp, 3, d)
        scale = _mod_chunk(mod_ref, grp, 4, d)
        for rows in halves:
            h_scr[rows, :] = (x_ref[rows, :] * (1 + scale) + shift).astype(BF16)
            acc_scr[rows, :] = ffn(rows)

    @pl.when(jnp.logical_and(f > 0, f < n_f - 1))
    def _():
        acc_scr[...] += ffn(slice(None))

    @pl.when(f == n_f - 1)
    def _():
        gate = _mod_chunk(mod_ref, grp, 5, d)
        for rows in halves:
            res = _post_residual(x_ref[rows, :], acc_scr[rows, :] + ffn(rows), gate, g_ref[...], b_ref[...], alpha)
            if split:
                @pl.when(i < tiles_p)
                def _():
                    outs[0][rows, :] = res

                @pl.when(i >= tiles_p)
                def _():
                    outs[1][rows, :] = res
            else:
                outs[0][rows, :] = res


def _mlp(x, mod_l, w1, w2, layer, ln_g, ln_b, tm, tf, tiles_p, tiles_per_s, alpha, split):
    t, d = x.shape
    n_f = w1.shape[2] // tf
    assert n_f >= 2
    kern = functools.partial(_mlp_kernel, tiles_p=tiles_p, tiles_per_s=tiles_per_s, d=d, n_f=n_f, alpha=alpha,
                             split=split)
    ln_g, ln_b = ln_g.reshape(1, d), ln_b.reshape(1, d)
    if split:
        out_specs = _pair_specs(tm, d, tiles_p, grid_rank=2)
        out_shape = [jax.ShapeDtypeStruct((tiles_p * tm, d), F32), jax.ShapeDtypeStruct((t - tiles_p * tm, d), F32)]
    else:
        out_specs = [pl.BlockSpec((tm, d), lambda i, f: (i, 0))]
        out_shape = [jax.ShapeDtypeStruct((t, d), F32)]
    return pl.pallas_call(
        kern,
        grid=(t // tm, n_f),
        in_specs=[pl.BlockSpec((tm, d), lambda i, f: (i, 0)),
                  pl.BlockSpec(mod_l.shape, lambda i, f: (0, 0)),
                  pl.BlockSpec((None, d, tf), lambda i, f: (layer, 0, f)),
                  pl.BlockSpec((None, tf, d), lambda i, f: (layer, f, 0)),
                  pl.BlockSpec((1, d), lambda i, f: (0, 0)),
                  pl.BlockSpec((1, d), lambda i, f: (0, 0))],
        out_specs=out_specs,
        out_shape=out_shape,
        scratch_shapes=[pltpu.VMEM((tm, d), BF16), pltpu.VMEM((tm, d), F32)],
        compiler_params=_cparams("arbitrary", "arbitrary"),
        name="mlp",
    )(x, mod_l, w1, w2, ln_g, ln_b)


def _rope_tables(n_pos, dk):
    ax = dk // 2
    half = ax // 2
    freqs = ROPE_BASE ** (-jnp.arange(half, dtype=F32) / half)
    pos = jnp.arange(n_pos)
    row = (pos // LATENT_GRID_W).astype(F32)
    col = (pos % LATENT_GRID_W).astype(F32)
    ang_r, ang_c = row[:, None] * freqs, col[:, None] * freqs
    cos = jnp.concatenate([jnp.cos(ang_r)] * 2 + [jnp.cos(ang_c)] * 2, axis=-1)
    sin = jnp.concatenate([-jnp.sin(ang_r), jnp.sin(ang_r), -jnp.sin(ang_c), jnp.sin(ang_c)], axis=-1)
    rep = LANES // dk
    return jnp.tile(cos, (1, rep)), jnp.tile(sin, (1, rep))


def _qkv_kernel(x_ref, mod_ref, w_ref, cos_ref, sin_ref, q_ref, k_ref, v_ref, kc_ref, vc_ref,
                *, tiles_p, tiles_per_s, d, dk, dv, q_scale):
    i = pl.program_id(0)
    g = _group_of_tile(i, tiles_p, tiles_per_s)
    shift = _mod_chunk(mod_ref, g, 0, d)
    scale = _mod_chunk(mod_ref, g, 1, d)
    h = (x_ref[...] * (1 + scale) + shift).astype(BF16)
    tn = q_ref.shape[-1]
    quarter = dk // 4
    latent = i >= tiles_p
    cos, sin = cos_ref[...], sin_ref[...]
    lane = lax.broadcasted_iota(jnp.int32, cos.shape, 1)
    first = (lane % (2 * quarter)) < quarter

    def rotated(y, o_ref, out_scale):
        for cb in range(tn // LANES):
            yb = y[:, cb * LANES:(cb + 1) * LANES]
            partner = jnp.where(first, pltpu.roll(yb, LANES - quarter, 1), pltpu.roll(yb, quarter, 1))
            out = jnp.where(latent, yb * cos + partner * sin, yb)
            o_ref[:, cb * LANES:(cb + 1) * LANES] = (out * out_scale).astype(o_ref.dtype)

    rotated(jnp.dot(h, w_ref[:, 0:tn], preferred_element_type=F32), q_ref, q_scale)
    y_k = jnp.dot(h, w_ref[:, tn:2 * tn], preferred_element_type=F32)
    rotated(y_k, k_ref, 1.0)
    y_v = jnp.dot(h, w_ref[:, 2 * tn:3 * tn], preferred_element_type=F32)
    v_ref[...] = y_v.astype(v_ref.dtype)

    @pl.when(i < tiles_p)
    def _():
        n_heads = kc_ref.shape[2]
        for m in range(2):
            for hh in range(n_heads):
                c0 = (m * n_heads + hh) * dk
                kc_ref[0, m, hh] = y_k[:, c0:c0 + dk]
        for hh in range(n_heads):
            vc_ref[0, hh] = y_v[:, hh * dv:(hh + 1) * dv]


def _qkv(x, mod_l, w_qkv, cos_t, sin_t, tm, tiles_p, tiles_per_s, dk, dv, bp, lp, q_scale):
    t, d = x.shape
    n_out = w_qkv.shape[1]
    assert n_out % 3 == 0 and lp % tm == 0
    tn = n_out // 3
    n_heads = tn // dv
    per_seq = lp // tm
    kern = functools.partial(_qkv_kernel, tiles_p=tiles_p, tiles_per_s=tiles_per_s, d=d, dk=dk, dv=dv,
                             q_scale=q_scale)
    pos_blk = lambda i: (jnp.maximum(i - tiles_p, 0) % tiles_per_s, 0)
    c_tile = lambda i: jnp.minimum(i, tiles_p - 1)
    row = pl.BlockSpec((tm, tn), lambda i: (i, 0))
    return pl.pallas_call(
        kern,
        grid=(t // tm,),
        in_specs=[pl.BlockSpec((tm, d), lambda i: (i, 0)),
                  pl.BlockSpec(mod_l.shape, lambda i: (0, 0)),
                  pl.BlockSpec((d, n_out), lambda i: (0, 0), pipeline_mode=pl.Buffered(1)),
                  pl.BlockSpec((tm, LANES), pos_blk),
                  pl.BlockSpec((tm, LANES), pos_blk)],
        out_specs=[row, row, row,
                   pl.BlockSpec((1, 2, n_heads, tm, dk), lambda i: (c_tile(i) // per_seq, 0, 0, c_tile(i) % per_seq, 0)),
                   pl.BlockSpec((1, n_heads, tm, dv), lambda i: (c_tile(i) // per_seq, 0, c_tile(i) % per_seq, 0))],
        out_shape=[jax.ShapeDtypeStruct((t, tn), BF16)] * 3
        + [jax.ShapeDtypeStruct((bp, 2, n_heads, lp, dk), F32), jax.ShapeDtypeStruct((bp, n_heads, lp, dv), F32)],
        compiler_params=_cparams("arbitrary"),
        name="qkv_proj",
    )(x, mod_l, w_qkv, cos_t, sin_t)


def _attn_kernel(*refs, heads_step, dk, dv, scale, fold_scale, lam_init, has_cache):
    if has_cache:
        q1_ref, q2_ref, k1_ref, k2_ref, v_ref, ck_ref, cv_ref, lamv_ref, sg_ref, o_ref, s_scr, w_scr = refs
    else:
        q1_ref, q2_ref, k1_ref, k2_ref, v_ref, lamv_ref, sg_ref, o_ref, s_scr, w_scr = refs
        ck_ref = cv_ref = None
    lv = lamv_ref[...]
    lam = (jnp.exp(jnp.sum(lv[0:1] * lv[1:2], axis=-1, keepdims=True))
           - jnp.exp(jnp.sum(lv[2:3] * lv[3:4], axis=-1, keepdims=True)) + lam_init)
    nt = (((1,), (1,)), ((), ()))
    tn = (((0,), (0,)), ((), ()))
    per_blk = LANES // dk
    tq, lk = q1_ref.shape[0], k1_ref.shape[0]
    lane = lax.broadcasted_iota(jnp.int32, (tq, LANES), 1)

    for blk in range(heads_step // per_blk):
        bs = slice(blk * LANES, (blk + 1) * LANES)
        for m, (q_ref, k_ref) in enumerate(((q1_ref, k1_ref), (q2_ref, k2_ref))):
            kb, qb = k_ref[:, bs], q_ref[:, bs]
            for sub in range(per_blk):
                head = blk * per_blk + sub
                qm = jnp.where((lane >= sub * dk) & (lane < (sub + 1) * dk), qb, jnp.zeros_like(qb))
                s_scr[2 * head + m, 0:lk, :] = lax.dot_general(kb, qm, nt, preferred_element_type=F32)
                if has_cache:
                    qc = qb[:, sub * dk:(sub + 1) * dk]
                    s_scr[2 * head + m, lk:, :] = lax.dot_general(ck_ref[0, 0, m, head].astype(BF16), qc, nt,
                                                                  preferred_element_type=F32)
    dens = []
    for hm in range(2 * heads_step):
        s = s_scr[hm]
        if not fold_scale:
            s = s * scale
        e = jnp.exp(s - jnp.max(s, axis=0, keepdims=True))
        dens.append(jnp.sum(e, axis=0, keepdims=True))
        s_scr[hm] = e
    for head in range(heads_step):
        ratio = lam * dens[2 * head] / dens[2 * head + 1]
        w_scr[head] = (s_scr[2 * head] - s_scr[2 * head + 1] * ratio).astype(BF16)
    for head in range(heads_step):
        vs = slice(head * dv, (head + 1) * dv)
        o_t = lax.dot_general(v_ref[:, vs], w_scr[head, 0:lk, :], tn, preferred_element_type=F32)
        if has_cache:
            o_t = o_t + lax.dot_general(cv_ref[0, 0, head].astype(BF16), w_scr[head, lk:, :], tn,
                                        preferred_element_type=F32)
        o_t = o_t * (1.0 / dens[2 * head])
        o_t = o_t * lax.rsqrt(jnp.mean(jnp.square(o_t), axis=0, keepdims=True) + LN_EPS)
        o = o_t.T * sg_ref[...] * (1.0 - lam_init)
        o_ref[:, vs] = o.astype(o_ref.dtype)


def _softmax_scale(dk):
    scale = dk ** -0.5
    return scale, math.frexp(scale)[0] == 0.5


def _attention(q, k, v, lamv, subln_g, *, row0, n_seq, lq, tq, hp, n_heads, dk, dv, lam_init, cache=None):
    per_blk = LANES // dk
    heads_step = hp * per_blk
    n_hblk = n_heads // heads_step
    map2 = n_heads * dk // (hp * LANES)
    scale, fold_scale = _softmax_scale(dk)
    qb0, kb0 = row0 // tq, row0 // lq
    n_q = lq // tq
    q_spec = lambda off: pl.BlockSpec((tq, hp * LANES), lambda b, h, qi: (qb0 + b * n_q + qi, off + h))
    k_spec = lambda off: pl.BlockSpec((lq, hp * LANES), lambda b, h, qi: (kb0 + b, off + h))
    in_specs = [q_spec(0), q_spec(map2), k_spec(0), k_spec(map2),
                pl.BlockSpec((lq, heads_step * dv), lambda b, h, qi: (kb0 + b, h))]
    args = [q, q, k, k, v]
    if cache is not None:
        cache_k, cache_v, o_i = cache
        past = cache_k.shape[-2]
        in_specs += [pl.BlockSpec((1, 1, 2, heads_step, past, dk), lambda b, h, qi: (b, o_i, 0, h, 0, 0)),
                     pl.BlockSpec((1, 1, heads_step, past, dv), lambda b, h, qi: (b, o_i, h, 0, 0))]
        args += [cache_k, cache_v]
    in_specs += [pl.BlockSpec(lamv.shape, lambda b, h, qi: (0, 0)),
                 pl.BlockSpec((1, dv), lambda b, h, qi: (0, 0))]
    args += [lamv, subln_g.reshape(1, dv)]
    kern = functools.partial(_attn_kernel, heads_step=heads_step, dk=dk, dv=dv, scale=scale,
                             fold_scale=fold_scale, lam_init=lam_init, has_cache=cache is not None)
    lk_all = lq + (cache[0].shape[-2] if cache is not None else 0)
    scratch = [pltpu.VMEM((2 * heads_step, lk_all, tq), F32), pltpu.VMEM((heads_step, lk_all, tq), BF16)]
    return pl.pallas_call(
        kern,
        grid=(n_seq, n_hblk, n_q),
        in_specs=in_specs,
        scratch_shapes=scratch,
        out_specs=pl.BlockSpec((tq, heads_step * dv), lambda b, h, qi: (b * n_q + qi, h)),
        out_shape=jax.ShapeDtypeStruct((n_seq * lq, n_heads * dv), BF16),
        compiler_params=_cparams("arbitrary", "arbitrary", "arbitrary"),
        name="diff_attn_cache" if cache is not None else "diff_attn",
    )(*args)


def _outc_kernel(x_ref, mod_ref, op_ref, os_ref, w_ref, g_ref, b_ref, o_ref, *, tiles_p, tiles_per_s, d, alpha):
    i = pl.program_id(0)
    grp = _group_of_tile(i, tiles_p, tiles_per_s)
    gate = _mod_chunk(mod_ref, grp, 2, d)
    tm = x_ref.shape[0]
    for rows in (slice(0, tm // 2), slice(tm // 2, tm)):
        o_in = jnp.where(i < tiles_p, op_ref[rows, :], os_ref[rows, :])
        out = jnp.dot(o_in, w_ref[...], preferred_element_type=F32)
        o_ref[rows, :] = _post_residual(x_ref[rows, :], out, gate, g_ref[...], b_ref[...], alpha)


def _outc(x, mod_l, o_p, o_s, w_out, ln_g, ln_b, tm, tiles_p, tiles_per_s, alpha):
    t, d = x.shape
    kin = o_p.shape[1]
    kern = functools.partial(_outc_kernel, tiles_p=tiles_p, tiles_per_s=tiles_per_s, d=d, alpha=alpha)
    ln_g, ln_b = ln_g.reshape(1, d), ln_b.reshape(1, d)
    full = lambda a: pl.BlockSpec(a.shape, lambda i: (0,) * a.ndim)
    return pl.pallas_call(
        kern,
        grid=(t // tm,),
        in_specs=[pl.BlockSpec((tm, d), lambda i: (i, 0)), full(mod_l)] + _pair_specs(tm, kin, tiles_p)
        + [full(w_out), full(ln_g), full(ln_b)],
        out_specs=pl.BlockSpec((tm, d), lambda i: (i, 0)),
        out_shape=jax.ShapeDtypeStruct((t, d), F32),
        compiler_params=_cparams("arbitrary"),
        name="outproj_c",
    )(x, mod_l, o_p, o_s, w_out, ln_g, ln_b)


class _Tiles(NamedTuple):
    rows: int
    rows_wide: int
    hidden: int
    s5_tokens: int
    conv_chunk: int
    attn_q: int
    attn_blocks: int


def _plan_tiles(tp, ts, lp, ls):
    tiles = _Tiles(rows=256, rows_wide=512, hidden=1024, s5_tokens=min(4096, tp, ts), conv_chunk=min(lp, 256),
                   attn_q=min(ls, 512), attn_blocks=2)
    for tm in (tiles.rows, tiles.rows_wide):
        assert tp % tm == 0 and ls % tm == 0
    tok = tiles.s5_tokens
    assert tok % lp == 0 and tok % ls == 0 and tp % tok == 0 and ts % tok == 0 and tp % ls == 0
    return tiles


def kernel(x_prompt, x_sample, state_s5_re, state_s5_im, cache_k, cache_v, c, c_ctx, w_mod, b_mod, ln_g, ln_b, w_in_ab, w_dw, b_dw, conv_ln_g, conv_ln_b, s5_lambda_re, s5_lambda_im, s5_log_dt, s5_b_re, s5_b_im, s5_c_re, s5_c_im, s5_d, w_glu, w_out_ab, w_qkv, lam_q1, lam_k1, lam_q2, lam_k2, subln_g, w_out_c, w_ff1, w_ff2):
    bp, lp, d = x_prompt.shape
    bs, ls, _ = x_sample.shape
    depth = w_mod.shape[0]
    tp, ts = bp * lp, bs * ls
    alpha = (2 * depth) ** 0.25
    assert 1 + bs <= MOD_ROWS

    tiles = _plan_tiles(tp, ts, lp, ls)
    tm, tm_mlp, tf = tiles.rows, tiles.rows_wide, tiles.hidden

    xp, xs = x_prompt.reshape(tp, d), x_sample.reshape(ts, d)
    cvec = jnp.zeros((MOD_ROWS, d), F32).at[0].set(c_ctx).at[1:1 + bs].set(c)
    mod = _modvec(cvec, w_mod, b_mod)

    g_ssm, n_ssm, p_ssm = s5_b_re.shape[2:]
    dk = lam_q1.shape[-1]
    dv = subln_g.shape[-1]
    n_heads = w_out_c.shape[1] // dv

    w_ff1_bf, w_ff2_bf = w_ff1.astype(BF16), w_ff2.astype(BF16)
    s_re, s_im, k_list, v_list = [], [], [], []
    for l in range(depth):
        mod_l = mod[l]
        if l % 2 == 0:
            e = l // 2
            if l > 0:
                xp, xs = x[:tp], x[tp:]
            ug, u = _inproj(xp, xs, mod_l, w_in_ab[e].astype(BF16), tm, tp // tm, ls // tm)
            y_conv = _conv_module(ug, w_dw[e], b_dw[e], conv_ln_g[e], conv_ln_b[e], tiles.conv_chunk, lp, ls, tp)
            prep = _s5_prep(s5_lambda_re[e], s5_lambda_im[e], s5_log_dt[e],
                            s5_b_re[e], s5_b_im[e], s5_c_re[e], s5_c_im[e])
            a_re, a_im = prep[5], prep[6]
            gn = g_ssm * n_ssm
            nat = lambda a: a.reshape(2, g_ssm, p_ssm, n_ssm)[:, :, 0].reshape(2, gn)
            a4 = jnp.stack([nat(a_re)[0], nat(a_im)[0], nat(a_re)[1], nat(a_im)[1]])
            tile_k, tile_b = _s5_tiles(n_ssm, p_ssm)
            tok = tiles.s5_tokens
            seq_p, seq_s, tiles_p5 = tok // lp, tok // ls, tp // tok
            ms = max(seq_p, seq_s)
            st = lambda a, dr: jnp.pad(a[:, e, dr].reshape(ts // tok, seq_s, gn), ((0, 0), (0, ms - seq_s), (0, 0)))
            h0_s = jnp.stack([st(state_s5_re, 0), st(state_s5_im, 0), st(state_s5_re, 1), st(state_s5_im, 1)], axis=1)
            h0 = jnp.concatenate([jnp.zeros((tiles_p5, 4, ms, gn), F32), h0_s], axis=0)
            y_scan, hf = _s5_chunked(u, h0, prep[:5], tile_k, tile_b, a4, tok=tok,
                                     geoms=((seq_p, lp // S5_T), (seq_s, ls // S5_T)), n_tiles_p=tiles_p5,
                                     n=n_ssm, p=p_ssm)
            hf = hf[:tiles_p5, :, :seq_p].transpose(1, 0, 2, 3).reshape(2, 2, bp, g_ssm, n_ssm)
            s_re.append(hf[:, 0].transpose(1, 0, 2, 3))
            s_im.append(hf[:, 1].transpose(1, 0, 2, 3))
            x = _outab(xp, xs, mod_l, y_conv, y_scan, u, s5_d[e], w_glu[e].astype(BF16),
                       w_out_ab[e].astype(BF16), ln_g[l, 0], ln_b[l, 0], tm, tp // tm, ls // tm, alpha)
        else:
            o_i = l // 2
            lam_init = 0.8 - 0.6 * math.exp(-0.3 * l)
            cos_t, sin_t = _rope_tables(ls, dk)
            scale, fold_scale = _softmax_scale(dk)
            q, k, v, k_new, v_new = _qkv(x, mod_l, w_qkv[o_i].astype(BF16), cos_t, sin_t, tm, tp // tm, ls // tm,
                                     dk, dv, bp, lp, scale if fold_scale else 1.0)
            lamv = jnp.stack([lam_q1[o_i], lam_k1[o_i], lam_q2[o_i], lam_k2[o_i]])
            geo = dict(n_heads=n_heads, dk=dk, dv=dv, lam_init=lam_init)
            o_p = _attention(q, k, v, lamv, subln_g[o_i], row0=0, n_seq=bp, lq=lp, tq=lp,
                             hp=n_heads * dk // LANES, **geo)
            o_s = _attention(q, k, v, lamv, subln_g[o_i], row0=tp, n_seq=bs, lq=ls,
                             tq=tiles.attn_q, hp=tiles.attn_blocks,
                             cache=(cache_k, cache_v, o_i), **geo)
            k_list.append(k_new)
            v_list.append(v_new)
            x = _outc(x, mod_l, o_p, o_s, w_out_c[o_i].astype(BF16), ln_g[l, 0], ln_b[l, 0],
                      tm_mlp, tp // tm_mlp, ls // tm_mlp, alpha)
        res = _mlp(x, mod_l, w_ff1_bf, w_ff2_bf, l, ln_g[l, 1], ln_b[l, 1],
                   tm_mlp, tf, tp // tm_mlp, ls // tm_mlp, alpha, split=l == depth - 1)
        x = res[0]

    return (res[0].reshape(bp, lp, d), res[1].reshape(bs, ls, d),
            jnp.stack(s_re, axis=1), jnp.stack(s_im, axis=1),
            jnp.stack(k_list, axis=1), jnp.stack(v_list, axis=1))
```

```python
import functools
import math
from typing import NamedTuple

import jax
import jax.numpy as jnp
import numpy as np
from jax import lax
from jax.experimental import pallas as pl
from jax.experimental.pallas import tpu as pltpu

F32 = jnp.float32
BF16 = jnp.bfloat16

LN_EPS = 1e-5
ROPE_BASE = 10000.0
LATENT_GRID_W = 64
MOD_ROWS = 8
V7X_VMEM_LIMIT = 56 * 1024 * 1024
LANES = 128
SUBLANES = 8


def _cparams(*sem):
    return pltpu.CompilerParams(dimension_semantics=sem, vmem_limit_bytes=V7X_VMEM_LIMIT)


def _layer_norm(z, g, b):
    mu = jnp.mean(z, axis=-1, keepdims=True)
    zc = z - mu
    var = jnp.mean(jnp.square(zc), axis=-1, keepdims=True)
    return zc * lax.rsqrt(var + LN_EPS) * g + b


def _group_of_tile(i, tiles_p, tiles_per_s):
    return jnp.where(i < tiles_p, 0, 1 + jnp.maximum(i - tiles_p, 0) // tiles_per_s)


def _mod_chunk(mod_ref, g, k, d):
    return mod_ref[pl.ds(g, 1), k * d:(k + 1) * d]


def _pair_specs(tm, width, tiles_p, grid_rank=1):
    if grid_rank == 1:
        return [pl.BlockSpec((tm, width), lambda i: (jnp.minimum(i, tiles_p - 1), 0)),
                pl.BlockSpec((tm, width), lambda i: (jnp.maximum(i - tiles_p, 0), 0))]
    return [pl.BlockSpec((tm, width), lambda i, f: (jnp.minimum(i, tiles_p - 1), 0)),
            pl.BlockSpec((tm, width), lambda i, f: (jnp.maximum(i - tiles_p, 0), 0))]


def _pair_rows(i, tiles_p, p_ref, s_ref):
    return jnp.where(i < tiles_p, p_ref[...], s_ref[...])


def _modvec_kernel(cv_ref, w_ref, b_ref, o_ref):
    cv = cv_ref[...]
    s = (cv * jax.nn.sigmoid(cv)).astype(BF16)
    o_ref[0] = jnp.dot(s, w_ref[0].astype(BF16), preferred_element_type=F32) + b_ref[0]


def _modvec(cvec, w_mod, b_mod, tn=1024):
    depth, d, n = w_mod.shape
    return pl.pallas_call(
        _modvec_kernel,
        grid=(depth, n // tn),
        in_specs=[pl.BlockSpec((MOD_ROWS, d), lambda l, j: (0, 0)),
                  pl.BlockSpec((1, d, tn), lambda l, j: (l, 0, j)),
                  pl.BlockSpec((1, 1, tn), lambda l, j: (l, 0, j))],
        out_specs=pl.BlockSpec((1, MOD_ROWS, tn), lambda l, j: (l, 0, j)),
        out_shape=jax.ShapeDtypeStruct((depth, MOD_ROWS, n), F32),
        compiler_params=_cparams("arbitrary", "arbitrary"),
        name="modvec",
    )(cvec, w_mod, b_mod.reshape(depth, 1, n))


def _inproj_kernel(xp_ref, xs_ref, mod_ref, w_ref, ug_ref, u_ref, *, tiles_p, tiles_per_s, d, c):
    i = pl.program_id(0)
    g = _group_of_tile(i, tiles_p, tiles_per_s)
    shift = _mod_chunk(mod_ref, g, 0, d)
    scale = _mod_chunk(mod_ref, g, 1, d)
    h = (_pair_rows(i, tiles_p, xp_ref, xs_ref) * (1 + scale) + shift).astype(BF16)
    a_val = jnp.dot(h, w_ref[:, 0:c], preferred_element_type=F32)
    a_gate = jnp.dot(h, w_ref[:, c:2 * c], preferred_element_type=F32)
    ug_ref[...] = a_val * jax.nn.sigmoid(a_gate)
    u_ref[...] = jnp.dot(h, w_ref[:, 2 * c:3 * c], preferred_element_type=F32)


def _inproj(xp, xs, mod_l, w_in, tm, tiles_p, tiles_per_s):
    t, d = xp.shape[0] + xs.shape[0], xp.shape[1]
    c = w_in.shape[1] // 3
    kern = functools.partial(_inproj_kernel, tiles_p=tiles_p, tiles_per_s=tiles_per_s, d=d, c=c)
    return pl.pallas_call(
        kern,
        grid=(t // tm,),
        in_specs=_pair_specs(tm, d, tiles_p) + [pl.BlockSpec(mod_l.shape, lambda i: (0, 0)),
                                                pl.BlockSpec(w_in.shape, lambda i: (0, 0), pipeline_mode=pl.Buffered(1))],
        out_specs=[pl.BlockSpec((tm, c), lambda i: (i, 0)),
                   pl.BlockSpec((tm, c), lambda i: (i, 0))],
        out_shape=[jax.ShapeDtypeStruct((t, c), F32), jax.ShapeDtypeStruct((t, c), F32)],
        compiler_params=_cparams("arbitrary"),
        name="inproj",
    )(xp, xs, mod_l, w_in)


CONV_HALO = 16
CONV_ROWS = 64
CONV_COLS = 128


def _conv_kernel(prev_ref, cur_ref, next_ref, w_ref, b_ref, g_ref, beta_ref, o_ref, pad_scr, sh_scr, conv_scr,
                 *, chunks_p, chunks_s, n_chunks_p, width, lc, c):
    i = pl.program_id(0)
    in_p = i < n_chunks_p
    k = jnp.where(in_p, i % chunks_p, jnp.maximum(i - n_chunks_p, 0) % chunks_s)
    last = jnp.where(in_p, chunks_p - 1, chunks_s - 1)
    has_prev = (k > 0).astype(F32)
    has_next = (k < last).astype(F32)
    pad_scr[0:CONV_HALO, :] = prev_ref[...] * has_prev
    pad_scr[CONV_HALO:CONV_HALO + lc, :] = cur_ref[...]
    pad_scr[CONV_HALO + lc:2 * CONV_HALO + lc, :] = next_ref[...] * has_next
    off = CONV_HALO - width // 2
    span = lc + CONV_HALO + SUBLANES
    for sft in range(SUBLANES):
        sh_scr[sft] = pad_scr[sft:sft + span, :]

    for r0 in range(0, lc, CONV_ROWS):
        for cb in range(c // CONV_COLS):
            cs = slice(cb * CONV_COLS, (cb + 1) * CONV_COLS)
            acc = jnp.zeros((CONV_ROWS, CONV_COLS), F32)
            for kk in range(width):
                whole, sft = divmod(kk + off, SUBLANES)
                base = r0 + whole * SUBLANES
                acc = acc + sh_scr[sft, base:base + CONV_ROWS, cs] * w_ref[kk:kk + 1, cs]
            conv_scr[r0:r0 + CONV_ROWS, cs] = acc + b_ref[:, cs]
    y = _layer_norm(conv_scr[...], g_ref[...], beta_ref[...])
    o_ref[...] = (y * jax.nn.sigmoid(y)).astype(o_ref.dtype)


def _conv_module(ug, w_dw, b_dw, ln_g, ln_b, lc, lp, ls, tp):
    t, c = ug.shape
    width = w_dw.shape[0]
    assert width // 2 < CONV_HALO and lc % CONV_HALO == 0 and lp % lc == 0 and ls % lc == 0
    hb = lc // CONV_HALO
    n_halo_blocks = t // CONV_HALO
    kern = functools.partial(_conv_kernel, chunks_p=lp // lc, chunks_s=ls // lc, n_chunks_p=tp // lc,
                             width=width, lc=lc, c=c)
    vec = lambda a: a.reshape(1, c)
    return pl.pallas_call(
        kern,
        grid=(t // lc,),
        in_specs=[pl.BlockSpec((CONV_HALO, c), lambda i: (jnp.maximum(i * hb - 1, 0), 0)),
                  pl.BlockSpec((lc, c), lambda i: (i, 0)),
                  pl.BlockSpec((CONV_HALO, c), lambda i: (jnp.minimum((i + 1) * hb, n_halo_blocks - 1), 0)),
                  pl.BlockSpec((width, c), lambda i: (0, 0)),
                  pl.BlockSpec((1, c), lambda i: (0, 0)),
                  pl.BlockSpec((1, c), lambda i: (0, 0)),
                  pl.BlockSpec((1, c), lambda i: (0, 0))],
        out_specs=pl.BlockSpec((lc, c), lambda i: (i, 0)),
        out_shape=jax.ShapeDtypeStruct((t, c), BF16),
        scratch_shapes=[pltpu.VMEM((lc + 2 * CONV_HALO, c), F32),
                        pltpu.VMEM((SUBLANES, lc + CONV_HALO + SUBLANES, c), F32), pltpu.VMEM((lc, c), F32)],
        compiler_params=_cparams("arbitrary"),
        name="conv_module",
    )(ug, ug, ug, w_dw, vec(b_dw), vec(ln_g), vec(ln_b))


S5_T = 16
S5_GB = 8


def _s5_prep_kernel(bt_re_ref, bt_im_ref, la_re_ref, la_im_ref, dta_ref, ct_re_ref, ct_im_ref, rep_ref, lb_re_ref,
                    lb_im_ref, dtb_ref, be_re_ref, be_im_ref, cs_re_ref, cs_ni_ref, kc_ref, a_re_ref, a_im_ref, *, n, p):
    t = S5_T
    fwd = pl.program_id(0) == 0

    lam_re, lam_im = la_re_ref[0, 0], la_im_ref[0, 0]
    dt = jnp.exp(dta_ref[0, 0])
    mag = jnp.exp(lam_re * dt)
    ar, ai = mag * jnp.cos(lam_im * dt), mag * jnp.sin(lam_im * dt)
    den = jnp.square(lam_re) + jnp.square(lam_im)
    coef_re = ((ar - 1) * lam_re + ai * lam_im) / den
    coef_im = (ai * lam_re - (ar - 1) * lam_im) / den
    b_re, b_im = bt_re_ref[0, 0], bt_im_ref[0, 0]
    bb_re = coef_re * b_re - coef_im * b_im
    bb_im = coef_re * b_im + coef_im * b_re
    rows = bb_re.shape[0]
    pw = []
    wr, wi = bb_re, bb_im
    pr, pi = jnp.ones_like(ar), jnp.zeros_like(ar)
    for k in range(t):
        pw.append((wr, wi))
        wr, wi = wr * ar - wi * ai, wr * ai + wi * ar
        pr, pi = pr * ar - pi * ai, pr * ai + pi * ar
    a_re_ref[0, 0] = pr
    a_im_ref[0, 0] = pi
    for j in range(t):
        be_re_ref[0, 0, j * rows:(j + 1) * rows, :] = jnp.where(fwd, pw[t - 1 - j][0], pw[j][0])
        be_im_ref[0, 0, j * rows:(j + 1) * rows, :] = jnp.where(fwd, pw[t - 1 - j][1], pw[j][1])

    hi = lax.Precision.HIGHEST
    ct_re = jnp.dot(ct_re_ref[0, 0], rep_ref[...], precision=hi, preferred_element_type=F32)
    ct_im = jnp.dot(ct_im_ref[0, 0], rep_ref[...], precision=hi, preferred_element_type=F32)
    shp = ct_re.shape
    lam_re, lam_im = lb_re_ref[0, 0], lb_im_ref[0, 0]
    dt = jnp.exp(dtb_ref[0, 0])
    mag = jnp.exp(lam_re * dt)
    sq_re = jnp.broadcast_to(mag * jnp.cos(lam_im * dt), shp)
    sq_im = jnp.broadcast_to(mag * jnp.sin(lam_im * dt), shp)
    blk = lax.broadcasted_iota(jnp.int32, shp, 1) // p
    k1 = jnp.where(fwd, blk + 1, t - blk)
    qr, qi = jnp.ones(shp, F32), jnp.zeros(shp, F32)
    n_bits = t.bit_length()
    for bit in range(n_bits):
        take = ((k1 >> bit) & 1) == 1
        qr, qi = (jnp.where(take, qr * sq_re - qi * sq_im, qr), jnp.where(take, qr * sq_im + qi * sq_re, qi))
        if bit + 1 < n_bits:
            sq_re, sq_im = sq_re * sq_re - sq_im * sq_im, 2.0 * (sq_re * sq_im)
    v_re = ct_re * qr - ct_im * qi
    v_im = ct_re * qi + ct_im * qr
    cs_re_ref[0, 0] = v_re
    cs_ni_ref[0, 0] = -v_im
    lane = lax.broadcasted_iota(jnp.int32, shp, 1)
    w = shp[1]
    v0_re = jnp.where(fwd, jnp.where(lane < p, ct_re, pltpu.roll(v_re, p, 1)),
                      jnp.where(lane >= w - p, ct_re, pltpu.roll(v_re, w - p, 1)))
    v0_im = jnp.where(fwd, jnp.where(lane < p, ct_im, pltpu.roll(v_im, p, 1)),
                      jnp.where(lane >= w - p, ct_im, pltpu.roll(v_im, w - p, 1)))
    for g in range(S5_GB):
        ra, rb = slice(g * p, (g + 1) * p), slice(g * n, (g + 1) * n)
        kc_ref[0, 0, ra, :] = (jnp.dot(bb_re[ra], v0_re[rb], precision=hi, preferred_element_type=F32)
                               - jnp.dot(bb_im[ra], v0_im[rb], precision=hi, preferred_element_type=F32))


def _s5_prep(lam_re, lam_im, log_dt, b_re, b_im, c_re, c_im):
    _, g, n, p = b_re.shape
    t = S5_T
    assert S5_GB * p == LANES and g % S5_GB == 0
    nb = g // S5_GB
    ra, rb = S5_GB * p, S5_GB * n
    lay_a = lambda a: jnp.broadcast_to(a[:, :, None, :], (2, g, p, n)).reshape(2, nb, ra, n)
    lay_b = lambda a: a.reshape(2, nb, rb, 1)
    bt = lambda a: a.transpose(0, 1, 3, 2).reshape(2, nb, ra, n)
    ct = lambda a: a.transpose(0, 1, 3, 2).reshape(2, nb, rb, p)
    rep_i = jnp.asarray(np.tile(np.eye(p, dtype=np.float32), (1, t)))
    dt_g = jnp.broadcast_to(log_dt[:, :, None], (2, g, n))
    blk = lambda r, c: pl.BlockSpec((1, 1, r, c), lambda d, i: (d, i, 0, 0))
    shp = lambda r, c: jax.ShapeDtypeStruct((2, nb, r, c), F32)
    kern = functools.partial(_s5_prep_kernel, n=n, p=p)
    return pl.pallas_call(
        kern,
        grid=(2, nb),
        in_specs=[blk(ra, n)] * 5 + [blk(rb, p)] * 2 + [pl.BlockSpec((p, t * p), lambda d, i: (0, 0))] + [blk(rb, 1)] * 3,
        out_specs=[blk(t * ra, n), blk(t * ra, n), blk(rb, t * p), blk(rb, t * p), blk(ra, t * p), blk(ra, n), blk(ra, n)],
        out_shape=[shp(t * ra, n), shp(t * ra, n), shp(rb, t * p), shp(rb, t * p), shp(ra, t * p), shp(ra, n), shp(ra, n)],
        compiler_params=_cparams("arbitrary", "arbitrary"),
        name="s5_prep",
    )(bt(b_re), bt(b_im), lay_a(lam_re), lay_a(lam_im), lay_a(dt_g), ct(c_re), ct(c_im), rep_i,
      lay_b(lam_re), lay_b(lam_im), lay_b(dt_g))


def _s5_expand(src, tile, row_div, row_mod, lane_div, lane_mod, precision=None):
    full = jnp.dot(src, tile, precision=precision, preferred_element_type=F32)
    r = lax.broadcasted_iota(jnp.int32, full.shape, 0) // row_div % row_mod
    l = lax.broadcasted_iota(jnp.int32, full.shape, 1) // lane_div % lane_mod
    return jnp.where(r == l, full, 0.0)


def _s5_chunk_kernel(u_ref, be_re_ref, be_im_ref, cs_re_ref, cs_ni_ref, kc_ref, tk_ref, tb_ref, a_ref, h0_ref,
                     y_ref, hf_ref, m8_scr, be8_scr, cs8_scr, e_scr, *, geoms, n_tiles_p, n, p):
    t, gb = S5_T, S5_GB
    cw = gb * p
    sw = gb * n
    rows = geoms[0][0] * geoms[0][1]
    tile_i = pl.program_id(1)

    @pl.when(tile_i == 0)
    def _():
        tk, tb = tk_ref[...].astype(BF16), tb_ref[...].astype(BF16)
        step = 4 * cw
        for r0 in range(0, t * cw, step):
            for part, (ref, d) in enumerate(((be_re_ref, 0), (be_im_ref, 0), (be_re_ref, 1), (be_im_ref, 1))):
                be8_scr[r0:r0 + step, part * sw:(part + 1) * sw] = _s5_expand(
                    ref[d, 0, r0:r0 + step, :].astype(BF16), tb, p, gb, n, gb).astype(BF16)
        for part, (ref, d) in enumerate(((cs_re_ref, 0), (cs_ni_ref, 0), (cs_re_ref, 1), (cs_ni_ref, 1))):
            cs8_scr[part * sw:(part + 1) * sw, :] = _s5_expand(ref[d, 0].astype(BF16), tk, n, gb, p, gb).astype(BF16)
        hi = lax.Precision.HIGHEST
        bd_f = _s5_expand(kc_ref[0, 0], tk_ref[...], p, gb, p, gb, hi)
        bd_r = _s5_expand(kc_ref[1, 0], tk_ref[...], p, gb, p, gb, hi)
        tile_f = lambda k: bd_f[:, k * cw:(k + 1) * cw]
        tile_r = lambda k: bd_r[:, (t - 1 - k) * cw:(t - k) * cw]
        for j in range(t):
            for i in range(t):
                blk = tile_f(i - j) if i > j else tile_r(j - i) if i < j else tile_f(0) + tile_r(0)
                m8_scr[j * cw:(j + 1) * cw, i * cw:(i + 1) * cw] = blk.astype(BF16)

    x = u_ref[...].reshape(rows, t * cw).astype(BF16)
    e = jnp.dot(x, be8_scr[...], preferred_element_type=F32)
    n_slab = e.shape[1] // LANES
    per_part = sw // LANES
    for k in range(n_slab):
        e_scr[k] = e[:, k * LANES:(k + 1) * LANES]

    def scan(nseq, nc):
        loops = [list(range(per_part))] if nseq <= 16 else [[q] for q in range(per_part)]
        for prs in loops:
            coef = [[jnp.broadcast_to(a_ref[r:r + 1, q * LANES:(q + 1) * LANES], (nseq, LANES)) for r in range(4)]
                    for q in prs]
            init = tuple(tuple(h0_ref[0, r, 0:nseq, q * LANES:(q + 1) * LANES] for r in range(4)) for q in prs)

            def step(c, carry):
                out = []
                for idx, q in enumerate(prs):
                    sf_re, sf_im, sr_re, sr_im = carry[idx]
                    af_re, af_im, ar_re, ar_im = coef[idx]
                    at_f = pl.ds(c, nseq, stride=nc)
                    at_r = pl.ds(nc - 1 - c, nseq, stride=nc)
                    ef_re, ef_im = e_scr[q, at_f, :], e_scr[per_part + q, at_f, :]
                    er_re, er_im = e_scr[2 * per_part + q, at_r, :], e_scr[3 * per_part + q, at_r, :]
                    e_scr[q, at_f, :] = sf_re
                    e_scr[per_part + q, at_f, :] = sf_im
                    e_scr[2 * per_part + q, at_r, :] = sr_re
                    e_scr[3 * per_part + q, at_r, :] = sr_im
                    out.append((af_re * sf_re - af_im * sf_im + ef_re, af_re * sf_im + af_im * sf_re + ef_im,
                                ar_re * sr_re - ar_im * sr_im + er_re, ar_re * sr_im + ar_im * sr_re + er_im))
                return tuple(out)

            fin = lax.fori_loop(0, nc, step, init)
            for idx, q in enumerate(prs):
                for r in range(4):
                    hf_ref[0, r, 0:nseq, q * LANES:(q + 1) * LANES] = fin[idx][r]

    hf_ref[...] = jnp.zeros_like(hf_ref)

    @pl.when(tile_i < n_tiles_p)
    def _():
        scan(*geoms[0])

    @pl.when(tile_i >= n_tiles_p)
    def _():
        scan(*geoms[1])

    s = jnp.concatenate([e_scr[k] for k in range(n_slab)], axis=-1).astype(BF16)
    y = (jnp.dot(x, m8_scr[...], preferred_element_type=F32)
         + jnp.dot(s, cs8_scr[...], preferred_element_type=F32))
    y_ref[...] = y.reshape(rows * t, cw)


def _s5_chunked(u, h0, prep, tile_k, tile_b, a4, *, tok, geoms, n_tiles_p, n, p):
    be_re, be_im, cs_re, cs_ni, kc = prep
    t, gb = S5_T, S5_GB
    cw, sw = gb * p, gb * n
    nb = u.shape[1] // cw
    n_tiles = u.shape[0] // tok
    ms = h0.shape[2]
    both = lambda a: pl.BlockSpec((2, 1) + a.shape[2:], lambda b, i: (0, b, 0, 0), pipeline_mode=pl.Buffered(1))
    const = lambda a: pl.BlockSpec(a.shape, lambda b, i: (0, 0), pipeline_mode=pl.Buffered(1))
    kern = functools.partial(_s5_chunk_kernel, geoms=geoms, n_tiles_p=n_tiles_p, n=n, p=p)
    return pl.pallas_call(
        kern,
        grid=(nb, n_tiles),
        in_specs=[pl.BlockSpec((tok, cw), lambda b, i: (i, b)),
                  both(be_re), both(be_im), both(cs_re), both(cs_ni), both(kc), const(tile_k), const(tile_b),
                  pl.BlockSpec((4, sw), lambda b, i: (0, b)),
                  pl.BlockSpec((1, 4, ms, sw), lambda b, i: (i, 0, 0, b))],
        out_specs=[pl.BlockSpec((tok, cw), lambda b, i: (i, b)),
                   pl.BlockSpec((1, 4, ms, sw), lambda b, i: (i, 0, 0, b))],
        out_shape=[jax.ShapeDtypeStruct(u.shape, F32), jax.ShapeDtypeStruct(h0.shape, F32)],
        scratch_shapes=[pltpu.VMEM((t * cw, t * cw), BF16), pltpu.VMEM((t * cw, 4 * sw), BF16),
                        pltpu.VMEM((4 * sw, t * cw), BF16), pltpu.VMEM((4 * sw // LANES, tok // t, LANES), F32)],
        compiler_params=_cparams("arbitrary", "arbitrary"),
        name="s5_chunked",
    )(u, be_re, be_im, cs_re, cs_ni, kc, tile_k, tile_b, a4, h0)


def _s5_tiles(n, p):
    t, gb = S5_T, S5_GB
    eye = lambda k: np.eye(k, dtype=np.float32)
    tile_k = np.einsum("ab,pq->apbq", eye(t), eye(p))[:, :, :, None, :] * np.ones((1, 1, 1, gb, 1), np.float32)
    tile_b = eye(n)[:, None, :] * np.ones((1, gb, 1), np.float32)
    return jnp.asarray(tile_k.reshape(t * p, t * gb * p)), jnp.asarray(tile_b.reshape(n, gb * n))


def _post_residual(x, y, gate, g, b, alpha):
    return _layer_norm(alpha * x + gate * y, g, b)


def _outab_kernel(xp_ref, xs_ref, mod_ref, yc_ref, ys_ref, u_ref, dsk_ref, wglu_ref, wout_ref, g_ref, b_ref, o_ref,
                  *, tiles_p, tiles_per_s, d, c, alpha):
    i = pl.program_id(0)
    grp = _group_of_tile(i, tiles_p, tiles_per_s)
    gate = _mod_chunk(mod_ref, grp, 2, d)
    tm = o_ref.shape[0]
    for rows in (slice(0, tm // 2), slice(tm // 2, tm)):
        y_s = ys_ref[rows, :] + dsk_ref[...] * u_ref[rows, :]
        y_s = jax.nn.gelu(y_s)
        z = jnp.dot(y_s.astype(BF16), wglu_ref[...], preferred_element_type=F32)
        y_ssm = y_s * jax.nn.sigmoid(z)
        out = (jnp.dot(yc_ref[rows, :], wout_ref[0:c, :], preferred_element_type=F32)
               + jnp.dot(y_ssm.astype(BF16), wout_ref[c:2 * c, :], preferred_element_type=F32))
        x = jnp.where(i < tiles_p, xp_ref[rows, :], xs_ref[rows, :])
        o_ref[rows, :] = _post_residual(x, out, gate, g_ref[...], b_ref[...], alpha)


def _outab(xp, xs, mod_l, y_conv, y_scan, u, d_skip, w_glu, w_out, ln_g, ln_b, tm, tiles_p, tiles_per_s, alpha):
    t, d = xp.shape[0] + xs.shape[0], xp.shape[1]
    c = u.shape[1]
    kern = functools.partial(_outab_kernel, tiles_p=tiles_p, tiles_per_s=tiles_per_s, d=d, c=c, alpha=alpha)
    row = lambda w: pl.BlockSpec((tm, w), lambda i: (i, 0))
    full = lambda a: pl.BlockSpec(a.shape, lambda i: (0,) * a.ndim, pipeline_mode=pl.Buffered(1))
    d_skip, ln_g, ln_b = d_skip.reshape(1, c), ln_g.reshape(1, d), ln_b.reshape(1, d)
    return pl.pallas_call(
        kern,
        grid=(t // tm,),
        in_specs=_pair_specs(tm, d, tiles_p) + [full(mod_l), row(c), row(c), row(c), full(d_skip), full(w_glu),
                                                full(w_out), full(ln_g), full(ln_b)],
        out_specs=row(d),
        out_shape=jax.ShapeDtypeStruct((t, d), F32),
        compiler_params=_cparams("arbitrary"),
        name="outproj_ab",
    )(xp, xs, mod_l, y_conv, y_scan, u, d_skip, w_glu, w_out, ln_g, ln_b)


def _mlp_kernel(x_ref, mod_ref, w1_ref, w2_ref, g_ref, b_ref, *rest, tiles_p, tiles_per_s, d, n_f, alpha, split):
    outs, (h_scr, acc_scr) = rest[:-2], rest[-2:]
    i = pl.program_id(0)
    f = pl.program_id(1)
    grp = _group_of_tile(i, tiles_p, tiles_per_s)
    tm = h_scr.shape[0]
    halves = [slice(0, tm // 2), slice(tm // 2, tm)]

    def ffn(rows):
        a = jnp.dot(h_scr[rows, :], w1_ref[...], preferred_element_type=F32)
        a = jnp.square(jnp.maximum(a, 0.0)).astype(BF16)
        return jnp.dot(a, w2_ref[...], preferred_element_type=F32)

    @pl.when(f == 0)
    def _():
        shift = _mod_chunk(mod_ref, grp, 3, d)
        scale = _mod_chunk(mod_ref, grp, 4, d)
        for rows in halves:
            h_scr[rows, :] = (x_ref[rows, :] * (1 + scale) + shift).astype(BF16)
            acc_scr[rows, :] = ffn(rows)

    @pl.when(jnp.logical_and(f > 0, f < n_f - 1))
    def _():
        acc_scr[...] += ffn(slice(None))

    @pl.when(f == n_f - 1)
    def _():
        gate = _mod_chunk(mod_ref, grp, 5, d)
        for rows in halves:
            res = _post_residual(x_ref[rows, :], acc_scr[rows, :] + ffn(rows), gate, g_ref[...], b_ref[...], alpha)
            if split:
                @pl.when(i < tiles_p)
                def _():
                    outs[0][rows, :] = res

                @pl.when(i >= tiles_p)
                def _():
                    outs[1][rows, :] = res
            else:
                outs[0][rows, :] = res


def _mlp(x, mod_l, w1, w2, layer, ln_g, ln_b, tm, tf, tiles_p, tiles_per_s, alpha, split):
    t, d = x.shape
    n_f = w1.shape[2] // tf
    assert n_f >= 2
    kern = functools.partial(_mlp_kernel, tiles_p=tiles_p, tiles_per_s=tiles_per_s, d=d, n_f=n_f, alpha=alpha,
                             split=split)
    ln_g, ln_b = ln_g.reshape(1, d), ln_b.reshape(1, d)
    if split:
        out_specs = _pair_specs(tm, d, tiles_p, grid_rank=2)
        out_shape = [jax.ShapeDtypeStruct((tiles_p * tm, d), F32), jax.ShapeDtypeStruct((t - tiles_p * tm, d), F32)]
    else:
        out_specs = [pl.BlockSpec((tm, d), lambda i, f: (i, 0))]
        out_shape = [jax.ShapeDtypeStruct((t, d), F32)]
    return pl.pallas_call(
        kern,
        grid=(t // tm, n_f),
        in_specs=[pl.BlockSpec((tm, d), lambda i, f: (i, 0)),
                  pl.BlockSpec(mod_l.shape, lambda i, f: (0, 0)),
                  pl.BlockSpec((None, d, tf), lambda i, f: (layer, 0, f)),
                  pl.BlockSpec((None, tf, d), lambda i, f: (layer, f, 0)),
                  pl.BlockSpec((1, d), lambda i, f: (0, 0)),
                  pl.BlockSpec((1, d), lambda i, f: (0, 0))],
        out_specs=out_specs,
        out_shape=out_shape,
        scratch_shapes=[pltpu.VMEM((tm, d), BF16), pltpu.VMEM((tm, d), F32)],
        compiler_params=_cparams("arbitrary", "arbitrary"),
        name="mlp",
    )(x, mod_l, w1, w2, ln_g, ln_b)


def _rope_tables(n_pos, dk):
    ax = dk // 2
    half = ax // 2
    freqs = ROPE_BASE ** (-jnp.arange(half, dtype=F32) / half)
    pos = jnp.arange(n_pos)
    row = (pos // LATENT_GRID_W).astype(F32)
    col = (pos % LATENT_GRID_W).astype(F32)
    ang_r, ang_c = row[:, None] * freqs, col[:, None] * freqs
    cos = jnp.concatenate([jnp.cos(ang_r)] * 2 + [jnp.cos(ang_c)] * 2, axis=-1)
    sin = jnp.concatenate([-jnp.sin(ang_r), jnp.sin(ang_r), -jnp.sin(ang_c), jnp.sin(ang_c)], axis=-1)
    rep = LANES // dk
    return jnp.tile(cos, (1, rep)), jnp.tile(sin, (1, rep))


def _qkv_kernel(x_ref, mod_ref, w_ref, cos_ref, sin_ref, q_ref, k_ref, v_ref, kc_ref, vc_ref,
                *, tiles_p, tiles_per_s, d, dk, dv, q_scale):
    i = pl.program_id(0)
    g = _group_of_tile(i, tiles_p, tiles_per_s)
    shift = _mod_chunk(mod_ref, g, 0, d)
    scale = _mod_chunk(mod_ref, g, 1, d)
    h = (x_ref[...] * (1 + scale) + shift).astype(BF16)
    tn = q_ref.shape[-1]
    quarter = dk // 4
    latent = i >= tiles_p
    cos, sin = cos_ref[...], sin_ref[...]
    lane = lax.broadcasted_iota(jnp.int32, cos.shape, 1)
    first = (lane % (2 * quarter)) < quarter

    def rotated(y, o_ref, out_scale):
        for cb in range(tn // LANES):
            yb = y[:, cb * LANES:(cb + 1) * LANES]
            partner = jnp.where(first, pltpu.roll(yb, LANES - quarter, 1), pltpu.roll(yb, quarter, 1))
            out = jnp.where(latent, yb * cos + partner * sin, yb)
            o_ref[:, cb * LANES:(cb + 1) * LANES] = (out * out_scale).astype(o_ref.dtype)

    rotated(jnp.dot(h, w_ref[:, 0:tn], preferred_element_type=F32), q_ref, q_scale)
    y_k = jnp.dot(h, w_ref[:, tn:2 * tn], preferred_element_type=F32)
    rotated(y_k, k_ref, 1.0)
    y_v = jnp.dot(h, w_ref[:, 2 * tn:3 * tn], preferred_element_type=F32)
    v_ref[...] = y_v.astype(v_ref.dtype)

    @pl.when(i < tiles_p)
    def _():
        n_heads = kc_ref.shape[2]
        for m in range(2):
            for hh in range(n_heads):
                c0 = (m * n_heads + hh) * dk
                kc_ref[0, m, hh] = y_k[:, c0:c0 + dk]
        for hh in range(n_heads):
            vc_ref[0, hh] = y_v[:, hh * dv:(hh + 1) * dv]


def _qkv(x, mod_l, w_qkv, cos_t, sin_t, tm, tiles_p, tiles_per_s, dk, dv, bp, lp, q_scale):
    t, d = x.shape
    n_out = w_qkv.shape[1]
    assert n_out % 3 == 0 and lp % tm == 0
    tn = n_out // 3
    n_heads = tn // dv
    per_seq = lp // tm
    kern = functools.partial(_qkv_kernel, tiles_p=tiles_p, tiles_per_s=tiles_per_s, d=d, dk=dk, dv=dv,
                             q_scale=q_scale)
    pos_blk = lambda i: (jnp.maximum(i - tiles_p, 0) % tiles_per_s, 0)
    c_tile = lambda i: jnp.minimum(i, tiles_p - 1)
    row = pl.BlockSpec((tm, tn), lambda i: (i, 0))
    return pl.pallas_call(
        kern,
        grid=(t // tm,),
        in_specs=[pl.BlockSpec((tm, d), lambda i: (i, 0)),
                  pl.BlockSpec(mod_l.shape, lambda i: (0, 0)),
                  pl.BlockSpec((d, n_out), lambda i: (0, 0), pipeline_mode=pl.Buffered(1)),
                  pl.BlockSpec((tm, LANES), pos_blk),
                  pl.BlockSpec((tm, LANES), pos_blk)],
        out_specs=[row, row, row,
                   pl.BlockSpec((1, 2, n_heads, tm, dk), lambda i: (c_tile(i) // per_seq, 0, 0, c_tile(i) % per_seq, 0)),
                   pl.BlockSpec((1, n_heads, tm, dv), lambda i: (c_tile(i) // per_seq, 0, c_tile(i) % per_seq, 0))],
        out_shape=[jax.ShapeDtypeStruct((t, tn), BF16)] * 3
        + [jax.ShapeDtypeStruct((bp, 2, n_heads, lp, dk), F32), jax.ShapeDtypeStruct((bp, n_heads, lp, dv), F32)],
        compiler_params=_cparams("arbitrary"),
        name="qkv_proj",
    )(x, mod_l, w_qkv, cos_t, sin_t)


def _attn_kernel(*refs, heads_step, dk, dv, scale, fold_scale, lam_init, has_cache):
    if has_cache:
        q1_ref, q2_ref, k1_ref, k2_ref, v_ref, ck_ref, cv_ref, lamv_ref, sg_ref, o_ref, s_scr, w_scr = refs
    else:
        q1_ref, q2_ref, k1_ref, k2_ref, v_ref, lamv_ref, sg_ref, o_ref, s_scr, w_scr = refs
        ck_ref = cv_ref = None
    lv = lamv_ref[...]
    lam = (jnp.exp(jnp.sum(lv[0:1] * lv[1:2], axis=-1, keepdims=True))
           - jnp.exp(jnp.sum(lv[2:3] * lv[3:4], axis=-1, keepdims=True)) + lam_init)
    nt = (((1,), (1,)), ((), ()))
    tn = (((0,), (0,)), ((), ()))
    per_blk = LANES // dk
    tq, lk = q1_ref.shape[0], k1_ref.shape[0]
    lane = lax.broadcasted_iota(jnp.int32, (tq, LANES), 1)

    for blk in range(heads_step // per_blk):
        bs = slice(blk * LANES, (blk + 1) * LANES)
        for m, (q_ref, k_ref) in enumerate(((q1_ref, k1_ref), (q2_ref, k2_ref))):
            kb, qb = k_ref[:, bs], q_ref[:, bs]
            for sub in range(per_blk):
                head = blk * per_blk + sub
                qm = jnp.where((lane >= sub * dk) & (lane < (sub + 1) * dk), qb, jnp.zeros_like(qb))
                s_scr[2 * head + m, 0:lk, :] = lax.dot_general(kb, qm, nt, preferred_element_type=F32)
                if has_cache:
                    qc = qb[:, sub * dk:(sub + 1) * dk]
                    s_scr[2 * head + m, lk:, :] = lax.dot_general(ck_ref[0, 0, m, head].astype(BF16), qc, nt,
                                                                  preferred_element_type=F32)
    dens = []
    for hm in range(2 * heads_step):
        s = s_scr[hm]
        if not fold_scale:
            s = s * scale
        e = jnp.exp(s - jnp.max(s, axis=0, keepdims=True))
        dens.append(jnp.sum(e, axis=0, keepdims=True))
        s_scr[hm] = e
    for head in range(heads_step):
        ratio = lam * dens[2 * head] / dens[2 * head + 1]
        w_scr[head] = (s_scr[2 * head] - s_scr[2 * head + 1] * ratio).astype(BF16)
    for head in range(heads_step):
        vs = slice(head * dv, (head + 1) * dv)
        o_t = lax.dot_general(v_ref[:, vs], w_scr[head, 0:lk, :], tn, preferred_element_type=F32)
        if has_cache:
            o_t = o_t + lax.dot_general(cv_ref[0, 0, head].astype(BF16), w_scr[head, lk:, :], tn,
                                        preferred_element_type=F32)
        o_t = o_t * (1.0 / dens[2 * head])
        o_t = o_t * lax.rsqrt(jnp.mean(jnp.square(o_t), axis=0, keepdims=True) + LN_EPS)
        o = o_t.T * sg_ref[...] * (1.0 - lam_init)
        o_ref[:, vs] = o.astype(o_ref.dtype)


def _softmax_scale(dk):
    scale = dk ** -0.5
    return scale, math.frexp(scale)[0] == 0.5


def _attention(q, k, v, lamv, subln_g, *, row0, n_seq, lq, tq, hp, n_heads, dk, dv, lam_init, cache=None):
    per_blk = LANES // dk
    heads_step = hp * per_blk
    n_hblk = n_heads // heads_step
    map2 = n_heads * dk // (hp * LANES)
    scale, fold_scale = _softmax_scale(dk)
    qb0, kb0 = row0 // tq, row0 // lq
    n_q = lq // tq
    q_spec = lambda off: pl.BlockSpec((tq, hp * LANES), lambda b, h, qi: (qb0 + b * n_q + qi, off + h))
    k_spec = lambda off: pl.BlockSpec((lq, hp * LANES), lambda b, h, qi: (kb0 + b, off + h))
    in_specs = [q_spec(0), q_spec(map2), k_spec(0), k_spec(map2),
                pl.BlockSpec((lq, heads_step * dv), lambda b, h, qi: (kb0 + b, h))]
    args = [q, q, k, k, v]
    if cache is not None:
        cache_k, cache_v, o_i = cache
        past = cache_k.shape[-2]
        in_specs += [pl.BlockSpec((1, 1, 2, heads_step, past, dk), lambda b, h, qi: (b, o_i, 0, h, 0, 0)),
                     pl.BlockSpec((1, 1, heads_step, past, dv), lambda b, h, qi: (b, o_i, h, 0, 0))]
        args += [cache_k, cache_v]
    in_specs += [pl.BlockSpec(lamv.shape, lambda b, h, qi: (0, 0)),
                 pl.BlockSpec((1, dv), lambda b, h, qi: (0, 0))]
    args += [lamv, subln_g.reshape(1, dv)]
    kern = functools.partial(_attn_kernel, heads_step=heads_step, dk=dk, dv=dv, scale=scale,
                             fold_scale=fold_scale, lam_init=lam_init, has_cache=cache is not None)
    lk_all = lq + (cache[0].shape[-2] if cache is not None else 0)
    scratch = [pltpu.VMEM((2 * heads_step, lk_all, tq), F32), pltpu.VMEM((heads_step, lk_all, tq), BF16)]
    return pl.pallas_call(
        kern,
        grid=(n_seq, n_hblk, n_q),
        in_specs=in_specs,
        scratch_shapes=scratch,
        out_specs=pl.BlockSpec((tq, heads_step * dv), lambda b, h, qi: (b * n_q + qi, h)),
        out_shape=jax.ShapeDtypeStruct((n_seq * lq, n_heads * dv), BF16),
        compiler_params=_cparams("arbitrary", "arbitrary", "arbitrary"),
        name="diff_attn_cache" if cache is not None else "diff_attn",
    )(*args)


def _outc_kernel(x_ref, mod_ref, op_ref, os_ref, w_ref, g_ref, b_ref, o_ref, *, tiles_p, tiles_per_s, d, alpha):
    i = pl.program_id(0)
    grp = _group_of_tile(i, tiles_p, tiles_per_s)
    gate = _mod_chunk(mod_ref, grp, 2, d)
    tm = x_ref.shape[0]
    for rows in (slice(0, tm // 2), slice(tm // 2, tm)):
        o_in = jnp.where(i < tiles_p, op_ref[rows, :], os_ref[rows, :])
        out = jnp.dot(o_in, w_ref[...], preferred_element_type=F32)
        o_ref[rows, :] = _post_residual(x_ref[rows, :], out, gate, g_ref[...], b_ref[...], alpha)


def _outc(x, mod_l, o_p, o_s, w_out, ln_g, ln_b, tm, tiles_p, tiles_per_s, alpha):
    t, d = x.shape
    kin = o_p.shape[1]
    kern = functools.partial(_outc_kernel, tiles_p=tiles_p, tiles_per_s=tiles_per_s, d=d, alpha=alpha)
    ln_g, ln_b = ln_g.reshape(1, d), ln_b.reshape(1, d)
    full = lambda a: pl.BlockSpec(a.shape, lambda i: (0,) * a.ndim)
    return pl.pallas_call(
        kern,
        grid=(t // tm,),
        in_specs=[pl.BlockSpec((tm, d), lambda i: (i, 0)), full(mod_l)] + _pair_specs(tm, kin, tiles_p)
        + [full(w_out), full(ln_g), full(ln_b)],
        out_specs=pl.BlockSpec((tm, d), lambda i: (i, 0)),
        out_shape=jax.ShapeDtypeStruct((t, d), F32),
        compiler_params=_cparams("arbitrary"),
        name="outproj_c",
    )(x, mod_l, o_p, o_s, w_out, ln_g, ln_b)


class _Tiles(NamedTuple):
    rows: int
    rows_wide: int
    hidden: int
    s5_tokens: int
    conv_chunk: int
    attn_q: int
    attn_blocks: int


def _plan_tiles(tp, ts, lp, ls):
    tiles = _Tiles(rows=256, rows_wide=512, hidden=1024, s5_tokens=min(4096, tp, ts), conv_chunk=min(lp, 256),
                   attn_q=min(ls, 512), attn_blocks=2)
    for tm in (tiles.rows, tiles.rows_wide):
        assert tp % tm == 0 and ls % tm == 0
    tok = tiles.s5_tokens
    assert tok % lp == 0 and tok % ls == 0 and tp % tok == 0 and ts % tok == 0 and tp % ls == 0
    return tiles


def kernel(x_prompt, x_sample, state_s5_re, state_s5_im, cache_k, cache_v, c, c_ctx, w_mod, b_mod, ln_g, ln_b, w_in_ab, w_dw, b_dw, conv_ln_g, conv_ln_b, s5_lambda_re, s5_lambda_im, s5_log_dt, s5_b_re, s5_b_im, s5_c_re, s5_c_im, s5_d, w_glu, w_out_ab, w_qkv, lam_q1, lam_k1, lam_q2, lam_k2, subln_g, w_out_c, w_ff1, w_ff2):
    bp, lp, d = x_prompt.shape
    bs, ls, _ = x_sample.shape
    depth = w_mod.shape[0]
    tp, ts = bp * lp, bs * ls
    alpha = (2 * depth) ** 0.25
    assert 1 + bs <= MOD_ROWS

    tiles = _plan_tiles(tp, ts, lp, ls)
    tm, tm_mlp, tf = tiles.rows, tiles.rows_wide, tiles.hidden

    xp, xs = x_prompt.reshape(tp, d), x_sample.reshape(ts, d)
    cvec = jnp.zeros((MOD_ROWS, d), F32).at[0].set(c_ctx).at[1:1 + bs].set(c)
    mod = _modvec(cvec, w_mod, b_mod)

    g_ssm, n_ssm, p_ssm = s5_b_re.shape[2:]
    dk = lam_q1.shape[-1]
    dv = subln_g.shape[-1]
    n_heads = w_out_c.shape[1] // dv

    w_ff1_bf, w_ff2_bf = w_ff1.astype(BF16), w_ff2.astype(BF16)
    s_re, s_im, k_list, v_list = [], [], [], []
    for l in range(depth):
        mod_l = mod[l]
        if l % 2 == 0:
            e = l // 2
            if l > 0:
                xp, xs = x[:tp], x[tp:]
            ug, u = _inproj(xp, xs, mod_l, w_in_ab[e].astype(BF16), tm_mlp, tp // tm_mlp, ls // tm_mlp)
            y_conv = _conv_module(ug, w_dw[e], b_dw[e], conv_ln_g[e], conv_ln_b[e], tiles.conv_chunk, lp, ls, tp)
            prep = _s5_prep(s5_lambda_re[e], s5_lambda_im[e], s5_log_dt[e],
                            s5_b_re[e], s5_b_im[e], s5_c_re[e], s5_c_im[e])
            a_re, a_im = prep[5], prep[6]
            gn = g_ssm * n_ssm
            nat = lambda a: a.reshape(2, g_ssm, p_ssm, n_ssm)[:, :, 0].reshape(2, gn)
            a4 = jnp.stack([nat(a_re)[0], nat(a_im)[0], nat(a_re)[1], nat(a_im)[1]])
            tile_k, tile_b = _s5_tiles(n_ssm, p_ssm)
            tok = tiles.s5_tokens
            seq_p, seq_s, tiles_p5 = tok // lp, tok // ls, tp // tok
            ms = max(seq_p, seq_s)
            st = lambda a, dr: jnp.pad(a[:, e, dr].reshape(ts // tok, seq_s, gn), ((0, 0), (0, ms - seq_s), (0, 0)))
            h0_s = jnp.stack([st(state_s5_re, 0), st(state_s5_im, 0), st(state_s5_re, 1), st(state_s5_im, 1)], axis=1)
            h0 = jnp.concatenate([jnp.zeros((tiles_p5, 4, ms, gn), F32), h0_s], axis=0)
            y_scan, hf = _s5_chunked(u, h0, prep[:5], tile_k, tile_b, a4, tok=tok,
                                     geoms=((seq_p, lp // S5_T), (seq_s, ls // S5_T)), n_tiles_p=tiles_p5,
                                     n=n_ssm, p=p_ssm)
            hf = hf[:tiles_p5, :, :seq_p].transpose(1, 0, 2, 3).reshape(2, 2, bp, g_ssm, n_ssm)
            s_re.append(hf[:, 0].transpose(1, 0, 2, 3))
            s_im.append(hf[:, 1].transpose(1, 0, 2, 3))
            x = _outab(xp, xs, mod_l, y_conv, y_scan, u, s5_d[e], w_glu[e].astype(BF16),
                       w_out_ab[e].astype(BF16), ln_g[l, 0], ln_b[l, 0], tm_mlp, tp // tm_mlp, ls // tm_mlp, alpha)
        else:
            o_i = l // 2
            lam_init = 0.8 - 0.6 * math.exp(-0.3 * l)
            cos_t, sin_t = _rope_tables(ls, dk)
            scale, fold_scale = _softmax_scale(dk)
            q, k, v, k_new, v_new = _qkv(x, mod_l, w_qkv[o_i].astype(BF16), cos_t, sin_t, tm, tp // tm, ls // tm,
                                     dk, dv, bp, lp, scale if fold_scale else 1.0)
            lamv = jnp.stack([lam_q1[o_i], lam_k1[o_i], lam_q2[o_i], lam_k2[o_i]])
            geo = dict(n_heads=n_heads, dk=dk, dv=dv, lam_init=lam_init)
            o_p = _attention(q, k, v, lamv, subln_g[o_i], row0=0, n_seq=bp, lq=lp, tq=lp,
                             hp=n_heads * dk // LANES, **geo)
            o_s = _attention(q, k, v, lamv, subln_g[o_i], row0=tp, n_seq=bs, lq=ls,
                             tq=tiles.attn_q, hp=tiles.attn_blocks,
                             cache=(cache_k, cache_v, o_i), **geo)
            k_list.append(k_new)
            v_list.append(v_new)
            x = _outc(x, mod_l, o_p, o_s, w_out_c[o_i].astype(BF16), ln_g[l, 0], ln_b[l, 0],
                      tm_mlp, tp // tm_mlp, ls // tm_mlp, alpha)
        res = _mlp(x, mod_l, w_ff1_bf, w_ff2_bf, l, ln_g[l, 1], ln_b[l, 1],
                   tm_mlp, tf, tp // tm_mlp, ls // tm_mlp, alpha, split=l == depth - 1)
        x = res[0]

    return (res[0].reshape(bp, lp, d), res[1].reshape(bs, ls, d),
            jnp.stack(s_re, axis=1), jnp.stack(s_im, axis=1),
            jnp.stack(k_list, axis=1), jnp.stack(v_list, axis=1))
```

```python
import functools
import math
from typing import NamedTuple

import jax
import jax.numpy as jnp
import numpy as np
from jax import lax
from jax.experimental import pallas as pl
from jax.experimental.pallas import tpu as pltpu

F32 = jnp.float32
BF16 = jnp.bfloat16

LN_EPS = 1e-5
ROPE_BASE = 10000.0
LATENT_GRID_W = 64
MOD_ROWS = 8
V7X_VMEM_LIMIT = 56 * 1024 * 1024
LANES = 128
SUBLANES = 8


def _cparams(*sem):
    return pltpu.CompilerParams(dimension_semantics=sem, vmem_limit_bytes=V7X_VMEM_LIMIT)


def _layer_norm(z, g, b):
    mu = jnp.mean(z, axis=-1, keepdims=True)
    zc = z - mu
    var = jnp.mean(jnp.square(zc), axis=-1, keepdims=True)
    return zc * lax.rsqrt(var + LN_EPS) * g + b


def _group_of_tile(i, tiles_p, tiles_per_s):
    return jnp.where(i < tiles_p, 0, 1 + jnp.maximum(i - tiles_p, 0) // tiles_per_s)


def _mod_chunk(mod_ref, g, k, d):
    return mod_ref[pl.ds(g, 1), k * d:(k + 1) * d]


def _pair_specs(tm, width, tiles_p):
    return [pl.BlockSpec((tm, width), lambda i: (jnp.minimum(i, tiles_p - 1), 0)),
            pl.BlockSpec((tm, width), lambda i: (jnp.maximum(i - tiles_p, 0), 0))]


def _pair_rows(i, tiles_p, p_ref, s_ref):
    return jnp.where(i < tiles_p, p_ref[...], s_ref[...])


def _modvec_kernel(cv_ref, w_ref, b_ref, o_ref):
    cv = cv_ref[...]
    s = (cv * jax.nn.sigmoid(cv)).astype(BF16)
    o_ref[0] = jnp.dot(s, w_ref[0].astype(BF16), preferred_element_type=F32) + b_ref[0]


def _modvec(cvec, w_mod, b_mod, tn=1024):
    depth, d, n = w_mod.shape
    return pl.pallas_call(
        _modvec_kernel,
        grid=(depth, n // tn),
        in_specs=[pl.BlockSpec((MOD_ROWS, d), lambda l, j: (0, 0)),
                  pl.BlockSpec((1, d, tn), lambda l, j: (l, 0, j)),
                  pl.BlockSpec((1, 1, tn), lambda l, j: (l, 0, j))],
        out_specs=pl.BlockSpec((1, MOD_ROWS, tn), lambda l, j: (l, 0, j)),
        out_shape=jax.ShapeDtypeStruct((depth, MOD_ROWS, n), F32),
        compiler_params=_cparams("arbitrary", "arbitrary"),
        name="modvec",
    )(cvec, w_mod, b_mod.reshape(depth, 1, n))


def _inproj_kernel(xp_ref, xs_ref, mod_ref, w_ref, ug_ref, u_ref, *, tiles_p, tiles_per_s, d, c):
    i = pl.program_id(0)
    g = _group_of_tile(i, tiles_p, tiles_per_s)
    shift = _mod_chunk(mod_ref, g, 0, d)
    scale = _mod_chunk(mod_ref, g, 1, d)
    h = (_pair_rows(i, tiles_p, xp_ref, xs_ref) * (1 + scale) + shift).astype(BF16)
    a_val = jnp.dot(h, w_ref[:, 0:c], preferred_element_type=F32)
    a_gate = jnp.dot(h, w_ref[:, c:2 * c], preferred_element_type=F32)
    ug_ref[...] = a_val * jax.nn.sigmoid(a_gate)
    u_ref[...] = jnp.dot(h, w_ref[:, 2 * c:3 * c], preferred_element_type=F32)


def _inproj(xp, xs, mod_l, w_in, tm, tiles_p, tiles_per_s):
    t, d = xp.shape[0] + xs.shape[0], xp.shape[1]
    c = w_in.shape[1] // 3
    kern = functools.partial(_inproj_kernel, tiles_p=tiles_p, tiles_per_s=tiles_per_s, d=d, c=c)
    return pl.pallas_call(
        kern,
        grid=(t // tm,),
        in_specs=_pair_specs(tm, d, tiles_p) + [pl.BlockSpec(mod_l.shape, lambda i: (0, 0)),
                                                pl.BlockSpec(w_in.shape, lambda i: (0, 0))],
        out_specs=[pl.BlockSpec((tm, c), lambda i: (i, 0)),
                   pl.BlockSpec((tm, c), lambda i: (i, 0))],
        out_shape=[jax.ShapeDtypeStruct((t, c), F32), jax.ShapeDtypeStruct((t, c), F32)],
        compiler_params=_cparams("arbitrary"),
        name="inproj",
    )(xp, xs, mod_l, w_in)


CONV_HALO = 16
CONV_ROWS = 64
CONV_COLS = 128


def _conv_kernel(prev_ref, cur_ref, next_ref, w_ref, b_ref, g_ref, beta_ref, o_ref, pad_scr, sh_scr, conv_scr,
                 *, chunks_p, chunks_s, n_chunks_p, width, lc, c):
    i = pl.program_id(0)
    in_p = i < n_chunks_p
    k = jnp.where(in_p, i % chunks_p, jnp.maximum(i - n_chunks_p, 0) % chunks_s)
    last = jnp.where(in_p, chunks_p - 1, chunks_s - 1)
    has_prev = (k > 0).astype(F32)
    has_next = (k < last).astype(F32)
    pad_scr[0:CONV_HALO, :] = prev_ref[...] * has_prev
    pad_scr[CONV_HALO:CONV_HALO + lc, :] = cur_ref[...]
    pad_scr[CONV_HALO + lc:2 * CONV_HALO + lc, :] = next_ref[...] * has_next
    off = CONV_HALO - width // 2
    span = lc + CONV_HALO + SUBLANES
    for sft in range(SUBLANES):
        sh_scr[sft] = pad_scr[sft:sft + span, :]

    for r0 in range(0, lc, CONV_ROWS):
        for cb in range(c // CONV_COLS):
            cs = slice(cb * CONV_COLS, (cb + 1) * CONV_COLS)
            acc = jnp.zeros((CONV_ROWS, CONV_COLS), F32)
            for kk in range(width):
                whole, sft = divmod(kk + off, SUBLANES)
                base = r0 + whole * SUBLANES
                acc = acc + sh_scr[sft, base:base + CONV_ROWS, cs] * w_ref[kk:kk + 1, cs]
            conv_scr[r0:r0 + CONV_ROWS, cs] = acc + b_ref[:, cs]
    y = _layer_norm(conv_scr[...], g_ref[...], beta_ref[...])
    o_ref[...] = (y * jax.nn.sigmoid(y)).astype(o_ref.dtype)


def _conv_module(ug, w_dw, b_dw, ln_g, ln_b, lc, lp, ls, tp):
    t, c = ug.shape
    width = w_dw.shape[0]
    assert width // 2 < CONV_HALO and lc % CONV_HALO == 0 and lp % lc == 0 and ls % lc == 0
    hb = lc // CONV_HALO
    n_halo_blocks = t // CONV_HALO
    kern = functools.partial(_conv_kernel, chunks_p=lp // lc, chunks_s=ls // lc, n_chunks_p=tp // lc,
                             width=width, lc=lc, c=c)
    vec = lambda a: a.reshape(1, c)
    return pl.pallas_call(
        kern,
        grid=(t // lc,),
        in_specs=[pl.BlockSpec((CONV_HALO, c), lambda i: (jnp.maximum(i * hb - 1, 0), 0)),
                  pl.BlockSpec((lc, c), lambda i: (i, 0)),
                  pl.BlockSpec((CONV_HALO, c), lambda i: (jnp.minimum((i + 1) * hb, n_halo_blocks - 1), 0)),
                  pl.BlockSpec((width, c), lambda i: (0, 0)),
                  pl.BlockSpec((1, c), lambda i: (0, 0)),
                  pl.BlockSpec((1, c), lambda i: (0, 0)),
                  pl.BlockSpec((1, c), lambda i: (0, 0))],
        out_specs=pl.BlockSpec((lc, c), lambda i: (i, 0)),
        out_shape=jax.ShapeDtypeStruct((t, c), BF16),
        scratch_shapes=[pltpu.VMEM((lc + 2 * CONV_HALO, c), F32),
                        pltpu.VMEM((SUBLANES, lc + CONV_HALO + SUBLANES, c), F32), pltpu.VMEM((lc, c), F32)],
        compiler_params=_cparams("arbitrary"),
        name="conv_module",
    )(ug, ug, ug, w_dw, vec(b_dw), vec(ln_g), vec(ln_b))


S5_T = 16
S5_GB = 8


def _s5_prep_kernel(bt_re_ref, bt_im_ref, la_re_ref, la_im_ref, dta_ref, ct_re_ref, ct_im_ref, rep_ref, lb_re_ref,
                    lb_im_ref, dtb_ref, be_re_ref, be_im_ref, cs_re_ref, cs_ni_ref, kc_ref, a_re_ref, a_im_ref, *, n, p):
    t = S5_T
    fwd = pl.program_id(0) == 0

    lam_re, lam_im = la_re_ref[0, 0], la_im_ref[0, 0]
    dt = jnp.exp(dta_ref[0, 0])
    mag = jnp.exp(lam_re * dt)
    ar, ai = mag * jnp.cos(lam_im * dt), mag * jnp.sin(lam_im * dt)
    den = jnp.square(lam_re) + jnp.square(lam_im)
    coef_re = ((ar - 1) * lam_re + ai * lam_im) / den
    coef_im = (ai * lam_re - (ar - 1) * lam_im) / den
    b_re, b_im = bt_re_ref[0, 0], bt_im_ref[0, 0]
    bb_re = coef_re * b_re - coef_im * b_im
    bb_im = coef_re * b_im + coef_im * b_re
    rows = bb_re.shape[0]
    pw = []
    wr, wi = bb_re, bb_im
    pr, pi = jnp.ones_like(ar), jnp.zeros_like(ar)
    for k in range(t):
        pw.append((wr, wi))
        wr, wi = wr * ar - wi * ai, wr * ai + wi * ar
        pr, pi = pr * ar - pi * ai, pr * ai + pi * ar
    a_re_ref[0, 0] = pr
    a_im_ref[0, 0] = pi
    for j in range(t):
        be_re_ref[0, 0, j * rows:(j + 1) * rows, :] = jnp.where(fwd, pw[t - 1 - j][0], pw[j][0])
        be_im_ref[0, 0, j * rows:(j + 1) * rows, :] = jnp.where(fwd, pw[t - 1 - j][1], pw[j][1])

    hi = lax.Precision.HIGHEST
    ct_re = jnp.dot(ct_re_ref[0, 0], rep_ref[...], precision=hi, preferred_element_type=F32)
    ct_im = jnp.dot(ct_im_ref[0, 0], rep_ref[...], precision=hi, preferred_element_type=F32)
    shp = ct_re.shape
    lam_re, lam_im = lb_re_ref[0, 0], lb_im_ref[0, 0]
    dt = jnp.exp(dtb_ref[0, 0])
    mag = jnp.exp(lam_re * dt)
    sq_re = jnp.broadcast_to(mag * jnp.cos(lam_im * dt), shp)
    sq_im = jnp.broadcast_to(mag * jnp.sin(lam_im * dt), shp)
    blk = lax.broadcasted_iota(jnp.int32, shp, 1) // p
    k1 = jnp.where(fwd, blk + 1, t - blk)
    qr, qi = jnp.ones(shp, F32), jnp.zeros(shp, F32)
    n_bits = t.bit_length()
    for bit in range(n_bits):
        take = ((k1 >> bit) & 1) == 1
        qr, qi = (jnp.where(take, qr * sq_re - qi * sq_im, qr), jnp.where(take, qr * sq_im + qi * sq_re, qi))
        if bit + 1 < n_bits:
            sq_re, sq_im = sq_re * sq_re - sq_im * sq_im, 2.0 * (sq_re * sq_im)
    v_re = ct_re * qr - ct_im * qi
    v_im = ct_re * qi + ct_im * qr
    cs_re_ref[0, 0] = v_re
    cs_ni_ref[0, 0] = -v_im
    lane = lax.broadcasted_iota(jnp.int32, shp, 1)
    w = shp[1]
    v0_re = jnp.where(fwd, jnp.where(lane < p, ct_re, pltpu.roll(v_re, p, 1)),
                      jnp.where(lane >= w - p, ct_re, pltpu.roll(v_re, w - p, 1)))
    v0_im = jnp.where(fwd, jnp.where(lane < p, ct_im, pltpu.roll(v_im, p, 1)),
                      jnp.where(lane >= w - p, ct_im, pltpu.roll(v_im, w - p, 1)))
    for g in range(S5_GB):
        ra, rb = slice(g * p, (g + 1) * p), slice(g * n, (g + 1) * n)
        kc_ref[0, 0, ra, :] = (jnp.dot(bb_re[ra], v0_re[rb], precision=hi, preferred_element_type=F32)
                               - jnp.dot(bb_im[ra], v0_im[rb], precision=hi, preferred_element_type=F32))


def _s5_prep(lam_re, lam_im, log_dt, b_re, b_im, c_re, c_im):
    _, g, n, p = b_re.shape
    t = S5_T
    assert S5_GB * p == LANES and g % S5_GB == 0
    nb = g // S5_GB
    ra, rb = S5_GB * p, S5_GB * n
    lay_a = lambda a: jnp.broadcast_to(a[:, :, None, :], (2, g, p, n)).reshape(2, nb, ra, n)
    lay_b = lambda a: a.reshape(2, nb, rb, 1)
    bt = lambda a: a.transpose(0, 1, 3, 2).reshape(2, nb, ra, n)
    ct = lambda a: a.transpose(0, 1, 3, 2).reshape(2, nb, rb, p)
    rep_i = jnp.asarray(np.tile(np.eye(p, dtype=np.float32), (1, t)))
    dt_g = jnp.broadcast_to(log_dt[:, :, None], (2, g, n))
    blk = lambda r, c: pl.BlockSpec((1, 1, r, c), lambda d, i: (d, i, 0, 0))
    shp = lambda r, c: jax.ShapeDtypeStruct((2, nb, r, c), F32)
    kern = functools.partial(_s5_prep_kernel, n=n, p=p)
    return pl.pallas_call(
        kern,
        grid=(2, nb),
        in_specs=[blk(ra, n)] * 5 + [blk(rb, p)] * 2 + [pl.BlockSpec((p, t * p), lambda d, i: (0, 0))] + [blk(rb, 1)] * 3,
        out_specs=[blk(t * ra, n), blk(t * ra, n), blk(rb, t * p), blk(rb, t * p), blk(ra, t * p), blk(ra, n), blk(ra, n)],
        out_shape=[shp(t * ra, n), shp(t * ra, n), shp(rb, t * p), shp(rb, t * p), shp(ra, t * p), shp(ra, n), shp(ra, n)],
        compiler_params=_cparams("arbitrary", "arbitrary"),
        name="s5_prep",
    )(bt(b_re), bt(b_im), lay_a(lam_re), lay_a(lam_im), lay_a(dt_g), ct(c_re), ct(c_im), rep_i,
      lay_b(lam_re), lay_b(lam_im), lay_b(dt_g))


def _s5_expand(src, tile, row_div, row_mod, lane_div, lane_mod, precision=None):
    full = jnp.dot(src, tile, precision=precision, preferred_element_type=F32)
    r = lax.broadcasted_iota(jnp.int32, full.shape, 0) // row_div % row_mod
    l = lax.broadcasted_iota(jnp.int32, full.shape, 1) // lane_div % lane_mod
    return jnp.where(r == l, full, 0.0)


def _s5_chunk_kernel(u_ref, be_re_ref, be_im_ref, cs_re_ref, cs_ni_ref, kc_ref, tk_ref, tb_ref, a_ref, h0_ref,
                     y_ref, hf_ref, m8_scr, be8_scr, cs8_scr, e_scr, *, geoms, n_tiles_p, n, p):
    t, gb = S5_T, S5_GB
    cw = gb * p
    sw = gb * n
    rows = geoms[0][0] * geoms[0][1]
    tile_i = pl.program_id(1)

    @pl.when(tile_i == 0)
    def _():
        tk, tb = tk_ref[...].astype(BF16), tb_ref[...].astype(BF16)
        step = 4 * cw
        for r0 in range(0, t * cw, step):
            for part, (ref, d) in enumerate(((be_re_ref, 0), (be_im_ref, 0), (be_re_ref, 1), (be_im_ref, 1))):
                be8_scr[r0:r0 + step, part * sw:(part + 1) * sw] = _s5_expand(
                    ref[d, 0, r0:r0 + step, :].astype(BF16), tb, p, gb, n, gb).astype(BF16)
        for part, (ref, d) in enumerate(((cs_re_ref, 0), (cs_ni_ref, 0), (cs_re_ref, 1), (cs_ni_ref, 1))):
            cs8_scr[part * sw:(part + 1) * sw, :] = _s5_expand(ref[d, 0].astype(BF16), tk, n, gb, p, gb).astype(BF16)
        hi = lax.Precision.HIGHEST
        bd_f = _s5_expand(kc_ref[0, 0], tk_ref[...], p, gb, p, gb, hi)
        bd_r = _s5_expand(kc_ref[1, 0], tk_ref[...], p, gb, p, gb, hi)
        tile_f = lambda k: bd_f[:, k * cw:(k + 1) * cw]
        tile_r = lambda k: bd_r[:, (t - 1 - k) * cw:(t - k) * cw]
        for j in range(t):
            for i in range(t):
                blk = tile_f(i - j) if i > j else tile_r(j - i) if i < j else tile_f(0) + tile_r(0)
                m8_scr[j * cw:(j + 1) * cw, i * cw:(i + 1) * cw] = blk.astype(BF16)

    x = u_ref[...].reshape(rows, t * cw).astype(BF16)
    e = jnp.dot(x, be8_scr[...], preferred_element_type=F32)
    n_slab = e.shape[1] // LANES
    per_part = sw // LANES
    for k in range(n_slab):
        e_scr[k] = e[:, k * LANES:(k + 1) * LANES]

    def scan(nseq, nc):
        loops = [list(range(per_part))] if nseq <= 16 else [[q] for q in range(per_part)]
        for prs in loops:
            coef = [[jnp.broadcast_to(a_ref[r:r + 1, q * LANES:(q + 1) * LANES], (nseq, LANES)) for r in range(4)]
                    for q in prs]
            init = tuple(tuple(h0_ref[0, r, 0:nseq, q * LANES:(q + 1) * LANES] for r in range(4)) for q in prs)

            def step(c, carry):
                out = []
                for idx, q in enumerate(prs):
                    sf_re, sf_im, sr_re, sr_im = carry[idx]
                    af_re, af_im, ar_re, ar_im = coef[idx]
                    at_f = pl.ds(c, nseq, stride=nc)
                    at_r = pl.ds(nc - 1 - c, nseq, stride=nc)
                    ef_re, ef_im = e_scr[q, at_f, :], e_scr[per_part + q, at_f, :]
                    er_re, er_im = e_scr[2 * per_part + q, at_r, :], e_scr[3 * per_part + q, at_r, :]
                    e_scr[q, at_f, :] = sf_re
                    e_scr[per_part + q, at_f, :] = sf_im
                    e_scr[2 * per_part + q, at_r, :] = sr_re
                    e_scr[3 * per_part + q, at_r, :] = sr_im
                    out.append((af_re * sf_re - af_im * sf_im + ef_re, af_re * sf_im + af_im * sf_re + ef_im,
                                ar_re * sr_re - ar_im * sr_im + er_re, ar_re * sr_im + ar_im * sr_re + er_im))
                return tuple(out)

            fin = lax.fori_loop(0, nc, step, init)
            for idx, q in enumerate(prs):
                for r in range(4):
                    hf_ref[0, r, 0:nseq, q * LANES:(q + 1) * LANES] = fin[idx][r]

    hf_ref[...] = jnp.zeros_like(hf_ref)

    @pl.when(tile_i < n_tiles_p)
    def _():
        scan(*geoms[0])

    @pl.when(tile_i >= n_tiles_p)
    def _():
        scan(*geoms[1])

    s = jnp.concatenate([e_scr[k] for k in range(n_slab)], axis=-1).astype(BF16)
    y = (jnp.dot(x, m8_scr[...], preferred_element_type=F32)
         + jnp.dot(s, cs8_scr[...], preferred_element_type=F32))
    y_ref[...] = y.reshape(rows * t, cw)


def _s5_chunked(u, h0, prep, tile_k, tile_b, a4, *, tok, geoms, n_tiles_p, n, p):
    be_re, be_im, cs_re, cs_ni, kc = prep
    t, gb = S5_T, S5_GB
    cw, sw = gb * p, gb * n
    nb = u.shape[1] // cw
    n_tiles = u.shape[0] // tok
    ms = h0.shape[2]
    both = lambda a: pl.BlockSpec((2, 1) + a.shape[2:], lambda b, i: (0, b, 0, 0), pipeline_mode=pl.Buffered(1))
    const = lambda a: pl.BlockSpec(a.shape, lambda b, i: (0, 0), pipeline_mode=pl.Buffered(1))
    kern = functools.partial(_s5_chunk_kernel, geoms=geoms, n_tiles_p=n_tiles_p, n=n, p=p)
    return pl.pallas_call(
        kern,
        grid=(nb, n_tiles),
        in_specs=[pl.BlockSpec((tok, cw), lambda b, i: (i, b)),
                  both(be_re), both(be_im), both(cs_re), both(cs_ni), both(kc), const(tile_k), const(tile_b),
                  pl.BlockSpec((4, sw), lambda b, i: (0, b)),
                  pl.BlockSpec((1, 4, ms, sw), lambda b, i: (i, 0, 0, b))],
        out_specs=[pl.BlockSpec((tok, cw), lambda b, i: (i, b)),
                   pl.BlockSpec((1, 4, ms, sw), lambda b, i: (i, 0, 0, b))],
        out_shape=[jax.ShapeDtypeStruct(u.shape, F32), jax.ShapeDtypeStruct(h0.shape, F32)],
        scratch_shapes=[pltpu.VMEM((t * cw, t * cw), BF16), pltpu.VMEM((t * cw, 4 * sw), BF16),
                        pltpu.VMEM((4 * sw, t * cw), BF16), pltpu.VMEM((4 * sw // LANES, tok // t, LANES), F32)],
        compiler_params=_cparams("arbitrary", "arbitrary"),
        name="s5_chunked",
    )(u, be_re, be_im, cs_re, cs_ni, kc, tile_k, tile_b, a4, h0)


def _s5_tiles(n, p):
    t, gb = S5_T, S5_GB
    eye = lambda k: np.eye(k, dtype=np.float32)
    tile_k = np.einsum("ab,pq->apbq", eye(t), eye(p))[:, :, :, None, :] * np.ones((1, 1, 1, gb, 1), np.float32)
    tile_b = eye(n)[:, None, :] * np.ones((1, gb, 1), np.float32)
    return jnp.asarray(tile_k.reshape(t * p, t * gb * p)), jnp.asarray(tile_b.reshape(n, gb * n))


def _post_residual(x, y, gate, g, b, alpha):
    return _layer_norm(alpha * x + gate * y, g, b)


def _outab_kernel(xp_ref, xs_ref, mod_ref, yc_ref, ys_ref, u_ref, dsk_ref, wglu_ref, wout_ref, g_ref, b_ref, o_ref,
                  *, tiles_p, tiles_per_s, d, c, alpha):
    i = pl.program_id(0)
    grp = _group_of_tile(i, tiles_p, tiles_per_s)
    y_s = ys_ref[...] + dsk_ref[...] * u_ref[...]
    y_s = jax.nn.gelu(y_s)
    z = jnp.dot(y_s.astype(BF16), wglu_ref[...], preferred_element_type=F32)
    y_ssm = y_s * jax.nn.sigmoid(z)
    out = (jnp.dot(yc_ref[...], wout_ref[0:c, :], preferred_element_type=F32)
           + jnp.dot(y_ssm.astype(BF16), wout_ref[c:2 * c, :], preferred_element_type=F32))
    gate = _mod_chunk(mod_ref, grp, 2, d)
    o_ref[...] = _post_residual(_pair_rows(i, tiles_p, xp_ref, xs_ref), out, gate, g_ref[...], b_ref[...], alpha)


def _outab(xp, xs, mod_l, y_conv, y_scan, u, d_skip, w_glu, w_out, ln_g, ln_b, tm, tiles_p, tiles_per_s, alpha):
    t, d = xp.shape[0] + xs.shape[0], xp.shape[1]
    c = u.shape[1]
    kern = functools.partial(_outab_kernel, tiles_p=tiles_p, tiles_per_s=tiles_per_s, d=d, c=c, alpha=alpha)
    row = lambda w: pl.BlockSpec((tm, w), lambda i: (i, 0))
    full = lambda a: pl.BlockSpec(a.shape, lambda i: (0,) * a.ndim)
    d_skip, ln_g, ln_b = d_skip.reshape(1, c), ln_g.reshape(1, d), ln_b.reshape(1, d)
    return pl.pallas_call(
        kern,
        grid=(t // tm,),
        in_specs=_pair_specs(tm, d, tiles_p) + [full(mod_l), row(c), row(c), row(c), full(d_skip), full(w_glu),
                                                full(w_out), full(ln_g), full(ln_b)],
        out_specs=row(d),
        out_shape=jax.ShapeDtypeStruct((t, d), F32),
        compiler_params=_cparams("arbitrary"),
        name="outproj_ab",
    )(xp, xs, mod_l, y_conv, y_scan, u, d_skip, w_glu, w_out, ln_g, ln_b)


def _mlp_kernel(x_ref, mod_ref, w1_ref, w2_ref, g_ref, b_ref, o_ref, h_scr, acc_scr,
                *, tile0, tiles_p, tiles_per_s, d, n_f, alpha):
    i = pl.program_id(0) + tile0
    f = pl.program_id(1)
    grp = _group_of_tile(i, tiles_p, tiles_per_s)
    tm = h_scr.shape[0]
    halves = [slice(0, tm // 2), slice(tm // 2, tm)]

    def ffn(rows):
        a = jnp.dot(h_scr[rows, :], w1_ref[...], preferred_element_type=F32)
        a = jnp.square(jnp.maximum(a, 0.0)).astype(BF16)
        return jnp.dot(a, w2_ref[...], preferred_element_type=F32)

    @pl.when(f == 0)
    def _():
        shift = _mod_chunk(mod_ref, grp, 3, d)
        scale = _mod_chunk(mod_ref, grp, 4, d)
        for rows in halves:
            h_scr[rows, :] = (x_ref[rows, :] * (1 + scale) + shift).astype(BF16)
            acc_scr[rows, :] = ffn(rows)

    @pl.when(jnp.logical_and(f > 0, f < n_f - 1))
    def _():
        acc_scr[...] += ffn(slice(None))

    @pl.when(f == n_f - 1)
    def _():
        gate = _mod_chunk(mod_ref, grp, 5, d)
        for rows in halves:
            o_ref[rows, :] = _post_residual(x_ref[rows, :], acc_scr[rows, :] + ffn(rows), gate, g_ref[...], b_ref[...],
                                            alpha)


def _mlp(x, mod_l, w1, w2, layer, ln_g, ln_b, tm, tf, tiles_p, tiles_per_s, alpha, tile0=0, n_tiles=None):
    t, d = x.shape
    n_f = w1.shape[2] // tf
    n_tiles = t // tm if n_tiles is None else n_tiles
    assert n_f >= 2
    kern = functools.partial(_mlp_kernel, tile0=tile0, tiles_p=tiles_p, tiles_per_s=tiles_per_s, d=d, n_f=n_f,
                             alpha=alpha)
    ln_g, ln_b = ln_g.reshape(1, d), ln_b.reshape(1, d)
    return pl.pallas_call(
        kern,
        grid=(n_tiles, n_f),
        in_specs=[pl.BlockSpec((tm, d), lambda i, f: (tile0 + i, 0)),
                  pl.BlockSpec(mod_l.shape, lambda i, f: (0, 0)),
                  pl.BlockSpec((None, d, tf), lambda i, f: (layer, 0, f)),
                  pl.BlockSpec((None, tf, d), lambda i, f: (layer, f, 0)),
                  pl.BlockSpec((1, d), lambda i, f: (0, 0)),
                  pl.BlockSpec((1, d), lambda i, f: (0, 0))],
        out_specs=pl.BlockSpec((tm, d), lambda i, f: (i, 0)),
        out_shape=jax.ShapeDtypeStruct((n_tiles * tm, d), F32),
        scratch_shapes=[pltpu.VMEM((tm, d), BF16), pltpu.VMEM((tm, d), F32)],
        compiler_params=_cparams("arbitrary", "arbitrary"),
        name="mlp",
    )(x, mod_l, w1, w2, ln_g, ln_b)


def _rope_tables(n_pos, dk):
    ax = dk // 2
    half = ax // 2
    freqs = ROPE_BASE ** (-jnp.arange(half, dtype=F32) / half)
    pos = jnp.arange(n_pos)
    row = (pos // LATENT_GRID_W).astype(F32)
    col = (pos % LATENT_GRID_W).astype(F32)
    ang_r, ang_c = row[:, None] * freqs, col[:, None] * freqs
    cos = jnp.concatenate([jnp.cos(ang_r)] * 2 + [jnp.cos(ang_c)] * 2, axis=-1)
    sin = jnp.concatenate([-jnp.sin(ang_r), jnp.sin(ang_r), -jnp.sin(ang_c), jnp.sin(ang_c)], axis=-1)
    rep = LANES // dk
    return jnp.tile(cos, (1, rep)), jnp.tile(sin, (1, rep))


def _qkv_kernel(x_ref, mod_ref, w_ref, cos_ref, sin_ref, q_ref, k_ref, v_ref, kc_ref, vc_ref,
                *, tiles_p, tiles_per_s, d, dk, dv, q_scale):
    i = pl.program_id(0)
    g = _group_of_tile(i, tiles_p, tiles_per_s)
    shift = _mod_chunk(mod_ref, g, 0, d)
    scale = _mod_chunk(mod_ref, g, 1, d)
    h = (x_ref[...] * (1 + scale) + shift).astype(BF16)
    tn = q_ref.shape[-1]
    quarter = dk // 4
    latent = i >= tiles_p
    cos, sin = cos_ref[...], sin_ref[...]
    lane = lax.broadcasted_iota(jnp.int32, cos.shape, 1)
    first = (lane % (2 * quarter)) < quarter

    def rotated(y, o_ref, out_scale):
        for cb in range(tn // LANES):
            yb = y[:, cb * LANES:(cb + 1) * LANES]
            partner = jnp.where(first, pltpu.roll(yb, LANES - quarter, 1), pltpu.roll(yb, quarter, 1))
            out = jnp.where(latent, yb * cos + partner * sin, yb)
            o_ref[:, cb * LANES:(cb + 1) * LANES] = (out * out_scale).astype(o_ref.dtype)

    rotated(jnp.dot(h, w_ref[:, 0:tn], preferred_element_type=F32), q_ref, q_scale)
    y_k = jnp.dot(h, w_ref[:, tn:2 * tn], preferred_element_type=F32)
    rotated(y_k, k_ref, 1.0)
    y_v = jnp.dot(h, w_ref[:, 2 * tn:3 * tn], preferred_element_type=F32)
    v_ref[...] = y_v.astype(v_ref.dtype)

    @pl.when(i < tiles_p)
    def _():
        n_heads = kc_ref.shape[2]
        for m in range(2):
            for hh in range(n_heads):
                c0 = (m * n_heads + hh) * dk
                kc_ref[0, m, hh] = y_k[:, c0:c0 + dk]
        for hh in range(n_heads):
            vc_ref[0, hh] = y_v[:, hh * dv:(hh + 1) * dv]


def _qkv(x, mod_l, w_qkv, cos_t, sin_t, tm, tiles_p, tiles_per_s, dk, dv, bp, lp, q_scale):
    t, d = x.shape
    n_out = w_qkv.shape[1]
    assert n_out % 3 == 0 and lp % tm == 0
    tn = n_out // 3
    n_heads = tn // dv
    per_seq = lp // tm
    kern = functools.partial(_qkv_kernel, tiles_p=tiles_p, tiles_per_s=tiles_per_s, d=d, dk=dk, dv=dv,
                             q_scale=q_scale)
    pos_blk = lambda i: (jnp.maximum(i - tiles_p, 0) % tiles_per_s, 0)
    c_tile = lambda i: jnp.minimum(i, tiles_p - 1)
    row = pl.BlockSpec((tm, tn), lambda i: (i, 0))
    return pl.pallas_call(
        kern,
        grid=(t // tm,),
        in_specs=[pl.BlockSpec((tm, d), lambda i: (i, 0)),
                  pl.BlockSpec(mod_l.shape, lambda i: (0, 0)),
                  pl.BlockSpec((d, n_out), lambda i: (0, 0), pipeline_mode=pl.Buffered(1)),
                  pl.BlockSpec((tm, LANES), pos_blk),
                  pl.BlockSpec((tm, LANES), pos_blk)],
        out_specs=[row, row, row,
                   pl.BlockSpec((1, 2, n_heads, tm, dk), lambda i: (c_tile(i) // per_seq, 0, 0, c_tile(i) % per_seq, 0)),
                   pl.BlockSpec((1, n_heads, tm, dv), lambda i: (c_tile(i) // per_seq, 0, c_tile(i) % per_seq, 0))],
        out_shape=[jax.ShapeDtypeStruct((t, tn), BF16)] * 3
        + [jax.ShapeDtypeStruct((bp, 2, n_heads, lp, dk), F32), jax.ShapeDtypeStruct((bp, n_heads, lp, dv), F32)],
        compiler_params=_cparams("arbitrary"),
        name="qkv_proj",
    )(x, mod_l, w_qkv, cos_t, sin_t)


def _attn_kernel(*refs, heads_step, dk, dv, scale, fold_scale, lam_init, has_cache):
    if has_cache:
        q1_ref, q2_ref, k1_ref, k2_ref, v_ref, ck_ref, cv_ref, lamv_ref, sg_ref, o_ref, s_scr, w_scr = refs
    else:
        q1_ref, q2_ref, k1_ref, k2_ref, v_ref, lamv_ref, sg_ref, o_ref, s_scr, w_scr = refs
        ck_ref = cv_ref = None
    lv = lamv_ref[...]
    lam = (jnp.exp(jnp.sum(lv[0:1] * lv[1:2], axis=-1, keepdims=True))
           - jnp.exp(jnp.sum(lv[2:3] * lv[3:4], axis=-1, keepdims=True)) + lam_init)
    nt = (((1,), (1,)), ((), ()))
    tn = (((0,), (0,)), ((), ()))
    per_blk = LANES // dk
    tq, lk = q1_ref.shape[0], k1_ref.shape[0]
    lane = lax.broadcasted_iota(jnp.int32, (tq, LANES), 1)

    for blk in range(heads_step // per_blk):
        bs = slice(blk * LANES, (blk + 1) * LANES)
        for m, (q_ref, k_ref) in enumerate(((q1_ref, k1_ref), (q2_ref, k2_ref))):
            kb, qb = k_ref[:, bs], q_ref[:, bs]
            for sub in range(per_blk):
                head = blk * per_blk + sub
                qm = jnp.where((lane >= sub * dk) & (lane < (sub + 1) * dk), qb, jnp.zeros_like(qb))
                s_scr[2 * head + m, 0:lk, :] = lax.dot_general(kb, qm, nt, preferred_element_type=F32)
                if has_cache:
                    qc = qb[:, sub * dk:(sub + 1) * dk]
                    s_scr[2 * head + m, lk:, :] = lax.dot_general(ck_ref[0, 0, m, head].astype(BF16), qc, nt,
                                                                  preferred_element_type=F32)
    dens = []
    for hm in range(2 * heads_step):
        s = s_scr[hm]
        if not fold_scale:
            s = s * scale
        e = jnp.exp(s - jnp.max(s, axis=0, keepdims=True))
        dens.append(jnp.sum(e, axis=0, keepdims=True))
        s_scr[hm] = e
    for head in range(heads_step):
        ratio = lam * dens[2 * head] / dens[2 * head + 1]
        w_scr[head] = (s_scr[2 * head] - s_scr[2 * head + 1] * ratio).astype(BF16)
    for head in range(heads_step):
        vs = slice(head * dv, (head + 1) * dv)
        o_t = lax.dot_general(v_ref[:, vs], w_scr[head, 0:lk, :], tn, preferred_element_type=F32)
        if has_cache:
            o_t = o_t + lax.dot_general(cv_ref[0, 0, head].astype(BF16), w_scr[head, lk:, :], tn,
                                        preferred_element_type=F32)
        o_t = o_t * (1.0 / dens[2 * head])
        o_t = o_t * lax.rsqrt(jnp.mean(jnp.square(o_t), axis=0, keepdims=True) + LN_EPS)
        o = o_t.T * sg_ref[...] * (1.0 - lam_init)
        o_ref[:, vs] = o.astype(o_ref.dtype)


def _softmax_scale(dk):
    scale = dk ** -0.5
    return scale, math.frexp(scale)[0] == 0.5


def _attention(q, k, v, lamv, subln_g, *, row0, n_seq, lq, tq, hp, n_heads, dk, dv, lam_init, cache=None):
    per_blk = LANES // dk
    heads_step = hp * per_blk
    n_hblk = n_heads // heads_step
    map2 = n_heads * dk // (hp * LANES)
    scale, fold_scale = _softmax_scale(dk)
    qb0, kb0 = row0 // tq, row0 // lq
    n_q = lq // tq
    q_spec = lambda off: pl.BlockSpec((tq, hp * LANES), lambda b, h, qi: (qb0 + b * n_q + qi, off + h))
    k_spec = lambda off: pl.BlockSpec((lq, hp * LANES), lambda b, h, qi: (kb0 + b, off + h))
    in_specs = [q_spec(0), q_spec(map2), k_spec(0), k_spec(map2),
                pl.BlockSpec((lq, heads_step * dv), lambda b, h, qi: (kb0 + b, h))]
    args = [q, q, k, k, v]
    if cache is not None:
        cache_k, cache_v, o_i = cache
        past = cache_k.shape[-2]
        in_specs += [pl.BlockSpec((1, 1, 2, heads_step, past, dk), lambda b, h, qi: (b, o_i, 0, h, 0, 0)),
                     pl.BlockSpec((1, 1, heads_step, past, dv), lambda b, h, qi: (b, o_i, h, 0, 0))]
        args += [cache_k, cache_v]
    in_specs += [pl.BlockSpec(lamv.shape, lambda b, h, qi: (0, 0)),
                 pl.BlockSpec((1, dv), lambda b, h, qi: (0, 0))]
    args += [lamv, subln_g.reshape(1, dv)]
    kern = functools.partial(_attn_kernel, heads_step=heads_step, dk=dk, dv=dv, scale=scale,
                             fold_scale=fold_scale, lam_init=lam_init, has_cache=cache is not None)
    lk_all = lq + (cache[0].shape[-2] if cache is not None else 0)
    scratch = [pltpu.VMEM((2 * heads_step, lk_all, tq), F32), pltpu.VMEM((heads_step, lk_all, tq), BF16)]
    return pl.pallas_call(
        kern,
        grid=(n_seq, n_hblk, n_q),
        in_specs=in_specs,
        scratch_shapes=scratch,
        out_specs=pl.BlockSpec((tq, heads_step * dv), lambda b, h, qi: (b * n_q + qi, h)),
        out_shape=jax.ShapeDtypeStruct((n_seq * lq, n_heads * dv), BF16),
        compiler_params=_cparams("arbitrary", "arbitrary", "arbitrary"),
        name="diff_attn_cache" if cache is not None else "diff_attn",
    )(*args)


def _outc_kernel(x_ref, mod_ref, op_ref, os_ref, w_ref, g_ref, b_ref, o_ref, *, tiles_p, tiles_per_s, d, alpha):
    i = pl.program_id(0)
    grp = _group_of_tile(i, tiles_p, tiles_per_s)
    gate = _mod_chunk(mod_ref, grp, 2, d)
    tm = x_ref.shape[0]
    for rows in (slice(0, tm // 2), slice(tm // 2, tm)):
        o_in = jnp.where(i < tiles_p, op_ref[rows, :], os_ref[rows, :])
        out = jnp.dot(o_in, w_ref[...], preferred_element_type=F32)
        o_ref[rows, :] = _post_residual(x_ref[rows, :], out, gate, g_ref[...], b_ref[...], alpha)


def _outc(x, mod_l, o_p, o_s, w_out, ln_g, ln_b, tm, tiles_p, tiles_per_s, alpha):
    t, d = x.shape
    kin = o_p.shape[1]
    kern = functools.partial(_outc_kernel, tiles_p=tiles_p, tiles_per_s=tiles_per_s, d=d, alpha=alpha)
    ln_g, ln_b = ln_g.reshape(1, d), ln_b.reshape(1, d)
    full = lambda a: pl.BlockSpec(a.shape, lambda i: (0,) * a.ndim)
    return pl.pallas_call(
        kern,
        grid=(t // tm,),
        in_specs=[pl.BlockSpec((tm, d), lambda i: (i, 0)), full(mod_l)] + _pair_specs(tm, kin, tiles_p)
        + [full(w_out), full(ln_g), full(ln_b)],
        out_specs=pl.BlockSpec((tm, d), lambda i: (i, 0)),
        out_shape=jax.ShapeDtypeStruct((t, d), F32),
        compiler_params=_cparams("arbitrary"),
        name="outproj_c",
    )(x, mod_l, o_p, o_s, w_out, ln_g, ln_b)


class _Tiles(NamedTuple):
    rows: int
    rows_wide: int
    hidden: int
    s5_tokens: int
    conv_chunk: int
    attn_q: int
    attn_blocks: int


def _plan_tiles(tp, ts, lp, ls):
    tiles = _Tiles(rows=256, rows_wide=512, hidden=1024, s5_tokens=min(4096, tp, ts), conv_chunk=min(lp, 256),
                   attn_q=min(ls, 512), attn_blocks=2)
    for tm in (tiles.rows, tiles.rows_wide):
        assert tp % tm == 0 and ls % tm == 0
    tok = tiles.s5_tokens
    assert tok % lp == 0 and tok % ls == 0 and tp % tok == 0 and ts % tok == 0 and tp % ls == 0
    return tiles


def kernel(x_prompt, x_sample, state_s5_re, state_s5_im, cache_k, cache_v, c, c_ctx, w_mod, b_mod, ln_g, ln_b, w_in_ab, w_dw, b_dw, conv_ln_g, conv_ln_b, s5_lambda_re, s5_lambda_im, s5_log_dt, s5_b_re, s5_b_im, s5_c_re, s5_c_im, s5_d, w_glu, w_out_ab, w_qkv, lam_q1, lam_k1, lam_q2, lam_k2, subln_g, w_out_c, w_ff1, w_ff2):
    bp, lp, d = x_prompt.shape
    bs, ls, _ = x_sample.shape
    depth = w_mod.shape[0]
    tp, ts = bp * lp, bs * ls
    alpha = (2 * depth) ** 0.25
    assert 1 + bs <= MOD_ROWS

    tiles = _plan_tiles(tp, ts, lp, ls)
    tm, tm_mlp, tf = tiles.rows, tiles.rows_wide, tiles.hidden

    xp, xs = x_prompt.reshape(tp, d), x_sample.reshape(ts, d)
    cvec = jnp.zeros((MOD_ROWS, d), F32).at[0].set(c_ctx).at[1:1 + bs].set(c)
    mod = _modvec(cvec, w_mod, b_mod)

    g_ssm, n_ssm, p_ssm = s5_b_re.shape[2:]
    dk = lam_q1.shape[-1]
    dv = subln_g.shape[-1]
    n_heads = w_out_c.shape[1] // dv

    w_ff1_bf, w_ff2_bf = w_ff1.astype(BF16), w_ff2.astype(BF16)
    s_re, s_im, k_list, v_list = [], [], [], []
    for l in range(depth):
        mod_l = mod[l]
        if l % 2 == 0:
            e = l // 2
            if l > 0:
                xp, xs = x[:tp], x[tp:]
            ug, u = _inproj(xp, xs, mod_l, w_in_ab[e].astype(BF16), tm, tp // tm, ls // tm)
            y_conv = _conv_module(ug, w_dw[e], b_dw[e], conv_ln_g[e], conv_ln_b[e], tiles.conv_chunk, lp, ls, tp)
            prep = _s5_prep(s5_lambda_re[e], s5_lambda_im[e], s5_log_dt[e],
                            s5_b_re[e], s5_b_im[e], s5_c_re[e], s5_c_im[e])
            a_re, a_im = prep[5], prep[6]
            gn = g_ssm * n_ssm
            nat = lambda a: a.reshape(2, g_ssm, p_ssm, n_ssm)[:, :, 0].reshape(2, gn)
            a4 = jnp.stack([nat(a_re)[0], nat(a_im)[0], nat(a_re)[1], nat(a_im)[1]])
            tile_k, tile_b = _s5_tiles(n_ssm, p_ssm)
            tok = tiles.s5_tokens
            seq_p, seq_s, tiles_p5 = tok // lp, tok // ls, tp // tok
            ms = max(seq_p, seq_s)
            st = lambda a, dr: jnp.pad(a[:, e, dr].reshape(ts // tok, seq_s, gn), ((0, 0), (0, ms - seq_s), (0, 0)))
            h0_s = jnp.stack([st(state_s5_re, 0), st(state_s5_im, 0), st(state_s5_re, 1), st(state_s5_im, 1)], axis=1)
            h0 = jnp.concatenate([jnp.zeros((tiles_p5, 4, ms, gn), F32), h0_s], axis=0)
            y_scan, hf = _s5_chunked(u, h0, prep[:5], tile_k, tile_b, a4, tok=tok,
                                     geoms=((seq_p, lp // S5_T), (seq_s, ls // S5_T)), n_tiles_p=tiles_p5,
                                     n=n_ssm, p=p_ssm)
            hf = hf[:tiles_p5, :, :seq_p].transpose(1, 0, 2, 3).reshape(2, 2, bp, g_ssm, n_ssm)
            s_re.append(hf[:, 0].transpose(1, 0, 2, 3))
            s_im.append(hf[:, 1].transpose(1, 0, 2, 3))
            x = _outab(xp, xs, mod_l, y_conv, y_scan, u, s5_d[e], w_glu[e].astype(BF16),
                       w_out_ab[e].astype(BF16), ln_g[l, 0], ln_b[l, 0], tm, tp // tm, ls // tm, alpha)
        else:
            o_i = l // 2
            lam_init = 0.8 - 0.6 * math.exp(-0.3 * l)
            cos_t, sin_t = _rope_tables(ls, dk)
            scale, fold_scale = _softmax_scale(dk)
            q, k, v, k_new, v_new = _qkv(x, mod_l, w_qkv[o_i].astype(BF16), cos_t, sin_t, tm, tp // tm, ls // tm,
                                     dk, dv, bp, lp, scale if fold_scale else 1.0)
            lamv = jnp.stack([lam_q1[o_i], lam_k1[o_i], lam_q2[o_i], lam_k2[o_i]])
            geo = dict(n_heads=n_heads, dk=dk, dv=dv, lam_init=lam_init)
            o_p = _attention(q, k, v, lamv, subln_g[o_i], row0=0, n_seq=bp, lq=lp, tq=lp,
                             hp=n_heads * dk // LANES, **geo)
            o_s = _attention(q, k, v, lamv, subln_g[o_i], row0=tp, n_seq=bs, lq=ls,
                             tq=tiles.attn_q, hp=tiles.attn_blocks,
                             cache=(cache_k, cache_v, o_i), **geo)
            k_list.append(k_new)
            v_list.append(v_new)
            x = _outc(x, mod_l, o_p, o_s, w_out_c[o_i].astype(BF16), ln_g[l, 0], ln_b[l, 0],
                      tm_mlp, tp // tm_mlp, ls // tm_mlp, alpha)
        mlp = functools.partial(_mlp, x, mod_l, w_ff1_bf, w_ff2_bf, l, ln_g[l, 1], ln_b[l, 1],
                                tm_mlp, tf, tp // tm_mlp, ls // tm_mlp, alpha)
        if l < depth - 1:
            x = mlp()
        else:
            y_p = mlp(tile0=0, n_tiles=tp // tm_mlp)
            y_s = mlp(tile0=tp // tm_mlp, n_tiles=ts // tm_mlp)

    return (y_p.reshape(bp, lp, d), y_s.reshape(bs, ls, d),
            jnp.stack(s_re, axis=1), jnp.stack(s_im, axis=1),
            jnp.stack(k_list, axis=1), jnp.stack(v_list, axis=1))
```

```python
import functools
import math
from typing import NamedTuple

import jax
import jax.numpy as jnp
import numpy as np
from jax import lax
from jax.experimental import pallas as pl
from jax.experimental.pallas import tpu as pltpu

F32 = jnp.float32
BF16 = jnp.bfloat16

LN_EPS = 1e-5
ROPE_BASE = 10000.0
LATENT_GRID_W = 64
MOD_ROWS = 8
V7X_VMEM_LIMIT = 56 * 1024 * 1024
LANES = 128
SUBLANES = 8


def _cparams(*sem):
    return pltpu.CompilerParams(dimension_semantics=sem, vmem_limit_bytes=V7X_VMEM_LIMIT)


def _layer_norm(z, g, b):
    mu = jnp.mean(z, axis=-1, keepdims=True)
    zc = z - mu
    var = jnp.mean(jnp.square(zc), axis=-1, keepdims=True)
    return zc * lax.rsqrt(var + LN_EPS) * g + b


def _group_of_tile(i, tiles_p, tiles_per_s):
    return jnp.where(i < tiles_p, 0, 1 + jnp.maximum(i - tiles_p, 0) // tiles_per_s)


def _mod_chunk(mod_ref, g, k, d):
    return mod_ref[pl.ds(g, 1), k * d:(k + 1) * d]


def _pair_specs(tm, width, tiles_p):
    return [pl.BlockSpec((tm, width), lambda i: (jnp.minimum(i, tiles_p - 1), 0)),
            pl.BlockSpec((tm, width), lambda i: (jnp.maximum(i - tiles_p, 0), 0))]


def _pair_rows(i, tiles_p, p_ref, s_ref):
    return jnp.where(i < tiles_p, p_ref[...], s_ref[...])


def _modvec_kernel(cv_ref, w_ref, b_ref, o_ref):
    cv = cv_ref[...]
    s = (cv * jax.nn.sigmoid(cv)).astype(BF16)
    o_ref[0] = jnp.dot(s, w_ref[0].astype(BF16), preferred_element_type=F32) + b_ref[0]


def _modvec(cvec, w_mod, b_mod, tn=1024):
    depth, d, n = w_mod.shape
    return pl.pallas_call(
        _modvec_kernel,
        grid=(depth, n // tn),
        in_specs=[pl.BlockSpec((MOD_ROWS, d), lambda l, j: (0, 0)),
                  pl.BlockSpec((1, d, tn), lambda l, j: (l, 0, j)),
                  pl.BlockSpec((1, 1, tn), lambda l, j: (l, 0, j))],
        out_specs=pl.BlockSpec((1, MOD_ROWS, tn), lambda l, j: (l, 0, j)),
        out_shape=jax.ShapeDtypeStruct((depth, MOD_ROWS, n), F32),
        compiler_params=_cparams("arbitrary", "arbitrary"),
        name="modvec",
    )(cvec, w_mod, b_mod.reshape(depth, 1, n))


def _inproj_kernel(xp_ref, xs_ref, mod_ref, w_ref, ug_ref, u_ref, *, tiles_p, tiles_per_s, d, c):
    i = pl.program_id(0)
    g = _group_of_tile(i, tiles_p, tiles_per_s)
    shift = _mod_chunk(mod_ref, g, 0, d)
    scale = _mod_chunk(mod_ref, g, 1, d)
    h = (_pair_rows(i, tiles_p, xp_ref, xs_ref) * (1 + scale) + shift).astype(BF16)
    a_val = jnp.dot(h, w_ref[:, 0:c], preferred_element_type=F32)
    a_gate = jnp.dot(h, w_ref[:, c:2 * c], preferred_element_type=F32)
    ug_ref[...] = a_val * jax.nn.sigmoid(a_gate)
    u_ref[...] = jnp.dot(h, w_ref[:, 2 * c:3 * c], preferred_element_type=F32)


def _inproj(xp, xs, mod_l, w_in, tm, tiles_p, tiles_per_s):
    t, d = xp.shape[0] + xs.shape[0], xp.shape[1]
    c = w_in.shape[1] // 3
    kern = functools.partial(_inproj_kernel, tiles_p=tiles_p, tiles_per_s=tiles_per_s, d=d, c=c)
    return pl.pallas_call(
        kern,
        grid=(t // tm,),
        in_specs=_pair_specs(tm, d, tiles_p) + [pl.BlockSpec(mod_l.shape, lambda i: (0, 0)),
                                                pl.BlockSpec(w_in.shape, lambda i: (0, 0))],
        out_specs=[pl.BlockSpec((tm, c), lambda i: (i, 0)),
                   pl.BlockSpec((tm, c), lambda i: (i, 0))],
        out_shape=[jax.ShapeDtypeStruct((t, c), F32), jax.ShapeDtypeStruct((t, c), F32)],
        compiler_params=_cparams("arbitrary"),
        name="inproj",
    )(xp, xs, mod_l, w_in)


CONV_HALO = 16
CONV_ROWS = 64
CONV_COLS = 128


def _conv_kernel(prev_ref, cur_ref, next_ref, w_ref, b_ref, g_ref, beta_ref, o_ref, pad_scr, sh_scr, conv_scr,
                 *, chunks_p, chunks_s, n_chunks_p, width, lc, c):
    i = pl.program_id(0)
    in_p = i < n_chunks_p
    k = jnp.where(in_p, i % chunks_p, jnp.maximum(i - n_chunks_p, 0) % chunks_s)
    last = jnp.where(in_p, chunks_p - 1, chunks_s - 1)
    has_prev = (k > 0).astype(F32)
    has_next = (k < last).astype(F32)
    pad_scr[0:CONV_HALO, :] = prev_ref[...] * has_prev
    pad_scr[CONV_HALO:CONV_HALO + lc, :] = cur_ref[...]
    pad_scr[CONV_HALO + lc:2 * CONV_HALO + lc, :] = next_ref[...] * has_next
    off = CONV_HALO - width // 2
    span = lc + CONV_HALO + SUBLANES
    for sft in range(SUBLANES):
        sh_scr[sft] = pad_scr[sft:sft + span, :]

    for r0 in range(0, lc, CONV_ROWS):
        for cb in range(c // CONV_COLS):
            cs = slice(cb * CONV_COLS, (cb + 1) * CONV_COLS)
            acc = jnp.zeros((CONV_ROWS, CONV_COLS), F32)
            for kk in range(width):
                whole, sft = divmod(kk + off, SUBLANES)
                base = r0 + whole * SUBLANES
                acc = acc + sh_scr[sft, base:base + CONV_ROWS, cs] * w_ref[kk:kk + 1, cs]
            conv_scr[r0:r0 + CONV_ROWS, cs] = acc + b_ref[:, cs]
    y = _layer_norm(conv_scr[...], g_ref[...], beta_ref[...])
    o_ref[...] = (y * jax.nn.sigmoid(y)).astype(o_ref.dtype)


def _conv_module(ug, w_dw, b_dw, ln_g, ln_b, lc, lp, ls, tp):
    t, c = ug.shape
    width = w_dw.shape[0]
    assert width // 2 < CONV_HALO and lc % CONV_HALO == 0 and lp % lc == 0 and ls % lc == 0
    hb = lc // CONV_HALO
    n_halo_blocks = t // CONV_HALO
    kern = functools.partial(_conv_kernel, chunks_p=lp // lc, chunks_s=ls // lc, n_chunks_p=tp // lc,
                             width=width, lc=lc, c=c)
    vec = lambda a: a.reshape(1, c)
    return pl.pallas_call(
        kern,
        grid=(t // lc,),
        in_specs=[pl.BlockSpec((CONV_HALO, c), lambda i: (jnp.maximum(i * hb - 1, 0), 0)),
                  pl.BlockSpec((lc, c), lambda i: (i, 0)),
                  pl.BlockSpec((CONV_HALO, c), lambda i: (jnp.minimum((i + 1) * hb, n_halo_blocks - 1), 0)),
                  pl.BlockSpec((width, c), lambda i: (0, 0)),
                  pl.BlockSpec((1, c), lambda i: (0, 0)),
                  pl.BlockSpec((1, c), lambda i: (0, 0)),
                  pl.BlockSpec((1, c), lambda i: (0, 0))],
        out_specs=pl.BlockSpec((lc, c), lambda i: (i, 0)),
        out_shape=jax.ShapeDtypeStruct((t, c), BF16),
        scratch_shapes=[pltpu.VMEM((lc + 2 * CONV_HALO, c), F32),
                        pltpu.VMEM((SUBLANES, lc + CONV_HALO + SUBLANES, c), F32), pltpu.VMEM((lc, c), F32)],
        compiler_params=_cparams("arbitrary"),
        name="conv_module",
    )(ug, ug, ug, w_dw, vec(b_dw), vec(ln_g), vec(ln_b))


S5_T = 16
S5_GB = 8


def _s5_prep_kernel(bt_re_ref, bt_im_ref, la_re_ref, la_im_ref, dta_ref, ct_re_ref, ct_im_ref, rep_ref, lb_re_ref,
                    lb_im_ref, dtb_ref, be_re_ref, be_im_ref, cs_re_ref, cs_ni_ref, kc_ref, a_re_ref, a_im_ref, *, n, p):
    t = S5_T
    fwd = pl.program_id(0) == 0

    lam_re, lam_im = la_re_ref[0, 0], la_im_ref[0, 0]
    dt = jnp.exp(dta_ref[0, 0])
    mag = jnp.exp(lam_re * dt)
    ar, ai = mag * jnp.cos(lam_im * dt), mag * jnp.sin(lam_im * dt)
    den = jnp.square(lam_re) + jnp.square(lam_im)
    coef_re = ((ar - 1) * lam_re + ai * lam_im) / den
    coef_im = (ai * lam_re - (ar - 1) * lam_im) / den
    b_re, b_im = bt_re_ref[0, 0], bt_im_ref[0, 0]
    bb_re = coef_re * b_re - coef_im * b_im
    bb_im = coef_re * b_im + coef_im * b_re
    rows = bb_re.shape[0]
    pw = []
    wr, wi = bb_re, bb_im
    pr, pi = jnp.ones_like(ar), jnp.zeros_like(ar)
    for k in range(t):
        pw.append((wr, wi))
        wr, wi = wr * ar - wi * ai, wr * ai + wi * ar
        pr, pi = pr * ar - pi * ai, pr * ai + pi * ar
    a_re_ref[0, 0] = pr
    a_im_ref[0, 0] = pi
    for j in range(t):
        be_re_ref[0, 0, j * rows:(j + 1) * rows, :] = jnp.where(fwd, pw[t - 1 - j][0], pw[j][0])
        be_im_ref[0, 0, j * rows:(j + 1) * rows, :] = jnp.where(fwd, pw[t - 1 - j][1], pw[j][1])

    hi = lax.Precision.HIGHEST
    ct_re = jnp.dot(ct_re_ref[0, 0], rep_ref[...], precision=hi, preferred_element_type=F32)
    ct_im = jnp.dot(ct_im_ref[0, 0], rep_ref[...], precision=hi, preferred_element_type=F32)
    shp = ct_re.shape
    lam_re, lam_im = lb_re_ref[0, 0], lb_im_ref[0, 0]
    dt = jnp.exp(dtb_ref[0, 0])
    mag = jnp.exp(lam_re * dt)
    sq_re = jnp.broadcast_to(mag * jnp.cos(lam_im * dt), shp)
    sq_im = jnp.broadcast_to(mag * jnp.sin(lam_im * dt), shp)
    blk = lax.broadcasted_iota(jnp.int32, shp, 1) // p
    k1 = jnp.where(fwd, blk + 1, t - blk)
    qr, qi = jnp.ones(shp, F32), jnp.zeros(shp, F32)
    n_bits = t.bit_length()
    for bit in range(n_bits):
        take = ((k1 >> bit) & 1) == 1
        qr, qi = (jnp.where(take, qr * sq_re - qi * sq_im, qr), jnp.where(take, qr * sq_im + qi * sq_re, qi))
        if bit + 1 < n_bits:
            sq_re, sq_im = sq_re * sq_re - sq_im * sq_im, 2.0 * (sq_re * sq_im)
    v_re = ct_re * qr - ct_im * qi
    v_im = ct_re * qi + ct_im * qr
    cs_re_ref[0, 0] = v_re
    cs_ni_ref[0, 0] = -v_im
    lane = lax.broadcasted_iota(jnp.int32, shp, 1)
    w = shp[1]
    v0_re = jnp.where(fwd, jnp.where(lane < p, ct_re, pltpu.roll(v_re, p, 1)),
                      jnp.where(lane >= w - p, ct_re, pltpu.roll(v_re, w - p, 1)))
    v0_im = jnp.where(fwd, jnp.where(lane < p, ct_im, pltpu.roll(v_im, p, 1)),
                      jnp.where(lane >= w - p, ct_im, pltpu.roll(v_im, w - p, 1)))
    for g in range(S5_GB):
        ra, rb = slice(g * p, (g + 1) * p), slice(g * n, (g + 1) * n)
        kc_ref[0, 0, ra, :] = (jnp.dot(bb_re[ra], v0_re[rb], precision=hi, preferred_element_type=F32)
                               - jnp.dot(bb_im[ra], v0_im[rb], precision=hi, preferred_element_type=F32))


def _s5_prep(lam_re, lam_im, log_dt, b_re, b_im, c_re, c_im):
    _, g, n, p = b_re.shape
    t = S5_T
    assert S5_GB * p == LANES and g % S5_GB == 0
    nb = g // S5_GB
    ra, rb = S5_GB * p, S5_GB * n
    lay_a = lambda a: jnp.broadcast_to(a[:, :, None, :], (2, g, p, n)).reshape(2, nb, ra, n)
    lay_b = lambda a: a.reshape(2, nb, rb, 1)
    bt = lambda a: a.transpose(0, 1, 3, 2).reshape(2, nb, ra, n)
    ct = lambda a: a.transpose(0, 1, 3, 2).reshape(2, nb, rb, p)
    rep_i = jnp.asarray(np.tile(np.eye(p, dtype=np.float32), (1, t)))
    dt_g = jnp.broadcast_to(log_dt[:, :, None], (2, g, n))
    blk = lambda r, c: pl.BlockSpec((1, 1, r, c), lambda d, i: (d, i, 0, 0))
    shp = lambda r, c: jax.ShapeDtypeStruct((2, nb, r, c), F32)
    kern = functools.partial(_s5_prep_kernel, n=n, p=p)
    return pl.pallas_call(
        kern,
        grid=(2, nb),
        in_specs=[blk(ra, n)] * 5 + [blk(rb, p)] * 2 + [pl.BlockSpec((p, t * p), lambda d, i: (0, 0))] + [blk(rb, 1)] * 3,
        out_specs=[blk(t * ra, n), blk(t * ra, n), blk(rb, t * p), blk(rb, t * p), blk(ra, t * p), blk(ra, n), blk(ra, n)],
        out_shape=[shp(t * ra, n), shp(t * ra, n), shp(rb, t * p), shp(rb, t * p), shp(ra, t * p), shp(ra, n), shp(ra, n)],
        compiler_params=_cparams("arbitrary", "arbitrary"),
        name="s5_prep",
    )(bt(b_re), bt(b_im), lay_a(lam_re), lay_a(lam_im), lay_a(dt_g), ct(c_re), ct(c_im), rep_i,
      lay_b(lam_re), lay_b(lam_im), lay_b(dt_g))


def _s5_expand(src, tile, row_div, row_mod, lane_div, lane_mod, precision=None):
    full = jnp.dot(src, tile, precision=precision, preferred_element_type=F32)
    r = lax.broadcasted_iota(jnp.int32, full.shape, 0) // row_div % row_mod
    l = lax.broadcasted_iota(jnp.int32, full.shape, 1) // lane_div % lane_mod
    return jnp.where(r == l, full, 0.0)


def _s5_chunk_kernel(u_ref, be_re_ref, be_im_ref, cs_re_ref, cs_ni_ref, kc_ref, tk_ref, tb_ref, a_ref, h0_ref,
                     y_ref, hf_ref, m8_scr, be8_scr, cs8_scr, e_scr, *, geoms, n_tiles_p, n, p):
    t, gb = S5_T, S5_GB
    cw = gb * p
    sw = gb * n
    rows = geoms[0][0] * geoms[0][1]
    tile_i = pl.program_id(1)

    @pl.when(tile_i == 0)
    def _():
        tk, tb = tk_ref[...].astype(BF16), tb_ref[...].astype(BF16)
        step = 4 * cw
        for r0 in range(0, t * cw, step):
            for part, (ref, d) in enumerate(((be_re_ref, 0), (be_im_ref, 0), (be_re_ref, 1), (be_im_ref, 1))):
                be8_scr[r0:r0 + step, part * sw:(part + 1) * sw] = _s5_expand(
                    ref[d, 0, r0:r0 + step, :].astype(BF16), tb, p, gb, n, gb).astype(BF16)
        for part, (ref, d) in enumerate(((cs_re_ref, 0), (cs_ni_ref, 0), (cs_re_ref, 1), (cs_ni_ref, 1))):
            cs8_scr[part * sw:(part + 1) * sw, :] = _s5_expand(ref[d, 0].astype(BF16), tk, n, gb, p, gb).astype(BF16)
        hi = lax.Precision.HIGHEST
        bd_f = _s5_expand(kc_ref[0, 0], tk_ref[...], p, gb, p, gb, hi)
        bd_r = _s5_expand(kc_ref[1, 0], tk_ref[...], p, gb, p, gb, hi)
        tile_f = lambda k: bd_f[:, k * cw:(k + 1) * cw]
        tile_r = lambda k: bd_r[:, (t - 1 - k) * cw:(t - k) * cw]
        for j in range(t):
            for i in range(t):
                blk = tile_f(i - j) if i > j else tile_r(j - i) if i < j else tile_f(0) + tile_r(0)
                m8_scr[j * cw:(j + 1) * cw, i * cw:(i + 1) * cw] = blk.astype(BF16)

    x = u_ref[...].reshape(rows, t * cw).astype(BF16)
    e = jnp.dot(x, be8_scr[...], preferred_element_type=F32)
    n_slab = e.shape[1] // LANES
    per_part = sw // LANES
    for k in range(n_slab):
        e_scr[k] = e[:, k * LANES:(k + 1) * LANES]

    def scan(nseq, nc):
        loops = [list(range(per_part))] if nseq <= 16 else [[q] for q in range(per_part)]
        for prs in loops:
            coef = [[jnp.broadcast_to(a_ref[r:r + 1, q * LANES:(q + 1) * LANES], (nseq, LANES)) for r in range(4)]
                    for q in prs]
            init = tuple(tuple(h0_ref[0, r, 0:nseq, q * LANES:(q + 1) * LANES] for r in range(4)) for q in prs)

            def step(c, carry):
                out = []
                for idx, q in enumerate(prs):
                    sf_re, sf_im, sr_re, sr_im = carry[idx]
                    af_re, af_im, ar_re, ar_im = coef[idx]
                    at_f = pl.ds(c, nseq, stride=nc)
                    at_r = pl.ds(nc - 1 - c, nseq, stride=nc)
                    ef_re, ef_im = e_scr[q, at_f, :], e_scr[per_part + q, at_f, :]
                    er_re, er_im = e_scr[2 * per_part + q, at_r, :], e_scr[3 * per_part + q, at_r, :]
                    e_scr[q, at_f, :] = sf_re
                    e_scr[per_part + q, at_f, :] = sf_im
                    e_scr[2 * per_part + q, at_r, :] = sr_re
                    e_scr[3 * per_part + q, at_r, :] = sr_im
                    out.append((af_re * sf_re - af_im * sf_im + ef_re, af_re * sf_im + af_im * sf_re + ef_im,
                                ar_re * sr_re - ar_im * sr_im + er_re, ar_re * sr_im + ar_im * sr_re + er_im))
                return tuple(out)

            fin = lax.fori_loop(0, nc, step, init)
            for idx, q in enumerate(prs):
                for r in range(4):
                    hf_ref[0, r, 0:nseq, q * LANES:(q + 1) * LANES] = fin[idx][r]

    hf_ref[...] = jnp.zeros_like(hf_ref)

    @pl.when(tile_i < n_tiles_p)
    def _():
        scan(*geoms[0])

    @pl.when(tile_i >= n_tiles_p)
    def _():
        scan(*geoms[1])

    s = jnp.concatenate([e_scr[k] for k in range(n_slab)], axis=-1).astype(BF16)
    y = (jnp.dot(x, m8_scr[...], preferred_element_type=F32)
         + jnp.dot(s, cs8_scr[...], preferred_element_type=F32))
    y_ref[...] = y.reshape(rows * t, cw)


def _s5_chunked(u, h0, prep, tile_k, tile_b, a4, *, tok, geoms, n_tiles_p, n, p):
    be_re, be_im, cs_re, cs_ni, kc = prep
    t, gb = S5_T, S5_GB
    cw, sw = gb * p, gb * n
    nb = u.shape[1] // cw
    n_tiles = u.shape[0] // tok
    ms = h0.shape[2]
    both = lambda a: pl.BlockSpec((2, 1) + a.shape[2:], lambda b, i: (0, b, 0, 0), pipeline_mode=pl.Buffered(1))
    const = lambda a: pl.BlockSpec(a.shape, lambda b, i: (0, 0), pipeline_mode=pl.Buffered(1))
    kern = functools.partial(_s5_chunk_kernel, geoms=geoms, n_tiles_p=n_tiles_p, n=n, p=p)
    return pl.pallas_call(
        kern,
        grid=(nb, n_tiles),
        in_specs=[pl.BlockSpec((tok, cw), lambda b, i: (i, b)),
                  both(be_re), both(be_im), both(cs_re), both(cs_ni), both(kc), const(tile_k), const(tile_b),
                  pl.BlockSpec((4, sw), lambda b, i: (0, b)),
                  pl.BlockSpec((1, 4, ms, sw), lambda b, i: (i, 0, 0, b))],
        out_specs=[pl.BlockSpec((tok, cw), lambda b, i: (i, b)),
                   pl.BlockSpec((1, 4, ms, sw), lambda b, i: (i, 0, 0, b))],
        out_shape=[jax.ShapeDtypeStruct(u.shape, F32), jax.ShapeDtypeStruct(h0.shape, F32)],
        scratch_shapes=[pltpu.VMEM((t * cw, t * cw), BF16), pltpu.VMEM((t * cw, 4 * sw), BF16),
                        pltpu.VMEM((4 * sw, t * cw), BF16), pltpu.VMEM((4 * sw // LANES, tok // t, LANES), F32)],
        compiler_params=_cparams("arbitrary", "arbitrary"),
        name="s5_chunked",
    )(u, be_re, be_im, cs_re, cs_ni, kc, tile_k, tile_b, a4, h0)


def _s5_tiles(n, p):
    t, gb = S5_T, S5_GB
    eye = lambda k: np.eye(k, dtype=np.float32)
    tile_k = np.einsum("ab,pq->apbq", eye(t), eye(p))[:, :, :, None, :] * np.ones((1, 1, 1, gb, 1), np.float32)
    tile_b = eye(n)[:, None, :] * np.ones((1, gb, 1), np.float32)
    return jnp.asarray(tile_k.reshape(t * p, t * gb * p)), jnp.asarray(tile_b.reshape(n, gb * n))


def _post_residual(x, y, gate, g, b, alpha):
    return _layer_norm(alpha * x + gate * y, g, b)


def _outab_kernel(xp_ref, xs_ref, mod_ref, yc_ref, ys_ref, u_ref, dsk_ref, wglu_ref, wout_ref, g_ref, b_ref, o_ref,
                  *, tiles_p, tiles_per_s, d, c, alpha):
    i = pl.program_id(0)
    grp = _group_of_tile(i, tiles_p, tiles_per_s)
    y_s = ys_ref[...] + dsk_ref[...] * u_ref[...]
    y_s = jax.nn.gelu(y_s)
    z = jnp.dot(y_s.astype(BF16), wglu_ref[...], preferred_element_type=F32)
    y_ssm = y_s * jax.nn.sigmoid(z)
    out = (jnp.dot(yc_ref[...], wout_ref[0:c, :], preferred_element_type=F32)
           + jnp.dot(y_ssm.astype(BF16), wout_ref[c:2 * c, :], preferred_element_type=F32))
    gate = _mod_chunk(mod_ref, grp, 2, d)
    o_ref[...] = _post_residual(_pair_rows(i, tiles_p, xp_ref, xs_ref), out, gate, g_ref[...], b_ref[...], alpha)


def _outab(xp, xs, mod_l, y_conv, y_scan, u, d_skip, w_glu, w_out, ln_g, ln_b, tm, tiles_p, tiles_per_s, alpha):
    t, d = xp.shape[0] + xs.shape[0], xp.shape[1]
    c = u.shape[1]
    kern = functools.partial(_outab_kernel, tiles_p=tiles_p, tiles_per_s=tiles_per_s, d=d, c=c, alpha=alpha)
    row = lambda w: pl.BlockSpec((tm, w), lambda i: (i, 0))
    full = lambda a: pl.BlockSpec(a.shape, lambda i: (0,) * a.ndim)
    d_skip, ln_g, ln_b = d_skip.reshape(1, c), ln_g.reshape(1, d), ln_b.reshape(1, d)
    return pl.pallas_call(
        kern,
        grid=(t // tm,),
        in_specs=_pair_specs(tm, d, tiles_p) + [full(mod_l), row(c), row(c), row(c), full(d_skip), full(w_glu),
                                                full(w_out), full(ln_g), full(ln_b)],
        out_specs=row(d),
        out_shape=jax.ShapeDtypeStruct((t, d), F32),
        compiler_params=_cparams("arbitrary"),
        name="outproj_ab",
    )(xp, xs, mod_l, y_conv, y_scan, u, d_skip, w_glu, w_out, ln_g, ln_b)


def _mlp_kernel(x_ref, mod_ref, w1_ref, w2_ref, g_ref, b_ref, o_ref, h_scr, acc_scr,
                *, tile0, tiles_p, tiles_per_s, d, n_f, alpha):
    i = pl.program_id(0) + tile0
    f = pl.program_id(1)
    grp = _group_of_tile(i, tiles_p, tiles_per_s)
    tm = h_scr.shape[0]
    halves = [slice(0, tm // 2), slice(tm // 2, tm)]

    def ffn(rows):
        a = jnp.dot(h_scr[rows, :], w1_ref[...], preferred_element_type=F32)
        a = jnp.square(jnp.maximum(a, 0.0)).astype(BF16)
        return jnp.dot(a, w2_ref[...], preferred_element_type=F32)

    @pl.when(f == 0)
    def _():
        shift = _mod_chunk(mod_ref, grp, 3, d)
        scale = _mod_chunk(mod_ref, grp, 4, d)
        for rows in halves:
            h_scr[rows, :] = (x_ref[rows, :] * (1 + scale) + shift).astype(BF16)
            acc_scr[rows, :] = ffn(rows)

    @pl.when(jnp.logical_and(f > 0, f < n_f - 1))
    def _():
        acc_scr[...] += ffn(slice(None))

    @pl.when(f == n_f - 1)
    def _():
        gate = _mod_chunk(mod_ref, grp, 5, d)
        for rows in halves:
            o_ref[rows, :] = _post_residual(x_ref[rows, :], acc_scr[rows, :] + ffn(rows), gate, g_ref[...], b_ref[...],
                                            alpha)


def _mlp(x, mod_l, w1, w2, layer, ln_g, ln_b, tm, tf, tiles_p, tiles_per_s, alpha, tile0=0, n_tiles=None):
    t, d = x.shape
    n_f = w1.shape[2] // tf
    n_tiles = t // tm if n_tiles is None else n_tiles
    assert n_f >= 2
    kern = functools.partial(_mlp_kernel, tile0=tile0, tiles_p=tiles_p, tiles_per_s=tiles_per_s, d=d, n_f=n_f,
                             alpha=alpha)
    ln_g, ln_b = ln_g.reshape(1, d), ln_b.reshape(1, d)
    return pl.pallas_call(
        kern,
        grid=(n_tiles, n_f),
        in_specs=[pl.BlockSpec((tm, d), lambda i, f: (tile0 + i, 0)),
                  pl.BlockSpec(mod_l.shape, lambda i, f: (0, 0)),
                  pl.BlockSpec((None, d, tf), lambda i, f: (layer, 0, f)),
                  pl.BlockSpec((None, tf, d), lambda i, f: (layer, f, 0)),
                  pl.BlockSpec((1, d), lambda i, f: (0, 0)),
                  pl.BlockSpec((1, d), lambda i, f: (0, 0))],
        out_specs=pl.BlockSpec((tm, d), lambda i, f: (i, 0)),
        out_shape=jax.ShapeDtypeStruct((n_tiles * tm, d), F32),
        scratch_shapes=[pltpu.VMEM((tm, d), BF16), pltpu.VMEM((tm, d), F32)],
        compiler_params=_cparams("arbitrary", "arbitrary"),
        name="mlp",
    )(x, mod_l, w1, w2, ln_g, ln_b)


def _rope_tables(n_pos, dk):
    ax = dk // 2
    half = ax // 2
    freqs = ROPE_BASE ** (-jnp.arange(half, dtype=F32) / half)
    pos = jnp.arange(n_pos)
    row = (pos // LATENT_GRID_W).astype(F32)
    col = (pos % LATENT_GRID_W).astype(F32)
    ang_r, ang_c = row[:, None] * freqs, col[:, None] * freqs
    cos = jnp.concatenate([jnp.cos(ang_r)] * 2 + [jnp.cos(ang_c)] * 2, axis=-1)
    sin = jnp.concatenate([-jnp.sin(ang_r), jnp.sin(ang_r), -jnp.sin(ang_c), jnp.sin(ang_c)], axis=-1)
    rep = LANES // dk
    return jnp.tile(cos, (1, rep)), jnp.tile(sin, (1, rep))


def _qkv_kernel(*refs, tile0, tiles_p, tiles_per_s, d, dk, dv, q_scale, latent):
    if latent:
        x_ref, mod_ref, w_ref, cos_ref, sin_ref, q_ref, k_ref, v_ref = refs
    else:
        x_ref, mod_ref, w_ref, q_ref, k_ref, v_ref, kc_ref, vc_ref = refs
    g = _group_of_tile(pl.program_id(0) + tile0, tiles_p, tiles_per_s)
    shift = _mod_chunk(mod_ref, g, 0, d)
    scale = _mod_chunk(mod_ref, g, 1, d)
    h = (x_ref[...] * (1 + scale) + shift).astype(BF16)
    tn = q_ref.shape[-1]
    quarter = dk // 4

    def emit(y, o_ref, out_scale):
        if not latent:
            o_ref[...] = (y * out_scale).astype(o_ref.dtype)
            return
        cos, sin = cos_ref[...], sin_ref[...]
        lane = lax.broadcasted_iota(jnp.int32, cos.shape, 1)
        first = (lane % (2 * quarter)) < quarter
        for cb in range(tn // LANES):
            yb = y[:, cb * LANES:(cb + 1) * LANES]
            partner = jnp.where(first, pltpu.roll(yb, LANES - quarter, 1), pltpu.roll(yb, quarter, 1))
            o_ref[:, cb * LANES:(cb + 1) * LANES] = ((yb * cos + partner * sin) * out_scale).astype(o_ref.dtype)

    emit(jnp.dot(h, w_ref[:, 0:tn], preferred_element_type=F32), q_ref, q_scale)
    y_k = jnp.dot(h, w_ref[:, tn:2 * tn], preferred_element_type=F32)
    emit(y_k, k_ref, 1.0)
    y_v = jnp.dot(h, w_ref[:, 2 * tn:3 * tn], preferred_element_type=F32)
    v_ref[...] = y_v.astype(v_ref.dtype)

    if not latent:
        n_heads = kc_ref.shape[2]
        for m in range(2):
            for hh in range(n_heads):
                c0 = (m * n_heads + hh) * dk
                kc_ref[0, m, hh] = y_k[:, c0:c0 + dk]
        for hh in range(n_heads):
            vc_ref[0, hh] = y_v[:, hh * dv:(hh + 1) * dv]


def _qkv(x, mod_l, w_qkv, tm, tiles_p, tiles_per_s, dk, dv, q_scale, *, rope=None, cache_shape=None):
    t, d = x.shape
    n_out = w_qkv.shape[1]
    assert n_out % 3 == 0 and (rope is None) != (cache_shape is None)
    tn = n_out // 3
    n_heads = tn // dv
    latent = rope is not None
    tile0 = tiles_p if latent else 0
    n_tiles = t // tm - tiles_p if latent else tiles_p
    kern = functools.partial(_qkv_kernel, tile0=tile0, tiles_p=tiles_p, tiles_per_s=tiles_per_s, d=d, dk=dk, dv=dv,
                             q_scale=q_scale, latent=latent)
    row = pl.BlockSpec((tm, tn), lambda i: (i, 0))
    in_specs = [pl.BlockSpec((tm, d), lambda i: (tile0 + i, 0)),
                pl.BlockSpec(mod_l.shape, lambda i: (0, 0)),
                pl.BlockSpec((d, n_out), lambda i: (0, 0), pipeline_mode=pl.Buffered(1))]
    args = [x, mod_l, w_qkv]
    out_specs = [row, row, row]
    out_shape = [jax.ShapeDtypeStruct((n_tiles * tm, tn), BF16)] * 3
    if latent:
        pos_blk = lambda i: (i % tiles_per_s, 0)
        in_specs += [pl.BlockSpec((tm, LANES), pos_blk), pl.BlockSpec((tm, LANES), pos_blk)]
        args += list(rope)
    else:
        bp, lp = cache_shape
        assert lp % tm == 0
        per_seq = lp // tm
        out_specs += [pl.BlockSpec((1, 2, n_heads, tm, dk), lambda i: (i // per_seq, 0, 0, i % per_seq, 0)),
                      pl.BlockSpec((1, n_heads, tm, dv), lambda i: (i // per_seq, 0, i % per_seq, 0))]
        out_shape += [jax.ShapeDtypeStruct((bp, 2, n_heads, lp, dk), F32),
                      jax.ShapeDtypeStruct((bp, n_heads, lp, dv), F32)]
    return pl.pallas_call(
        kern,
        grid=(n_tiles,),
        in_specs=in_specs,
        out_specs=out_specs,
        out_shape=out_shape,
        compiler_params=_cparams("arbitrary"),
        name="qkv_proj_latent" if latent else "qkv_proj",
    )(*args)


def _attn_kernel(*refs, heads_step, dk, dv, scale, fold_scale, lam_init, has_cache):
    if has_cache:
        q1_ref, q2_ref, k1_ref, k2_ref, v_ref, ck_ref, cv_ref, lamv_ref, sg_ref, o_ref, s_scr, w_scr = refs
    else:
        q1_ref, q2_ref, k1_ref, k2_ref, v_ref, lamv_ref, sg_ref, o_ref, s_scr, w_scr = refs
        ck_ref = cv_ref = None
    lv = lamv_ref[...]
    lam = (jnp.exp(jnp.sum(lv[0:1] * lv[1:2], axis=-1, keepdims=True))
           - jnp.exp(jnp.sum(lv[2:3] * lv[3:4], axis=-1, keepdims=True)) + lam_init)
    nt = (((1,), (1,)), ((), ()))
    tn = (((0,), (0,)), ((), ()))
    per_blk = LANES // dk
    tq, lk = q1_ref.shape[0], k1_ref.shape[0]
    lane = lax.broadcasted_iota(jnp.int32, (tq, LANES), 1)

    for blk in range(heads_step // per_blk):
        bs = slice(blk * LANES, (blk + 1) * LANES)
        for m, (q_ref, k_ref) in enumerate(((q1_ref, k1_ref), (q2_ref, k2_ref))):
            kb, qb = k_ref[:, bs], q_ref[:, bs]
            for sub in range(per_blk):
                head = blk * per_blk + sub
                qm = jnp.where((lane >= sub * dk) & (lane < (sub + 1) * dk), qb, jnp.zeros_like(qb))
                s_scr[2 * head + m, 0:lk, :] = lax.dot_general(kb, qm, nt, preferred_element_type=F32)
                if has_cache:
                    qc = qb[:, sub * dk:(sub + 1) * dk]
                    s_scr[2 * head + m, lk:, :] = lax.dot_general(ck_ref[0, 0, m, head].astype(BF16), qc, nt,
                                                                  preferred_element_type=F32)
    dens = []
    for hm in range(2 * heads_step):
        s = s_scr[hm]
        if not fold_scale:
            s = s * scale
        e = jnp.exp(s - jnp.max(s, axis=0, keepdims=True))
        dens.append(jnp.sum(e, axis=0, keepdims=True))
        s_scr[hm] = e
    for head in range(heads_step):
        ratio = lam * dens[2 * head] / dens[2 * head + 1]
        w_scr[head] = (s_scr[2 * head] - s_scr[2 * head + 1] * ratio).astype(BF16)
    for head in range(heads_step):
        vs = slice(head * dv, (head + 1) * dv)
        o_t = lax.dot_general(v_ref[:, vs], w_scr[head, 0:lk, :], tn, preferred_element_type=F32)
        if has_cache:
            o_t = o_t + lax.dot_general(cv_ref[0, 0, head].astype(BF16), w_scr[head, lk:, :], tn,
                                        preferred_element_type=F32)
        o_t = o_t * (1.0 / dens[2 * head])
        o_t = o_t * lax.rsqrt(jnp.mean(jnp.square(o_t), axis=0, keepdims=True) + LN_EPS)
        o = o_t.T * sg_ref[...] * (1.0 - lam_init)
        o_ref[:, vs] = o.astype(o_ref.dtype)


def _softmax_scale(dk):
    scale = dk ** -0.5
    return scale, math.frexp(scale)[0] == 0.5


def _attention(q, k, v, lamv, subln_g, *, row0, n_seq, lq, tq, hp, n_heads, dk, dv, lam_init, cache=None):
    per_blk = LANES // dk
    heads_step = hp * per_blk
    n_hblk = n_heads // heads_step
    map2 = n_heads * dk // (hp * LANES)
    scale, fold_scale = _softmax_scale(dk)
    qb0, kb0 = row0 // tq, row0 // lq
    n_q = lq // tq
    q_spec = lambda off: pl.BlockSpec((tq, hp * LANES), lambda b, h, qi: (qb0 + b * n_q + qi, off + h))
    k_spec = lambda off: pl.BlockSpec((lq, hp * LANES), lambda b, h, qi: (kb0 + b, off + h))
    in_specs = [q_spec(0), q_spec(map2), k_spec(0), k_spec(map2),
                pl.BlockSpec((lq, heads_step * dv), lambda b, h, qi: (kb0 + b, h))]
    args = [q, q, k, k, v]
    if cache is not None:
        cache_k, cache_v, o_i = cache
        past = cache_k.shape[-2]
        in_specs += [pl.BlockSpec((1, 1, 2, heads_step, past, dk), lambda b, h, qi: (b, o_i, 0, h, 0, 0)),
                     pl.BlockSpec((1, 1, heads_step, past, dv), lambda b, h, qi: (b, o_i, h, 0, 0))]
        args += [cache_k, cache_v]
    in_specs += [pl.BlockSpec(lamv.shape, lambda b, h, qi: (0, 0)),
                 pl.BlockSpec((1, dv), lambda b, h, qi: (0, 0))]
    args += [lamv, subln_g.reshape(1, dv)]
    kern = functools.partial(_attn_kernel, heads_step=heads_step, dk=dk, dv=dv, scale=scale,
                             fold_scale=fold_scale, lam_init=lam_init, has_cache=cache is not None)
    lk_all = lq + (cache[0].shape[-2] if cache is not None else 0)
    scratch = [pltpu.VMEM((2 * heads_step, lk_all, tq), F32), pltpu.VMEM((heads_step, lk_all, tq), BF16)]
    return pl.pallas_call(
        kern,
        grid=(n_seq, n_hblk, n_q),
        in_specs=in_specs,
        scratch_shapes=scratch,
        out_specs=pl.BlockSpec((tq, heads_step * dv), lambda b, h, qi: (b * n_q + qi, h)),
        out_shape=jax.ShapeDtypeStruct((n_seq * lq, n_heads * dv), BF16),
        compiler_params=_cparams("arbitrary", "arbitrary", "arbitrary"),
        name="diff_attn_cache" if cache is not None else "diff_attn",
    )(*args)


def _outc_kernel(x_ref, mod_ref, op_ref, os_ref, w_ref, g_ref, b_ref, o_ref, *, tiles_p, tiles_per_s, d, alpha):
    i = pl.program_id(0)
    grp = _group_of_tile(i, tiles_p, tiles_per_s)
    gate = _mod_chunk(mod_ref, grp, 2, d)
    tm = x_ref.shape[0]
    for rows in (slice(0, tm // 2), slice(tm // 2, tm)):
        o_in = jnp.where(i < tiles_p, op_ref[rows, :], os_ref[rows, :])
        out = jnp.dot(o_in, w_ref[...], preferred_element_type=F32)
        o_ref[rows, :] = _post_residual(x_ref[rows, :], out, gate, g_ref[...], b_ref[...], alpha)


def _outc(x, mod_l, o_p, o_s, w_out, ln_g, ln_b, tm, tiles_p, tiles_per_s, alpha):
    t, d = x.shape
    kin = o_p.shape[1]
    kern = functools.partial(_outc_kernel, tiles_p=tiles_p, tiles_per_s=tiles_per_s, d=d, alpha=alpha)
    ln_g, ln_b = ln_g.reshape(1, d), ln_b.reshape(1, d)
    full = lambda a: pl.BlockSpec(a.shape, lambda i: (0,) * a.ndim)
    return pl.pallas_call(
        kern,
        grid=(t // tm,),
        in_specs=[pl.BlockSpec((tm, d), lambda i: (i, 0)), full(mod_l)] + _pair_specs(tm, kin, tiles_p)
        + [full(w_out), full(ln_g), full(ln_b)],
        out_specs=pl.BlockSpec((tm, d), lambda i: (i, 0)),
        out_shape=jax.ShapeDtypeStruct((t, d), F32),
        compiler_params=_cparams("arbitrary"),
        name="outproj_c",
    )(x, mod_l, o_p, o_s, w_out, ln_g, ln_b)


class _Tiles(NamedTuple):
    rows: int
    rows_wide: int
    hidden: int
    s5_tokens: int
    conv_chunk: int
    attn_q: int
    attn_blocks: int


def _plan_tiles(tp, ts, lp, ls):
    tiles = _Tiles(rows=256, rows_wide=512, hidden=1024, s5_tokens=min(4096, tp, ts), conv_chunk=min(lp, 256),
                   attn_q=min(ls, 512), attn_blocks=2)
    for tm in (tiles.rows, tiles.rows_wide):
        assert tp % tm == 0 and ls % tm == 0
    tok = tiles.s5_tokens
    assert tok % lp == 0 and tok % ls == 0 and tp % tok == 0 and ts % tok == 0 and tp % ls == 0
    return tiles


def kernel(x_prompt, x_sample, state_s5_re, state_s5_im, cache_k, cache_v, c, c_ctx, w_mod, b_mod, ln_g, ln_b, w_in_ab, w_dw, b_dw, conv_ln_g, conv_ln_b, s5_lambda_re, s5_lambda_im, s5_log_dt, s5_b_re, s5_b_im, s5_c_re, s5_c_im, s5_d, w_glu, w_out_ab, w_qkv, lam_q1, lam_k1, lam_q2, lam_k2, subln_g, w_out_c, w_ff1, w_ff2):
    bp, lp, d = x_prompt.shape
    bs, ls, _ = x_sample.shape
    depth = w_mod.shape[0]
    tp, ts = bp * lp, bs * ls
    alpha = (2 * depth) ** 0.25
    assert 1 + bs <= MOD_ROWS

    tiles = _plan_tiles(tp, ts, lp, ls)
    tm, tm_mlp, tf = tiles.rows, tiles.rows_wide, tiles.hidden

    xp, xs = x_prompt.reshape(tp, d), x_sample.reshape(ts, d)
    cvec = jnp.zeros((MOD_ROWS, d), F32).at[0].set(c_ctx).at[1:1 + bs].set(c)
    mod = _modvec(cvec, w_mod, b_mod)

    g_ssm, n_ssm, p_ssm = s5_b_re.shape[2:]
    dk = lam_q1.shape[-1]
    dv = subln_g.shape[-1]
    n_heads = w_out_c.shape[1] // dv

    w_ff1_bf, w_ff2_bf = w_ff1.astype(BF16), w_ff2.astype(BF16)
    s_re, s_im, k_list, v_list = [], [], [], []
    for l in range(depth):
        mod_l = mod[l]
        if l % 2 == 0:
            e = l // 2
            if l > 0:
                xp, xs = x[:tp], x[tp:]
            ug, u = _inproj(xp, xs, mod_l, w_in_ab[e].astype(BF16), tm, tp // tm, ls // tm)
            y_conv = _conv_module(ug, w_dw[e], b_dw[e], conv_ln_g[e], conv_ln_b[e], tiles.conv_chunk, lp, ls, tp)
            prep = _s5_prep(s5_lambda_re[e], s5_lambda_im[e], s5_log_dt[e],
                            s5_b_re[e], s5_b_im[e], s5_c_re[e], s5_c_im[e])
            a_re, a_im = prep[5], prep[6]
            gn = g_ssm * n_ssm
            nat = lambda a: a.reshape(2, g_ssm, p_ssm, n_ssm)[:, :, 0].reshape(2, gn)
            a4 = jnp.stack([nat(a_re)[0], nat(a_im)[0], nat(a_re)[1], nat(a_im)[1]])
            tile_k, tile_b = _s5_tiles(n_ssm, p_ssm)
            tok = tiles.s5_tokens
            seq_p, seq_s, tiles_p5 = tok // lp, tok // ls, tp // tok
            ms = max(seq_p, seq_s)
            st = lambda a, dr: jnp.pad(a[:, e, dr].reshape(ts // tok, seq_s, gn), ((0, 0), (0, ms - seq_s), (0, 0)))
            h0_s = jnp.stack([st(state_s5_re, 0), st(state_s5_im, 0), st(state_s5_re, 1), st(state_s5_im, 1)], axis=1)
            h0 = jnp.concatenate([jnp.zeros((tiles_p5, 4, ms, gn), F32), h0_s], axis=0)
            y_scan, hf = _s5_chunked(u, h0, prep[:5], tile_k, tile_b, a4, tok=tok,
                                     geoms=((seq_p, lp // S5_T), (seq_s, ls // S5_T)), n_tiles_p=tiles_p5,
                                     n=n_ssm, p=p_ssm)
            hf = hf[:tiles_p5, :, :seq_p].transpose(1, 0, 2, 3).reshape(2, 2, bp, g_ssm, n_ssm)
            s_re.append(hf[:, 0].transpose(1, 0, 2, 3))
            s_im.append(hf[:, 1].transpose(1, 0, 2, 3))
            x = _outab(xp, xs, mod_l, y_conv, y_scan, u, s5_d[e], w_glu[e].astype(BF16),
                       w_out_ab[e].astype(BF16), ln_g[l, 0], ln_b[l, 0], tm, tp // tm, ls // tm, alpha)
        else:
            o_i = l // 2
            lam_init = 0.8 - 0.6 * math.exp(-0.3 * l)
            cos_t, sin_t = _rope_tables(ls, dk)
            scale, fold_scale = _softmax_scale(dk)
            w_qkv_bf = w_qkv[o_i].astype(BF16)
            qkv_geo = (tm, tp // tm, ls // tm, dk, dv, scale if fold_scale else 1.0)
            q_p, k_p, v_p, k_new, v_new = _qkv(x, mod_l, w_qkv_bf, *qkv_geo, cache_shape=(bp, lp))
            q_s, k_s, v_s = _qkv(x, mod_l, w_qkv_bf, *qkv_geo, rope=(cos_t, sin_t))
            lamv = jnp.stack([lam_q1[o_i], lam_k1[o_i], lam_q2[o_i], lam_k2[o_i]])
            geo = dict(n_heads=n_heads, dk=dk, dv=dv, lam_init=lam_init)
            o_p = _attention(q_p, k_p, v_p, lamv, subln_g[o_i], row0=0, n_seq=bp, lq=lp, tq=lp,
                             hp=n_heads * dk // LANES, **geo)
            o_s = _attention(q_s, k_s, v_s, lamv, subln_g[o_i], row0=0, n_seq=bs, lq=ls,
                             tq=tiles.attn_q, hp=tiles.attn_blocks,
                             cache=(cache_k, cache_v, o_i), **geo)
            k_list.append(k_new)
            v_list.append(v_new)
            x = _outc(x, mod_l, o_p, o_s, w_out_c[o_i].astype(BF16), ln_g[l, 0], ln_b[l, 0],
                      tm_mlp, tp // tm_mlp, ls // tm_mlp, alpha)
        mlp = functools.partial(_mlp, x, mod_l, w_ff1_bf, w_ff2_bf, l, ln_g[l, 1], ln_b[l, 1],
                                tm_mlp, tf, tp // tm_mlp, ls // tm_mlp, alpha)
        if l < depth - 1:
            x = mlp()
        else:
            y_p = mlp(tile0=0, n_tiles=tp // tm_mlp)
            y_s = mlp(tile0=tp // tm_mlp, n_tiles=ts // tm_mlp)

    return (y_p.reshape(bp, lp, d), y_s.reshape(bs, ls, d),
            jnp.stack(s_re, axis=1), jnp.stack(s_im, axis=1),
            jnp.stack(k_list, axis=1), jnp.stack(v_list, axis=1))
```
